```python
import math
import jax
import jax.numpy as jnp
from jax import lax
import numpy as np

D_MODEL = 1024
BATCH = 16
SEQ = 2048
DEPTH = 2
DEC_BATCH = 8
DEC_SEQ = 64
PAST_LEN = 4096

CHUNK = 64
N_EVEN = (DEPTH + 1) // 2
N_ODD = DEPTH // 2
EPS = 1e-6

A_HEADS = 8
A_HEAD_DIM = 64
A_W = A_HEADS * A_HEAD_DIM
IDX_HEADS = 4
IDX_DIM = 64
TOPK_MAX = 256
Q_BLOCK = 64
N_BUCKETS = 32
MAX_DISTANCE = 128

B_HEADS = 4
B_KEY_DIM = 64
B_VAL_DIM = 128
B_KW = B_HEADS * B_KEY_DIM
B_VW = B_HEADS * B_VAL_DIM
B_GATE_RANK = 16
B_GATE_TAU = 16.0

C_HEADS = 8
C_HEAD_DIM = 128
C_W = C_HEADS * C_HEAD_DIM
CONV_WIDTH = 4
CONV_CH = 3 * C_W

D_FF = 2816

EVEN_SIZES = (A_W, A_W, A_W, IDX_HEADS * IDX_DIM, IDX_DIM, IDX_HEADS, B_KW, B_KW, B_VW, B_VW, B_GATE_RANK)
EVEN_IN = 3 * A_W + IDX_HEADS * IDX_DIM + IDX_DIM + IDX_HEADS + 2 * B_KW + 2 * B_VW + B_GATE_RANK
ODD_SIZES = (CONV_CH, C_W, C_HEADS, C_HEADS)
ODD_IN = CONV_CH + C_W + 2 * C_HEADS

kernel_name = 'streaming_dsa_gla_gdn_macaron_step'


def _rmsnorm(x, g):
    xf = x.astype(jnp.float32)
    y = xf * lax.rsqrt(jnp.mean(xf * xf, axis=-1, keepdims=True) + EPS)
    return (y * g.astype(jnp.float32)).astype(x.dtype)


def _l2norm(x):
    xf = x.astype(jnp.float32)
    return xf * lax.rsqrt(jnp.sum(xf * xf, axis=-1, keepdims=True) + EPS)


def _swiglu(h, wg, wu, wd):
    return (jax.nn.silu(h @ wg) * (h @ wu)) @ wd


def _split(x, sizes):
    parts, start = [], 0
    for s in sizes:
        parts.append(x[..., start:start + s])
        start += s
    return parts


def _t5_bucket(rel):
    half = N_BUCKETS // 2
    max_exact = half // 2
    n = jnp.abs(rel)
    nf = jnp.maximum(n, 1).astype(jnp.float32)
    large = max_exact + (jnp.log(nf / max_exact) / math.log(MAX_DISTANCE / max_exact) * (half - max_exact)).astype(jnp.int32)
    large = jnp.minimum(large, half - 1)
    return jnp.where(rel > 0, half, 0) + jnp.where(n < max_exact, n, large)


def _dsa_attention(q, k, v, q_idx, w_idx, k_idx, q_pos, k_pos, rel_bias, topk, qblock):
    bsz, lq, nh, hd = q.shape
    nblk = lq // qblock
    k_chunk = k_pos // CHUNK
    gather = jax.vmap(lambda arr, idx: arr[idx])

    def blk(args):
        qb, qib, wb, pb = args
        s = jax.nn.relu(jnp.einsum('bqhd,bkd->bqhk', qib, k_idx))
        score = jnp.einsum('bqh,bqhk->bqk', wb, s).astype(jnp.float32)
        adm = k_chunk[None, :] <= (pb // CHUNK)[:, None]
        score = jnp.where(adm[None], score, -jnp.inf)
        _, sel = lax.top_k(score, topk)
        k_sel = gather(k, sel)
        v_sel = gather(v, sel)
        sel_pos = k_pos[sel]
        logits = jnp.einsum('bqhd,bqkhd->bhqk', qb, k_sel).astype(jnp.float32) * (hd ** -0.5)
        bias = rel_bias[_t5_bucket(sel_pos - pb[None, :, None])].astype(jnp.float32)
        logits = logits + jnp.transpose(bias, (0, 3, 1, 2))
        ok = (sel_pos // CHUNK) <= (pb // CHUNK)[None, :, None]
        logits = jnp.where(ok[:, None], logits, -jnp.inf)
        p = jax.nn.softmax(logits, axis=-1).astype(v.dtype)
        return jnp.einsum('bhqk,bqkhd->bqhd', p, v_sel)

    def to_blocks(t):
        return jnp.swapaxes(t.reshape((bsz, nblk, qblock) + t.shape[2:]), 0, 1)

    out = lax.map(blk, (to_blocks(q), to_blocks(q_idx), to_blocks(w_idx), q_pos.reshape(nblk, qblock)))
    return jnp.swapaxes(out, 0, 1).reshape(bsz, lq, nh, hd)


def _gla_chunked(q, k, v, g, s0, chunk):
    bsz, L, nh, dk = q.shape
    dv = v.shape[-1]
    n = L // chunk

    def to_chunks(t):
        return jnp.transpose(t.astype(jnp.float32).reshape(bsz, n, chunk, nh, t.shape[-1]), (1, 0, 3, 2, 4))

    causal = jnp.tril(jnp.ones((chunk, chunk), dtype=bool))
    mid = chunk // 2

    def step(S, inp):
        qc, kc, vc, gc = inp
        b = jnp.cumsum(gc, axis=-2)
        b_mid = b[..., mid:mid + 1, :]
        b_last = b[..., -1:, :]
        o_inter = jnp.einsum('bhcd,bhde->bhce', qc * jnp.exp(b), S)
        a = jnp.einsum('bhtd,bhsd->bhts', qc * jnp.exp(b - b_mid), kc * jnp.exp(b_mid - b))
        a = jnp.where(causal, a, 0.0)
        o = o_inter + jnp.einsum('bhts,bhse->bhte', a, vc)
        S_new = jnp.swapaxes(jnp.exp(b_last), -1, -2) * S + jnp.einsum('bhcd,bhce->bhde', kc * jnp.exp(b_last - b), vc)
        return S_new, o

    S, o = lax.scan(step, s0.astype(jnp.float32), (to_chunks(q), to_chunks(k), to_chunks(v), to_chunks(g)))
    o = jnp.transpose(o, (1, 0, 3, 2, 4)).reshape(bsz, L, nh, dv)
    return o, S.astype(s0.dtype)


def _gated_delta_chunked(q, k, v, g, beta, s0, chunk):
    bsz, L, nh, dk = q.shape
    dv = v.shape[-1]
    n = L // chunk

    def to_chunks(t):
        t = t.astype(jnp.float32).reshape((bsz, n, chunk, nh) + t.shape[3:])
        return jnp.moveaxis(t, (1, 3), (0, 2))

    incl = jnp.tril(jnp.ones((chunk, chunk), dtype=bool))
    strict = jnp.tril(jnp.ones((chunk, chunk), dtype=bool), k=-1)
    eye = jnp.eye(chunk, dtype=jnp.float32)

    def step(S, inp):
        qc, kc, vc, gc, bc = inp
        gcum = jnp.cumsum(gc, axis=-1)
        decay = jnp.exp(jnp.where(incl, gcum[..., :, None] - gcum[..., None, :], -jnp.inf))
        kb = kc * bc[..., None]
        lower = jnp.where(strict, jnp.einsum('bhtd,bhsd->bhts', kb, kc) * decay, 0.0)
        rhs = jnp.concatenate([vc * bc[..., None], kb * jnp.exp(gcum)[..., None]], axis=-1)
        sol = lax.linalg.triangular_solve(eye + lower, rhs, left_side=True, lower=True, unit_diagonal=True)
        u, w = sol[..., :dv], sol[..., dv:]
        v_new = u - jnp.einsum('bhcd,bhde->bhce', w, S)
        attn = jnp.where(incl, jnp.einsum('bhtd,bhsd->bhts', qc, kc) * decay, 0.0)
        o = jnp.einsum('bhcd,bhde->bhce', qc * jnp.exp(gcum)[..., None], S) + jnp.einsum('bhts,bhse->bhte', attn, v_new)
        g_last = gcum[..., -1:]
        S_new = jnp.exp(g_last)[..., None] * S + jnp.einsum('bhcd,bhce->bhde', kc * jnp.exp(g_last - gcum)[..., None], v_new)
        return S_new, o

    S, o = lax.scan(step, s0.astype(jnp.float32), (to_chunks(q), to_chunks(k), to_chunks(v), to_chunks(g), to_chunks(beta)))
    o = jnp.moveaxis(o, (0, 2), (1, 3)).reshape(bsz, L, nh, dv)
    return o, S.astype(s0.dtype)


def _mix_even(h, past_k, past_v, past_ki, s0, w_in, q_norm, k_norm, rel_bias, gate_w2, gate_b2, gla_norm, w_out):
    bsz, L, _ = h.shape
    q, k, v, qi, ki, wi, qb, kb, vb, rb, ab = _split(h @ w_in, EVEN_SIZES)
    q = _rmsnorm(q.reshape(bsz, L, A_HEADS, A_HEAD_DIM), q_norm)
    k = _rmsnorm(k.reshape(bsz, L, A_HEADS, A_HEAD_DIM), k_norm)
    v = v.reshape(bsz, L, A_HEADS, A_HEAD_DIM)
    qi = qi.reshape(bsz, L, IDX_HEADS, IDX_DIM)
    if past_k is None:
        past_len = 0
        k_all, v_all, ki_all = k, v, ki
    else:
        past_len = past_k.shape[1]
        k_all = jnp.concatenate([past_k.astype(k.dtype), k], axis=1)
        v_all = jnp.concatenate([past_v.astype(v.dtype), v], axis=1)
        ki_all = jnp.concatenate([past_ki.astype(ki.dtype), ki], axis=1)
    n_keys = past_len + L
    q_pos = past_len + jnp.arange(L, dtype=jnp.int32)
    k_pos = jnp.arange(n_keys, dtype=jnp.int32)
    o_a = _dsa_attention(q, k_all, v_all, qi, wi, ki_all, q_pos, k_pos, rel_bias,
                         min(TOPK_MAX, n_keys // 4), min(Q_BLOCK, L))
    log_alpha = jax.nn.log_sigmoid((ab @ gate_w2 + gate_b2).astype(jnp.float32)) / B_GATE_TAU
    o_b, s_new = _gla_chunked(qb.reshape(bsz, L, B_HEADS, B_KEY_DIM) * (B_KEY_DIM ** -0.5),
                              kb.reshape(bsz, L, B_HEADS, B_KEY_DIM),
                              vb.reshape(bsz, L, B_HEADS, B_VAL_DIM),
                              log_alpha.reshape(bsz, L, B_HEADS, B_KEY_DIM), s0, min(CHUNK, L))
    o_b = _rmsnorm(o_b.astype(h.dtype), gla_norm).reshape(bsz, L, B_VW) * jax.nn.silu(rb)
    out = jnp.concatenate([o_a.reshape(bsz, L, A_W), o_b], axis=-1) @ w_out
    return out, k, v, ki, s_new


def _mix_odd(h, conv_prev, s0, w_in, conv_w, a_log, dt_bias, norm_g, w_out):
    bsz, L, _ = h.shape
    qkv, z, a, b = _split(h @ w_in, ODD_SIZES)
    xpad = jnp.concatenate([conv_prev.astype(qkv.dtype), qkv], axis=1)
    conv = xpad[:, 0:L] * conv_w[0]
    for j in range(1, CONV_WIDTH):
        conv = conv + xpad[:, j:j + L] * conv_w[j]
    q, k, v = _split(jax.nn.silu(conv), (C_W, C_W, C_W))
    q = _l2norm(q.reshape(bsz, L, C_HEADS, C_HEAD_DIM)) * (C_HEAD_DIM ** -0.5)
    k = _l2norm(k.reshape(bsz, L, C_HEADS, C_HEAD_DIM))
    v = v.reshape(bsz, L, C_HEADS, C_HEAD_DIM)
    beta = jax.nn.sigmoid(b.astype(jnp.float32))
    g = -jnp.exp(a_log.astype(jnp.float32)) * jax.nn.softplus(a.astype(jnp.float32) + dt_bias.astype(jnp.float32))
    o, s_new = _gated_delta_chunked(q, k, v, g, beta, s0, min(CHUNK, L))
    o = _rmsnorm(o.astype(h.dtype), norm_g) * jax.nn.silu(z.reshape(bsz, L, C_HEADS, C_HEAD_DIM))
    out = o.reshape(bsz, L, C_W) @ w_out
    return out, xpad[:, -(CONV_WIDTH - 1):], s_new


def _trunk(x, past, prm):
    bsz = x.shape[0]
    ks, vs, kis, glas, deltas, convs = [], [], [], [], [], []
    for i in range(DEPTH):
        h = _rmsnorm(x, prm['ffn_norm'][i, 0])
        x = x + 0.5 * _swiglu(h, prm['ffn_w_gate'][i, 0], prm['ffn_w_up'][i, 0], prm['ffn_w_down'][i, 0])
        h = _rmsnorm(x, prm['mix_norm'][i])
        j = i // 2
        if i % 2 == 0:
            if past is None:
                pk = pv = pki = None
                s0 = jnp.zeros((bsz, B_HEADS, B_KEY_DIM, B_VAL_DIM), x.dtype)
            else:
                pk, pv, pki, s0 = past['k'][j], past['v'][j], past['ki'][j], past['gla'][j]
            mixed, k_new, v_new, ki_new, s_new = _mix_even(
                h, pk, pv, pki, s0, prm['ev_w_in'][j], prm['ev_q_norm'][j], prm['ev_k_norm'][j],
                prm['rel_bias'], prm['ev_gate_w2'][j], prm['ev_gate_b2'][j], prm['ev_gla_norm'][j], prm['ev_w_out'][j])
            ks.append(k_new)
            vs.append(v_new)
            kis.append(ki_new)
            glas.append(s_new)
        else:
            if past is None:
                cprev = jnp.zeros((bsz, CONV_WIDTH - 1, CONV_CH), x.dtype)
                s0 = jnp.zeros((bsz, C_HEADS, C_HEAD_DIM, C_HEAD_DIM), x.dtype)
            else:
                cprev, s0 = past['conv'][j], past['delta'][j]
            mixed, c_new, s_new = _mix_odd(
                h, cprev, s0, prm['od_w_in'][j], prm['od_conv_w'][j], prm['od_a_log'][j],
                prm['od_dt_bias'][j], prm['od_norm'][j], prm['od_w_out'][j])
            convs.append(c_new)
            deltas.append(s_new)
        x = x + mixed
        h = _rmsnorm(x, prm['ffn_norm'][i, 1])
        x = x + 0.5 * _swiglu(h, prm['ffn_w_gate'][i, 1], prm['ffn_w_up'][i, 1], prm['ffn_w_down'][i, 1])
    return x, jnp.stack(ks), jnp.stack(vs), jnp.stack(kis), jnp.stack(glas), jnp.stack(deltas), jnp.stack(convs)


def setup_inputs(seed: int = 0) -> dict:
    key = jax.random.key(seed)
    keys = iter(jax.random.split(key, 40))

    def nrm(shape, scale):
        return scale * jax.random.normal(next(keys), shape, jnp.float32)

    d = {}
    d['x_prompt'] = nrm((BATCH, SEQ, D_MODEL), 1.0)
    d['x_sample'] = nrm((DEC_BATCH, DEC_SEQ, D_MODEL), 1.0)
    d['cache_attn_k'] = nrm((N_EVEN, DEC_BATCH, PAST_LEN, A_HEADS, A_HEAD_DIM), 1.0)
    d['cache_attn_v'] = nrm((N_EVEN, DEC_BATCH, PAST_LEN, A_HEADS, A_HEAD_DIM), 1.0)
    d['cache_idx_k'] = nrm((N_EVEN, DEC_BATCH, PAST_LEN, IDX_DIM), 1.0)
    d['state_gla'] = nrm((N_EVEN, DEC_BATCH, B_HEADS, B_KEY_DIM, B_VAL_DIM), 0.1)
    d['state_delta'] = nrm((N_ODD, DEC_BATCH, C_HEADS, C_HEAD_DIM, C_HEAD_DIM), 0.1)
    d['state_conv'] = nrm((N_ODD, DEC_BATCH, CONV_WIDTH - 1, CONV_CH), 1.0)
    d['ffn_norm'] = 1.0 + nrm((DEPTH, 2, D_MODEL), 0.02)
    d['ffn_w_gate'] = nrm((DEPTH, 2, D_MODEL, D_FF), D_MODEL ** -0.5)
    d['ffn_w_up'] = nrm((DEPTH, 2, D_MODEL, D_FF), D_MODEL ** -0.5)
    d['ffn_w_down'] = nrm((DEPTH, 2, D_FF, D_MODEL), D_FF ** -0.5)
    d['mix_norm'] = 1.0 + nrm((DEPTH, D_MODEL), 0.02)
    d['ev_w_in'] = nrm((N_EVEN, D_MODEL, EVEN_IN), D_MODEL ** -0.5)
    d['ev_q_norm'] = 1.0 + nrm((N_EVEN, A_HEAD_DIM), 0.02)
    d['ev_k_norm'] = 1.0 + nrm((N_EVEN, A_HEAD_DIM), 0.02)
    d['rel_bias'] = nrm((N_BUCKETS, A_HEADS), 0.5)
    d['ev_gate_w2'] = nrm((N_EVEN, B_GATE_RANK, B_KW), B_GATE_RANK ** -0.5)
    d['ev_gate_b2'] = nrm((N_EVEN, B_KW), 0.1)
    d['ev_gla_norm'] = 1.0 + nrm((N_EVEN, B_VAL_DIM), 0.02)
    d['ev_w_out'] = nrm((N_EVEN, A_W + B_VW, D_MODEL), (A_W + B_VW) ** -0.5)
    d['od_w_in'] = nrm((N_ODD, D_MODEL, ODD_IN), D_MODEL ** -0.5)
    d['od_conv_w'] = nrm((N_ODD, CONV_WIDTH, CONV_CH), CONV_WIDTH ** -0.5)
    d['od_a_log'] = jnp.log(jax.random.uniform(next(keys), (N_ODD, C_HEADS), jnp.float32, 1.0, 16.0))
    dt = jnp.exp(jax.random.uniform(next(keys), (N_ODD, C_HEADS), jnp.float32, math.log(1e-3), math.log(1e-1)))
    d['od_dt_bias'] = dt + jnp.log(-jnp.expm1(-dt))
    d['od_norm'] = 1.0 + nrm((N_ODD, C_HEAD_DIM), 0.02)
    d['od_w_out'] = nrm((N_ODD, C_W, D_MODEL), C_W ** -0.5)
    return d


def reference(x_prompt, x_sample, cache_attn_k, cache_attn_v, cache_idx_k, state_gla, state_delta, state_conv,
              ffn_norm, ffn_w_gate, ffn_w_up, ffn_w_down, mix_norm, ev_w_in, ev_q_norm, ev_k_norm, rel_bias,
              ev_gate_w2, ev_gate_b2, ev_gla_norm, ev_w_out, od_w_in, od_conv_w, od_a_log, od_dt_bias, od_norm,
              od_w_out):
    prm = {'ffn_norm': ffn_norm, 'ffn_w_gate': ffn_w_gate, 'ffn_w_up': ffn_w_up, 'ffn_w_down': ffn_w_down,
           'mix_norm': mix_norm, 'ev_w_in': ev_w_in, 'ev_q_norm': ev_q_norm, 'ev_k_norm': ev_k_norm,
           'rel_bias': rel_bias, 'ev_gate_w2': ev_gate_w2, 'ev_gate_b2': ev_gate_b2, 'ev_gla_norm': ev_gla_norm,
           'ev_w_out': ev_w_out, 'od_w_in': od_w_in, 'od_conv_w': od_conv_w, 'od_a_log': od_a_log,
           'od_dt_bias': od_dt_bias, 'od_norm': od_norm, 'od_w_out': od_w_out}
    past = {'k': cache_attn_k, 'v': cache_attn_v, 'ki': cache_idx_k, 'gla': state_gla,
            'delta': state_delta, 'conv': state_conv}
    y_prompt, p_attn_k, p_attn_v, p_idx_k, p_gla, p_delta, p_conv = _trunk(x_prompt, None, prm)
    y_sample, s_attn_k, s_attn_v, s_idx_k, s_gla, s_delta, s_conv = _trunk(x_sample, past, prm)
    return (y_prompt, y_sample, p_attn_k, p_attn_v, p_idx_k, p_gla, p_delta, p_conv,
            s_attn_k, s_attn_v, s_idx_k, s_gla, s_delta, s_conv)
```

```python
import functools
import math

import jax
import jax.numpy as jnp
from jax import lax
from jax.experimental import pallas as pl
from jax.experimental.pallas import tpu as pltpu

D_MODEL = 1024
DEPTH = 2
CHUNK = 64
EPS = 1e-6

A_HEADS = 8
A_HEAD_DIM = 64
A_W = A_HEADS * A_HEAD_DIM
IDX_HEADS = 4
IDX_DIM = 64
TOPK_MAX = 256
N_BUCKETS = 32
MAX_DISTANCE = 128

B_HEADS = 4
B_KEY_DIM = 64
B_VAL_DIM = 128
B_KW = B_HEADS * B_KEY_DIM
B_VW = B_HEADS * B_VAL_DIM
B_GATE_RANK = 16
B_GATE_TAU = 16.0

C_HEADS = 8
C_HEAD_DIM = 128
C_W = C_HEADS * C_HEAD_DIM
CONV_WIDTH = 4
CONV_CH = 3 * C_W

D_FF = 2816

V7X_LANES = 128
V7X_SUBLANES = 8
V7X_VMEM_LIMIT_BYTES = 56 * 1024 * 1024

_F32 = jnp.float32
_BF16 = jnp.bfloat16
_HI = lax.Precision.HIGHEST
_NEG = -1e30
_NN = (((1,), (0,)), ((), ()))
_NT = (((1,), (1,)), ((), ()))
_TN = (((0,), (0,)), ((), ()))

_EV_Q, _EV_K, _EV_V = 0, 512, 1024
_EV_IDX = 1536
_EV_IDX_W = 384
_EV_WI_OFF = 256 + IDX_DIM
_EV_AB_OFF = _EV_WI_OFF + IDX_HEADS
_EV_QB, _EV_KB, _EV_VB, _EV_RB = 1920, 2176, 2432, 2944
_EV_COLS = 3456
_OD_QKV, _OD_Z, _OD_AB = 0, 3072, 4096
_OD_COLS = 4224


def _params(n_axes):
    return pltpu.CompilerParams(dimension_semantics=("arbitrary",) * n_axes,
                                vmem_limit_bytes=V7X_VMEM_LIMIT_BYTES)


def _rms(x, g):
    return x * lax.rsqrt(jnp.mean(x * x, axis=-1, keepdims=True) + EPS) * g


def _silu(x):
    return x * jax.nn.sigmoid(x)


def _dot(a, b, dims=_NN):
    return lax.dot_general(a.astype(_BF16), b.astype(_BF16), dims, preferred_element_type=_F32)


def _dot_hi(a, b, dims=_NN):
    return lax.dot_general(a, b, dims, precision=_HI, preferred_element_type=_F32)


def _full(shape):
    return pl.BlockSpec(shape, lambda *_: (0,) * len(shape))


def _ffn_kernel(x_ref, g_ref, wg_ref, wu_ref, wd_ref, o_ref, h_ref, acc_ref):
    f = pl.program_id(1)

    @pl.when(f == 0)
    def _():
        h_ref[...] = _rms(x_ref[...], g_ref[...]).astype(_BF16)
        acc_ref[...] = jnp.zeros_like(acc_ref)

    h = h_ref[...]
    gate = jnp.dot(h, wg_ref[...], preferred_element_type=_F32)
    up = jnp.dot(h, wu_ref[...], preferred_element_type=_F32)
    act = (_silu(gate) * up).astype(_BF16)
    acc_ref[...] += jnp.dot(act, wd_ref[...], preferred_element_type=_F32)

    @pl.when(f == pl.num_programs(1) - 1)
    def _():
        o_ref[...] = x_ref[...] + 0.5 * acc_ref[...]


def _ffn_half(x, g, wg, wu, wd):
    t, d = x.shape
    ff = wg.shape[1]
    tm = min(t, 1024)
    tf = 256
    return pl.pallas_call(
        _ffn_kernel,
        grid=(t // tm, ff // tf),
        in_specs=[pl.BlockSpec((tm, d), lambda i, f: (i, 0)),
                  pl.BlockSpec((1, d), lambda i, f: (0, 0)),
                  pl.BlockSpec((d, tf), lambda i, f: (0, f)),
                  pl.BlockSpec((d, tf), lambda i, f: (0, f)),
                  pl.BlockSpec((tf, d), lambda i, f: (f, 0))],
        out_specs=pl.BlockSpec((tm, d), lambda i, f: (i, 0)),
        out_shape=jax.ShapeDtypeStruct((t, d), _F32),
        scratch_shapes=[pltpu.VMEM((tm, d), _BF16), pltpu.VMEM((tm, d), _F32)],
        compiler_params=_params(2),
        name="ffn_half",
    )(x, g.reshape(1, d), wg, wu, wd)


def _even_in_kernel(x_ref, g_ref, w_ref, qg_ref, kg_ref, hm_ref, w2_ref, b2_ref,
                    qn_ref, k_ref, v_ref, idx_ref, qb_ref, kb_ref, vb_ref, rb_ref, la_ref):
    h = _rms(x_ref[...], g_ref[...]).astype(_BF16)

    def proj(lo, width):
        return jnp.dot(h, w_ref[:, lo:lo + width], preferred_element_type=_F32)

    def headnorm(t, gain):
        t2 = t * t
        hi = t2.astype(_BF16)
        lo = (t2 - hi.astype(_F32)).astype(_BF16)
        ms = (jnp.dot(hi, hm_ref[...], preferred_element_type=_F32)
              + jnp.dot(lo, hm_ref[...], preferred_element_type=_F32))
        return t * lax.rsqrt(ms + EPS) * gain

    qn_ref[...] = headnorm(proj(_EV_Q, A_W), qg_ref[...]).astype(_BF16)
    k_ref[...] = headnorm(proj(_EV_K, A_W), kg_ref[...])
    v_ref[...] = proj(_EV_V, A_W)
    idx = proj(_EV_IDX, _EV_IDX_W)
    idx_ref[...] = idx
    qb_ref[...] = proj(_EV_QB, B_KW) * (B_KEY_DIM ** -0.5)
    kb_ref[...] = proj(_EV_KB, B_KW)
    vb_ref[...] = proj(_EV_VB, B_VW)
    rb_ref[...] = proj(_EV_RB, B_VW)
    z = _dot_hi(idx[:, 256:384], w2_ref[...]) + b2_ref[...]
    la_ref[...] = (jnp.minimum(z, 0.0) - jnp.log1p(jnp.exp(-jnp.abs(z)))) * (1.0 / B_GATE_TAU)


def _even_in_proj(x, g, w, q_gain, k_gain, head_mean, w2, b2):
    t, d = x.shape
    tm = min(t, 256)
    row = lambda n: pl.BlockSpec((tm, n), lambda i: (i, 0))
    outs = [(A_W, _BF16), (A_W, _F32), (A_W, _F32), (_EV_IDX_W, _F32), (B_KW, _F32), (B_KW, _F32),
            (B_VW, _F32), (B_VW, _F32), (B_KW, _F32)]
    return pl.pallas_call(
        _even_in_kernel,
        grid=(t // tm,),
        in_specs=[row(d), _full((1, d)), _full((d, _EV_COLS)), _full((1, A_W)), _full((1, A_W)),
                  _full((A_W, A_W)), _full((V7X_LANES, B_KW)), _full((1, B_KW))],
        out_specs=[row(n) for n, _ in outs],
        out_shape=[jax.ShapeDtypeStruct((t, n), dt) for n, dt in outs],
        compiler_params=_params(1),
        name="even_in_proj",
    )(x, g.reshape(1, d), w, q_gain, k_gain, head_mean, w2, b2)


def _dsa_kernel(rb_ref, q_ref, qi_ref, wit_ref, k_ref, vt_ref, ki_ref, bkt_ref, o_ref,
                key_ref, mask_ref, nb_ref, *, past, tq, kc, topk, r0_off):
    t = pl.program_id(1)

    @pl.when((pl.program_id(0) == 0) & (t == 0))
    def _():
        bk = bkt_ref[...]
        for h in range(A_HEADS):
            acc = jnp.zeros(bk.shape, _F32)
            for bb in range(N_BUCKETS):
                acc = jnp.where(bk == bb, rb_ref[bb, h], acc)
            nb_ref[h] = acc - rb_ref[N_BUCKETS // 2 - 1, h]

    q0 = past + t * tq
    n_chunks = (q0 + tq + kc - 1) // kc
    q_chunk = (q0 + lax.broadcasted_iota(jnp.int32, (1, tq), 1)) // CHUNK
    k_iota = lax.broadcasted_iota(jnp.int32, (kc, tq), 0)
    upper_half = lax.broadcasted_iota(jnp.int32, (tq, V7X_LANES), 1) >= A_HEAD_DIM

    def half_select(x, hh):
        keep = upper_half if hh == 1 else jnp.logical_not(upper_half)
        return jnp.where(keep, x, jnp.zeros_like(x))

    def admissible(r0):
        return ((r0 + k_iota) // CHUNK) <= q_chunk

    qi = qi_ref[0]
    qih = [half_select(qi[:, (h // 2) * V7X_LANES:(h // 2 + 1) * V7X_LANES], h % 2)
           for h in range(IDX_HEADS)]
    wit = wit_ref[0]

    def score_body(j, carry):
        r0 = pl.multiple_of(j * kc, kc)
        kic = ki_ref[0, pl.ds(r0, kc), :]
        s = jnp.zeros((kc, tq), _F32)
        for h in range(IDX_HEADS):
            d = _dot_hi(kic, qih[h], _NT)
            s = s + wit[h:h + 1, :] * jnp.maximum(d, 0.0)
        s = jnp.where(s == 0.0, 0.0, s)
        s = jnp.where(admissible(r0), s, -jnp.inf)
        bits = pltpu.bitcast(s, jnp.int32)
        key_ref[pl.ds(r0, kc), :] = bits ^ ((bits >> 31) & jnp.int32(0x7FFFFFFF))
        return carry

    lax.fori_loop(0, n_chunks, score_body, 0)

    int_min = jnp.int32(-2 ** 31)

    def count_ge(cs):
        def body(j, acc):
            r0 = pl.multiple_of(j * kc, kc)
            m = jnp.where(key_ref[pl.ds(r0, kc), :] >= cs, 1, 0)
            return acc + m.reshape(kc // V7X_SUBLANES, V7X_SUBLANES, tq).sum(axis=0)
        acc = lax.fori_loop(0, n_chunks, body, jnp.zeros((V7X_SUBLANES, tq), jnp.int32))
        return acc.sum(axis=0, keepdims=True)

    def bit_body(i, tu):
        cu = tu | jnp.left_shift(jnp.int32(1), 31 - i)
        return jnp.where(count_ge(cu ^ int_min) >= topk, cu, tu)

    tu = lax.fori_loop(0, 32, bit_body, jnp.zeros((1, tq), jnp.int32))
    ts = tu ^ int_min
    n_greater = count_ge(ts + 1)
    room = (topk - n_greater).astype(_F32)

    tri = (lax.broadcasted_iota(jnp.int32, (kc, kc), 0)
           >= lax.broadcasted_iota(jnp.int32, (kc, kc), 1)).astype(_BF16)

    def mask_body(j, seen):
        r0 = pl.multiple_of(j * kc, kc)
        blk = key_ref[pl.ds(r0, kc), :]
        eq = blk == ts
        rank = jnp.dot(tri, jnp.where(eq, 1.0, 0.0).astype(_BF16), preferred_element_type=_F32) + seen
        val = jnp.where(blk > ts, 0.0, jnp.where(eq, jnp.where(rank <= room, 0.0, _NEG), _NEG))
        mask_ref[pl.ds(r0, kc), :] = jnp.where(admissible(r0), val, _NEG)
        return rank[kc - 1:kc, :]

    lax.fori_loop(0, n_chunks, mask_body, jnp.zeros((1, tq), _F32))

    eye = (lax.broadcasted_iota(jnp.int32, (tq, tq), 0)
           == lax.broadcasted_iota(jnp.int32, (tq, tq), 1)).astype(_BF16)
    for pair in range(A_HEADS // 2):
        lanes = slice(pair * V7X_LANES, (pair + 1) * V7X_LANES)
        q_pair = q_ref[0][:, lanes]
        outs = []
        for hh in range(2):
            h = 2 * pair + hh
            qh = half_select(q_pair, hh)

            def attn_body(j, carry, h=h, qh=qh):
                m, l, acc = carry
                r0 = pl.multiple_of(j * kc, kc)
                kblk = k_ref[0, pl.ds(r0, kc), lanes]
                lt = lax.dot_general(kblk, qh, _NT, preferred_element_type=_F32)
                lt = lt + mask_ref[pl.ds(r0, kc), :]
                off = pl.multiple_of(jnp.maximum(r0 - q0 + r0_off, 0), CHUNK)
                lt = lt + nb_ref[h, pl.ds(off, kc), :]
                m_new = jnp.maximum(m, lt.max(axis=0, keepdims=True))
                p = jnp.exp(lt - m_new)
                alpha = jnp.exp(m - m_new)
                l = alpha * l + p.sum(axis=0, keepdims=True)
                vblk = vt_ref[0, h * A_HEAD_DIM:(h + 1) * A_HEAD_DIM, pl.ds(r0, kc)]
                acc = alpha * acc + jnp.dot(vblk, p.astype(_BF16), preferred_element_type=_F32)
                return m_new, l, acc

            init = (jnp.full((1, tq), _NEG, _F32), jnp.zeros((1, tq), _F32),
                    jnp.zeros((A_HEAD_DIM, tq), _F32))
            _, l, acc = lax.fori_loop(0, n_chunks, attn_body, init)
            outs.append(acc / l)
        o_t = jnp.concatenate(outs, axis=0).astype(_BF16)
        o_ref[0, :, lanes] = lax.dot_general(eye, o_t, _NT, preferred_element_type=_F32).astype(_BF16)


def _t5_bucket_table(rel):
    half = N_BUCKETS // 2
    max_exact = half // 2
    n = jnp.abs(rel)
    nf = jnp.maximum(n, 1).astype(jnp.float32)
    large = max_exact + (jnp.log(nf / max_exact) / math.log(MAX_DISTANCE / max_exact)
                         * (half - max_exact)).astype(jnp.int32)
    large = jnp.minimum(large, half - 1)
    return jnp.where(rel > 0, half, 0) + jnp.where(n < max_exact, n, large)


def _dsa_attention(qn, idx, k_all, v_all, ki_all, rel_bias, past):
    bsz, lq, _ = qn.shape
    nk = k_all.shape[1]
    topk = min(TOPK_MAX, nk // 4)
    tq = min(V7X_LANES, lq)
    kc = 256
    nk_pad = -(-nk // kc) * kc
    pad = ((0, 0), (0, nk_pad - nk), (0, 0))
    k16 = jnp.pad(k_all.astype(_BF16), pad)
    vt16 = jnp.swapaxes(jnp.pad(v_all.astype(_BF16), pad), 1, 2)
    ki2 = jnp.pad(jnp.concatenate([ki_all, ki_all], axis=-1), pad)
    wit = jnp.swapaxes(idx[:, :, _EV_WI_OFF:_EV_WI_OFF + V7X_SUBLANES], 1, 2)
    r0_off = kc + MAX_DISTANCE
    nbr = r0_off + kc + tq
    rel = (jnp.arange(nbr, dtype=jnp.int32)[:, None] - r0_off) - jnp.arange(tq, dtype=jnp.int32)[None, :]
    bkt = _t5_bucket_table(rel)
    kern = functools.partial(_dsa_kernel, past=past, tq=tq, kc=kc, topk=topk, r0_off=r0_off)
    return pl.pallas_call(
        kern,
        grid=(bsz, lq // tq),
        in_specs=[pl.BlockSpec(memory_space=pltpu.SMEM),
                  pl.BlockSpec((1, tq, A_W), lambda b, t: (b, t, 0)),
                  pl.BlockSpec((1, tq, 2 * V7X_LANES), lambda b, t: (b, t, 0)),
                  pl.BlockSpec((1, V7X_SUBLANES, tq), lambda b, t: (b, 0, t)),
                  pl.BlockSpec((1, nk_pad, A_W), lambda b, t: (b, 0, 0)),
                  pl.BlockSpec((1, A_W, nk_pad), lambda b, t: (b, 0, 0)),
                  pl.BlockSpec((1, nk_pad, V7X_LANES), lambda b, t: (b, 0, 0)),
                  pl.BlockSpec((nbr, tq), lambda b, t: (0, 0))],
        out_specs=pl.BlockSpec((1, tq, A_W), lambda b, t: (b, t, 0)),
        out_shape=jax.ShapeDtypeStruct((bsz, lq, A_W), _BF16),
        scratch_shapes=[pltpu.VMEM((nk_pad, tq), jnp.int32), pltpu.VMEM((nk_pad, tq), _F32),
                        pltpu.VMEM((A_HEADS, nbr, tq), _F32)],
        compiler_params=_params(2),
        name="dsa_attention",
    )(rel_bias, qn, idx, wit, k16, vt16, ki2, bkt)


def _gla_kernel(q_ref, k_ref, g_ref, v_ref, r_ref, s0_ref, ng_ref, o_ref, s_out_ref, st_ref):
    c = pl.program_id(1)
    ii = lax.broadcasted_iota(jnp.int32, (CHUNK, CHUNK), 0)
    jj = lax.broadcasted_iota(jnp.int32, (CHUNK, CHUNK), 1)
    causal = ii >= jj
    eye = (ii == jj).astype(_F32)

    @pl.when(c == 0)
    def _():
        eye_v = (lax.broadcasted_iota(jnp.int32, (B_VAL_DIM, B_VAL_DIM), 0)
                 == lax.broadcasted_iota(jnp.int32, (B_VAL_DIM, B_VAL_DIM), 1)).astype(_F32)
        for h in range(B_HEADS):
            st_ref[h] = _dot_hi(eye_v, s0_ref[0, h], _NT)

    b = _dot_hi(causal.astype(_F32), g_ref[0])
    mid = CHUNK // 2
    b_mid = b[mid:mid + 1, :]
    b_last = b[CHUNK - 1:CHUNK, :]
    q = q_ref[0]
    k = k_ref[0]
    qe = q * jnp.exp(b)
    qm = q * jnp.exp(b - b_mid)
    km = k * jnp.exp(b_mid - b)
    kl = k * jnp.exp(b_last - b)
    decay = jnp.exp(b_last)
    for h in range(B_HEADS):
        ks = slice(h * B_KEY_DIM, (h + 1) * B_KEY_DIM)
        vs = slice(h * B_VAL_DIM, (h + 1) * B_VAL_DIM)
        v = v_ref[0][:, vs]
        a = jnp.where(causal, _dot_hi(qm[:, ks], km[:, ks], _NT), 0.0)
        st = st_ref[h]
        o = _dot_hi(qe[:, ks], st, _NT) + _dot_hi(a, v)
        st_ref[h] = decay[:, ks] * st + _dot_hi(v, kl[:, ks], _TN)
        on = _rms(o, ng_ref[...])
        o_ref[0, :, vs] = (on * _silu(r_ref[0][:, vs])).astype(_BF16)

    @pl.when(c == pl.num_programs(1) - 1)
    def _():
        for h in range(B_HEADS):
            s_out_ref[0, h] = _dot_hi(eye, st_ref[h], _NT)


def _gla(qb, kb, la, vb, rb, s0, norm_g):
    bsz, lq, _ = qb.shape
    tok = lambda n: pl.BlockSpec((1, CHUNK, n), lambda b, c: (b, c, 0))
    state = pl.BlockSpec((1, B_HEADS, B_KEY_DIM, B_VAL_DIM), lambda b, c: (b, 0, 0, 0))
    return pl.pallas_call(
        _gla_kernel,
        grid=(bsz, lq // CHUNK),
        in_specs=[tok(B_KW), tok(B_KW), tok(B_KW), tok(B_VW), tok(B_VW), state,
                  pl.BlockSpec((1, B_VAL_DIM), lambda b, c: (0, 0))],
        out_specs=[tok(B_VW), state],
        out_shape=[jax.ShapeDtypeStruct((bsz, lq, B_VW), _BF16),
                   jax.ShapeDtypeStruct(s0.shape, _F32)],
        scratch_shapes=[pltpu.VMEM((B_HEADS, B_VAL_DIM, B_KEY_DIM), _F32)],
        compiler_params=_params(2),
        name="gla",
    )(qb, kb, la, vb, rb, s0, norm_g.reshape(1, B_VAL_DIM))


def _out_proj_kernel(*refs):
    x_ref, o_ref = refs[0], refs[-1]
    n = (len(refs) - 2) // 2
    acc = x_ref[...]
    for i in range(n):
        acc = acc + jnp.dot(refs[1 + i][...], refs[1 + n + i][...], preferred_element_type=_F32)
    o_ref[...] = acc


def _out_proj(x, acts, weights):
    t, d = x.shape
    tm = min(t, 512)
    return pl.pallas_call(
        _out_proj_kernel,
        grid=(t // tm,),
        in_specs=([pl.BlockSpec((tm, d), lambda i: (i, 0))]
                  + [pl.BlockSpec((tm, a.shape[1]), lambda i: (i, 0)) for a in acts]
                  + [_full(w.shape) for w in weights]),
        out_specs=pl.BlockSpec((tm, d), lambda i: (i, 0)),
        out_shape=jax.ShapeDtypeStruct((t, d), _F32),
        compiler_params=_params(1),
        name="out_proj",
    )(x, *acts, *weights)


def _odd_in_kernel(x_ref, g_ref, w_ref, alog_ref, dt_ref, qkv_ref, z_ref, gb_ref):
    h = _rms(x_ref[...], g_ref[...]).astype(_BF16)
    qkv_ref[...] = jnp.dot(h, w_ref[:, _OD_QKV:_OD_QKV + CONV_CH], preferred_element_type=_F32)
    z_ref[...] = jnp.dot(h, w_ref[:, _OD_Z:_OD_Z + C_W], preferred_element_type=_F32)
    ab = jnp.dot(h, w_ref[:, _OD_AB:_OD_AB + V7X_LANES], preferred_element_type=_F32)
    xa = ab + dt_ref[...]
    g = -jnp.exp(alog_ref[...]) * (jnp.maximum(xa, 0.0) + jnp.log1p(jnp.exp(-jnp.abs(xa))))
    lane = lax.broadcasted_iota(jnp.int32, ab.shape, 1)
    gb_ref[...] = jnp.where(lane < C_HEADS, g, jax.nn.sigmoid(ab))


def _odd_in_proj(x, g, w, a_log, dt_bias):
    t, d = x.shape
    tm = min(t, 256)
    row = lambda n: pl.BlockSpec((tm, n), lambda i: (i, 0))
    outs = [CONV_CH, C_W, V7X_LANES]
    return pl.pallas_call(
        _odd_in_kernel,
        grid=(t // tm,),
        in_specs=[row(d), _full((1, d)), _full((d, _OD_COLS)), _full((1, V7X_LANES)), _full((1, V7X_LANES))],
        out_specs=[row(n) for n in outs],
        out_shape=[jax.ShapeDtypeStruct((t, n), _F32) for n in outs],
        compiler_params=_params(1),
        name="odd_in_proj",
    )(x, g.reshape(1, d), w, a_log, dt_bias)


def _delta_kernel(qkv_ref, z_ref, gb_ref, cw_ref, cprev_ref, s0_ref, ng_ref, o_ref, s_out_ref,
                  xbuf_ref, act_ref, st_ref):
    c = pl.program_id(1)
    halo = V7X_SUBLANES

    @pl.when(c == 0)
    def _():
        st_ref[...] = s0_ref[0]
        xbuf_ref[0:halo, :] = cprev_ref[0]

    xbuf_ref[halo:halo + CHUNK, :] = qkv_ref[0]
    conv = xbuf_ref[halo:halo + CHUNK, :] * cw_ref[CONV_WIDTH - 1:CONV_WIDTH, :]
    for j in range(CONV_WIDTH - 1):
        sh = CONV_WIDTH - 1 - j
        conv = conv + xbuf_ref[halo - sh:halo - sh + CHUNK, :] * cw_ref[j:j + 1, :]
    xbuf_ref[0:halo, :] = xbuf_ref[CHUNK:CHUNK + halo, :]
    act_ref[...] = _silu(conv)

    ii = lax.broadcasted_iota(jnp.int32, (CHUNK, CHUNK), 0)
    jj = lax.broadcasted_iota(jnp.int32, (CHUNK, CHUNK), 1)
    incl = ii >= jj
    strict = ii > jj
    incl_f = incl.astype(_F32)
    strict_f = strict.astype(_F32)
    eye = (ii == jj).astype(_F32)

    gb = gb_ref[0]
    gcum_all = _dot_hi(incl_f, gb)
    for h in range(C_HEADS):
        hs = slice(h * C_HEAD_DIM, (h + 1) * C_HEAD_DIM)
        q = act_ref[:, hs]
        k = act_ref[:, C_W + h * C_HEAD_DIM:C_W + (h + 1) * C_HEAD_DIM]
        v = act_ref[:, 2 * C_W + h * C_HEAD_DIM:2 * C_W + (h + 1) * C_HEAD_DIM]
        q = q * lax.rsqrt(jnp.sum(q * q, axis=-1, keepdims=True) + EPS) * (C_HEAD_DIM ** -0.5)
        k = k * lax.rsqrt(jnp.sum(k * k, axis=-1, keepdims=True) + EPS)
        g = gb[:, h:h + 1]
        beta = gb[:, C_HEADS + h:C_HEADS + h + 1]
        gcum = gcum_all[:, h:h + 1]
        g_last = gcum_all[CHUNK - 1:CHUNK, h:h + 1]
        dlog = _dot_hi(incl_f, g * strict_f)
        decay = jnp.where(incl, jnp.exp(dlog), 0.0)
        kb = k * beta
        lower = jnp.where(strict, _dot_hi(kb, k, _NT) * decay, 0.0)
        pw = -lower
        inv = eye + pw
        for _ in range(5):
            pw = _dot_hi(pw, pw)
            inv = inv + _dot_hi(inv, pw)
        rhs = jnp.concatenate([v * beta, kb * jnp.exp(gcum)], axis=-1)
        sol = _dot_hi(inv, rhs)
        u = sol[:, :C_HEAD_DIM]
        w = sol[:, C_HEAD_DIM:]
        st = st_ref[h]
        v_new = u - _dot_hi(w, st)
        attn = jnp.where(incl, _dot_hi(q, k, _NT) * decay, 0.0)
        o = _dot_hi(q * jnp.exp(gcum), st) + _dot_hi(attn, v_new)
        st_ref[h] = jnp.exp(g_last) * st + _dot_hi(k * jnp.exp(g_last - gcum), v_new, _TN)
        on = _rms(o, ng_ref[...])
        o_ref[0, :, hs] = (on * _silu(z_ref[0][:, hs])).astype(_BF16)

    @pl.when(c == pl.num_programs(1) - 1)
    def _():
        s_out_ref[0] = st_ref[...]


def _delta(qkv, z, gb, conv_w, conv_prev8, s0, norm_g):
    bsz, lq, _ = qkv.shape
    tok = lambda n: pl.BlockSpec((1, CHUNK, n), lambda b, c: (b, c, 0))
    state = pl.BlockSpec((1, C_HEADS, C_HEAD_DIM, C_HEAD_DIM), lambda b, c: (b, 0, 0, 0))
    return pl.pallas_call(
        _delta_kernel,
        grid=(bsz, lq // CHUNK),
        in_specs=[tok(CONV_CH), tok(C_W), tok(V7X_LANES),
                  pl.BlockSpec((CONV_WIDTH, CONV_CH), lambda b, c: (0, 0)),
                  pl.BlockSpec((1, V7X_SUBLANES, CONV_CH), lambda b, c: (b, 0, 0)),
                  state,
                  pl.BlockSpec((1, C_HEAD_DIM), lambda b, c: (0, 0))],
        out_specs=[tok(C_W), state],
        out_shape=[jax.ShapeDtypeStruct((bsz, lq, C_W), _BF16),
                   jax.ShapeDtypeStruct(s0.shape, _F32)],
        scratch_shapes=[pltpu.VMEM((CHUNK + V7X_SUBLANES, CONV_CH), _F32),
                        pltpu.VMEM((CHUNK, CONV_CH), _F32),
                        pltpu.VMEM((C_HEADS, C_HEAD_DIM, C_HEAD_DIM), _F32)],
        compiler_params=_params(2),
        name="gated_delta",
    )(qkv, z, gb, conv_w, conv_prev8, s0, norm_g.reshape(1, C_HEAD_DIM))


def _prep_even_w_in(w):
    sizes = (A_W, A_W, A_W, IDX_HEADS * IDX_DIM, IDX_DIM, IDX_HEADS, B_KW, B_KW, B_VW, B_VW, B_GATE_RANK)
    offs = [0]
    for s in sizes:
        offs.append(offs[-1] + s)
    part = lambda i: w[:, offs[i]:offs[i + 1]]
    q, k, v, qi, ki, wi, qb, kb, vb, rb, ab = [part(i) for i in range(len(sizes))]
    pad = jnp.zeros((w.shape[0], _EV_IDX_W - (256 + IDX_DIM + IDX_HEADS + B_GATE_RANK)), w.dtype)
    return jnp.concatenate([q, k, v, qi, ki, wi, ab, pad, qb, kb, vb, rb], axis=1).astype(_BF16)


def _prep_odd_w_in(w):
    pad = jnp.zeros((w.shape[0], _OD_COLS - w.shape[1]), w.dtype)
    return jnp.concatenate([w, pad], axis=1).astype(_BF16)


def _lane_pad(v, n=V7X_LANES):
    return jnp.pad(v, (0, n - v.shape[0])).reshape(1, n)


def _trunk(x, past, prm):
    bsz, lq, d = x.shape
    t = bsz * lq
    xf = x.reshape(t, d)

    def ffn(xf, i, j):
        return _ffn_half(xf, prm['ffn_norm'][i, j], prm['ffn_w_gate'][i, j].astype(_BF16),
                         prm['ffn_w_up'][i, j].astype(_BF16), prm['ffn_w_down'][i, j].astype(_BF16))

    xf = ffn(xf, 0, 0)
    head_mean = jnp.kron(jnp.eye(A_HEADS, dtype=_F32),
                         jnp.full((A_HEAD_DIM, A_HEAD_DIM), 1.0 / A_HEAD_DIM, _F32)).astype(_BF16)
    q_gain = (jnp.tile(prm['ev_q_norm'][0], A_HEADS) * (A_HEAD_DIM ** -0.5)).reshape(1, A_W)
    k_gain = jnp.tile(prm['ev_k_norm'][0], A_HEADS).reshape(1, A_W)
    w2 = jnp.zeros((V7X_LANES, B_KW), _F32).at[
        _EV_AB_OFF - 256:_EV_AB_OFF - 256 + B_GATE_RANK].set(prm['ev_gate_w2'][0])
    qn, k_new, v_new, idx, qb, kb, vb, rb, la = _even_in_proj(
        xf, prm['mix_norm'][0], _prep_even_w_in(prm['ev_w_in'][0]), q_gain, k_gain, head_mean,
        w2, prm['ev_gate_b2'][0].reshape(1, B_KW))
    r3 = lambda a: a.reshape(bsz, lq, a.shape[-1])
    ki_new = r3(idx)[:, :, 256:256 + IDX_DIM]
    if past is None:
        past_len = 0
        k_all, v_all, ki_all = r3(k_new), r3(v_new), ki_new
        s0_gla = jnp.zeros((bsz, B_HEADS, B_KEY_DIM, B_VAL_DIM), _F32)
    else:
        past_len = past['k'].shape[2]
        k_all = jnp.concatenate([past['k'][0].reshape(bsz, past_len, A_W), r3(k_new)], axis=1)
        v_all = jnp.concatenate([past['v'][0].reshape(bsz, past_len, A_W), r3(v_new)], axis=1)
        ki_all = jnp.concatenate([past['ki'][0], ki_new], axis=1)
        s0_gla = past['gla'][0]
    o_a = _dsa_attention(r3(qn), r3(idx), k_all, v_all, ki_all, prm['rel_bias'], past_len)
    o_b, s_gla = _gla(r3(qb), r3(kb), r3(la), r3(vb), r3(rb), s0_gla, prm['ev_gla_norm'][0])
    w_out = prm['ev_w_out'][0].astype(_BF16)
    xf = _out_proj(xf, [o_a.reshape(t, A_W), o_b.reshape(t, B_VW)], [w_out[:A_W], w_out[A_W:]])
    xf = ffn(xf, 0, 1)

    xf = ffn(xf, 1, 0)
    a_log = _lane_pad(prm['od_a_log'][0])
    dt_bias = _lane_pad(prm['od_dt_bias'][0])
    qkv, z, gb = _odd_in_proj(xf, prm['mix_norm'][1], _prep_odd_w_in(prm['od_w_in'][0]), a_log, dt_bias)
    if past is None:
        conv_prev = jnp.zeros((bsz, CONV_WIDTH - 1, CONV_CH), _F32)
        s0_delta = jnp.zeros((bsz, C_HEADS, C_HEAD_DIM, C_HEAD_DIM), _F32)
    else:
        conv_prev, s0_delta = past['conv'][0], past['delta'][0]
    conv_prev8 = jnp.pad(conv_prev, ((0, 0), (V7X_SUBLANES - (CONV_WIDTH - 1), 0), (0, 0)))
    o_c, s_delta = _delta(r3(qkv), r3(z), r3(gb), prm['od_conv_w'][0], conv_prev8, s0_delta,
                          prm['od_norm'][0])
    conv_new = jnp.concatenate([conv_prev, r3(qkv)], axis=1)[:, -(CONV_WIDTH - 1):]
    xf = _out_proj(xf, [o_c.reshape(t, C_W)], [prm['od_w_out'][0].astype(_BF16)])
    xf = ffn(xf, 1, 1)

    y = xf.reshape(bsz, lq, d)
    k_out = r3(k_new).reshape(1, bsz, lq, A_HEADS, A_HEAD_DIM)
    v_out = r3(v_new).reshape(1, bsz, lq, A_HEADS, A_HEAD_DIM)
    return y, k_out, v_out, ki_new[None], s_gla[None], s_delta[None], conv_new[None]


def kernel(x_prompt, x_sample, cache_attn_k, cache_attn_v, cache_idx_k, state_gla, state_delta, state_conv,
           ffn_norm, ffn_w_gate, ffn_w_up, ffn_w_down, mix_norm, ev_w_in, ev_q_norm, ev_k_norm, rel_bias,
           ev_gate_w2, ev_gate_b2, ev_gla_norm, ev_w_out, od_w_in, od_conv_w, od_a_log, od_dt_bias, od_norm,
           od_w_out):
    prm = {'ffn_norm': ffn_norm, 'ffn_w_gate': ffn_w_gate, 'ffn_w_up': ffn_w_up, 'ffn_w_down': ffn_w_down,
           'mix_norm': mix_norm, 'ev_w_in': ev_w_in, 'ev_q_norm': ev_q_norm, 'ev_k_norm': ev_k_norm,
           'rel_bias': rel_bias, 'ev_gate_w2': ev_gate_w2, 'ev_gate_b2': ev_gate_b2, 'ev_gla_norm': ev_gla_norm,
           'ev_w_out': ev_w_out, 'od_w_in': od_w_in, 'od_conv_w': od_conv_w, 'od_a_log': od_a_log,
           'od_dt_bias': od_dt_bias, 'od_norm': od_norm, 'od_w_out': od_w_out}
    past = {'k': cache_attn_k, 'v': cache_attn_v, 'ki': cache_idx_k, 'gla': state_gla,
            'delta': state_delta, 'conv': state_conv}
    y_prompt, p_k, p_v, p_ki, p_gla, p_delta, p_conv = _trunk(x_prompt, None, prm)
    y_sample, s_k, s_v, s_ki, s_gla, s_delta, s_conv = _trunk(x_sample, past, prm)
    return (y_prompt, y_sample, p_k, p_v, p_ki, p_gla, p_delta, p_conv,
            s_k, s_v, s_ki, s_gla, s_delta, s_conv)
```

```python
import functools
import math

import jax
import jax.numpy as jnp
from jax import lax
from jax.experimental import pallas as pl
from jax.experimental.pallas import tpu as pltpu

D_MODEL = 1024
DEPTH = 2
CHUNK = 64
EPS = 1e-6

A_HEADS = 8
A_HEAD_DIM = 64
A_W = A_HEADS * A_HEAD_DIM
IDX_HEADS = 4
IDX_DIM = 64
TOPK_MAX = 256
N_BUCKETS = 32
MAX_DISTANCE = 128

B_HEADS = 4
B_KEY_DIM = 64
B_VAL_DIM = 128
B_KW = B_HEADS * B_KEY_DIM
B_VW = B_HEADS * B_VAL_DIM
B_GATE_RANK = 16
B_GATE_TAU = 16.0

C_HEADS = 8
C_HEAD_DIM = 128
C_W = C_HEADS * C_HEAD_DIM
CONV_WIDTH = 4
CONV_CH = 3 * C_W

D_FF = 2816

V7X_LANES = 128
V7X_SUBLANES = 8
V7X_VMEM_LIMIT_BYTES = 56 * 1024 * 1024

_F32 = jnp.float32
_BF16 = jnp.bfloat16
_HI = lax.Precision.HIGHEST
_NEG = -1e30
_NN = (((1,), (0,)), ((), ()))
_NT = (((1,), (1,)), ((), ()))
_TN = (((0,), (0,)), ((), ()))

_EV_Q, _EV_K, _EV_V = 0, 512, 1024
_EV_IDX = 1536
_EV_IDX_W = 384
_EV_WI_OFF = 256 + IDX_DIM
_EV_AB_OFF = _EV_WI_OFF + IDX_HEADS
_EV_QB, _EV_KB, _EV_VB, _EV_RB = 1920, 2176, 2432, 2944
_EV_COLS = 3456
_OD_QKV, _OD_Z, _OD_AB = 0, 3072, 4096
_OD_COLS = 4224


def _params(n_axes):
    return pltpu.CompilerParams(dimension_semantics=("arbitrary",) * n_axes,
                                vmem_limit_bytes=V7X_VMEM_LIMIT_BYTES)


def _rms(x, g):
    return x * lax.rsqrt(jnp.mean(x * x, axis=-1, keepdims=True) + EPS) * g


def _silu(x):
    return x * jax.nn.sigmoid(x)


def _dot(a, b, dims=_NN):
    return lax.dot_general(a.astype(_BF16), b.astype(_BF16), dims, preferred_element_type=_F32)


def _dot_hi(a, b, dims=_NN):
    return lax.dot_general(a, b, dims, precision=_HI, preferred_element_type=_F32)


def _full(shape):
    return pl.BlockSpec(shape, lambda *_: (0,) * len(shape))


def _ffn_kernel(x_ref, g_ref, wg_ref, wu_ref, wd_ref, o_ref, h_ref, acc_ref):
    f = pl.program_id(1)

    @pl.when(f == 0)
    def _():
        h_ref[...] = _rms(x_ref[...], g_ref[...]).astype(_BF16)
        acc_ref[...] = jnp.zeros_like(acc_ref)

    h = h_ref[...]
    gate = jnp.dot(h, wg_ref[...], preferred_element_type=_F32)
    up = jnp.dot(h, wu_ref[...], preferred_element_type=_F32)
    act = (_silu(gate) * up).astype(_BF16)
    acc_ref[...] += jnp.dot(act, wd_ref[...], preferred_element_type=_F32)

    @pl.when(f == pl.num_programs(1) - 1)
    def _():
        o_ref[...] = x_ref[...] + 0.5 * acc_ref[...]


def _ffn_half(x, g, wg, wu, wd):
    t, d = x.shape
    ff = wg.shape[1]
    tm = min(t, 1024)
    tf = 256
    return pl.pallas_call(
        _ffn_kernel,
        grid=(t // tm, ff // tf),
        in_specs=[pl.BlockSpec((tm, d), lambda i, f: (i, 0)),
                  pl.BlockSpec((1, d), lambda i, f: (0, 0)),
                  pl.BlockSpec((d, tf), lambda i, f: (0, f)),
                  pl.BlockSpec((d, tf), lambda i, f: (0, f)),
                  pl.BlockSpec((tf, d), lambda i, f: (f, 0))],
        out_specs=pl.BlockSpec((tm, d), lambda i, f: (i, 0)),
        out_shape=jax.ShapeDtypeStruct((t, d), _F32),
        scratch_shapes=[pltpu.VMEM((tm, d), _BF16), pltpu.VMEM((tm, d), _F32)],
        compiler_params=_params(2),
        name="ffn_half",
    )(x, g.reshape(1, d), wg, wu, wd)


def _even_in_kernel(x_ref, g_ref, w_ref, qg_ref, kg_ref, hm_ref, w2_ref, b2_ref,
                    qn_ref, k_ref, v_ref, idx_ref, qb_ref, kb_ref, vb_ref, rb_ref, la_ref):
    h = _rms(x_ref[...], g_ref[...]).astype(_BF16)

    def proj(lo, width):
        return jnp.dot(h, w_ref[:, lo:lo + width], preferred_element_type=_F32)

    def headnorm(t, gain):
        t2 = t * t
        hi = t2.astype(_BF16)
        lo = (t2 - hi.astype(_F32)).astype(_BF16)
        ms = (jnp.dot(hi, hm_ref[...], preferred_element_type=_F32)
              + jnp.dot(lo, hm_ref[...], preferred_element_type=_F32))
        return t * lax.rsqrt(ms + EPS) * gain

    qn_ref[...] = headnorm(proj(_EV_Q, A_W), qg_ref[...]).astype(_BF16)
    k_ref[...] = headnorm(proj(_EV_K, A_W), kg_ref[...])
    v_ref[...] = proj(_EV_V, A_W)
    idx = proj(_EV_IDX, _EV_IDX_W)
    idx_ref[...] = idx
    qb_ref[...] = proj(_EV_QB, B_KW) * (B_KEY_DIM ** -0.5)
    kb_ref[...] = proj(_EV_KB, B_KW)
    vb_ref[...] = proj(_EV_VB, B_VW)
    rb_ref[...] = proj(_EV_RB, B_VW)
    z = _dot_hi(idx[:, 256:384], w2_ref[...]) + b2_ref[...]
    la_ref[...] = (jnp.minimum(z, 0.0) - jnp.log1p(jnp.exp(-jnp.abs(z)))) * (1.0 / B_GATE_TAU)


def _even_in_proj(x, g, w, q_gain, k_gain, head_mean, w2, b2):
    t, d = x.shape
    tm = min(t, 256)
    row = lambda n: pl.BlockSpec((tm, n), lambda i: (i, 0))
    outs = [(A_W, _BF16), (A_W, _F32), (A_W, _F32), (_EV_IDX_W, _F32), (B_KW, _F32), (B_KW, _F32),
            (B_VW, _F32), (B_VW, _F32), (B_KW, _F32)]
    return pl.pallas_call(
        _even_in_kernel,
        grid=(t // tm,),
        in_specs=[row(d), _full((1, d)), _full((d, _EV_COLS)), _full((1, A_W)), _full((1, A_W)),
                  _full((A_W, A_W)), _full((V7X_LANES, B_KW)), _full((1, B_KW))],
        out_specs=[row(n) for n, _ in outs],
        out_shape=[jax.ShapeDtypeStruct((t, n), dt) for n, dt in outs],
        compiler_params=_params(1),
        name="even_in_proj",
    )(x, g.reshape(1, d), w, q_gain, k_gain, head_mean, w2, b2)


def _dsa_kernel(rb_ref, qt_ref, qit_ref, wit_ref, k_ref, vt_ref, ki_ref, bkt_ref, o_ref,
                key_ref, mask_ref, nb_ref, ki3_ref, acc_ref, lt_ref, p_ref, *, past, tq, kc, topk, r0_off):
    t = pl.program_id(1)

    @pl.when((pl.program_id(0) == 0) & (t == 0))
    def _():
        bk = bkt_ref[...]
        for h in range(A_HEADS):
            acc = jnp.zeros(bk.shape, _F32)
            for bb in range(N_BUCKETS):
                acc = jnp.where(bk == bb, rb_ref[bb, h], acc)
            nb_ref[h] = acc - rb_ref[N_BUCKETS // 2 - 1, h]

    q0 = past + t * tq
    n_chunks = (q0 + tq + kc - 1) // kc
    q_chunk = (q0 + lax.broadcasted_iota(jnp.int32, (1, tq), 1)) // CHUNK
    k_iota = lax.broadcasted_iota(jnp.int32, (kc, tq), 0)

    def admissible(r0):
        return ((r0 + k_iota) // CHUNK) <= q_chunk

    @pl.when(t == 0)
    def _():
        def split_body(j, carry):
            r0 = pl.multiple_of(j * kc, kc)
            x = ki_ref[0, pl.ds(r0, kc), :]
            hi = x.astype(_BF16)
            ki3_ref[pl.ds(r0, kc), 0:V7X_LANES] = hi
            ki3_ref[pl.ds(r0, kc), V7X_LANES:2 * V7X_LANES] = (x - hi.astype(_F32)).astype(_BF16)
            return carry
        lax.fori_loop(0, ki3_ref.shape[0] // kc, split_body, 0)

    qit = qit_ref[0]
    q_cols = []
    for h in range(IDX_HEADS):
        x = qit[h * IDX_DIM:(h + 1) * IDX_DIM, :]
        hi = x.astype(_BF16)
        lo = (x - hi.astype(_F32)).astype(_BF16)
        q_cols.append(jnp.concatenate([hi, lo, hi, jnp.zeros_like(hi)], axis=0))
    q3 = jnp.concatenate(q_cols, axis=1)
    wit = wit_ref[0]

    def score_body(j, carry):
        r0 = pl.multiple_of(j * kc, kc)
        d = jnp.dot(ki3_ref[pl.ds(r0, kc), :], q3, preferred_element_type=_F32)
        s = jnp.zeros((kc, tq), _F32)
        for h in range(IDX_HEADS):
            s = s + wit[h:h + 1, :] * jnp.maximum(d[:, h * tq:(h + 1) * tq], 0.0)
        s = jnp.where(s == 0.0, 0.0, s)
        s = jnp.where(admissible(r0), s, -jnp.inf)
        bits = pltpu.bitcast(s, jnp.int32)
        key_ref[pl.ds(r0, kc), :] = bits ^ ((bits >> 31) & jnp.int32(0x7FFFFFFF))
        return carry

    lax.fori_loop(0, n_chunks, score_body, 0)

    int_min = jnp.int32(-2 ** 31)

    def count_ge(cs):
        def body(j, acc):
            r0 = pl.multiple_of(j * kc, kc)
            m = jnp.where(key_ref[pl.ds(r0, kc), :] >= cs, 1, 0)
            return acc + m.reshape(kc // V7X_SUBLANES, V7X_SUBLANES, tq).sum(axis=0)
        acc = lax.fori_loop(0, n_chunks, body, jnp.zeros((V7X_SUBLANES, tq), jnp.int32))
        return acc.sum(axis=0, keepdims=True)

    def bit_body(i, tu):
        cu = tu | jnp.left_shift(jnp.int32(1), 31 - i)
        return jnp.where(count_ge(cu ^ int_min) >= topk, cu, tu)

    tu = lax.fori_loop(0, 32, bit_body, jnp.zeros((1, tq), jnp.int32))
    ts = tu ^ int_min
    n_greater = count_ge(ts + 1)
    room = (topk - n_greater).astype(_F32)

    tri = (lax.broadcasted_iota(jnp.int32, (kc, kc), 0)
           >= lax.broadcasted_iota(jnp.int32, (kc, kc), 1)).astype(_BF16)

    def mask_body(j, seen):
        r0 = pl.multiple_of(j * kc, kc)
        blk = key_ref[pl.ds(r0, kc), :]
        eq = blk == ts
        rank = jnp.dot(tri, jnp.where(eq, 1.0, 0.0).astype(_BF16), preferred_element_type=_F32) + seen
        val = jnp.where(blk > ts, 0.0, jnp.where(eq, jnp.where(rank <= room, 0.0, _NEG), _NEG))
        mask_ref[pl.ds(r0, kc), :] = jnp.where(admissible(r0), val, _NEG)
        return rank[kc - 1:kc, :]

    lax.fori_loop(0, n_chunks, mask_body, jnp.zeros((1, tq), _F32))

    upper_rows = lax.broadcasted_iota(jnp.int32, (V7X_LANES, tq), 0) >= A_HEAD_DIM
    q_pairs = []
    for pr in range(A_HEADS // 2):
        blk = qt_ref[0, pr * V7X_LANES:(pr + 1) * V7X_LANES, :].astype(_F32)
        q_pairs.append(jnp.concatenate([jnp.where(upper_rows, 0.0, blk), jnp.where(upper_rows, blk, 0.0)],
                                       axis=1).astype(_BF16))
    acc_ref[...] = jnp.zeros_like(acc_ref)

    def attn_body(j, carry):
        ms, ls = carry
        r0 = pl.multiple_of(j * kc, kc)
        msk = mask_ref[pl.ds(r0, kc), :]
        off = pl.multiple_of(jnp.maximum(r0 - q0 + r0_off, 0), CHUNK)
        for pr in range(A_HEADS // 2):
            kblk = k_ref[0, pl.ds(r0, kc), pr * V7X_LANES:(pr + 1) * V7X_LANES]
            lt2 = jnp.dot(kblk, q_pairs[pr], preferred_element_type=_F32)
            for hh in range(2):
                h = 2 * pr + hh
                lt_ref[h] = lt2[:, hh * tq:(hh + 1) * tq] + msk + nb_ref[h, pl.ds(off, kc), :]
        new_ms, new_ls, alphas = [], [], []
        for h in range(A_HEADS):
            lt = lt_ref[h]
            m_new = jnp.maximum(ms[h], lt.max(axis=0, keepdims=True))
            p = jnp.exp(lt - m_new)
            alpha = jnp.exp(ms[h] - m_new)
            new_ms.append(m_new)
            new_ls.append(alpha * ls[h] + p.sum(axis=0, keepdims=True))
            alphas.append(alpha)
            p_ref[h] = p.astype(_BF16)
        for h in range(A_HEADS):
            rows = slice(h * A_HEAD_DIM, (h + 1) * A_HEAD_DIM)
            pv = jnp.dot(vt_ref[0, rows, pl.ds(r0, kc)], p_ref[h], preferred_element_type=_F32)
            acc_ref[rows, :] = alphas[h] * acc_ref[rows, :] + pv
        return tuple(new_ms), tuple(new_ls)

    init = (tuple(jnp.full((1, tq), _NEG, _F32) for _ in range(A_HEADS)),
            tuple(jnp.zeros((1, tq), _F32) for _ in range(A_HEADS)))
    _, ls = lax.fori_loop(0, n_chunks, attn_body, init)

    eye = (lax.broadcasted_iota(jnp.int32, (tq, tq), 0)
           == lax.broadcasted_iota(jnp.int32, (tq, tq), 1)).astype(_BF16)
    for pair in range(A_HEADS // 2):
        lanes = slice(pair * V7X_LANES, (pair + 1) * V7X_LANES)
        o_t = jnp.concatenate(
            [acc_ref[(2 * pair + hh) * A_HEAD_DIM:(2 * pair + hh + 1) * A_HEAD_DIM, :] / ls[2 * pair + hh]
             for hh in range(2)], axis=0).astype(_BF16)
        o_ref[0, :, lanes] = lax.dot_general(eye, o_t, _NT, preferred_element_type=_F32).astype(_BF16)


def _t5_bucket_table(rel):
    half = N_BUCKETS // 2
    max_exact = half // 2
    n = jnp.abs(rel)
    nf = jnp.maximum(n, 1).astype(jnp.float32)
    large = max_exact + (jnp.log(nf / max_exact) / math.log(MAX_DISTANCE / max_exact)
                         * (half - max_exact)).astype(jnp.int32)
    large = jnp.minimum(large, half - 1)
    return jnp.where(rel > 0, half, 0) + jnp.where(n < max_exact, n, large)


def _dsa_attention(qn, idx, k_all, v_all, ki_all, rel_bias, past):
    bsz, lq, _ = qn.shape
    nk = k_all.shape[1]
    topk = min(TOPK_MAX, nk // 4)
    tq = min(V7X_LANES, lq)
    kc = 256
    nk_pad = -(-nk // kc) * kc
    pad = ((0, 0), (0, nk_pad - nk), (0, 0))
    k16 = jnp.pad(k_all.astype(_BF16), pad)
    vt16 = jnp.swapaxes(jnp.pad(v_all.astype(_BF16), pad), 1, 2)
    ki2 = jnp.pad(jnp.concatenate([ki_all, ki_all], axis=-1), pad)
    wit = jnp.swapaxes(idx[:, :, _EV_WI_OFF:_EV_WI_OFF + V7X_SUBLANES], 1, 2)
    qt = jnp.swapaxes(qn, 1, 2)
    qit = jnp.swapaxes(idx[:, :, :IDX_HEADS * IDX_DIM], 1, 2)
    r0_off = kc + MAX_DISTANCE
    nbr = r0_off + kc + tq
    rel = (jnp.arange(nbr, dtype=jnp.int32)[:, None] - r0_off) - jnp.arange(tq, dtype=jnp.int32)[None, :]
    bkt = _t5_bucket_table(rel)
    kern = functools.partial(_dsa_kernel, past=past, tq=tq, kc=kc, topk=topk, r0_off=r0_off)
    return pl.pallas_call(
        kern,
        grid=(bsz, lq // tq),
        in_specs=[pl.BlockSpec(memory_space=pltpu.SMEM),
                  pl.BlockSpec((1, A_W, tq), lambda b, t: (b, 0, t)),
                  pl.BlockSpec((1, IDX_HEADS * IDX_DIM, tq), lambda b, t: (b, 0, t)),
                  pl.BlockSpec((1, V7X_SUBLANES, tq), lambda b, t: (b, 0, t)),
                  pl.BlockSpec((1, nk_pad, A_W), lambda b, t: (b, 0, 0)),
                  pl.BlockSpec((1, A_W, nk_pad), lambda b, t: (b, 0, 0)),
                  pl.BlockSpec((1, nk_pad, V7X_LANES), lambda b, t: (b, 0, 0)),
                  pl.BlockSpec((nbr, tq), lambda b, t: (0, 0))],
        out_specs=pl.BlockSpec((1, tq, A_W), lambda b, t: (b, t, 0)),
        out_shape=jax.ShapeDtypeStruct((bsz, lq, A_W), _BF16),
        scratch_shapes=[pltpu.VMEM((nk_pad, tq), jnp.int32), pltpu.VMEM((nk_pad, tq), _F32),
                        pltpu.VMEM((A_HEADS, nbr, tq), _F32),
                        pltpu.VMEM((nk_pad, 2 * V7X_LANES), _BF16), pltpu.VMEM((A_W, tq), _F32),
                        pltpu.VMEM((A_HEADS, kc, tq), _F32), pltpu.VMEM((A_HEADS, kc, tq), _BF16)],
        compiler_params=_params(2),
        name="dsa_attention",
    )(rel_bias, qt, qit, wit, k16, vt16, ki2, bkt)


def _gla_kernel(q_ref, k_ref, g_ref, v_ref, r_ref, s0_ref, ng_ref, o_ref, s_out_ref, st_ref):
    c = pl.program_id(1)
    causal = (lax.broadcasted_iota(jnp.int32, (CHUNK, CHUNK), 0)
              >= lax.broadcasted_iota(jnp.int32, (CHUNK, CHUNK), 1))
    upper_half = lax.broadcasted_iota(jnp.int32, (CHUNK, V7X_LANES), 1) >= B_KEY_DIM
    eye = (lax.broadcasted_iota(jnp.int32, (V7X_LANES, V7X_LANES), 0)
           == lax.broadcasted_iota(jnp.int32, (V7X_LANES, V7X_LANES), 1)).astype(_F32)
    first_row = lax.broadcasted_iota(jnp.int32, (V7X_SUBLANES, V7X_LANES), 0) == 0

    @pl.when(c == 0)
    def _():
        st_ref[...] = s0_ref[0]

    b = _dot_hi(causal.astype(_F32), g_ref[0])
    mid = CHUNK // 2
    b_mid = b[mid:mid + 1, :]
    b_last = b[CHUNK - 1:CHUNK, :]
    q = q_ref[0]
    k = k_ref[0]
    qe = q * jnp.exp(b)
    qm = q * jnp.exp(b - b_mid)
    km = k * jnp.exp(b_mid - b)
    kl = k * jnp.exp(b_last - b)

    def head_half(x, h):
        xp = x[:, (h // 2) * V7X_LANES:(h // 2 + 1) * V7X_LANES]
        keep = upper_half if h % 2 == 1 else jnp.logical_not(upper_half)
        return jnp.where(keep, xp, 0.0)

    heads = range(B_HEADS)
    vs = [v_ref[0][:, h * B_VAL_DIM:(h + 1) * B_VAL_DIM] for h in heads]
    a = [jnp.where(causal, _dot(head_half(qm, h), km[:, (h // 2) * V7X_LANES:(h // 2 + 1) * V7X_LANES], _NT), 0.0)
         for h in heads]
    st = [st_ref[p * V7X_LANES:(p + 1) * V7X_LANES, :] for p in range(B_HEADS // 2)]
    o = [_dot(head_half(qe, h), st[h // 2]) + _dot(a[h], vs[h]) for h in heads]
    upd = [_dot(head_half(kl, h), vs[h], _TN) for h in heads]
    for p in range(B_HEADS // 2):
        row = jnp.where(first_row, b_last[:, p * V7X_LANES:(p + 1) * V7X_LANES], 0.0)
        col = _dot_hi(eye, row, _NT)[:, 0:1]
        st_ref[p * V7X_LANES:(p + 1) * V7X_LANES, :] = jnp.exp(col) * st[p] + upd[2 * p] + upd[2 * p + 1]
    for h in heads:
        cols = slice(h * B_VAL_DIM, (h + 1) * B_VAL_DIM)
        o_ref[0, :, cols] = (_rms(o[h], ng_ref[...]) * _silu(r_ref[0][:, cols])).astype(_BF16)

    @pl.when(c == pl.num_programs(1) - 1)
    def _():
        s_out_ref[0] = st_ref[...]


def _gla(qb, kb, la, vb, rb, s0, norm_g):
    bsz, lq, _ = qb.shape
    tok = lambda n: pl.BlockSpec((1, CHUNK, n), lambda b, c: (b, c, 0))
    state = pl.BlockSpec((1, B_KW, B_VAL_DIM), lambda b, c: (b, 0, 0))
    o, s_new = pl.pallas_call(
        _gla_kernel,
        grid=(bsz, lq // CHUNK),
        in_specs=[tok(B_KW), tok(B_KW), tok(B_KW), tok(B_VW), tok(B_VW), state,
                  pl.BlockSpec((1, B_VAL_DIM), lambda b, c: (0, 0))],
        out_specs=[tok(B_VW), state],
        out_shape=[jax.ShapeDtypeStruct((bsz, lq, B_VW), _BF16),
                   jax.ShapeDtypeStruct((bsz, B_KW, B_VAL_DIM), _F32)],
        scratch_shapes=[pltpu.VMEM((B_KW, B_VAL_DIM), _F32)],
        compiler_params=_params(2),
        name="gla",
    )(qb, kb, la, vb, rb, s0.reshape(bsz, B_KW, B_VAL_DIM), norm_g.reshape(1, B_VAL_DIM))
    return o, s_new.reshape(s0.shape)


def _out_proj_kernel(*refs):
    x_ref, o_ref = refs[0], refs[-1]
    n = (len(refs) - 2) // 2
    acc = x_ref[...]
    for i in range(n):
        acc = acc + jnp.dot(refs[1 + i][...], refs[1 + n + i][...], preferred_element_type=_F32)
    o_ref[...] = acc


def _out_proj(x, acts, weights):
    t, d = x.shape
    tm = min(t, 512)
    return pl.pallas_call(
        _out_proj_kernel,
        grid=(t // tm,),
        in_specs=([pl.BlockSpec((tm, d), lambda i: (i, 0))]
                  + [pl.BlockSpec((tm, a.shape[1]), lambda i: (i, 0)) for a in acts]
                  + [_full(w.shape) for w in weights]),
        out_specs=pl.BlockSpec((tm, d), lambda i: (i, 0)),
        out_shape=jax.ShapeDtypeStruct((t, d), _F32),
        compiler_params=_params(1),
        name="out_proj",
    )(x, *acts, *weights)


def _odd_in_kernel(x_ref, g_ref, w_ref, alog_ref, dt_ref, qkv_ref, z_ref, gb_ref):
    h = _rms(x_ref[...], g_ref[...]).astype(_BF16)
    qkv_ref[...] = jnp.dot(h, w_ref[:, _OD_QKV:_OD_QKV + CONV_CH], preferred_element_type=_F32)
    z_ref[...] = jnp.dot(h, w_ref[:, _OD_Z:_OD_Z + C_W], preferred_element_type=_F32)
    ab = jnp.dot(h, w_ref[:, _OD_AB:_OD_AB + V7X_LANES], preferred_element_type=_F32)
    xa = ab + dt_ref[...]
    g = -jnp.exp(alog_ref[...]) * (jnp.maximum(xa, 0.0) + jnp.log1p(jnp.exp(-jnp.abs(xa))))
    lane = lax.broadcasted_iota(jnp.int32, ab.shape, 1)
    gb_ref[...] = jnp.where(lane < C_HEADS, g, jax.nn.sigmoid(ab))


def _odd_in_proj(x, g, w, a_log, dt_bias):
    t, d = x.shape
    tm = min(t, 256)
    row = lambda n: pl.BlockSpec((tm, n), lambda i: (i, 0))
    outs = [CONV_CH, C_W, V7X_LANES]
    return pl.pallas_call(
        _odd_in_kernel,
        grid=(t // tm,),
        in_specs=[row(d), _full((1, d)), _full((d, _OD_COLS)), _full((1, V7X_LANES)), _full((1, V7X_LANES))],
        out_specs=[row(n) for n in outs],
        out_shape=[jax.ShapeDtypeStruct((t, n), _F32) for n in outs],
        compiler_params=_params(1),
        name="odd_in_proj",
    )(x, g.reshape(1, d), w, a_log, dt_bias)


def _split_bf16(x):
    hi = x.astype(_BF16).astype(_F32)
    return hi, x - hi


def _delta_kernel(qkv_ref, z_ref, gb_ref, cw_ref, cprev_ref, s0_ref, ng_ref, o_ref, s_out_ref,
                  xbuf_ref, act_ref, st_ref):
    c = pl.program_id(1)
    halo = V7X_SUBLANES

    @pl.when(c == 0)
    def _():
        st_ref[...] = s0_ref[0]
        xbuf_ref[0:halo, :] = cprev_ref[0]

    xbuf_ref[halo:halo + CHUNK, :] = qkv_ref[0]
    conv = xbuf_ref[halo:halo + CHUNK, :] * cw_ref[CONV_WIDTH - 1:CONV_WIDTH, :]
    for j in range(CONV_WIDTH - 1):
        sh = CONV_WIDTH - 1 - j
        conv = conv + xbuf_ref[halo - sh:halo - sh + CHUNK, :] * cw_ref[j:j + 1, :]
    xbuf_ref[0:halo, :] = xbuf_ref[CHUNK:CHUNK + halo, :]
    act_ref[...] = _silu(conv)

    t_i = lax.broadcasted_iota(jnp.int32, (CHUNK, V7X_LANES), 0)
    s_i = lax.broadcasted_iota(jnp.int32, (CHUNK, V7X_LANES), 1)
    incl = t_i >= s_i
    strict = t_i > s_i
    lower_half = s_i < CHUNK
    eye = (t_i == s_i).astype(_F32)
    incl64 = (lax.broadcasted_iota(jnp.int32, (CHUNK, CHUNK), 0)
              >= lax.broadcasted_iota(jnp.int32, (CHUNK, CHUNK), 1)).astype(_F32)
    eye128 = (lax.broadcasted_iota(jnp.int32, (V7X_LANES, V7X_LANES), 0)
              == lax.broadcasted_iota(jnp.int32, (V7X_LANES, V7X_LANES), 1)).astype(_F32)
    zeros_blk = jnp.zeros((CHUNK, V7X_LANES), _F32)

    def mm3_lhs(a):
        a_hi, a_lo = _split_bf16(a)
        first = jnp.where(lower_half, a_hi, pltpu.roll(a_lo, CHUNK, 1))
        return jnp.concatenate([first, a_hi], axis=1).astype(_BF16)

    def mm3_rhs(b):
        b_hi, b_lo = _split_bf16(b)
        return jnp.concatenate([b_hi, b_hi, b_lo, jnp.zeros_like(b_lo)], axis=0).astype(_BF16)

    def mm(lhs, rhs):
        return jnp.dot(lhs, rhs, preferred_element_type=_F32)

    heads = range(C_HEADS)
    gb = gb_ref[0]
    gcum_all = _dot_hi(incl64, gb)
    gcum_rows = _dot_hi(eye128, jnp.concatenate([gcum_all, zeros_blk], axis=0), _NT)
    g_last = gcum_all[CHUNK - 1:CHUNK, :]

    qs, ks, kbs, decays, rhss, qgs, kdecs = [], [], [], [], [], [], []
    for h in heads:
        q = act_ref[:, h * C_HEAD_DIM:(h + 1) * C_HEAD_DIM]
        k = act_ref[:, C_W + h * C_HEAD_DIM:C_W + (h + 1) * C_HEAD_DIM]
        v = act_ref[:, 2 * C_W + h * C_HEAD_DIM:2 * C_W + (h + 1) * C_HEAD_DIM]
        q = q * lax.rsqrt(jnp.sum(q * q, axis=-1, keepdims=True) + EPS) * (C_HEAD_DIM ** -0.5)
        k = k * lax.rsqrt(jnp.sum(k * k, axis=-1, keepdims=True) + EPS)
        beta = gb[:, C_HEADS + h:C_HEADS + h + 1]
        gcum = gcum_all[:, h:h + 1]
        kb = k * beta
        qs.append(q)
        ks.append(k)
        kbs.append(kb)
        decays.append(jnp.exp(jnp.where(incl, gcum - gcum_rows[h:h + 1, :], _NEG)))
        rhss.append(jnp.concatenate([v * beta, kb * jnp.exp(gcum)], axis=-1))
        qgs.append(q * jnp.exp(gcum))
        kdecs.append(k * jnp.exp(g_last[:, h:h + 1] - gcum))

    lowers, attns = [], []
    for h in heads:
        prod = _dot(jnp.concatenate([kbs[h], qs[h]], axis=0),
                    jnp.concatenate([ks[h], zeros_blk], axis=0), _NT)
        lowers.append(jnp.where(strict, prod[:CHUNK] * decays[h], 0.0))
        attns.append(jnp.where(incl, prod[CHUNK:] * decays[h], 0.0))

    pws = [-lowers[h] for h in heads]
    invs = [eye + pws[h] for h in heads]
    pw_r = [mm3_rhs(pws[h]) for h in heads]
    for _ in range(5):
        pws = [mm(mm3_lhs(pws[h]), pw_r[h]) for h in heads]
        pw_r = [mm3_rhs(pws[h]) for h in heads]
        invs = [invs[h] + mm(mm3_lhs(invs[h]), pw_r[h]) for h in heads]
    sols = [mm(mm3_lhs(invs[h]), mm3_rhs(rhss[h])) for h in heads]

    for h in heads:
        st = st_ref[h]
        u = sols[h][:, :C_HEAD_DIM]
        w = sols[h][:, C_HEAD_DIM:]
        ws = _dot(jnp.concatenate([w, qgs[h]], axis=0), st)
        v_new = u - ws[:CHUNK]
        o = ws[CHUNK:] + _dot(attns[h][:, :CHUNK], v_new)
        st_ref[h] = jnp.exp(g_last[:, h:h + 1]) * st + _dot(kdecs[h], v_new, _TN)
        cols = slice(h * C_HEAD_DIM, (h + 1) * C_HEAD_DIM)
        o_ref[0, :, cols] = (_rms(o, ng_ref[...]) * _silu(z_ref[0][:, cols])).astype(_BF16)

    @pl.when(c == pl.num_programs(1) - 1)
    def _():
        s_out_ref[0] = st_ref[...]


def _delta(qkv, z, gb, conv_w, conv_prev8, s0, norm_g):
    bsz, lq, _ = qkv.shape
    tok = lambda n: pl.BlockSpec((1, CHUNK, n), lambda b, c: (b, c, 0))
    state = pl.BlockSpec((1, C_HEADS, C_HEAD_DIM, C_HEAD_DIM), lambda b, c: (b, 0, 0, 0))
    return pl.pallas_call(
        _delta_kernel,
        grid=(bsz, lq // CHUNK),
        in_specs=[tok(CONV_CH), tok(C_W), tok(V7X_LANES),
                  pl.BlockSpec((CONV_WIDTH, CONV_CH), lambda b, c: (0, 0)),
                  pl.BlockSpec((1, V7X_SUBLANES, CONV_CH), lambda b, c: (b, 0, 0)),
                  state,
                  pl.BlockSpec((1, C_HEAD_DIM), lambda b, c: (0, 0))],
        out_specs=[tok(C_W), state],
        out_shape=[jax.ShapeDtypeStruct((bsz, lq, C_W), _BF16),
                   jax.ShapeDtypeStruct(s0.shape, _F32)],
        scratch_shapes=[pltpu.VMEM((CHUNK + V7X_SUBLANES, CONV_CH), _F32),
                        pltpu.VMEM((CHUNK, CONV_CH), _F32),
                        pltpu.VMEM((C_HEADS, C_HEAD_DIM, C_HEAD_DIM), _F32)],
        compiler_params=_params(2),
        name="gated_delta",
    )(qkv, z, gb, conv_w, conv_prev8, s0, norm_g.reshape(1, C_HEAD_DIM))


def _prep_even_w_in(w):
    sizes = (A_W, A_W, A_W, IDX_HEADS * IDX_DIM, IDX_DIM, IDX_HEADS, B_KW, B_KW, B_VW, B_VW, B_GATE_RANK)
    offs = [0]
    for s in sizes:
        offs.append(offs[-1] + s)
    part = lambda i: w[:, offs[i]:offs[i + 1]]
    q, k, v, qi, ki, wi, qb, kb, vb, rb, ab = [part(i) for i in range(len(sizes))]
    pad = jnp.zeros((w.shape[0], _EV_IDX_W - (256 + IDX_DIM + IDX_HEADS + B_GATE_RANK)), w.dtype)
    return jnp.concatenate([q, k, v, qi, ki, wi, ab, pad, qb, kb, vb, rb], axis=1).astype(_BF16)


def _prep_odd_w_in(w):
    pad = jnp.zeros((w.shape[0], _OD_COLS - w.shape[1]), w.dtype)
    return jnp.concatenate([w, pad], axis=1).astype(_BF16)


def _lane_pad(v, n=V7X_LANES):
    return jnp.pad(v, (0, n - v.shape[0])).reshape(1, n)


def _trunk(x, past, prm):
    bsz, lq, d = x.shape
    t = bsz * lq
    xf = x.reshape(t, d)

    def ffn(xf, i, j):
        return _ffn_half(xf, prm['ffn_norm'][i, j], prm['ffn_w_gate'][i, j].astype(_BF16),
                         prm['ffn_w_up'][i, j].astype(_BF16), prm['ffn_w_down'][i, j].astype(_BF16))

    xf = ffn(xf, 0, 0)
    head_mean = jnp.kron(jnp.eye(A_HEADS, dtype=_F32),
                         jnp.full((A_HEAD_DIM, A_HEAD_DIM), 1.0 / A_HEAD_DIM, _F32)).astype(_BF16)
    q_gain = (jnp.tile(prm['ev_q_norm'][0], A_HEADS) * (A_HEAD_DIM ** -0.5)).reshape(1, A_W)
    k_gain = jnp.tile(prm['ev_k_norm'][0], A_HEADS).reshape(1, A_W)
    w2 = jnp.zeros((V7X_LANES, B_KW), _F32).at[
        _EV_AB_OFF - 256:_EV_AB_OFF - 256 + B_GATE_RANK].set(prm['ev_gate_w2'][0])
    qn, k_new, v_new, idx, qb, kb, vb, rb, la = _even_in_proj(
        xf, prm['mix_norm'][0], _prep_even_w_in(prm['ev_w_in'][0]), q_gain, k_gain, head_mean,
        w2, prm['ev_gate_b2'][0].reshape(1, B_KW))
    r3 = lambda a: a.reshape(bsz, lq, a.shape[-1])
    ki_new = r3(idx)[:, :, 256:256 + IDX_DIM]
    if past is None:
        past_len = 0
        k_all, v_all, ki_all = r3(k_new), r3(v_new), ki_new
        s0_gla = jnp.zeros((bsz, B_HEADS, B_KEY_DIM, B_VAL_DIM), _F32)
    else:
        past_len = past['k'].shape[2]
        k_all = jnp.concatenate([past['k'][0].reshape(bsz, past_len, A_W), r3(k_new)], axis=1)
        v_all = jnp.concatenate([past['v'][0].reshape(bsz, past_len, A_W), r3(v_new)], axis=1)
        ki_all = jnp.concatenate([past['ki'][0], ki_new], axis=1)
        s0_gla = past['gla'][0]
    o_a = _dsa_attention(r3(qn), r3(idx), k_all, v_all, ki_all, prm['rel_bias'], past_len)
    o_b, s_gla = _gla(r3(qb), r3(kb), r3(la), r3(vb), r3(rb), s0_gla, prm['ev_gla_norm'][0])
    w_out = prm['ev_w_out'][0].astype(_BF16)
    xf = _out_proj(xf, [o_a.reshape(t, A_W), o_b.reshape(t, B_VW)], [w_out[:A_W], w_out[A_W:]])
    xf = ffn(xf, 0, 1)

    xf = ffn(xf, 1, 0)
    a_log = _lane_pad(prm['od_a_log'][0])
    dt_bias = _lane_pad(prm['od_dt_bias'][0])
    qkv, z, gb = _odd_in_proj(xf, prm['mix_norm'][1], _prep_odd_w_in(prm['od_w_in'][0]), a_log, dt_bias)
    if past is None:
        conv_prev = jnp.zeros((bsz, CONV_WIDTH - 1, CONV_CH), _F32)
        s0_delta = jnp.zeros((bsz, C_HEADS, C_HEAD_DIM, C_HEAD_DIM), _F32)
    else:
        conv_prev, s0_delta = past['conv'][0], past['delta'][0]
    conv_prev8 = jnp.pad(conv_prev, ((0, 0), (V7X_SUBLANES - (CONV_WIDTH - 1), 0), (0, 0)))
    o_c, s_delta = _delta(r3(qkv), r3(z), r3(gb), prm['od_conv_w'][0], conv_prev8, s0_delta,
                          prm['od_norm'][0])
    conv_new = jnp.concatenate([conv_prev, r3(qkv)], axis=1)[:, -(CONV_WIDTH - 1):]
    xf = _out_proj(xf, [o_c.reshape(t, C_W)], [prm['od_w_out'][0].astype(_BF16)])
    xf = ffn(xf, 1, 1)

    y = xf.reshape(bsz, lq, d)
    k_out = r3(k_new).reshape(1, bsz, lq, A_HEADS, A_HEAD_DIM)
    v_out = r3(v_new).reshape(1, bsz, lq, A_HEADS, A_HEAD_DIM)
    return y, k_out, v_out, ki_new[None], s_gla[None], s_delta[None], conv_new[None]


def kernel(x_prompt, x_sample, cache_attn_k, cache_attn_v, cache_idx_k, state_gla, state_delta, state_conv,
           ffn_norm, ffn_w_gate, ffn_w_up, ffn_w_down, mix_norm, ev_w_in, ev_q_norm, ev_k_norm, rel_bias,
           ev_gate_w2, ev_gate_b2, ev_gla_norm, ev_w_out, od_w_in, od_conv_w, od_a_log, od_dt_bias, od_norm,
           od_w_out):
    prm = {'ffn_norm': ffn_norm, 'ffn_w_gate': ffn_w_gate, 'ffn_w_up': ffn_w_up, 'ffn_w_down': ffn_w_down,
           'mix_norm': mix_norm, 'ev_w_in': ev_w_in, 'ev_q_norm': ev_q_norm, 'ev_k_norm': ev_k_norm,
           'rel_bias': rel_bias, 'ev_gate_w2': ev_gate_w2, 'ev_gate_b2': ev_gate_b2, 'ev_gla_norm': ev_gla_norm,
           'ev_w_out': ev_w_out, 'od_w_in': od_w_in, 'od_conv_w': od_conv_w, 'od_a_log': od_a_log,
           'od_dt_bias': od_dt_bias, 'od_norm': od_norm, 'od_w_out': od_w_out}
    past = {'k': cache_attn_k, 'v': cache_attn_v, 'ki': cache_idx_k, 'gla': state_gla,
            'delta': state_delta, 'conv': state_conv}
    y_prompt, p_k, p_v, p_ki, p_gla, p_delta, p_conv = _trunk(x_prompt, None, prm)
    y_sample, s_k, s_v, s_ki, s_gla, s_delta, s_conv = _trunk(x_sample, past, prm)
    return (y_prompt, y_sample, p_k, p_v, p_ki, p_gla, p_delta, p_conv,
            s_k, s_v, s_ki, s_gla, s_delta, s_conv)
```

```python
import functools
import math

import jax
import jax.numpy as jnp
from jax import lax
from jax.experimental import pallas as pl
from jax.experimental.pallas import tpu as pltpu

D_MODEL = 1024
DEPTH = 2
CHUNK = 64
EPS = 1e-6

A_HEADS = 8
A_HEAD_DIM = 64
A_W = A_HEADS * A_HEAD_DIM
IDX_HEADS = 4
IDX_DIM = 64
TOPK_MAX = 256
N_BUCKETS = 32
MAX_DISTANCE = 128

B_HEADS = 4
B_KEY_DIM = 64
B_VAL_DIM = 128
B_KW = B_HEADS * B_KEY_DIM
B_VW = B_HEADS * B_VAL_DIM
B_GATE_RANK = 16
B_GATE_TAU = 16.0

C_HEADS = 8
C_HEAD_DIM = 128
C_W = C_HEADS * C_HEAD_DIM
CONV_WIDTH = 4
CONV_CH = 3 * C_W

D_FF = 2816

V7X_LANES = 128
V7X_SUBLANES = 8
V7X_VMEM_LIMIT_BYTES = 56 * 1024 * 1024

_F32 = jnp.float32
_BF16 = jnp.bfloat16
_HI = lax.Precision.HIGHEST
_NEG = -1e30
_NN = (((1,), (0,)), ((), ()))
_NT = (((1,), (1,)), ((), ()))
_TN = (((0,), (0,)), ((), ()))

_EV_Q, _EV_K, _EV_V = 0, 512, 1024
_EV_IDX = 1536
_EV_IDX_W = 384
_EV_WI_OFF = 256 + IDX_DIM
_EV_AB_OFF = _EV_WI_OFF + IDX_HEADS
_EV_QB, _EV_KB, _EV_VB, _EV_RB = 1920, 2176, 2432, 2944
_EV_COLS = 3456
_OD_QKV, _OD_Z, _OD_AB = 0, 3072, 4096
_OD_COLS = 4224


def _params(n_axes):
    return pltpu.CompilerParams(dimension_semantics=("arbitrary",) * n_axes,
                                vmem_limit_bytes=V7X_VMEM_LIMIT_BYTES)


def _rms(x, g):
    return x * lax.rsqrt(jnp.mean(x * x, axis=-1, keepdims=True) + EPS) * g


def _silu(x):
    return x * jax.nn.sigmoid(x)


def _dot(a, b, dims=_NN):
    return lax.dot_general(a.astype(_BF16), b.astype(_BF16), dims, preferred_element_type=_F32)


def _dot_hi(a, b, dims=_NN):
    return lax.dot_general(a, b, dims, precision=_HI, preferred_element_type=_F32)


def _full(shape):
    return pl.BlockSpec(shape, lambda *_: (0,) * len(shape))


_FFN_COLS = 256


def _ffn_kernel(x_ref, g_ref, wg_ref, wu_ref, wd_ref, o_ref, act_ref):
    x = x_ref[...]
    h = _rms(x, g_ref[...]).astype(_BF16)
    for c in range(wg_ref.shape[1] // _FFN_COLS):
        cols = slice(c * _FFN_COLS, (c + 1) * _FFN_COLS)
        gate = jnp.dot(h, wg_ref[:, cols], preferred_element_type=_F32)
        up = jnp.dot(h, wu_ref[:, cols], preferred_element_type=_F32)
        act_ref[:, cols] = (_silu(gate) * up).astype(_BF16)
    o_ref[...] = x + 0.5 * jnp.dot(act_ref[...], wd_ref[...], preferred_element_type=_F32)


def _ffn_half(x, g, wg, wu, wd):
    t, d = x.shape
    ff = wg.shape[1]
    tm = min(t, 512)
    resident = lambda shape: pl.BlockSpec(shape, lambda i: (0, 0), pipeline_mode=pl.Buffered(1))
    return pl.pallas_call(
        _ffn_kernel,
        grid=(t // tm,),
        in_specs=[pl.BlockSpec((tm, d), lambda i: (i, 0)),
                  resident((1, d)), resident((d, ff)), resident((d, ff)), resident((ff, d))],
        out_specs=pl.BlockSpec((tm, d), lambda i: (i, 0)),
        out_shape=jax.ShapeDtypeStruct((t, d), _F32),
        scratch_shapes=[pltpu.VMEM((tm, ff), _BF16)],
        compiler_params=_params(1),
        name="ffn_half",
    )(x, g.reshape(1, d), wg, wu, wd)


def _even_in_kernel(x_ref, g_ref, w_ref, qg_ref, kg_ref, hm_ref, w2_ref, b2_ref,
                    qn_ref, k_ref, v_ref, idx_ref, qb_ref, kb_ref, vb_ref, rb_ref, la_ref):
    h = _rms(x_ref[...], g_ref[...]).astype(_BF16)

    def proj(lo, width):
        return jnp.dot(h, w_ref[:, lo:lo + width], preferred_element_type=_F32)

    def headnorm(t, gain):
        t2 = t * t
        hi = t2.astype(_BF16)
        lo = (t2 - hi.astype(_F32)).astype(_BF16)
        ms = (jnp.dot(hi, hm_ref[...], preferred_element_type=_F32)
              + jnp.dot(lo, hm_ref[...], preferred_element_type=_F32))
        return t * lax.rsqrt(ms + EPS) * gain

    qn_ref[...] = headnorm(proj(_EV_Q, A_W), qg_ref[...]).astype(_BF16)
    k_ref[...] = headnorm(proj(_EV_K, A_W), kg_ref[...])
    v_ref[...] = proj(_EV_V, A_W)
    idx = proj(_EV_IDX, _EV_IDX_W)
    idx_ref[...] = idx
    qb_ref[...] = proj(_EV_QB, B_KW) * (B_KEY_DIM ** -0.5)
    kb_ref[...] = proj(_EV_KB, B_KW)
    vb_ref[...] = proj(_EV_VB, B_VW)
    rb_ref[...] = proj(_EV_RB, B_VW)
    z = _dot_hi(idx[:, 256:384], w2_ref[...]) + b2_ref[...]
    la_ref[...] = (jnp.minimum(z, 0.0) - jnp.log1p(jnp.exp(-jnp.abs(z)))) * (1.0 / B_GATE_TAU)


def _even_in_proj(x, g, w, q_gain, k_gain, head_mean, w2, b2):
    t, d = x.shape
    tm = min(t, 256)
    row = lambda n: pl.BlockSpec((tm, n), lambda i: (i, 0))
    outs = [(A_W, _BF16), (A_W, _F32), (A_W, _F32), (_EV_IDX_W, _F32), (B_KW, _F32), (B_KW, _F32),
            (B_VW, _F32), (B_VW, _F32), (B_KW, _F32)]
    return pl.pallas_call(
        _even_in_kernel,
        grid=(t // tm,),
        in_specs=[row(d), _full((1, d)), _full((d, _EV_COLS)), _full((1, A_W)), _full((1, A_W)),
                  _full((A_W, A_W)), _full((V7X_LANES, B_KW)), _full((1, B_KW))],
        out_specs=[row(n) for n, _ in outs],
        out_shape=[jax.ShapeDtypeStruct((t, n), dt) for n, dt in outs],
        compiler_params=_params(1),
        name="even_in_proj",
    )(x, g.reshape(1, d), w, q_gain, k_gain, head_mean, w2, b2)


def _dsa_kernel(rb_ref, qt_ref, qit_ref, wit_ref, k_ref, vt_ref, ki_ref, bkt_ref, o_ref,
                key_ref, mask_ref, nb_ref, ki3_ref, acc_ref, lt_ref, p_ref, *, past, tq, kc, topk, r0_off):
    t = pl.program_id(1)

    @pl.when((pl.program_id(0) == 0) & (t == 0))
    def _():
        bk = bkt_ref[...]
        for h in range(A_HEADS):
            acc = jnp.zeros(bk.shape, _F32)
            for bb in range(N_BUCKETS):
                acc = jnp.where(bk == bb, rb_ref[bb, h], acc)
            nb_ref[h] = acc - rb_ref[N_BUCKETS // 2 - 1, h]

    q0 = past + t * tq
    n_chunks = (q0 + tq + kc - 1) // kc
    q_chunk = (q0 + lax.broadcasted_iota(jnp.int32, (1, tq), 1)) // CHUNK
    k_iota = lax.broadcasted_iota(jnp.int32, (kc, tq), 0)

    def admissible(r0):
        return ((r0 + k_iota) // CHUNK) <= q_chunk

    @pl.when(t == 0)
    def _():
        def split_body(j, carry):
            r0 = pl.multiple_of(j * kc, kc)
            x = ki_ref[0, pl.ds(r0, kc), :]
            hi = x.astype(_BF16)
            ki3_ref[pl.ds(r0, kc), 0:V7X_LANES] = hi
            ki3_ref[pl.ds(r0, kc), V7X_LANES:2 * V7X_LANES] = (x - hi.astype(_F32)).astype(_BF16)
            return carry
        lax.fori_loop(0, ki3_ref.shape[0] // kc, split_body, 0)

    qit = qit_ref[0]
    q_cols = []
    for h in range(IDX_HEADS):
        x = qit[h * IDX_DIM:(h + 1) * IDX_DIM, :]
        hi = x.astype(_BF16)
        lo = (x - hi.astype(_F32)).astype(_BF16)
        q_cols.append(jnp.concatenate([hi, lo, hi, jnp.zeros_like(hi)], axis=0))
    q3 = jnp.concatenate(q_cols, axis=1)
    wit = wit_ref[0]

    def score_body(j, carry):
        r0 = pl.multiple_of(j * kc, kc)
        d = jnp.dot(ki3_ref[pl.ds(r0, kc), :], q3, preferred_element_type=_F32)
        s = jnp.zeros((kc, tq), _F32)
        for h in range(IDX_HEADS):
            s = s + wit[h:h + 1, :] * jnp.maximum(d[:, h * tq:(h + 1) * tq], 0.0)
        s = jnp.where(s == 0.0, 0.0, s)
        s = jnp.where(admissible(r0), s, -jnp.inf)
        bits = pltpu.bitcast(s, jnp.int32)
        key_ref[pl.ds(r0, kc), :] = bits ^ ((bits >> 31) & jnp.int32(0x7FFFFFFF))
        return carry

    lax.fori_loop(0, n_chunks, score_body, 0)

    int_min = jnp.int32(-2 ** 31)

    def count_ge(cs):
        def body(j, acc):
            r0 = pl.multiple_of(j * kc, kc)
            m = jnp.where(key_ref[pl.ds(r0, kc), :] >= cs, 1, 0)
            return acc + m.reshape(kc // V7X_SUBLANES, V7X_SUBLANES, tq).sum(axis=0)
        acc = lax.fori_loop(0, n_chunks, body, jnp.zeros((V7X_SUBLANES, tq), jnp.int32))
        return acc.sum(axis=0, keepdims=True)

    def bit_body(i, tu):
        cu = tu | jnp.left_shift(jnp.int32(1), 31 - i)
        return jnp.where(count_ge(cu ^ int_min) >= topk, cu, tu)

    tu = lax.fori_loop(0, 32, bit_body, jnp.zeros((1, tq), jnp.int32))
    ts = tu ^ int_min
    n_greater = count_ge(ts + 1)
    room = (topk - n_greater).astype(_F32)

    tri = (lax.broadcasted_iota(jnp.int32, (kc, kc), 0)
           >= lax.broadcasted_iota(jnp.int32, (kc, kc), 1)).astype(_BF16)

    def mask_body(j, seen):
        r0 = pl.multiple_of(j * kc, kc)
        blk = key_ref[pl.ds(r0, kc), :]
        eq = blk == ts
        rank = jnp.dot(tri, jnp.where(eq, 1.0, 0.0).astype(_BF16), preferred_element_type=_F32) + seen
        val = jnp.where(blk > ts, 0.0, jnp.where(eq, jnp.where(rank <= room, 0.0, _NEG), _NEG))
        mask_ref[pl.ds(r0, kc), :] = jnp.where(admissible(r0), val, _NEG)
        return rank[kc - 1:kc, :]

    lax.fori_loop(0, n_chunks, mask_body, jnp.zeros((1, tq), _F32))

    upper_rows = lax.broadcasted_iota(jnp.int32, (V7X_LANES, tq), 0) >= A_HEAD_DIM
    q_pairs = []
    for pr in range(A_HEADS // 2):
        blk = qt_ref[0, pr * V7X_LANES:(pr + 1) * V7X_LANES, :].astype(_F32)
        q_pairs.append(jnp.concatenate([jnp.where(upper_rows, 0.0, blk), jnp.where(upper_rows, blk, 0.0)],
                                       axis=1).astype(_BF16))
    acc_ref[...] = jnp.zeros_like(acc_ref)

    def attn_body(j, carry):
        ms, ls = carry
        r0 = pl.multiple_of(j * kc, kc)
        msk = mask_ref[pl.ds(r0, kc), :]
        off = pl.multiple_of(jnp.maximum(r0 - q0 + r0_off, 0), CHUNK)
        for pr in range(A_HEADS // 2):
            kblk = k_ref[0, pl.ds(r0, kc), pr * V7X_LANES:(pr + 1) * V7X_LANES]
            lt2 = jnp.dot(kblk, q_pairs[pr], preferred_element_type=_F32)
            for hh in range(2):
                h = 2 * pr + hh
                lt_ref[h] = lt2[:, hh * tq:(hh + 1) * tq] + msk + nb_ref[h, pl.ds(off, kc), :]
        new_ms, new_ls, alphas = [], [], []
        for h in range(A_HEADS):
            lt = lt_ref[h]
            m_new = jnp.maximum(ms[h], lt.max(axis=0, keepdims=True))
            p = jnp.exp(lt - m_new)
            alpha = jnp.exp(ms[h] - m_new)
            new_ms.append(m_new)
            new_ls.append(alpha * ls[h] + p.sum(axis=0, keepdims=True))
            alphas.append(alpha)
            p_ref[h] = p.astype(_BF16)
        for h in range(A_HEADS):
            rows = slice(h * A_HEAD_DIM, (h + 1) * A_HEAD_DIM)
            pv = jnp.dot(vt_ref[0, rows, pl.ds(r0, kc)], p_ref[h], preferred_element_type=_F32)
            acc_ref[rows, :] = alphas[h] * acc_ref[rows, :] + pv
        return tuple(new_ms), tuple(new_ls)

    init = (tuple(jnp.full((1, tq), _NEG, _F32) for _ in range(A_HEADS)),
            tuple(jnp.zeros((1, tq), _F32) for _ in range(A_HEADS)))
    _, ls = lax.fori_loop(0, n_chunks, attn_body, init)

    eye = (lax.broadcasted_iota(jnp.int32, (tq, tq), 0)
           == lax.broadcasted_iota(jnp.int32, (tq, tq), 1)).astype(_BF16)
    for pair in range(A_HEADS // 2):
        lanes = slice(pair * V7X_LANES, (pair + 1) * V7X_LANES)
        o_t = jnp.concatenate(
            [acc_ref[(2 * pair + hh) * A_HEAD_DIM:(2 * pair + hh + 1) * A_HEAD_DIM, :] / ls[2 * pair + hh]
             for hh in range(2)], axis=0).astype(_BF16)
        o_ref[0, :, lanes] = lax.dot_general(eye, o_t, _NT, preferred_element_type=_F32).astype(_BF16)


def _t5_bucket_table(rel):
    half = N_BUCKETS // 2
    max_exact = half // 2
    n = jnp.abs(rel)
    nf = jnp.maximum(n, 1).astype(jnp.float32)
    large = max_exact + (jnp.log(nf / max_exact) / math.log(MAX_DISTANCE / max_exact)
                         * (half - max_exact)).astype(jnp.int32)
    large = jnp.minimum(large, half - 1)
    return jnp.where(rel > 0, half, 0) + jnp.where(n < max_exact, n, large)


def _dsa_attention(qn, idx, k_all, v_all, ki_all, rel_bias, past):
    bsz, lq, _ = qn.shape
    nk = k_all.shape[1]
    topk = min(TOPK_MAX, nk // 4)
    tq = min(V7X_LANES, lq)
    kc = 256
    nk_pad = -(-nk // kc) * kc
    pad = ((0, 0), (0, nk_pad - nk), (0, 0))
    k16 = jnp.pad(k_all.astype(_BF16), pad)
    vt16 = jnp.swapaxes(jnp.pad(v_all.astype(_BF16), pad), 1, 2)
    ki2 = jnp.pad(jnp.concatenate([ki_all, ki_all], axis=-1), pad)
    wit = jnp.swapaxes(idx[:, :, _EV_WI_OFF:_EV_WI_OFF + V7X_SUBLANES], 1, 2)
    qt = jnp.swapaxes(qn, 1, 2)
    qit = jnp.swapaxes(idx[:, :, :IDX_HEADS * IDX_DIM], 1, 2)
    r0_off = kc + MAX_DISTANCE
    nbr = r0_off + kc + tq
    rel = (jnp.arange(nbr, dtype=jnp.int32)[:, None] - r0_off) - jnp.arange(tq, dtype=jnp.int32)[None, :]
    bkt = _t5_bucket_table(rel)
    kern = functools.partial(_dsa_kernel, past=past, tq=tq, kc=kc, topk=topk, r0_off=r0_off)
    return pl.pallas_call(
        kern,
        grid=(bsz, lq // tq),
        in_specs=[pl.BlockSpec(memory_space=pltpu.SMEM),
                  pl.BlockSpec((1, A_W, tq), lambda b, t: (b, 0, t)),
                  pl.BlockSpec((1, IDX_HEADS * IDX_DIM, tq), lambda b, t: (b, 0, t)),
                  pl.BlockSpec((1, V7X_SUBLANES, tq), lambda b, t: (b, 0, t)),
                  pl.BlockSpec((1, nk_pad, A_W), lambda b, t: (b, 0, 0)),
                  pl.BlockSpec((1, A_W, nk_pad), lambda b, t: (b, 0, 0)),
                  pl.BlockSpec((1, nk_pad, V7X_LANES), lambda b, t: (b, 0, 0)),
                  pl.BlockSpec((nbr, tq), lambda b, t: (0, 0))],
        out_specs=pl.BlockSpec((1, tq, A_W), lambda b, t: (b, t, 0)),
        out_shape=jax.ShapeDtypeStruct((bsz, lq, A_W), _BF16),
        scratch_shapes=[pltpu.VMEM((nk_pad, tq), jnp.int32), pltpu.VMEM((nk_pad, tq), _F32),
                        pltpu.VMEM((A_HEADS, nbr, tq), _F32),
                        pltpu.VMEM((nk_pad, 2 * V7X_LANES), _BF16), pltpu.VMEM((A_W, tq), _F32),
                        pltpu.VMEM((A_HEADS, kc, tq), _F32), pltpu.VMEM((A_HEADS, kc, tq), _BF16)],
        compiler_params=_params(2),
        name="dsa_attention",
    )(rel_bias, qt, qit, wit, k16, vt16, ki2, bkt)


def _gla_kernel(q_ref, k_ref, g_ref, v_ref, r_ref, s0_ref, ng_ref, o_ref, s_out_ref, st_ref):
    c = pl.program_id(1)
    causal = (lax.broadcasted_iota(jnp.int32, (CHUNK, CHUNK), 0)
              >= lax.broadcasted_iota(jnp.int32, (CHUNK, CHUNK), 1))
    upper_half = lax.broadcasted_iota(jnp.int32, (CHUNK, V7X_LANES), 1) >= B_KEY_DIM
    eye = (lax.broadcasted_iota(jnp.int32, (V7X_LANES, V7X_LANES), 0)
           == lax.broadcasted_iota(jnp.int32, (V7X_LANES, V7X_LANES), 1)).astype(_F32)
    first_row = lax.broadcasted_iota(jnp.int32, (V7X_SUBLANES, V7X_LANES), 0) == 0

    @pl.when(c == 0)
    def _():
        st_ref[...] = s0_ref[0]

    b = _dot_hi(causal.astype(_F32), g_ref[0])
    mid = CHUNK // 2
    b_mid = b[mid:mid + 1, :]
    b_last = b[CHUNK - 1:CHUNK, :]
    q = q_ref[0]
    k = k_ref[0]
    qe = q * jnp.exp(b)
    qm = q * jnp.exp(b - b_mid)
    km = k * jnp.exp(b_mid - b)
    kl = k * jnp.exp(b_last - b)

    def head_half(x, h):
        xp = x[:, (h // 2) * V7X_LANES:(h // 2 + 1) * V7X_LANES]
        keep = upper_half if h % 2 == 1 else jnp.logical_not(upper_half)
        return jnp.where(keep, xp, 0.0)

    heads = range(B_HEADS)
    vs = [v_ref[0][:, h * B_VAL_DIM:(h + 1) * B_VAL_DIM] for h in heads]
    a = [jnp.where(causal, _dot(head_half(qm, h), km[:, (h // 2) * V7X_LANES:(h // 2 + 1) * V7X_LANES], _NT), 0.0)
         for h in heads]
    st = [st_ref[p * V7X_LANES:(p + 1) * V7X_LANES, :] for p in range(B_HEADS // 2)]
    o = [_dot(head_half(qe, h), st[h // 2]) + _dot(a[h], vs[h]) for h in heads]
    upd = [_dot(head_half(kl, h), vs[h], _TN) for h in heads]
    for p in range(B_HEADS // 2):
        row = jnp.where(first_row, b_last[:, p * V7X_LANES:(p + 1) * V7X_LANES], 0.0)
        col = _dot_hi(eye, row, _NT)[:, 0:1]
        st_ref[p * V7X_LANES:(p + 1) * V7X_LANES, :] = jnp.exp(col) * st[p] + upd[2 * p] + upd[2 * p + 1]
    for h in heads:
        cols = slice(h * B_VAL_DIM, (h + 1) * B_VAL_DIM)
        o_ref[0, :, cols] = (_rms(o[h], ng_ref[...]) * _silu(r_ref[0][:, cols])).astype(_BF16)

    @pl.when(c == pl.num_programs(1) - 1)
    def _():
        s_out_ref[0] = st_ref[...]


def _gla(qb, kb, la, vb, rb, s0, norm_g):
    bsz, lq, _ = qb.shape
    tok = lambda n: pl.BlockSpec((1, CHUNK, n), lambda b, c: (b, c, 0))
    state = pl.BlockSpec((1, B_KW, B_VAL_DIM), lambda b, c: (b, 0, 0))
    o, s_new = pl.pallas_call(
        _gla_kernel,
        grid=(bsz, lq // CHUNK),
        in_specs=[tok(B_KW), tok(B_KW), tok(B_KW), tok(B_VW), tok(B_VW), state,
                  pl.BlockSpec((1, B_VAL_DIM), lambda b, c: (0, 0))],
        out_specs=[tok(B_VW), state],
        out_shape=[jax.ShapeDtypeStruct((bsz, lq, B_VW), _BF16),
                   jax.ShapeDtypeStruct((bsz, B_KW, B_VAL_DIM), _F32)],
        scratch_shapes=[pltpu.VMEM((B_KW, B_VAL_DIM), _F32)],
        compiler_params=_params(2),
        name="gla",
    )(qb, kb, la, vb, rb, s0.reshape(bsz, B_KW, B_VAL_DIM), norm_g.reshape(1, B_VAL_DIM))
    return o, s_new.reshape(s0.shape)


def _out_proj_kernel(*refs):
    x_ref, o_ref = refs[0], refs[-1]
    n = (len(refs) - 2) // 2
    acc = x_ref[...]
    for i in range(n):
        acc = acc + jnp.dot(refs[1 + i][...], refs[1 + n + i][...], preferred_element_type=_F32)
    o_ref[...] = acc


def _out_proj(x, acts, weights):
    t, d = x.shape
    tm = min(t, 512)
    return pl.pallas_call(
        _out_proj_kernel,
        grid=(t // tm,),
        in_specs=([pl.BlockSpec((tm, d), lambda i: (i, 0))]
                  + [pl.BlockSpec((tm, a.shape[1]), lambda i: (i, 0)) for a in acts]
                  + [_full(w.shape) for w in weights]),
        out_specs=pl.BlockSpec((tm, d), lambda i: (i, 0)),
        out_shape=jax.ShapeDtypeStruct((t, d), _F32),
        compiler_params=_params(1),
        name="out_proj",
    )(x, *acts, *weights)


def _odd_in_kernel(x_ref, g_ref, w_ref, alog_ref, dt_ref, qkv_ref, z_ref, gb_ref):
    h = _rms(x_ref[...], g_ref[...]).astype(_BF16)
    qkv_ref[...] = jnp.dot(h, w_ref[:, _OD_QKV:_OD_QKV + CONV_CH], preferred_element_type=_F32)
    z_ref[...] = jnp.dot(h, w_ref[:, _OD_Z:_OD_Z + C_W], preferred_element_type=_F32)
    ab = jnp.dot(h, w_ref[:, _OD_AB:_OD_AB + V7X_LANES], preferred_element_type=_F32)
    xa = ab + dt_ref[...]
    g = -jnp.exp(alog_ref[...]) * (jnp.maximum(xa, 0.0) + jnp.log1p(jnp.exp(-jnp.abs(xa))))
    lane = lax.broadcasted_iota(jnp.int32, ab.shape, 1)
    gb_ref[...] = jnp.where(lane < C_HEADS, g, jax.nn.sigmoid(ab))


def _odd_in_proj(x, g, w, a_log, dt_bias):
    t, d = x.shape
    tm = min(t, 256)
    row = lambda n: pl.BlockSpec((tm, n), lambda i: (i, 0))
    outs = [CONV_CH, C_W, V7X_LANES]
    return pl.pallas_call(
        _odd_in_kernel,
        grid=(t // tm,),
        in_specs=[row(d), _full((1, d)), _full((d, _OD_COLS)), _full((1, V7X_LANES)), _full((1, V7X_LANES))],
        out_specs=[row(n) for n in outs],
        out_shape=[jax.ShapeDtypeStruct((t, n), _F32) for n in outs],
        compiler_params=_params(1),
        name="odd_in_proj",
    )(x, g.reshape(1, d), w, a_log, dt_bias)


def _split_bf16(x):
    hi = x.astype(_BF16).astype(_F32)
    return hi, x - hi


def _delta_kernel(qkv_ref, z_ref, gb_ref, cw_ref, cprev_ref, s0_ref, ng_ref, o_ref, s_out_ref,
                  xbuf_ref, act_ref, st_ref):
    c = pl.program_id(1)
    halo = V7X_SUBLANES

    @pl.when(c == 0)
    def _():
        st_ref[...] = s0_ref[0]
        xbuf_ref[0:halo, :] = cprev_ref[0]

    xbuf_ref[halo:halo + CHUNK, :] = qkv_ref[0]
    conv = xbuf_ref[halo:halo + CHUNK, :] * cw_ref[CONV_WIDTH - 1:CONV_WIDTH, :]
    for j in range(CONV_WIDTH - 1):
        sh = CONV_WIDTH - 1 - j
        conv = conv + xbuf_ref[halo - sh:halo - sh + CHUNK, :] * cw_ref[j:j + 1, :]
    xbuf_ref[0:halo, :] = xbuf_ref[CHUNK:CHUNK + halo, :]
    act_ref[...] = _silu(conv)

    t_i = lax.broadcasted_iota(jnp.int32, (CHUNK, V7X_LANES), 0)
    s_i = lax.broadcasted_iota(jnp.int32, (CHUNK, V7X_LANES), 1)
    lower_half = s_i < CHUNK
    s_mod = jnp.where(lower_half, s_i, s_i - CHUNK)
    incl = t_i >= s_mod
    strict = t_i > s_mod
    eye = (t_i == s_mod).astype(_F32)
    incl64 = (lax.broadcasted_iota(jnp.int32, (CHUNK, CHUNK), 0)
              >= lax.broadcasted_iota(jnp.int32, (CHUNK, CHUNK), 1)).astype(_F32)
    eye128 = (lax.broadcasted_iota(jnp.int32, (V7X_LANES, V7X_LANES), 0)
              == lax.broadcasted_iota(jnp.int32, (V7X_LANES, V7X_LANES), 1)).astype(_F32)

    def split3(x):
        hi16 = x.astype(_BF16)
        hi = hi16.astype(_F32)
        lo = x - hi
        lhs = jnp.concatenate([jnp.where(lower_half, hi, lo).astype(_BF16), hi16], axis=1)
        lo16 = lo.astype(_BF16)
        rhs = jnp.concatenate([hi16, hi16, lo16, jnp.zeros_like(lo16)], axis=0)
        return lhs, rhs

    def mm(lhs, rhs):
        return jnp.dot(lhs, rhs, preferred_element_type=_F32)

    heads = range(C_HEADS)
    gb = gb_ref[0]
    gcum_all = _dot_hi(incl64, gb)
    gcum_rows = _dot_hi(eye128, jnp.concatenate([gcum_all, gcum_all], axis=0), _NT)
    g_last = gcum_all[CHUNK - 1:CHUNK, :]

    qs, ks, kbs, decays, rhss, qgs, kdecs = [], [], [], [], [], [], []
    for h in heads:
        q = act_ref[:, h * C_HEAD_DIM:(h + 1) * C_HEAD_DIM]
        k = act_ref[:, C_W + h * C_HEAD_DIM:C_W + (h + 1) * C_HEAD_DIM]
        v = act_ref[:, 2 * C_W + h * C_HEAD_DIM:2 * C_W + (h + 1) * C_HEAD_DIM]
        q = q * lax.rsqrt(jnp.sum(q * q, axis=-1, keepdims=True) + EPS) * (C_HEAD_DIM ** -0.5)
        k = k * lax.rsqrt(jnp.sum(k * k, axis=-1, keepdims=True) + EPS)
        beta = gb[:, C_HEADS + h:C_HEADS + h + 1]
        gcum = gcum_all[:, h:h + 1]
        kb = k * beta
        qs.append(q)
        ks.append(k)
        kbs.append(kb)
        decays.append(jnp.exp(jnp.where(incl, gcum - gcum_rows[h:h + 1, :], _NEG)))
        rhss.append(jnp.concatenate([v * beta, kb * jnp.exp(gcum)], axis=-1))
        qgs.append(q * jnp.exp(gcum))
        kdecs.append(k * jnp.exp(g_last[:, h:h + 1] - gcum))

    lowers, attns = [], []
    for h in heads:
        prod = _dot(jnp.concatenate([kbs[h], qs[h]], axis=0),
                    jnp.concatenate([ks[h], ks[h]], axis=0), _NT)
        lowers.append(jnp.where(strict, prod[:CHUNK] * decays[h], 0.0))
        attns.append(jnp.where(incl, prod[CHUNK:] * decays[h], 0.0))

    pws = [-lowers[h] for h in heads]
    invs = [eye + pws[h] for h in heads]
    pw_l, pw_r = zip(*[split3(pws[h]) for h in heads])
    pws = [mm(pw_l[h], pw_r[h]) for h in heads]
    for _ in range(4):
        pw_l, pw_r = zip(*[split3(pws[h]) for h in heads])
        both = [mm(jnp.concatenate([split3(invs[h])[0], pw_l[h]], axis=0), pw_r[h]) for h in heads]
        invs = [invs[h] + both[h][:CHUNK] for h in heads]
        pws = [both[h][CHUNK:] for h in heads]
    invs = [invs[h] + mm(split3(invs[h])[0], split3(pws[h])[1]) for h in heads]
    sols = []
    for h in heads:
        r_hi16 = rhss[h].astype(_BF16)
        r_lo16 = (rhss[h] - r_hi16.astype(_F32)).astype(_BF16)
        sols.append(mm(split3(invs[h])[0], jnp.concatenate([r_hi16, r_hi16, r_lo16, jnp.zeros_like(r_lo16)], axis=0)))

    for h in heads:
        st = st_ref[h]
        u = sols[h][:, :C_HEAD_DIM]
        w = sols[h][:, C_HEAD_DIM:]
        ws = _dot(jnp.concatenate([w, qgs[h]], axis=0), st)
        v_new = u - ws[:CHUNK]
        o = ws[CHUNK:] + _dot(attns[h][:, :CHUNK], v_new)
        st_ref[h] = jnp.exp(g_last[:, h:h + 1]) * st + _dot(kdecs[h], v_new, _TN)
        cols = slice(h * C_HEAD_DIM, (h + 1) * C_HEAD_DIM)
        o_ref[0, :, cols] = (_rms(o, ng_ref[...]) * _silu(z_ref[0][:, cols])).astype(_BF16)

    @pl.when(c == pl.num_programs(1) - 1)
    def _():
        s_out_ref[0] = st_ref[...]


def _delta(qkv, z, gb, conv_w, conv_prev8, s0, norm_g):
    bsz, lq, _ = qkv.shape
    tok = lambda n: pl.BlockSpec((1, CHUNK, n), lambda b, c: (b, c, 0))
    state = pl.BlockSpec((1, C_HEADS, C_HEAD_DIM, C_HEAD_DIM), lambda b, c: (b, 0, 0, 0))
    return pl.pallas_call(
        _delta_kernel,
        grid=(bsz, lq // CHUNK),
        in_specs=[tok(CONV_CH), tok(C_W), tok(V7X_LANES),
                  pl.BlockSpec((CONV_WIDTH, CONV_CH), lambda b, c: (0, 0)),
                  pl.BlockSpec((1, V7X_SUBLANES, CONV_CH), lambda b, c: (b, 0, 0)),
                  state,
                  pl.BlockSpec((1, C_HEAD_DIM), lambda b, c: (0, 0))],
        out_specs=[tok(C_W), state],
        out_shape=[jax.ShapeDtypeStruct((bsz, lq, C_W), _BF16),
                   jax.ShapeDtypeStruct(s0.shape, _F32)],
        scratch_shapes=[pltpu.VMEM((CHUNK + V7X_SUBLANES, CONV_CH), _F32),
                        pltpu.VMEM((CHUNK, CONV_CH), _F32),
                        pltpu.VMEM((C_HEADS, C_HEAD_DIM, C_HEAD_DIM), _F32)],
        compiler_params=_params(2),
        name="gated_delta",
    )(qkv, z, gb, conv_w, conv_prev8, s0, norm_g.reshape(1, C_HEAD_DIM))


def _prep_even_w_in(w):
    sizes = (A_W, A_W, A_W, IDX_HEADS * IDX_DIM, IDX_DIM, IDX_HEADS, B_KW, B_KW, B_VW, B_VW, B_GATE_RANK)
    offs = [0]
    for s in sizes:
        offs.append(offs[-1] + s)
    part = lambda i: w[:, offs[i]:offs[i + 1]]
    q, k, v, qi, ki, wi, qb, kb, vb, rb, ab = [part(i) for i in range(len(sizes))]
    pad = jnp.zeros((w.shape[0], _EV_IDX_W - (256 + IDX_DIM + IDX_HEADS + B_GATE_RANK)), w.dtype)
    return jnp.concatenate([q, k, v, qi, ki, wi, ab, pad, qb, kb, vb, rb], axis=1).astype(_BF16)


def _prep_odd_w_in(w):
    pad = jnp.zeros((w.shape[0], _OD_COLS - w.shape[1]), w.dtype)
    return jnp.concatenate([w, pad], axis=1).astype(_BF16)


def _lane_pad(v, n=V7X_LANES):
    return jnp.pad(v, (0, n - v.shape[0])).reshape(1, n)


def _trunk(x, past, prm):
    bsz, lq, d = x.shape
    t = bsz * lq
    xf = x.reshape(t, d)

    def ffn(xf, i, j):
        return _ffn_half(xf, prm['ffn_norm'][i, j], prm['ffn_w_gate'][i, j].astype(_BF16),
                         prm['ffn_w_up'][i, j].astype(_BF16), prm['ffn_w_down'][i, j].astype(_BF16))

    xf = ffn(xf, 0, 0)
    head_mean = jnp.kron(jnp.eye(A_HEADS, dtype=_F32),
                         jnp.full((A_HEAD_DIM, A_HEAD_DIM), 1.0 / A_HEAD_DIM, _F32)).astype(_BF16)
    q_gain = (jnp.tile(prm['ev_q_norm'][0], A_HEADS) * (A_HEAD_DIM ** -0.5)).reshape(1, A_W)
    k_gain = jnp.tile(prm['ev_k_norm'][0], A_HEADS).reshape(1, A_W)
    w2 = jnp.zeros((V7X_LANES, B_KW), _F32).at[
        _EV_AB_OFF - 256:_EV_AB_OFF - 256 + B_GATE_RANK].set(prm['ev_gate_w2'][0])
    qn, k_new, v_new, idx, qb, kb, vb, rb, la = _even_in_proj(
        xf, prm['mix_norm'][0], _prep_even_w_in(prm['ev_w_in'][0]), q_gain, k_gain, head_mean,
        w2, prm['ev_gate_b2'][0].reshape(1, B_KW))
    r3 = lambda a: a.reshape(bsz, lq, a.shape[-1])
    ki_new = r3(idx)[:, :, 256:256 + IDX_DIM]
    if past is None:
        past_len = 0
        k_all, v_all, ki_all = r3(k_new), r3(v_new), ki_new
        s0_gla = jnp.zeros((bsz, B_HEADS, B_KEY_DIM, B_VAL_DIM), _F32)
    else:
        past_len = past['k'].shape[2]
        k_all = jnp.concatenate([past['k'][0].reshape(bsz, past_len, A_W), r3(k_new)], axis=1)
        v_all = jnp.concatenate([past['v'][0].reshape(bsz, past_len, A_W), r3(v_new)], axis=1)
        ki_all = jnp.concatenate([past['ki'][0], ki_new], axis=1)
        s0_gla = past['gla'][0]
    o_a = _dsa_attention(r3(qn), r3(idx), k_all, v_all, ki_all, prm['rel_bias'], past_len)
    o_b, s_gla = _gla(r3(qb), r3(kb), r3(la), r3(vb), r3(rb), s0_gla, prm['ev_gla_norm'][0])
    w_out = prm['ev_w_out'][0].astype(_BF16)
    xf = _out_proj(xf, [o_a.reshape(t, A_W), o_b.reshape(t, B_VW)], [w_out[:A_W], w_out[A_W:]])
    xf = ffn(xf, 0, 1)

    xf = ffn(xf, 1, 0)
    a_log = _lane_pad(prm['od_a_log'][0])
    dt_bias = _lane_pad(prm['od_dt_bias'][0])
    qkv, z, gb = _odd_in_proj(xf, prm['mix_norm'][1], _prep_odd_w_in(prm['od_w_in'][0]), a_log, dt_bias)
    if past is None:
        conv_prev = jnp.zeros((bsz, CONV_WIDTH - 1, CONV_CH), _F32)
        s0_delta = jnp.zeros((bsz, C_HEADS, C_HEAD_DIM, C_HEAD_DIM), _F32)
    else:
        conv_prev, s0_delta = past['conv'][0], past['delta'][0]
    conv_prev8 = jnp.pad(conv_prev, ((0, 0), (V7X_SUBLANES - (CONV_WIDTH - 1), 0), (0, 0)))
    o_c, s_delta = _delta(r3(qkv), r3(z), r3(gb), prm['od_conv_w'][0], conv_prev8, s0_delta,
                          prm['od_norm'][0])
    conv_new = jnp.concatenate([conv_prev, r3(qkv)], axis=1)[:, -(CONV_WIDTH - 1):]
    xf = _out_proj(xf, [o_c.reshape(t, C_W)], [prm['od_w_out'][0].astype(_BF16)])
    xf = ffn(xf, 1, 1)

    y = xf.reshape(bsz, lq, d)
    k_out = r3(k_new).reshape(1, bsz, lq, A_HEADS, A_HEAD_DIM)
    v_out = r3(v_new).reshape(1, bsz, lq, A_HEADS, A_HEAD_DIM)
    return y, k_out, v_out, ki_new[None], s_gla[None], s_delta[None], conv_new[None]


def kernel(x_prompt, x_sample, cache_attn_k, cache_attn_v, cache_idx_k, state_gla, state_delta, state_conv,
           ffn_norm, ffn_w_gate, ffn_w_up, ffn_w_down, mix_norm, ev_w_in, ev_q_norm, ev_k_norm, rel_bias,
           ev_gate_w2, ev_gate_b2, ev_gla_norm, ev_w_out, od_w_in, od_conv_w, od_a_log, od_dt_bias, od_norm,
           od_w_out):
    prm = {'ffn_norm': ffn_norm, 'ffn_w_gate': ffn_w_gate, 'ffn_w_up': ffn_w_up, 'ffn_w_down': ffn_w_down,
           'mix_norm': mix_norm, 'ev_w_in': ev_w_in, 'ev_q_norm': ev_q_norm, 'ev_k_norm': ev_k_norm,
           'rel_bias': rel_bias, 'ev_gate_w2': ev_gate_w2, 'ev_gate_b2': ev_gate_b2, 'ev_gla_norm': ev_gla_norm,
           'ev_w_out': ev_w_out, 'od_w_in': od_w_in, 'od_conv_w': od_conv_w, 'od_a_log': od_a_log,
           'od_dt_bias': od_dt_bias, 'od_norm': od_norm, 'od_w_out': od_w_out}
    past = {'k': cache_attn_k, 'v': cache_attn_v, 'ki': cache_idx_k, 'gla': state_gla,
            'delta': state_delta, 'conv': state_conv}
    y_prompt, p_k, p_v, p_ki, p_gla, p_delta, p_conv = _trunk(x_prompt, None, prm)
    y_sample, s_k, s_v, s_ki, s_gla, s_delta, s_conv = _trunk(x_sample, past, prm)
    return (y_prompt, y_sample, p_k, p_v, p_ki, p_gla, p_delta, p_conv,
            s_k, s_v, s_ki, s_gla, s_delta, s_conv)
```

```python
import functools
import math

import jax
import jax.numpy as jnp
from jax import lax
from jax.experimental import pallas as pl
from jax.experimental.pallas import tpu as pltpu

D_MODEL = 1024
DEPTH = 2
CHUNK = 64
EPS = 1e-6

A_HEADS = 8
A_HEAD_DIM = 64
A_W = A_HEADS * A_HEAD_DIM
IDX_HEADS = 4
IDX_DIM = 64
TOPK_MAX = 256
N_BUCKETS = 32
MAX_DISTANCE = 128

B_HEADS = 4
B_KEY_DIM = 64
B_VAL_DIM = 128
B_KW = B_HEADS * B_KEY_DIM
B_VW = B_HEADS * B_VAL_DIM
B_GATE_RANK = 16
B_GATE_TAU = 16.0

C_HEADS = 8
C_HEAD_DIM = 128
C_W = C_HEADS * C_HEAD_DIM
CONV_WIDTH = 4
CONV_CH = 3 * C_W

D_FF = 2816

V7X_LANES = 128
V7X_SUBLANES = 8
V7X_VMEM_LIMIT_BYTES = 56 * 1024 * 1024
PACK16 = 2 * V7X_SUBLANES
HALF16 = 1 << 15

_F32 = jnp.float32
_BF16 = jnp.bfloat16
_HI = lax.Precision.HIGHEST
_NEG = -1e30
_NN = (((1,), (0,)), ((), ()))
_NT = (((1,), (1,)), ((), ()))
_TN = (((0,), (0,)), ((), ()))

_EV_Q, _EV_K, _EV_V = 0, 512, 1024
_EV_IDX = 1536
_EV_IDX_W = 384
_EV_WI_OFF = 256 + IDX_DIM
_EV_AB_OFF = _EV_WI_OFF + IDX_HEADS
_EV_QB, _EV_KB, _EV_VB, _EV_RB = 1920, 2176, 2432, 2944
_EV_COLS = 3456
_OD_QKV, _OD_Z, _OD_AB = 0, 3072, 4096
_OD_COLS = 4224


def _params(n_axes):
    return pltpu.CompilerParams(dimension_semantics=("arbitrary",) * n_axes,
                                vmem_limit_bytes=V7X_VMEM_LIMIT_BYTES)


def _rms(x, g):
    return x * lax.rsqrt(jnp.mean(x * x, axis=-1, keepdims=True) + EPS) * g


def _silu(x):
    return x * jax.nn.sigmoid(x)


def _dot(a, b, dims=_NN):
    return lax.dot_general(a.astype(_BF16), b.astype(_BF16), dims, preferred_element_type=_F32)


def _dot_hi(a, b, dims=_NN):
    return lax.dot_general(a, b, dims, precision=_HI, preferred_element_type=_F32)


def _full(shape):
    return pl.BlockSpec(shape, lambda *_: (0,) * len(shape))


_FFN_COLS = 256


def _ffn_kernel(x_ref, g_ref, wg_ref, wu_ref, wd_ref, o_ref, act_ref):
    x = x_ref[...]
    h = _rms(x, g_ref[...]).astype(_BF16)
    for c in range(wg_ref.shape[1] // _FFN_COLS):
        cols = slice(c * _FFN_COLS, (c + 1) * _FFN_COLS)
        gate = jnp.dot(h, wg_ref[:, cols], preferred_element_type=_F32)
        up = jnp.dot(h, wu_ref[:, cols], preferred_element_type=_F32)
        act_ref[:, cols] = (_silu(gate) * up).astype(_BF16)
    o_ref[...] = x + 0.5 * jnp.dot(act_ref[...], wd_ref[...], preferred_element_type=_F32)


def _ffn_half(x, g, wg, wu, wd):
    t, d = x.shape
    ff = wg.shape[1]
    tm = min(t, 512)
    resident = lambda shape: pl.BlockSpec(shape, lambda i: (0, 0), pipeline_mode=pl.Buffered(1))
    return pl.pallas_call(
        _ffn_kernel,
        grid=(t // tm,),
        in_specs=[pl.BlockSpec((tm, d), lambda i: (i, 0)),
                  resident((1, d)), resident((d, ff)), resident((d, ff)), resident((ff, d))],
        out_specs=pl.BlockSpec((tm, d), lambda i: (i, 0)),
        out_shape=jax.ShapeDtypeStruct((t, d), _F32),
        scratch_shapes=[pltpu.VMEM((tm, ff), _BF16)],
        compiler_params=_params(1),
        name="ffn_half",
    )(x, g.reshape(1, d), wg, wu, wd)


def _even_in_kernel(x_ref, g_ref, w_ref, qg_ref, kg_ref, hm_ref, w2_ref, b2_ref,
                    qn_ref, k_ref, v_ref, k16_ref, v16_ref, idx_ref, qb_ref, kb_ref, vb_ref, rb_ref, la_ref):
    h = _rms(x_ref[...], g_ref[...]).astype(_BF16)

    def proj(lo, width):
        return jnp.dot(h, w_ref[:, lo:lo + width], preferred_element_type=_F32)

    def headnorm(t, gain):
        t2 = t * t
        hi = t2.astype(_BF16)
        lo = (t2 - hi.astype(_F32)).astype(_BF16)
        ms = (jnp.dot(hi, hm_ref[...], preferred_element_type=_F32)
              + jnp.dot(lo, hm_ref[...], preferred_element_type=_F32))
        return t * lax.rsqrt(ms + EPS) * gain

    qn_ref[...] = headnorm(proj(_EV_Q, A_W), qg_ref[...]).astype(_BF16)
    k = headnorm(proj(_EV_K, A_W), kg_ref[...])
    v = proj(_EV_V, A_W)
    k_ref[...] = k
    v_ref[...] = v
    k16_ref[...] = k.astype(_BF16)
    v16_ref[...] = v.astype(_BF16)
    idx = proj(_EV_IDX, _EV_IDX_W)
    idx_ref[...] = idx
    qb_ref[...] = proj(_EV_QB, B_KW) * (B_KEY_DIM ** -0.5)
    kb_ref[...] = proj(_EV_KB, B_KW)
    vb_ref[...] = proj(_EV_VB, B_VW)
    rb_ref[...] = proj(_EV_RB, B_VW)
    z = _dot_hi(idx[:, 256:384], w2_ref[...]) + b2_ref[...]
    la_ref[...] = (jnp.minimum(z, 0.0) - jnp.log1p(jnp.exp(-jnp.abs(z)))) * (1.0 / B_GATE_TAU)


def _even_in_proj(x, g, w, q_gain, k_gain, head_mean, w2, b2):
    t, d = x.shape
    tm = min(t, 256)
    row = lambda n: pl.BlockSpec((tm, n), lambda i: (i, 0))
    outs = [(A_W, _BF16), (A_W, _F32), (A_W, _F32), (A_W, _BF16), (A_W, _BF16), (_EV_IDX_W, _F32), (B_KW, _F32), (B_KW, _F32),
            (B_VW, _F32), (B_VW, _F32), (B_KW, _F32)]
    return pl.pallas_call(
        _even_in_kernel,
        grid=(t // tm,),
        in_specs=[row(d), _full((1, d)), _full((d, _EV_COLS)), _full((1, A_W)), _full((1, A_W)),
                  _full((A_W, A_W)), _full((V7X_LANES, B_KW)), _full((1, B_KW))],
        out_specs=[row(n) for n, _ in outs],
        out_shape=[jax.ShapeDtypeStruct((t, n), dt) for n, dt in outs],
        compiler_params=_params(1),
        name="even_in_proj",
    )(x, g.reshape(1, d), w, q_gain, k_gain, head_mean, w2, b2)


def _dsa_kernel(rb_ref, qt_ref, qit_ref, wit_ref, k_ref, vt_ref, ki_ref, bkt_ref, o_ref,
                key_ref, hi_ref, lo_ref, mask_ref, nb_ref, ki3_ref, acc_ref, lt_ref, p_ref, *, past, tq, kc, topk, r0_off):
    t = pl.program_id(1)

    @pl.when((pl.program_id(0) == 0) & (t == 0))
    def _():
        bk = bkt_ref[...]
        for h in range(A_HEADS):
            acc = jnp.zeros(bk.shape, _F32)
            for bb in range(N_BUCKETS):
                acc = jnp.where(bk == bb, rb_ref[bb, h], acc)
            nb_ref[h] = acc - rb_ref[N_BUCKETS // 2 - 1, h]

    q0 = past + t * tq
    n_chunks = (q0 + tq + kc - 1) // kc
    q_limit = ((q0 + lax.broadcasted_iota(jnp.int32, (1, tq), 1)) // CHUNK + 1) * CHUNK
    k_iota = lax.broadcasted_iota(jnp.int32, (kc, tq), 0)

    def admissible(r0):
        return k_iota < (q_limit - r0)

    @pl.when(t == 0)
    def _():
        def split_body(j, carry):
            r0 = pl.multiple_of(j * kc, kc)
            x = ki_ref[0, pl.ds(r0, kc), :]
            hi = x.astype(_BF16)
            ki3_ref[pl.ds(r0, kc), 0:V7X_LANES] = hi
            ki3_ref[pl.ds(r0, kc), V7X_LANES:2 * V7X_LANES] = (x - hi.astype(_F32)).astype(_BF16)
            return carry
        lax.fori_loop(0, ki3_ref.shape[0] // kc, split_body, 0)

    qit = qit_ref[0]
    q_cols = []
    for h in range(IDX_HEADS):
        x = qit[h * IDX_DIM:(h + 1) * IDX_DIM, :]
        hi = x.astype(_BF16)
        lo = (x - hi.astype(_F32)).astype(_BF16)
        q_cols.append(jnp.concatenate([hi, lo, hi, jnp.zeros_like(hi)], axis=0))
    q3 = jnp.concatenate(q_cols, axis=1)
    wit = wit_ref[0]

    def score_body(j, carry):
        r0 = pl.multiple_of(j * kc, kc)
        d = jnp.dot(ki3_ref[pl.ds(r0, kc), :], q3, preferred_element_type=_F32)
        s = jnp.zeros((kc, tq), _F32)
        for h in range(IDX_HEADS):
            s = s + wit[h:h + 1, :] * jnp.maximum(d[:, h * tq:(h + 1) * tq], 0.0)
        s = jnp.where(s == 0.0, 0.0, s)
        s = jnp.where(admissible(r0), s, -jnp.inf)
        bits = pltpu.bitcast(s, jnp.int32)
        key = bits ^ ((bits >> 31) & jnp.int32(0x7FFFFFFF))
        key_ref[pl.ds(r0, kc), :] = key
        hi_ref[pl.ds(r0, kc), :] = (key >> 16).astype(jnp.int16)
        lo_ref[pl.ds(r0, kc), :] = ((key & 0xFFFF) - HALF16).astype(jnp.int16)
        return carry

    @pl.when(t == 0)
    def _():
        hi_ref[...] = jnp.full(hi_ref.shape, -HALF16, jnp.int16)
        lo_ref[...] = jnp.full(lo_ref.shape, -HALF16, jnp.int16)

    lax.fori_loop(0, n_chunks, score_body, 0)

    kc2 = 2 * kc
    n_pairs = (n_chunks + 1) // 2

    def count_ge16(ref, c16):
        def body(j, acc):
            r0 = pl.multiple_of(j * kc2, kc2)
            m = jnp.where(ref[pl.ds(r0, kc2), :] >= c16, jnp.int16(1), jnp.int16(0))
            parts = [m[i * PACK16:(i + 1) * PACK16, :] for i in range(kc2 // PACK16)]
            while len(parts) > 1:
                parts = [parts[i] + parts[i + 1] for i in range(0, len(parts), 2)]
            return acc + parts[0]
        acc = lax.fori_loop(0, n_pairs, body, jnp.zeros((PACK16, tq), jnp.int16))
        return acc.astype(jnp.int32).sum(axis=0, keepdims=True)

    def select16(ref, base):
        def bit_body(i, tu):
            cu = tu | jnp.left_shift(jnp.int32(1), 15 - i)
            return jnp.where(base + count_ge16(ref, (cu - HALF16).astype(jnp.int16)) >= topk, cu, tu)
        return lax.fori_loop(0, 16, bit_body, jnp.zeros((1, tq), jnp.int32))

    t_hi = select16(hi_ref, 0) - HALF16
    above = count_ge16(hi_ref, (t_hi + 1).astype(jnp.int16))
    t_hi16 = t_hi.astype(jnp.int16)

    def bucket_body(j, carry):
        r0 = pl.multiple_of(j * kc2, kc2)
        lo_ref[pl.ds(r0, kc2), :] = jnp.where(hi_ref[pl.ds(r0, kc2), :] == t_hi16, lo_ref[pl.ds(r0, kc2), :],
                                              jnp.int16(-HALF16))
        return carry

    lax.fori_loop(0, n_pairs, bucket_body, 0)
    t_lo = select16(lo_ref, above)
    ts = (t_hi << 16) | t_lo

    def count_keys(above_or_equal):
        def body(j, acc):
            r0 = pl.multiple_of(j * kc, kc)
            blk = key_ref[pl.ds(r0, kc), :]
            m = jnp.where(blk >= ts if above_or_equal else blk > ts, 1, 0)
            return acc + m.reshape(kc // V7X_SUBLANES, V7X_SUBLANES, tq).sum(axis=0)
        acc = lax.fori_loop(0, n_chunks, body, jnp.zeros((V7X_SUBLANES, tq), jnp.int32))
        return acc.sum(axis=0, keepdims=True)

    n_ge = count_keys(True)
    n_greater = count_keys(False)
    room = (topk - n_greater).astype(_F32)

    tri = (lax.broadcasted_iota(jnp.int32, (kc, kc), 0)
           >= lax.broadcasted_iota(jnp.int32, (kc, kc), 1)).astype(_BF16)

    def tie_mask_body(j, seen):
        r0 = pl.multiple_of(j * kc, kc)
        blk = key_ref[pl.ds(r0, kc), :]
        eq = blk == ts
        rank = jnp.dot(tri, jnp.where(eq, 1.0, 0.0).astype(_BF16), preferred_element_type=_F32) + seen
        val = jnp.where(blk > ts, 0.0, jnp.where(eq, jnp.where(rank <= room, 0.0, _NEG), _NEG))
        mask_ref[pl.ds(r0, kc), :] = jnp.where(admissible(r0), val, _NEG)
        return rank[kc - 1:kc, :]

    def plain_mask_body(j, carry):
        r0 = pl.multiple_of(j * kc, kc)
        val = jnp.where(key_ref[pl.ds(r0, kc), :] >= ts, 0.0, _NEG)
        mask_ref[pl.ds(r0, kc), :] = jnp.where(admissible(r0), val, _NEG)
        return carry

    lax.cond(jnp.max(n_ge) > topk,
             lambda: lax.fori_loop(0, n_chunks, tie_mask_body, jnp.zeros((1, tq), _F32)) * 0.0,
             lambda: lax.fori_loop(0, n_chunks, plain_mask_body, jnp.zeros((1, tq), _F32)))

    upper_rows = lax.broadcasted_iota(jnp.int32, (V7X_LANES, tq), 0) >= A_HEAD_DIM
    q_pairs = []
    for pr in range(A_HEADS // 2):
        blk = qt_ref[0, pr * V7X_LANES:(pr + 1) * V7X_LANES, :].astype(_F32)
        q_pairs.append(jnp.concatenate([jnp.where(upper_rows, 0.0, blk), jnp.where(upper_rows, blk, 0.0)],
                                       axis=1).astype(_BF16))
    acc_ref[...] = jnp.zeros_like(acc_ref)

    def attn_body(j, carry):
        ms, ls = carry
        r0 = pl.multiple_of(j * kc, kc)
        msk = mask_ref[pl.ds(r0, kc), :]
        off = pl.multiple_of(jnp.maximum(r0 - q0 + r0_off, 0), CHUNK)
        for pr in range(A_HEADS // 2):
            kblk = k_ref[0, pl.ds(r0, kc), pr * V7X_LANES:(pr + 1) * V7X_LANES]
            lt2 = jnp.dot(kblk, q_pairs[pr], preferred_element_type=_F32)
            for hh in range(2):
                h = 2 * pr + hh
                lt_ref[h] = lt2[:, hh * tq:(hh + 1) * tq] + msk + nb_ref[h, pl.ds(off, kc), :]
        new_ms, new_ls, alphas = [], [], []
        for h in range(A_HEADS):
            lt = lt_ref[h]
            m_new = jnp.maximum(ms[h], lt.max(axis=0, keepdims=True))
            p = jnp.exp(lt - m_new)
            alpha = jnp.exp(ms[h] - m_new)
            new_ms.append(m_new)
            new_ls.append(alpha * ls[h] + p.sum(axis=0, keepdims=True))
            alphas.append(alpha)
            p_ref[h] = p.astype(_BF16)
        for h in range(A_HEADS):
            rows = slice(h * A_HEAD_DIM, (h + 1) * A_HEAD_DIM)
            pv = jnp.dot(vt_ref[0, rows, pl.ds(r0, kc)], p_ref[h], preferred_element_type=_F32)
            acc_ref[rows, :] = alphas[h] * acc_ref[rows, :] + pv
        return tuple(new_ms), tuple(new_ls)

    init = (tuple(jnp.full((1, tq), _NEG, _F32) for _ in range(A_HEADS)),
            tuple(jnp.zeros((1, tq), _F32) for _ in range(A_HEADS)))
    _, ls = lax.fori_loop(0, n_chunks, attn_body, init)

    eye = (lax.broadcasted_iota(jnp.int32, (tq, tq), 0)
           == lax.broadcasted_iota(jnp.int32, (tq, tq), 1)).astype(_BF16)
    for pair in range(A_HEADS // 2):
        lanes = slice(pair * V7X_LANES, (pair + 1) * V7X_LANES)
        o_t = jnp.concatenate(
            [acc_ref[(2 * pair + hh) * A_HEAD_DIM:(2 * pair + hh + 1) * A_HEAD_DIM, :] / ls[2 * pair + hh]
             for hh in range(2)], axis=0).astype(_BF16)
        o_ref[0, :, lanes] = lax.dot_general(eye, o_t, _NT, preferred_element_type=_F32).astype(_BF16)


def _t5_bucket_table(rel):
    half = N_BUCKETS // 2
    max_exact = half // 2
    n = jnp.abs(rel)
    nf = jnp.maximum(n, 1).astype(jnp.float32)
    large = max_exact + (jnp.log(nf / max_exact) / math.log(MAX_DISTANCE / max_exact)
                         * (half - max_exact)).astype(jnp.int32)
    large = jnp.minimum(large, half - 1)
    return jnp.where(rel > 0, half, 0) + jnp.where(n < max_exact, n, large)


def _dsa_attention(qn, idx, k_all, v_all, ki_all, rel_bias, past):
    bsz, lq, _ = qn.shape
    nk = k_all.shape[1]
    topk = min(TOPK_MAX, nk // 4)
    tq = min(V7X_LANES, lq)
    kc = 256
    nk_pad = -(-nk // (2 * kc)) * (2 * kc)
    pad = ((0, 0), (0, nk_pad - nk), (0, 0))
    k16 = jnp.pad(k_all, pad)
    vt16 = jnp.swapaxes(jnp.pad(v_all, pad), 1, 2)
    ki2 = jnp.pad(jnp.concatenate([ki_all, ki_all], axis=-1), pad)
    wit = jnp.swapaxes(idx[:, :, _EV_WI_OFF:_EV_WI_OFF + V7X_SUBLANES], 1, 2)
    qt = jnp.swapaxes(qn, 1, 2)
    qit = jnp.swapaxes(idx[:, :, :IDX_HEADS * IDX_DIM], 1, 2)
    r0_off = kc + MAX_DISTANCE
    nbr = r0_off + kc + tq
    rel = (jnp.arange(nbr, dtype=jnp.int32)[:, None] - r0_off) - jnp.arange(tq, dtype=jnp.int32)[None, :]
    bkt = _t5_bucket_table(rel)
    kern = functools.partial(_dsa_kernel, past=past, tq=tq, kc=kc, topk=topk, r0_off=r0_off)
    return pl.pallas_call(
        kern,
        grid=(bsz, lq // tq),
        in_specs=[pl.BlockSpec(memory_space=pltpu.SMEM),
                  pl.BlockSpec((1, A_W, tq), lambda b, t: (b, 0, t)),
                  pl.BlockSpec((1, IDX_HEADS * IDX_DIM, tq), lambda b, t: (b, 0, t)),
                  pl.BlockSpec((1, V7X_SUBLANES, tq), lambda b, t: (b, 0, t)),
                  pl.BlockSpec((1, nk_pad, A_W), lambda b, t: (b, 0, 0)),
                  pl.BlockSpec((1, A_W, nk_pad), lambda b, t: (b, 0, 0)),
                  pl.BlockSpec((1, nk_pad, V7X_LANES), lambda b, t: (b, 0, 0)),
                  pl.BlockSpec((nbr, tq), lambda b, t: (0, 0))],
        out_specs=pl.BlockSpec((1, tq, A_W), lambda b, t: (b, t, 0)),
        out_shape=jax.ShapeDtypeStruct((bsz, lq, A_W), _BF16),
        scratch_shapes=[pltpu.VMEM((nk_pad, tq), jnp.int32), pltpu.VMEM((nk_pad, tq), jnp.int16),
                        pltpu.VMEM((nk_pad, tq), jnp.int16), pltpu.VMEM((nk_pad, tq), _F32),
                        pltpu.VMEM((A_HEADS, nbr, tq), _F32),
                        pltpu.VMEM((nk_pad, 2 * V7X_LANES), _BF16), pltpu.VMEM((A_W, tq), _F32),
                        pltpu.VMEM((A_HEADS, kc, tq), _F32), pltpu.VMEM((A_HEADS, kc, tq), _BF16)],
        compiler_params=_params(2),
        name="dsa_attention",
    )(rel_bias, qt, qit, wit, k16, vt16, ki2, bkt)


def _split3_rows(x):
    hi = x.astype(_BF16)
    r1 = x - hi.astype(_F32)
    mid = r1.astype(_BF16)
    lo = (r1 - mid.astype(_F32)).astype(_BF16)
    return jnp.concatenate([hi, mid, lo], axis=0)


def _gla_kernel(q_ref, k_ref, g_ref, v_ref, r_ref, s0_ref, ng_ref, o_ref, s_out_ref, st_ref, *, n_sub):
    c = pl.program_id(1)
    causal = (lax.broadcasted_iota(jnp.int32, (CHUNK, CHUNK), 0)
              >= lax.broadcasted_iota(jnp.int32, (CHUNK, CHUNK), 1))
    causal3 = jnp.concatenate([causal.astype(_BF16)] * 3, axis=1)
    upper_half = lax.broadcasted_iota(jnp.int32, (CHUNK, V7X_LANES), 1) >= B_KEY_DIM
    diag = (lax.broadcasted_iota(jnp.int32, (V7X_LANES, V7X_LANES), 0)
            == lax.broadcasted_iota(jnp.int32, (V7X_LANES, V7X_LANES), 1))
    pairs = range(B_HEADS // 2)
    heads = range(B_HEADS)

    @pl.when(c == 0)
    def _():
        st_ref[...] = s0_ref[0]

    def pair_lanes(x, p):
        return x[:, p * V7X_LANES:(p + 1) * V7X_LANES]

    def head_half(x, h):
        keep = upper_half if h % 2 == 1 else jnp.logical_not(upper_half)
        return jnp.where(keep, pair_lanes(x, h // 2), 0.0)

    parts = []
    for s in range(n_sub):
        rows = slice(s * CHUNK, (s + 1) * CHUNK)
        b = jnp.dot(causal3, _split3_rows(g_ref[0, rows, :]), preferred_element_type=_F32)
        mid = CHUNK // 2
        b_mid = b[mid:mid + 1, :]
        b_last = b[CHUNK - 1:CHUNK, :]
        q = q_ref[0, rows, :]
        k = k_ref[0, rows, :]
        qe = q * jnp.exp(b)
        qm = q * jnp.exp(b - b_mid)
        km = k * jnp.exp(b_mid - b)
        kl = k * jnp.exp(b_last - b)
        vs = [v_ref[0, rows, h * B_VAL_DIM:(h + 1) * B_VAL_DIM] for h in heads]
        a = [jnp.where(causal, _dot(head_half(qm, h), pair_lanes(km, h // 2), _NT), 0.0) for h in heads]
        av = [_dot(a[h], vs[h]) for h in heads]
        upd = [_dot(head_half(kl, h), vs[h], _TN) for h in heads]
        dcol = [jnp.sum(jnp.where(diag, jnp.exp(pair_lanes(b_last, p)), 0.0), axis=1, keepdims=True)
                for p in pairs]
        parts.append((qe, av, upd, dcol))

    st = [st_ref[p * V7X_LANES:(p + 1) * V7X_LANES, :] for p in pairs]
    for s in range(n_sub):
        rows = slice(s * CHUNK, (s + 1) * CHUNK)
        qe, av, upd, dcol = parts[s]
        o = [_dot(head_half(qe, h), st[h // 2]) + av[h] for h in heads]
        st = [dcol[p] * st[p] + upd[2 * p] + upd[2 * p + 1] for p in pairs]
        for h in heads:
            cols = slice(h * B_VAL_DIM, (h + 1) * B_VAL_DIM)
            o_ref[0, rows, cols] = (_rms(o[h], ng_ref[...]) * _silu(r_ref[0, rows, cols])).astype(_BF16)
    for p in pairs:
        st_ref[p * V7X_LANES:(p + 1) * V7X_LANES, :] = st[p]

    @pl.when(c == pl.num_programs(1) - 1)
    def _():
        s_out_ref[0] = st_ref[...]


def _gla(qb, kb, la, vb, rb, s0, norm_g):
    bsz, lq, _ = qb.shape
    n_sub = min(4, lq // CHUNK)
    tok = lambda n: pl.BlockSpec((1, n_sub * CHUNK, n), lambda b, c: (b, c, 0))
    state = pl.BlockSpec((1, B_KW, B_VAL_DIM), lambda b, c: (b, 0, 0))
    o, s_new = pl.pallas_call(
        functools.partial(_gla_kernel, n_sub=n_sub),
        grid=(bsz, lq // (n_sub * CHUNK)),
        in_specs=[tok(B_KW), tok(B_KW), tok(B_KW), tok(B_VW), tok(B_VW), state,
                  pl.BlockSpec((1, B_VAL_DIM), lambda b, c: (0, 0))],
        out_specs=[tok(B_VW), state],
        out_shape=[jax.ShapeDtypeStruct((bsz, lq, B_VW), _BF16),
                   jax.ShapeDtypeStruct((bsz, B_KW, B_VAL_DIM), _F32)],
        scratch_shapes=[pltpu.VMEM((B_KW, B_VAL_DIM), _F32)],
        compiler_params=_params(2),
        name="gla",
    )(qb, kb, la, vb, rb, s0.reshape(bsz, B_KW, B_VAL_DIM), norm_g.reshape(1, B_VAL_DIM))
    return o, s_new.reshape(s0.shape)


def _out_proj_kernel(*refs):
    x_ref, o_ref = refs[0], refs[-1]
    n = (len(refs) - 2) // 2
    acc = x_ref[...]
    for i in range(n):
        acc = acc + jnp.dot(refs[1 + i][...], refs[1 + n + i][...], preferred_element_type=_F32)
    o_ref[...] = acc


def _out_proj(x, acts, weights):
    t, d = x.shape
    tm = min(t, 512)
    return pl.pallas_call(
        _out_proj_kernel,
        grid=(t // tm,),
        in_specs=([pl.BlockSpec((tm, d), lambda i: (i, 0))]
                  + [pl.BlockSpec((tm, a.shape[1]), lambda i: (i, 0)) for a in acts]
                  + [_full(w.shape) for w in weights]),
        out_specs=pl.BlockSpec((tm, d), lambda i: (i, 0)),
        out_shape=jax.ShapeDtypeStruct((t, d), _F32),
        compiler_params=_params(1),
        name="out_proj",
    )(x, *acts, *weights)


def _odd_in_kernel(x_ref, g_ref, w_ref, alog_ref, dt_ref, qkv_ref, z_ref, gb_ref):
    h = _rms(x_ref[...], g_ref[...]).astype(_BF16)
    qkv_ref[...] = jnp.dot(h, w_ref[:, _OD_QKV:_OD_QKV + CONV_CH], preferred_element_type=_F32)
    z_ref[...] = jnp.dot(h, w_ref[:, _OD_Z:_OD_Z + C_W], preferred_element_type=_F32)
    ab = jnp.dot(h, w_ref[:, _OD_AB:_OD_AB + V7X_LANES], preferred_element_type=_F32)
    xa = ab + dt_ref[...]
    g = -jnp.exp(alog_ref[...]) * (jnp.maximum(xa, 0.0) + jnp.log1p(jnp.exp(-jnp.abs(xa))))
    lane = lax.broadcasted_iota(jnp.int32, ab.shape, 1)
    gb_ref[...] = jnp.where(lane < C_HEADS, g, jax.nn.sigmoid(ab))


def _odd_in_proj(x, g, w, a_log, dt_bias):
    t, d = x.shape
    tm = min(t, 256)
    row = lambda n: pl.BlockSpec((tm, n), lambda i: (i, 0))
    outs = [CONV_CH, C_W, V7X_LANES]
    return pl.pallas_call(
        _odd_in_kernel,
        grid=(t // tm,),
        in_specs=[row(d), _full((1, d)), _full((d, _OD_COLS)), _full((1, V7X_LANES)), _full((1, V7X_LANES))],
        out_specs=[row(n) for n in outs],
        out_shape=[jax.ShapeDtypeStruct((t, n), _F32) for n in outs],
        compiler_params=_params(1),
        name="odd_in_proj",
    )(x, g.reshape(1, d), w, a_log, dt_bias)


def _delta_kernel(qkv_ref, z_ref, gb_ref, cw_ref, cprev_ref, s0_ref, ng_ref, o_ref, s_out_ref,
                  xbuf_ref, act_ref, st_ref, *, n_sub):
    c = pl.program_id(1)
    halo = V7X_SUBLANES
    rows_all = n_sub * CHUNK

    @pl.when(c == 0)
    def _():
        st_ref[...] = s0_ref[0]
        xbuf_ref[0:halo, :] = cprev_ref[0]

    xbuf_ref[halo:halo + rows_all, :] = qkv_ref[0]
    conv = xbuf_ref[halo:halo + rows_all, :] * cw_ref[CONV_WIDTH - 1:CONV_WIDTH, :]
    for j in range(CONV_WIDTH - 1):
        sh = CONV_WIDTH - 1 - j
        conv = conv + xbuf_ref[halo - sh:halo - sh + rows_all, :] * cw_ref[j:j + 1, :]
    xbuf_ref[0:halo, :] = xbuf_ref[rows_all:rows_all + halo, :]
    act_ref[...] = _silu(conv)

    t_i = lax.broadcasted_iota(jnp.int32, (CHUNK, V7X_LANES), 0)
    s_i = lax.broadcasted_iota(jnp.int32, (CHUNK, V7X_LANES), 1)
    lower_half = s_i < CHUNK
    s_mod = jnp.where(lower_half, s_i, s_i - CHUNK)
    incl = t_i >= s_mod
    strict = t_i > s_mod
    eye = (t_i == s_mod).astype(_F32)
    incl64 = (lax.broadcasted_iota(jnp.int32, (CHUNK, CHUNK), 0)
              >= lax.broadcasted_iota(jnp.int32, (CHUNK, CHUNK), 1)).astype(_F32)
    eye128 = (lax.broadcasted_iota(jnp.int32, (V7X_LANES, V7X_LANES), 0)
              == lax.broadcasted_iota(jnp.int32, (V7X_LANES, V7X_LANES), 1)).astype(_F32)

    def split3(x):
        hi16 = x.astype(_BF16)
        hi = hi16.astype(_F32)
        lo = x - hi
        lhs = jnp.concatenate([jnp.where(lower_half, hi, lo).astype(_BF16), hi16], axis=1)
        lo16 = lo.astype(_BF16)
        rhs = jnp.concatenate([hi16, hi16, lo16, jnp.zeros_like(lo16)], axis=0)
        return lhs, rhs

    def mm(lhs, rhs):
        return jnp.dot(lhs, rhs, preferred_element_type=_F32)

    heads = range(C_HEADS)
    units = [(s, h) for s in range(n_sub) for h in heads]
    g_lasts = []
    qs, ks, kbs, decays, rhss, qgs, kdecs = [], [], [], [], [], [], []
    for s in range(n_sub):
        rows = slice(s * CHUNK, (s + 1) * CHUNK)
        gb = gb_ref[0, rows, :]
        gcum_all = _dot_hi(incl64, gb)
        gcum_rows = _dot_hi(eye128, jnp.concatenate([gcum_all, gcum_all], axis=0), _NT)
        g_last = gcum_all[CHUNK - 1:CHUNK, :]
        g_lasts.append(g_last)
        for h in heads:
            q = act_ref[rows, h * C_HEAD_DIM:(h + 1) * C_HEAD_DIM]
            k = act_ref[rows, C_W + h * C_HEAD_DIM:C_W + (h + 1) * C_HEAD_DIM]
            v = act_ref[rows, 2 * C_W + h * C_HEAD_DIM:2 * C_W + (h + 1) * C_HEAD_DIM]
            q = q * lax.rsqrt(jnp.sum(q * q, axis=-1, keepdims=True) + EPS) * (C_HEAD_DIM ** -0.5)
            k = k * lax.rsqrt(jnp.sum(k * k, axis=-1, keepdims=True) + EPS)
            beta = gb[:, C_HEADS + h:C_HEADS + h + 1]
            gcum = gcum_all[:, h:h + 1]
            kb = k * beta
            qs.append(q)
            ks.append(k)
            kbs.append(kb)
            decays.append(jnp.exp(jnp.where(incl, gcum - gcum_rows[h:h + 1, :], _NEG)))
            rhss.append(jnp.concatenate([v * beta, kb * jnp.exp(gcum)], axis=-1))
            qgs.append(q * jnp.exp(gcum))
            kdecs.append(k * jnp.exp(g_last[:, h:h + 1] - gcum))

    n_units = range(len(units))
    lowers, attns = [], []
    for u in n_units:
        prod = _dot(jnp.concatenate([kbs[u], qs[u]], axis=0),
                    jnp.concatenate([ks[u], ks[u]], axis=0), _NT)
        lowers.append(jnp.where(strict, prod[:CHUNK] * decays[u], 0.0))
        attns.append(jnp.where(incl, prod[CHUNK:] * decays[u], 0.0))

    pws = [-lowers[u] for u in n_units]
    invs = [eye + pws[u] for u in n_units]
    pw_l, pw_r = zip(*[split3(pws[u]) for u in n_units])
    pws = [mm(pw_l[u], pw_r[u]) for u in n_units]
    for _ in range(4):
        pw_l, pw_r = zip(*[split3(pws[u]) for u in n_units])
        both = [mm(jnp.concatenate([split3(invs[u])[0], pw_l[u]], axis=0), pw_r[u]) for u in n_units]
        invs = [invs[u] + both[u][:CHUNK] for u in n_units]
        pws = [both[u][CHUNK:] for u in n_units]
    invs = [invs[u] + mm(split3(invs[u])[0], split3(pws[u])[1]) for u in n_units]
    sols = []
    for u in n_units:
        r_hi16 = rhss[u].astype(_BF16)
        r_lo16 = (rhss[u] - r_hi16.astype(_F32)).astype(_BF16)
        sols.append(mm(split3(invs[u])[0], jnp.concatenate([r_hi16, r_hi16, r_lo16, jnp.zeros_like(r_lo16)], axis=0)))

    st = [st_ref[h] for h in heads]
    for u, (s, h) in enumerate(units):
        rows = slice(s * CHUNK, (s + 1) * CHUNK)
        w_u = sols[u][:, :C_HEAD_DIM]
        w_w = sols[u][:, C_HEAD_DIM:]
        ws = _dot(jnp.concatenate([w_w, qgs[u]], axis=0), st[h])
        v_new = w_u - ws[:CHUNK]
        o = ws[CHUNK:] + _dot(attns[u][:, :CHUNK], v_new)
        st[h] = jnp.exp(g_lasts[s][:, h:h + 1]) * st[h] + _dot(kdecs[u], v_new, _TN)
        cols = slice(h * C_HEAD_DIM, (h + 1) * C_HEAD_DIM)
        o_ref[0, rows, cols] = (_rms(o, ng_ref[...]) * _silu(z_ref[0, rows, cols])).astype(_BF16)
    for h in heads:
        st_ref[h] = st[h]

    @pl.when(c == pl.num_programs(1) - 1)
    def _():
        s_out_ref[0] = st_ref[...]


def _delta(qkv, z, gb, conv_w, conv_prev8, s0, norm_g):
    bsz, lq, _ = qkv.shape
    n_sub = min(2, lq // CHUNK)
    rows_all = n_sub * CHUNK
    tok = lambda n: pl.BlockSpec((1, rows_all, n), lambda b, c: (b, c, 0))
    state = pl.BlockSpec((1, C_HEADS, C_HEAD_DIM, C_HEAD_DIM), lambda b, c: (b, 0, 0, 0))
    return pl.pallas_call(
        functools.partial(_delta_kernel, n_sub=n_sub),
        grid=(bsz, lq // rows_all),
        in_specs=[tok(CONV_CH), tok(C_W), tok(V7X_LANES),
                  pl.BlockSpec((CONV_WIDTH, CONV_CH), lambda b, c: (0, 0)),
                  pl.BlockSpec((1, V7X_SUBLANES, CONV_CH), lambda b, c: (b, 0, 0)),
                  state,
                  pl.BlockSpec((1, C_HEAD_DIM), lambda b, c: (0, 0))],
        out_specs=[tok(C_W), state],
        out_shape=[jax.ShapeDtypeStruct((bsz, lq, C_W), _BF16),
                   jax.ShapeDtypeStruct(s0.shape, _F32)],
        scratch_shapes=[pltpu.VMEM((rows_all + V7X_SUBLANES, CONV_CH), _F32),
                        pltpu.VMEM((rows_all, CONV_CH), _F32),
                        pltpu.VMEM((C_HEADS, C_HEAD_DIM, C_HEAD_DIM), _F32)],
        compiler_params=_params(2),
        name="gated_delta",
    )(qkv, z, gb, conv_w, conv_prev8, s0, norm_g.reshape(1, C_HEAD_DIM))


def _prep_even_w_in(w):
    sizes = (A_W, A_W, A_W, IDX_HEADS * IDX_DIM, IDX_DIM, IDX_HEADS, B_KW, B_KW, B_VW, B_VW, B_GATE_RANK)
    offs = [0]
    for s in sizes:
        offs.append(offs[-1] + s)
    part = lambda i: w[:, offs[i]:offs[i + 1]]
    q, k, v, qi, ki, wi, qb, kb, vb, rb, ab = [part(i) for i in range(len(sizes))]
    pad = jnp.zeros((w.shape[0], _EV_IDX_W - (256 + IDX_DIM + IDX_HEADS + B_GATE_RANK)), w.dtype)
    return jnp.concatenate([q, k, v, qi, ki, wi, ab, pad, qb, kb, vb, rb], axis=1).astype(_BF16)


def _prep_odd_w_in(w):
    pad = jnp.zeros((w.shape[0], _OD_COLS - w.shape[1]), w.dtype)
    return jnp.concatenate([w, pad], axis=1).astype(_BF16)


def _lane_pad(v, n=V7X_LANES):
    return jnp.pad(v, (0, n - v.shape[0])).reshape(1, n)


def _trunk(x, past, prm):
    bsz, lq, d = x.shape
    t = bsz * lq
    xf = x.reshape(t, d)

    def ffn(xf, i, j):
        return _ffn_half(xf, prm['ffn_norm'][i, j], prm['ffn_w_gate'][i, j].astype(_BF16),
                         prm['ffn_w_up'][i, j].astype(_BF16), prm['ffn_w_down'][i, j].astype(_BF16))

    xf = ffn(xf, 0, 0)
    head_mean = jnp.kron(jnp.eye(A_HEADS, dtype=_F32),
                         jnp.full((A_HEAD_DIM, A_HEAD_DIM), 1.0 / A_HEAD_DIM, _F32)).astype(_BF16)
    q_gain = (jnp.tile(prm['ev_q_norm'][0], A_HEADS) * (A_HEAD_DIM ** -0.5)).reshape(1, A_W)
    k_gain = jnp.tile(prm['ev_k_norm'][0], A_HEADS).reshape(1, A_W)
    w2 = jnp.zeros((V7X_LANES, B_KW), _F32).at[
        _EV_AB_OFF - 256:_EV_AB_OFF - 256 + B_GATE_RANK].set(prm['ev_gate_w2'][0])
    qn, k_new, v_new, k16, v16, idx, qb, kb, vb, rb, la = _even_in_proj(
        xf, prm['mix_norm'][0], _prep_even_w_in(prm['ev_w_in'][0]), q_gain, k_gain, head_mean,
        w2, prm['ev_gate_b2'][0].reshape(1, B_KW))
    r3 = lambda a: a.reshape(bsz, lq, a.shape[-1])
    ki_new = r3(idx)[:, :, 256:256 + IDX_DIM]
    if past is None:
        past_len = 0
        k_all, v_all, ki_all = r3(k16), r3(v16), ki_new
        s0_gla = jnp.zeros((bsz, B_HEADS, B_KEY_DIM, B_VAL_DIM), _F32)
    else:
        past_len = past['k'].shape[2]
        k_all = jnp.concatenate([past['k'][0].reshape(bsz, past_len, A_W).astype(_BF16), r3(k16)], axis=1)
        v_all = jnp.concatenate([past['v'][0].reshape(bsz, past_len, A_W).astype(_BF16), r3(v16)], axis=1)
        ki_all = jnp.concatenate([past['ki'][0], ki_new], axis=1)
        s0_gla = past['gla'][0]
    o_a = _dsa_attention(r3(qn), r3(idx), k_all, v_all, ki_all, prm['rel_bias'], past_len)
    o_b, s_gla = _gla(r3(qb), r3(kb), r3(la), r3(vb), r3(rb), s0_gla, prm['ev_gla_norm'][0])
    w_out = prm['ev_w_out'][0].astype(_BF16)
    xf = _out_proj(xf, [o_a.reshape(t, A_W), o_b.reshape(t, B_VW)], [w_out[:A_W], w_out[A_W:]])
    xf = ffn(xf, 0, 1)

    xf = ffn(xf, 1, 0)
    a_log = _lane_pad(prm['od_a_log'][0])
    dt_bias = _lane_pad(prm['od_dt_bias'][0])
    qkv, z, gb = _odd_in_proj(xf, prm['mix_norm'][1], _prep_odd_w_in(prm['od_w_in'][0]), a_log, dt_bias)
    if past is None:
        conv_prev = jnp.zeros((bsz, CONV_WIDTH - 1, CONV_CH), _F32)
        s0_delta = jnp.zeros((bsz, C_HEADS, C_HEAD_DIM, C_HEAD_DIM), _F32)
    else:
        conv_prev, s0_delta = past['conv'][0], past['delta'][0]
    conv_prev8 = jnp.pad(conv_prev, ((0, 0), (V7X_SUBLANES - (CONV_WIDTH - 1), 0), (0, 0)))
    o_c, s_delta = _delta(r3(qkv), r3(z), r3(gb), prm['od_conv_w'][0], conv_prev8, s0_delta,
                          prm['od_norm'][0])
    conv_new = jnp.concatenate([conv_prev, r3(qkv)], axis=1)[:, -(CONV_WIDTH - 1):]
    xf = _out_proj(xf, [o_c.reshape(t, C_W)], [prm['od_w_out'][0].astype(_BF16)])
    xf = ffn(xf, 1, 1)

    y = xf.reshape(bsz, lq, d)
    k_out = r3(k_new).reshape(1, bsz, lq, A_HEADS, A_HEAD_DIM)
    v_out = r3(v_new).reshape(1, bsz, lq, A_HEADS, A_HEAD_DIM)
    return y, k_out, v_out, ki_new[None], s_gla[None], s_delta[None], conv_new[None]


def kernel(x_prompt, x_sample, cache_attn_k, cache_attn_v, cache_idx_k, state_gla, state_delta, state_conv,
           ffn_norm, ffn_w_gate, ffn_w_up, ffn_w_down, mix_norm, ev_w_in, ev_q_norm, ev_k_norm, rel_bias,
           ev_gate_w2, ev_gate_b2, ev_gla_norm, ev_w_out, od_w_in, od_conv_w, od_a_log, od_dt_bias, od_norm,
           od_w_out):
    prm = {'ffn_norm': ffn_norm, 'ffn_w_gate': ffn_w_gate, 'ffn_w_up': ffn_w_up, 'ffn_w_down': ffn_w_down,
           'mix_norm': mix_norm, 'ev_w_in': ev_w_in, 'ev_q_norm': ev_q_norm, 'ev_k_norm': ev_k_norm,
           'rel_bias': rel_bias, 'ev_gate_w2': ev_gate_w2, 'ev_gate_b2': ev_gate_b2, 'ev_gla_norm': ev_gla_norm,
           'ev_w_out': ev_w_out, 'od_w_in': od_w_in, 'od_conv_w': od_conv_w, 'od_a_log': od_a_log,
           'od_dt_bias': od_dt_bias, 'od_norm': od_norm, 'od_w_out': od_w_out}
    past = {'k': cache_attn_k, 'v': cache_attn_v, 'ki': cache_idx_k, 'gla': state_gla,
            'delta': state_delta, 'conv': state_conv}
    y_prompt, p_k, p_v, p_ki, p_gla, p_delta, p_conv = _trunk(x_prompt, None, prm)
    y_sample, s_k, s_v, s_ki, s_gla, s_delta, s_conv = _trunk(x_sample, past, prm)
    return (y_prompt, y_sample, p_k, p_v, p_ki, p_gla, p_delta, p_conv,
            s_k, s_v, s_ki, s_gla, s_delta, s_conv)
```

```python
import functools
import math

import jax
import jax.numpy as jnp
from jax import lax
from jax.experimental import pallas as pl
from jax.experimental.pallas import tpu as pltpu

D_MODEL = 1024
DEPTH = 2
CHUNK = 64
EPS = 1e-6

A_HEADS = 8
A_HEAD_DIM = 64
A_W = A_HEADS * A_HEAD_DIM
IDX_HEADS = 4
IDX_DIM = 64
TOPK_MAX = 256
N_BUCKETS = 32
MAX_DISTANCE = 128

B_HEADS = 4
B_KEY_DIM = 64
B_VAL_DIM = 128
B_KW = B_HEADS * B_KEY_DIM
B_VW = B_HEADS * B_VAL_DIM
B_GATE_RANK = 16
B_GATE_TAU = 16.0

C_HEADS = 8
C_HEAD_DIM = 128
C_W = C_HEADS * C_HEAD_DIM
CONV_WIDTH = 4
CONV_CH = 3 * C_W

D_FF = 2816

V7X_LANES = 128
V7X_SUBLANES = 8
V7X_VMEM_LIMIT_BYTES = 56 * 1024 * 1024

_F32 = jnp.float32
_BF16 = jnp.bfloat16
_HI = lax.Precision.HIGHEST
_NEG = -1e30
_NN = (((1,), (0,)), ((), ()))
_NT = (((1,), (1,)), ((), ()))
_TN = (((0,), (0,)), ((), ()))

_EV_Q, _EV_K, _EV_V = 0, 512, 1024
_EV_IDX = 1536
_EV_IDX_W = 384
_EV_WI_OFF = 256 + IDX_DIM
_EV_AB_OFF = _EV_WI_OFF + IDX_HEADS
_EV_QB, _EV_KB, _EV_VB, _EV_RB = 1920, 2176, 2432, 2944
_EV_COLS = 3456
_OD_QKV, _OD_Z, _OD_AB = 0, 3072, 4096
_OD_COLS = 4224


def _params(n_axes):
    return pltpu.CompilerParams(dimension_semantics=("arbitrary",) * n_axes,
                                vmem_limit_bytes=V7X_VMEM_LIMIT_BYTES)


def _rms(x, g):
    return x * lax.rsqrt(jnp.mean(x * x, axis=-1, keepdims=True) + EPS) * g


def _silu(x):
    return x * jax.nn.sigmoid(x)


def _dot(a, b, dims=_NN):
    return lax.dot_general(a.astype(_BF16), b.astype(_BF16), dims, preferred_element_type=_F32)


def _dot_hi(a, b, dims=_NN):
    return lax.dot_general(a, b, dims, precision=_HI, preferred_element_type=_F32)


def _full(shape):
    return pl.BlockSpec(shape, lambda *_: (0,) * len(shape))


_FFN_COLS = 256


def _ffn_kernel(*refs, n_proj):
    x_ref = refs[0]
    g_ref, wg_ref, wu_ref, wd_ref, o_ref, act_ref = refs[1 + 2 * n_proj:]
    x = x_ref[...]
    for i in range(n_proj):
        x = x + jnp.dot(refs[1 + i][...], refs[1 + n_proj + i][...], preferred_element_type=_F32)
    h = _rms(x, g_ref[...]).astype(_BF16)
    for c in range(wg_ref.shape[1] // _FFN_COLS):
        cols = slice(c * _FFN_COLS, (c + 1) * _FFN_COLS)
        gate = jnp.dot(h, wg_ref[:, cols], preferred_element_type=_F32)
        up = jnp.dot(h, wu_ref[:, cols], preferred_element_type=_F32)
        act_ref[:, cols] = (_silu(gate) * up).astype(_BF16)
    o_ref[...] = x + 0.5 * jnp.dot(act_ref[...], wd_ref[...], preferred_element_type=_F32)


def _ffn_half(x, g, wg, wu, wd, acts=(), proj_w=()):
    t, d = x.shape
    ff = wg.shape[1]
    tm = min(t, 512)
    resident = lambda shape: pl.BlockSpec(shape, lambda i: (0, 0), pipeline_mode=pl.Buffered(1))
    return pl.pallas_call(
        functools.partial(_ffn_kernel, n_proj=len(acts)),
        grid=(t // tm,),
        in_specs=([pl.BlockSpec((tm, d), lambda i: (i, 0))]
                  + [pl.BlockSpec((tm, a.shape[1]), lambda i: (i, 0)) for a in acts]
                  + [resident(w.shape) for w in proj_w]
                  + [resident((1, d)), resident((d, ff)), resident((d, ff)), resident((ff, d))]),
        out_specs=pl.BlockSpec((tm, d), lambda i: (i, 0)),
        out_shape=jax.ShapeDtypeStruct((t, d), _F32),
        scratch_shapes=[pltpu.VMEM((tm, ff), _BF16)],
        compiler_params=_params(1),
        name="ffn_half",
    )(x, *acts, *proj_w, g.reshape(1, d), wg, wu, wd)


def _even_in_kernel(x_ref, g_ref, w_ref, qg_ref, kg_ref, hm_ref, w2_ref, b2_ref,
                    qn_ref, k_ref, v_ref, k16_ref, v16_ref, idx_ref, qb_ref, kb_ref, vb_ref, rb_ref, la_ref):
    h = _rms(x_ref[...], g_ref[...]).astype(_BF16)

    def proj(lo, width):
        return jnp.dot(h, w_ref[:, lo:lo + width], preferred_element_type=_F32)

    def headnorm(t, gain):
        t2 = t * t
        hi = t2.astype(_BF16)
        lo = (t2 - hi.astype(_F32)).astype(_BF16)
        ms = (jnp.dot(hi, hm_ref[...], preferred_element_type=_F32)
              + jnp.dot(lo, hm_ref[...], preferred_element_type=_F32))
        return t * lax.rsqrt(ms + EPS) * gain

    qn_ref[...] = headnorm(proj(_EV_Q, A_W), qg_ref[...]).astype(_BF16)
    k = headnorm(proj(_EV_K, A_W), kg_ref[...])
    v = proj(_EV_V, A_W)
    k_ref[...] = k
    v_ref[...] = v
    k16_ref[...] = k.astype(_BF16)
    v16_ref[...] = v.astype(_BF16)
    idx = proj(_EV_IDX, _EV_IDX_W)
    idx_ref[...] = idx
    qb_ref[...] = proj(_EV_QB, B_KW) * (B_KEY_DIM ** -0.5)
    kb_ref[...] = proj(_EV_KB, B_KW)
    vb_ref[...] = proj(_EV_VB, B_VW)
    rb_ref[...] = proj(_EV_RB, B_VW)
    z = _dot_hi(idx[:, 256:384], w2_ref[...]) + b2_ref[...]
    la_ref[...] = (jnp.minimum(z, 0.0) - jnp.log1p(jnp.exp(-jnp.abs(z)))) * (1.0 / B_GATE_TAU)


def _even_in_proj(x, g, w, q_gain, k_gain, head_mean, w2, b2):
    t, d = x.shape
    tm = min(t, 512)
    row = lambda n: pl.BlockSpec((tm, n), lambda i: (i, 0))
    outs = [(A_W, _BF16), (A_W, _F32), (A_W, _F32), (A_W, _BF16), (A_W, _BF16), (_EV_IDX_W, _F32), (B_KW, _F32), (B_KW, _F32),
            (B_VW, _F32), (B_VW, _F32), (B_KW, _F32)]
    w_spec = pl.BlockSpec((d, _EV_COLS), lambda i: (0, 0), pipeline_mode=pl.Buffered(1))
    return pl.pallas_call(
        _even_in_kernel,
        grid=(t // tm,),
        in_specs=[row(d), _full((1, d)), w_spec, _full((1, A_W)), _full((1, A_W)),
                  _full((A_W, A_W)), _full((V7X_LANES, B_KW)), _full((1, B_KW))],
        out_specs=[row(n) for n, _ in outs],
        out_shape=[jax.ShapeDtypeStruct((t, n), dt) for n, dt in outs],
        compiler_params=_params(1),
        name="even_in_proj",
    )(x, g.reshape(1, d), w, q_gain, k_gain, head_mean, w2, b2)


def _dsa_kernel(rb_ref, qt_ref, qit_ref, wit_ref, k_ref, vt_ref, ki_ref, bkt_ref, o_ref,
                key_ref, mask_ref, nb_ref, ki3_ref, acc_ref, lt_ref, p_ref, *, past, tq, kc, topk, r0_off):
    t = pl.program_id(1)

    @pl.when((pl.program_id(0) == 0) & (t == 0))
    def _():
        bk = bkt_ref[...]
        for h in range(A_HEADS):
            acc = jnp.zeros(bk.shape, _F32)
            for bb in range(N_BUCKETS):
                acc = jnp.where(bk == bb, rb_ref[bb, h], acc)
            nb_ref[h] = acc - rb_ref[N_BUCKETS // 2 - 1, h]

    q0 = past + t * tq
    n_chunks = (q0 + tq + kc - 1) // kc
    q_limit = ((q0 + lax.broadcasted_iota(jnp.int32, (1, tq), 1)) // CHUNK + 1) * CHUNK
    k_iota = lax.broadcasted_iota(jnp.int32, (kc, tq), 0)

    def admissible(r0):
        return k_iota < (q_limit - r0)

    @pl.when(t == 0)
    def _():
        def split_body(j, carry):
            r0 = pl.multiple_of(j * kc, kc)
            x = ki_ref[0, pl.ds(r0, kc), :]
            hi = x.astype(_BF16)
            ki3_ref[pl.ds(r0, kc), 0:V7X_LANES] = hi
            ki3_ref[pl.ds(r0, kc), V7X_LANES:2 * V7X_LANES] = (x - hi.astype(_F32)).astype(_BF16)
            return carry
        lax.fori_loop(0, ki3_ref.shape[0] // kc, split_body, 0)

    qit = qit_ref[0]
    q_cols = []
    for h in range(IDX_HEADS):
        x = qit[h * IDX_DIM:(h + 1) * IDX_DIM, :]
        hi = x.astype(_BF16)
        lo = (x - hi.astype(_F32)).astype(_BF16)
        q_cols.append(jnp.concatenate([hi, lo, hi, jnp.zeros_like(hi)], axis=0))
    q3 = jnp.concatenate(q_cols, axis=1)
    wit = wit_ref[0]

    def score_body(j, carry):
        r0 = pl.multiple_of(j * kc, kc)
        d = jnp.dot(ki3_ref[pl.ds(r0, kc), :], q3, preferred_element_type=_F32)
        s = jnp.zeros((kc, tq), _F32)
        for h in range(IDX_HEADS):
            s = s + wit[h:h + 1, :] * jnp.maximum(d[:, h * tq:(h + 1) * tq], 0.0)
        s = jnp.where(s == 0.0, 0.0, s)
        s = jnp.where(admissible(r0), s, -jnp.inf)
        bits = pltpu.bitcast(s, jnp.int32)
        key = bits ^ ((bits >> 31) & jnp.int32(0x7FFFFFFF))
        key_ref[pl.ds(r0, kc), :] = key
        return carry

    lax.fori_loop(0, n_chunks, score_body, 0)

    int_min = jnp.int32(-2 ** 31)

    def count_ge(cs):
        def body(j, acc):
            r0 = pl.multiple_of(j * kc, kc)
            m = jnp.where(key_ref[pl.ds(r0, kc), :] >= cs, 1, 0)
            return acc + m.reshape(kc // V7X_SUBLANES, V7X_SUBLANES, tq).sum(axis=0)
        acc = lax.fori_loop(0, n_chunks, body, jnp.zeros((V7X_SUBLANES, tq), jnp.int32))
        return acc.sum(axis=0, keepdims=True)

    def bit_body(i, tu):
        cu = tu | jnp.left_shift(jnp.int32(1), 31 - i)
        return jnp.where(count_ge(cu ^ int_min) >= topk, cu, tu)

    tu = lax.fori_loop(0, 32, bit_body, jnp.zeros((1, tq), jnp.int32))
    ts = tu ^ int_min

    def count_keys(above_or_equal):
        def body(j, acc):
            r0 = pl.multiple_of(j * kc, kc)
            blk = key_ref[pl.ds(r0, kc), :]
            m = jnp.where(blk >= ts if above_or_equal else blk > ts, 1, 0)
            return acc + m.reshape(kc // V7X_SUBLANES, V7X_SUBLANES, tq).sum(axis=0)
        acc = lax.fori_loop(0, n_chunks, body, jnp.zeros((V7X_SUBLANES, tq), jnp.int32))
        return acc.sum(axis=0, keepdims=True)

    n_ge = count_keys(True)
    n_greater = count_keys(False)
    room = (topk - n_greater).astype(_F32)

    tri = (lax.broadcasted_iota(jnp.int32, (kc, kc), 0)
           >= lax.broadcasted_iota(jnp.int32, (kc, kc), 1)).astype(_BF16)

    def tie_mask_body(j, seen):
        r0 = pl.multiple_of(j * kc, kc)
        blk = key_ref[pl.ds(r0, kc), :]
        eq = blk == ts
        rank = jnp.dot(tri, jnp.where(eq, 1.0, 0.0).astype(_BF16), preferred_element_type=_F32) + seen
        val = jnp.where(blk > ts, 0.0, jnp.where(eq, jnp.where(rank <= room, 0.0, _NEG), _NEG))
        mask_ref[pl.ds(r0, kc), :] = jnp.where(admissible(r0), val, _NEG)
        return rank[kc - 1:kc, :]

    def plain_mask_body(j, carry):
        r0 = pl.multiple_of(j * kc, kc)
        val = jnp.where(key_ref[pl.ds(r0, kc), :] >= ts, 0.0, _NEG)
        mask_ref[pl.ds(r0, kc), :] = jnp.where(admissible(r0), val, _NEG)
        return carry

    lax.cond(jnp.max(n_ge) > topk,
             lambda: lax.fori_loop(0, n_chunks, tie_mask_body, jnp.zeros((1, tq), _F32)) * 0.0,
             lambda: lax.fori_loop(0, n_chunks, plain_mask_body, jnp.zeros((1, tq), _F32)))

    upper_rows = lax.broadcasted_iota(jnp.int32, (V7X_LANES, tq), 0) >= A_HEAD_DIM
    q_pairs = []
    for pr in range(A_HEADS // 2):
        blk = qt_ref[0, pr * V7X_LANES:(pr + 1) * V7X_LANES, :].astype(_F32)
        q_pairs.append(jnp.concatenate([jnp.where(upper_rows, 0.0, blk), jnp.where(upper_rows, blk, 0.0)],
                                       axis=1).astype(_BF16))
    acc_ref[...] = jnp.zeros_like(acc_ref)

    def attn_body(j, carry):
        ms, ls = carry
        r0 = pl.multiple_of(j * kc, kc)
        msk = mask_ref[pl.ds(r0, kc), :]
        off = pl.multiple_of(jnp.maximum(r0 - q0 + r0_off, 0), CHUNK)
        for pr in range(A_HEADS // 2):
            kblk = k_ref[0, pl.ds(r0, kc), pr * V7X_LANES:(pr + 1) * V7X_LANES]
            lt2 = jnp.dot(kblk, q_pairs[pr], preferred_element_type=_F32)
            for hh in range(2):
                h = 2 * pr + hh
                lt_ref[h] = lt2[:, hh * tq:(hh + 1) * tq] + msk + nb_ref[h, pl.ds(off, kc), :]
        new_ms, new_ls, alphas = [], [], []
        for h in range(A_HEADS):
            lt = lt_ref[h]
            m_new = jnp.maximum(ms[h], lt.max(axis=0, keepdims=True))
            p = jnp.exp(lt - m_new)
            alpha = jnp.exp(ms[h] - m_new)
            new_ms.append(m_new)
            new_ls.append(alpha * ls[h] + p.sum(axis=0, keepdims=True))
            alphas.append(alpha)
            p_ref[h] = p.astype(_BF16)
        for h in range(A_HEADS):
            rows = slice(h * A_HEAD_DIM, (h + 1) * A_HEAD_DIM)
            pv = jnp.dot(vt_ref[0, rows, pl.ds(r0, kc)], p_ref[h], preferred_element_type=_F32)
            acc_ref[rows, :] = alphas[h] * acc_ref[rows, :] + pv
        return tuple(new_ms), tuple(new_ls)

    init = (tuple(jnp.full((1, tq), _NEG, _F32) for _ in range(A_HEADS)),
            tuple(jnp.zeros((1, tq), _F32) for _ in range(A_HEADS)))
    _, ls = lax.fori_loop(0, n_chunks, attn_body, init)

    eye = (lax.broadcasted_iota(jnp.int32, (tq, tq), 0)
           == lax.broadcasted_iota(jnp.int32, (tq, tq), 1)).astype(_BF16)
    for pair in range(A_HEADS // 2):
        lanes = slice(pair * V7X_LANES, (pair + 1) * V7X_LANES)
        o_t = jnp.concatenate(
            [acc_ref[(2 * pair + hh) * A_HEAD_DIM:(2 * pair + hh + 1) * A_HEAD_DIM, :] / ls[2 * pair + hh]
             for hh in range(2)], axis=0).astype(_BF16)
        o_ref[0, :, lanes] = lax.dot_general(eye, o_t, _NT, preferred_element_type=_F32).astype(_BF16)


def _t5_bucket_table(rel):
    half = N_BUCKETS // 2
    max_exact = half // 2
    n = jnp.abs(rel)
    nf = jnp.maximum(n, 1).astype(jnp.float32)
    large = max_exact + (jnp.log(nf / max_exact) / math.log(MAX_DISTANCE / max_exact)
                         * (half - max_exact)).astype(jnp.int32)
    large = jnp.minimum(large, half - 1)
    return jnp.where(rel > 0, half, 0) + jnp.where(n < max_exact, n, large)


def _dsa_attention(qn, idx, k_all, v_all, ki_all, rel_bias, past):
    bsz, lq, _ = qn.shape
    nk = k_all.shape[1]
    topk = min(TOPK_MAX, nk // 4)
    tq = min(V7X_LANES, lq)
    kc = 256
    nk_pad = -(-nk // kc) * kc
    pad = ((0, 0), (0, nk_pad - nk), (0, 0))
    k16 = jnp.pad(k_all, pad)
    vt16 = jnp.swapaxes(jnp.pad(v_all, pad), 1, 2)
    ki2 = jnp.pad(jnp.concatenate([ki_all, ki_all], axis=-1), pad)
    wit = jnp.swapaxes(idx[:, :, _EV_WI_OFF:_EV_WI_OFF + V7X_SUBLANES], 1, 2)
    qt = jnp.swapaxes(qn, 1, 2)
    qit = jnp.swapaxes(idx[:, :, :IDX_HEADS * IDX_DIM], 1, 2)
    r0_off = kc + MAX_DISTANCE
    nbr = r0_off + kc + tq
    rel = (jnp.arange(nbr, dtype=jnp.int32)[:, None] - r0_off) - jnp.arange(tq, dtype=jnp.int32)[None, :]
    bkt = _t5_bucket_table(rel)
    kern = functools.partial(_dsa_kernel, past=past, tq=tq, kc=kc, topk=topk, r0_off=r0_off)
    return pl.pallas_call(
        kern,
        grid=(bsz, lq // tq),
        in_specs=[pl.BlockSpec(memory_space=pltpu.SMEM),
                  pl.BlockSpec((1, A_W, tq), lambda b, t: (b, 0, t)),
                  pl.BlockSpec((1, IDX_HEADS * IDX_DIM, tq), lambda b, t: (b, 0, t)),
                  pl.BlockSpec((1, V7X_SUBLANES, tq), lambda b, t: (b, 0, t)),
                  pl.BlockSpec((1, nk_pad, A_W), lambda b, t: (b, 0, 0)),
                  pl.BlockSpec((1, A_W, nk_pad), lambda b, t: (b, 0, 0)),
                  pl.BlockSpec((1, nk_pad, V7X_LANES), lambda b, t: (b, 0, 0)),
                  pl.BlockSpec((nbr, tq), lambda b, t: (0, 0))],
        out_specs=pl.BlockSpec((1, tq, A_W), lambda b, t: (b, t, 0)),
        out_shape=jax.ShapeDtypeStruct((bsz, lq, A_W), _BF16),
        scratch_shapes=[pltpu.VMEM((nk_pad, tq), jnp.int32), pltpu.VMEM((nk_pad, tq), _F32),
                        pltpu.VMEM((A_HEADS, nbr, tq), _F32),
                        pltpu.VMEM((nk_pad, 2 * V7X_LANES), _BF16), pltpu.VMEM((A_W, tq), _F32),
                        pltpu.VMEM((A_HEADS, kc, tq), _F32), pltpu.VMEM((A_HEADS, kc, tq), _BF16)],
        compiler_params=_params(2),
        name="dsa_attention",
    )(rel_bias, qt, qit, wit, k16, vt16, ki2, bkt)


def _split3_rows(x):
    hi = x.astype(_BF16)
    r1 = x - hi.astype(_F32)
    mid = r1.astype(_BF16)
    lo = (r1 - mid.astype(_F32)).astype(_BF16)
    return jnp.concatenate([hi, mid, lo], axis=0)


def _gla_kernel(q_ref, k_ref, g_ref, v_ref, r_ref, s0_ref, ng_ref, o_ref, s_out_ref, st_ref, *, n_sub):
    c = pl.program_id(1)
    causal = (lax.broadcasted_iota(jnp.int32, (CHUNK, CHUNK), 0)
              >= lax.broadcasted_iota(jnp.int32, (CHUNK, CHUNK), 1))
    causal3 = jnp.concatenate([causal.astype(_BF16)] * 3, axis=1)
    upper_half = lax.broadcasted_iota(jnp.int32, (CHUNK, V7X_LANES), 1) >= B_KEY_DIM
    diag = (lax.broadcasted_iota(jnp.int32, (V7X_LANES, V7X_LANES), 0)
            == lax.broadcasted_iota(jnp.int32, (V7X_LANES, V7X_LANES), 1))
    pairs = range(B_HEADS // 2)
    heads = range(B_HEADS)

    @pl.when(c == 0)
    def _():
        st_ref[...] = s0_ref[0]

    def pair_lanes(x, p):
        return x[:, p * V7X_LANES:(p + 1) * V7X_LANES]

    def head_half(x, h):
        keep = upper_half if h % 2 == 1 else jnp.logical_not(upper_half)
        return jnp.where(keep, pair_lanes(x, h // 2), 0.0)

    parts = []
    for s in range(n_sub):
        rows = slice(s * CHUNK, (s + 1) * CHUNK)
        b = jnp.dot(causal3, _split3_rows(g_ref[0, rows, :]), preferred_element_type=_F32)
        mid = CHUNK // 2
        b_mid = b[mid:mid + 1, :]
        b_last = b[CHUNK - 1:CHUNK, :]
        q = q_ref[0, rows, :]
        k = k_ref[0, rows, :]
        qe = q * jnp.exp(b)
        qm = q * jnp.exp(b - b_mid)
        km = k * jnp.exp(b_mid - b)
        kl = k * jnp.exp(b_last - b)
        vs = [v_ref[0, rows, h * B_VAL_DIM:(h + 1) * B_VAL_DIM] for h in heads]
        a = [jnp.where(causal, _dot(head_half(qm, h), pair_lanes(km, h // 2), _NT), 0.0) for h in heads]
        av = [_dot(a[h], vs[h]) for h in heads]
        upd = [_dot(head_half(kl, h), vs[h], _TN) for h in heads]
        dcol = [jnp.sum(jnp.where(diag, jnp.exp(pair_lanes(b_last, p)), 0.0), axis=1, keepdims=True)
                for p in pairs]
        parts.append((qe, av, upd, dcol))

    st = [st_ref[p * V7X_LANES:(p + 1) * V7X_LANES, :] for p in pairs]
    for s in range(n_sub):
        rows = slice(s * CHUNK, (s + 1) * CHUNK)
        qe, av, upd, dcol = parts[s]
        o = [_dot(head_half(qe, h), st[h // 2]) + av[h] for h in heads]
        st = [dcol[p] * st[p] + upd[2 * p] + upd[2 * p + 1] for p in pairs]
        for h in heads:
            cols = slice(h * B_VAL_DIM, (h + 1) * B_VAL_DIM)
            o_ref[0, rows, cols] = (_rms(o[h], ng_ref[...]) * _silu(r_ref[0, rows, cols])).astype(_BF16)
    for p in pairs:
        st_ref[p * V7X_LANES:(p + 1) * V7X_LANES, :] = st[p]

    @pl.when(c == pl.num_programs(1) - 1)
    def _():
        s_out_ref[0] = st_ref[...]


def _gla(qb, kb, la, vb, rb, s0, norm_g):
    bsz, lq, _ = qb.shape
    n_sub = min(4, lq // CHUNK)
    tok = lambda n: pl.BlockSpec((1, n_sub * CHUNK, n), lambda b, c: (b, c, 0))
    state = pl.BlockSpec((1, B_KW, B_VAL_DIM), lambda b, c: (b, 0, 0))
    o, s_new = pl.pallas_call(
        functools.partial(_gla_kernel, n_sub=n_sub),
        grid=(bsz, lq // (n_sub * CHUNK)),
        in_specs=[tok(B_KW), tok(B_KW), tok(B_KW), tok(B_VW), tok(B_VW), state,
                  pl.BlockSpec((1, B_VAL_DIM), lambda b, c: (0, 0))],
        out_specs=[tok(B_VW), state],
        out_shape=[jax.ShapeDtypeStruct((bsz, lq, B_VW), _BF16),
                   jax.ShapeDtypeStruct((bsz, B_KW, B_VAL_DIM), _F32)],
        scratch_shapes=[pltpu.VMEM((B_KW, B_VAL_DIM), _F32)],
        compiler_params=_params(2),
        name="gla",
    )(qb, kb, la, vb, rb, s0.reshape(bsz, B_KW, B_VAL_DIM), norm_g.reshape(1, B_VAL_DIM))
    return o, s_new.reshape(s0.shape)


def _odd_in_kernel(x_ref, g_ref, w_ref, alog_ref, dt_ref, cw_ref, cprev_ref, act_ref, z_ref, gb_ref, tail_ref,
                   xbuf_ref, *, seg, tiles_per_batch):
    halo = V7X_SUBLANES
    h = _rms(x_ref[...], g_ref[...]).astype(_BF16)
    z_ref[...] = jnp.dot(h, w_ref[:, _OD_Z:_OD_Z + C_W], preferred_element_type=_F32)
    ab = jnp.dot(h, w_ref[:, _OD_AB:_OD_AB + V7X_LANES], preferred_element_type=_F32)
    xa = ab + dt_ref[...]
    g = -jnp.exp(alog_ref[...]) * (jnp.maximum(xa, 0.0) + jnp.log1p(jnp.exp(-jnp.abs(xa))))
    lane = lax.broadcasted_iota(jnp.int32, ab.shape, 1)
    gb_ref[...] = jnp.where(lane < C_HEADS, g, jax.nn.sigmoid(ab))

    for s in range(x_ref.shape[0] // seg):
        rows = slice(s * seg, (s + 1) * seg)
        if tiles_per_batch > 1:
            @pl.when(pl.program_id(0) % tiles_per_batch == 0)
            def _():
                xbuf_ref[0:halo, :] = cprev_ref[0]
        else:
            xbuf_ref[0:halo, :] = cprev_ref[s]
        xbuf_ref[halo:halo + seg, :] = jnp.dot(h[rows], w_ref[:, _OD_QKV:_OD_QKV + CONV_CH],
                                               preferred_element_type=_F32)
        conv = xbuf_ref[halo:halo + seg, :] * cw_ref[CONV_WIDTH - 1:CONV_WIDTH, :]
        for j in range(CONV_WIDTH - 1):
            sh = CONV_WIDTH - 1 - j
            conv = conv + xbuf_ref[halo - sh:halo - sh + seg, :] * cw_ref[j:j + 1, :]
        tail = xbuf_ref[seg:seg + halo, :]
        xbuf_ref[0:halo, :] = tail
        tail_ref[s] = tail
        act = _silu(conv)
        for hd in range(C_HEADS):
            for part, scale in ((0, C_HEAD_DIM ** -0.5), (1, 1.0)):
                cols = slice(part * C_W + hd * C_HEAD_DIM, part * C_W + (hd + 1) * C_HEAD_DIM)
                a = act[:, cols]
                act_ref[rows, cols] = a * (lax.rsqrt(jnp.sum(a * a, axis=-1, keepdims=True) + EPS) * scale)
        act_ref[rows, 2 * C_W:] = act[:, 2 * C_W:]


def _odd_in_proj(x, g, w, a_log, dt_bias, conv_w, conv_prev8, lq):
    t, d = x.shape
    tm = min(t, 256)
    seg = min(tm, lq)
    batches_per_tile = tm // seg
    tiles_per_batch = lq // seg
    row = lambda n: pl.BlockSpec((tm, n), lambda i: (i, 0))
    per_batch = pl.BlockSpec((batches_per_tile, V7X_SUBLANES, CONV_CH), lambda i: (i // tiles_per_batch, 0, 0))
    outs = [CONV_CH, C_W, V7X_LANES]
    return pl.pallas_call(
        functools.partial(_odd_in_kernel, seg=seg, tiles_per_batch=tiles_per_batch),
        grid=(t // tm,),
        in_specs=[row(d), _full((1, d)), _full((d, _OD_COLS)), _full((1, V7X_LANES)), _full((1, V7X_LANES)),
                  _full((CONV_WIDTH, CONV_CH)), per_batch],
        out_specs=[row(n) for n in outs] + [per_batch],
        out_shape=([jax.ShapeDtypeStruct((t, n), _F32) for n in outs]
                   + [jax.ShapeDtypeStruct((t // lq, V7X_SUBLANES, CONV_CH), _F32)]),
        scratch_shapes=[pltpu.VMEM((seg + V7X_SUBLANES, CONV_CH), _F32)],
        compiler_params=_params(1),
        name="odd_in_proj",
    )(x, g.reshape(1, d), w, a_log, dt_bias, conv_w, conv_prev8)


def _delta_kernel(act_ref, z_ref, gb_ref, s0_ref, ng_ref, o_ref, s_out_ref, st_ref, *, n_sub):
    c = pl.program_id(1)

    @pl.when(c == 0)
    def _():
        st_ref[...] = s0_ref[0]

    t_i = lax.broadcasted_iota(jnp.int32, (CHUNK, V7X_LANES), 0)
    s_i = lax.broadcasted_iota(jnp.int32, (CHUNK, V7X_LANES), 1)
    lower_half = s_i < CHUNK
    s_mod = jnp.where(lower_half, s_i, s_i - CHUNK)
    incl = t_i >= s_mod
    strict = t_i > s_mod
    eye = (t_i == s_mod).astype(_F32)
    incl64 = (lax.broadcasted_iota(jnp.int32, (CHUNK, CHUNK), 0)
              >= lax.broadcasted_iota(jnp.int32, (CHUNK, CHUNK), 1)).astype(_F32)
    eye128 = (lax.broadcasted_iota(jnp.int32, (V7X_LANES, V7X_LANES), 0)
              == lax.broadcasted_iota(jnp.int32, (V7X_LANES, V7X_LANES), 1)).astype(_F32)

    def split3(x):
        hi16 = x.astype(_BF16)
        hi = hi16.astype(_F32)
        lo = x - hi
        lhs = jnp.concatenate([jnp.where(lower_half, hi, lo).astype(_BF16), hi16], axis=1)
        lo16 = lo.astype(_BF16)
        rhs = jnp.concatenate([hi16, hi16, lo16, jnp.zeros_like(lo16)], axis=0)
        return lhs, rhs

    def mm(lhs, rhs):
        return jnp.dot(lhs, rhs, preferred_element_type=_F32)

    heads = range(C_HEADS)
    units = [(s, h) for s in range(n_sub) for h in heads]
    g_lasts = []
    qs, ks, kbs, decays, rhss, qgs, kdecs = [], [], [], [], [], [], []
    for s in range(n_sub):
        rows = slice(s * CHUNK, (s + 1) * CHUNK)
        gb = gb_ref[0, rows, :]
        gcum_all = _dot_hi(incl64, gb)
        gcum_rows = _dot_hi(eye128, jnp.concatenate([gcum_all, gcum_all], axis=0), _NT)
        g_last = gcum_all[CHUNK - 1:CHUNK, :]
        g_lasts.append(g_last)
        for h in heads:
            q = act_ref[0, rows, h * C_HEAD_DIM:(h + 1) * C_HEAD_DIM]
            k = act_ref[0, rows, C_W + h * C_HEAD_DIM:C_W + (h + 1) * C_HEAD_DIM]
            v = act_ref[0, rows, 2 * C_W + h * C_HEAD_DIM:2 * C_W + (h + 1) * C_HEAD_DIM]
            beta = gb[:, C_HEADS + h:C_HEADS + h + 1]
            gcum = gcum_all[:, h:h + 1]
            kb = k * beta
            qs.append(q)
            ks.append(k)
            kbs.append(kb)
            decays.append(jnp.exp(jnp.where(incl, gcum - gcum_rows[h:h + 1, :], _NEG)))
            rhss.append(jnp.concatenate([v * beta, kb * jnp.exp(gcum)], axis=-1))
            qgs.append(q * jnp.exp(gcum))
            kdecs.append(k * jnp.exp(g_last[:, h:h + 1] - gcum))

    n_units = range(len(units))
    lowers, attns = [], []
    for u in n_units:
        prod = _dot(jnp.concatenate([kbs[u], qs[u]], axis=0),
                    jnp.concatenate([ks[u], ks[u]], axis=0), _NT)
        lowers.append(jnp.where(strict, prod[:CHUNK] * decays[u], 0.0))
        attns.append(jnp.where(incl, prod[CHUNK:] * decays[u], 0.0))

    pws = [-lowers[u] for u in n_units]
    invs = [eye + pws[u] for u in n_units]
    pw_l, pw_r = zip(*[split3(pws[u]) for u in n_units])
    pws = [mm(pw_l[u], pw_r[u]) for u in n_units]
    for _ in range(4):
        pw_l, pw_r = zip(*[split3(pws[u]) for u in n_units])
        both = [mm(jnp.concatenate([split3(invs[u])[0], pw_l[u]], axis=0), pw_r[u]) for u in n_units]
        invs = [invs[u] + both[u][:CHUNK] for u in n_units]
        pws = [both[u][CHUNK:] for u in n_units]
    invs = [invs[u] + mm(split3(invs[u])[0], split3(pws[u])[1]) for u in n_units]
    sols = []
    for u in n_units:
        r_hi16 = rhss[u].astype(_BF16)
        r_lo16 = (rhss[u] - r_hi16.astype(_F32)).astype(_BF16)
        sols.append(mm(split3(invs[u])[0], jnp.concatenate([r_hi16, r_hi16, r_lo16, jnp.zeros_like(r_lo16)], axis=0)))

    st = [st_ref[h] for h in heads]
    for u, (s, h) in enumerate(units):
        rows = slice(s * CHUNK, (s + 1) * CHUNK)
        w_u = sols[u][:, :C_HEAD_DIM]
        w_w = sols[u][:, C_HEAD_DIM:]
        ws = _dot(jnp.concatenate([w_w, qgs[u]], axis=0), st[h])
        v_new = w_u - ws[:CHUNK]
        o = ws[CHUNK:] + _dot(attns[u][:, :CHUNK], v_new)
        st[h] = jnp.exp(g_lasts[s][:, h:h + 1]) * st[h] + _dot(kdecs[u], v_new, _TN)
        cols = slice(h * C_HEAD_DIM, (h + 1) * C_HEAD_DIM)
        o_ref[0, rows, cols] = (_rms(o, ng_ref[...]) * _silu(z_ref[0, rows, cols])).astype(_BF16)
    for h in heads:
        st_ref[h] = st[h]

    @pl.when(c == pl.num_programs(1) - 1)
    def _():
        s_out_ref[0] = st_ref[...]


def _delta(act, z, gb, s0, norm_g):
    bsz, lq, _ = act.shape
    n_sub = min(2, lq // CHUNK)
    rows_all = n_sub * CHUNK
    tok = lambda n: pl.BlockSpec((1, rows_all, n), lambda b, c: (b, c, 0))
    state = pl.BlockSpec((1, C_HEADS, C_HEAD_DIM, C_HEAD_DIM), lambda b, c: (b, 0, 0, 0))
    return pl.pallas_call(
        functools.partial(_delta_kernel, n_sub=n_sub),
        grid=(bsz, lq // rows_all),
        in_specs=[tok(CONV_CH), tok(C_W), tok(V7X_LANES), state,
                  pl.BlockSpec((1, C_HEAD_DIM), lambda b, c: (0, 0))],
        out_specs=[tok(C_W), state],
        out_shape=[jax.ShapeDtypeStruct((bsz, lq, C_W), _BF16),
                   jax.ShapeDtypeStruct(s0.shape, _F32)],
        scratch_shapes=[pltpu.VMEM((C_HEADS, C_HEAD_DIM, C_HEAD_DIM), _F32)],
        compiler_params=_params(2),
        name="gated_delta",
    )(act, z, gb, s0, norm_g.reshape(1, C_HEAD_DIM))


def _prep_even_w_in(w):
    sizes = (A_W, A_W, A_W, IDX_HEADS * IDX_DIM, IDX_DIM, IDX_HEADS, B_KW, B_KW, B_VW, B_VW, B_GATE_RANK)
    offs = [0]
    for s in sizes:
        offs.append(offs[-1] + s)
    part = lambda i: w[:, offs[i]:offs[i + 1]]
    q, k, v, qi, ki, wi, qb, kb, vb, rb, ab = [part(i) for i in range(len(sizes))]
    pad = jnp.zeros((w.shape[0], _EV_IDX_W - (256 + IDX_DIM + IDX_HEADS + B_GATE_RANK)), w.dtype)
    return jnp.concatenate([q, k, v, qi, ki, wi, ab, pad, qb, kb, vb, rb], axis=1).astype(_BF16)


def _prep_odd_w_in(w):
    pad = jnp.zeros((w.shape[0], _OD_COLS - w.shape[1]), w.dtype)
    return jnp.concatenate([w, pad], axis=1).astype(_BF16)


def _lane_pad(v, n=V7X_LANES):
    return jnp.pad(v, (0, n - v.shape[0])).reshape(1, n)


def _trunk(x, past, prm):
    bsz, lq, d = x.shape
    t = bsz * lq
    xf = x.reshape(t, d)

    def ffn(xf, i, j, acts=(), proj_w=()):
        return _ffn_half(xf, prm['ffn_norm'][i, j], prm['ffn_w_gate'][i, j].astype(_BF16),
                         prm['ffn_w_up'][i, j].astype(_BF16), prm['ffn_w_down'][i, j].astype(_BF16),
                         acts, proj_w)

    xf = ffn(xf, 0, 0)
    head_mean = jnp.kron(jnp.eye(A_HEADS, dtype=_F32),
                         jnp.full((A_HEAD_DIM, A_HEAD_DIM), 1.0 / A_HEAD_DIM, _F32)).astype(_BF16)
    q_gain = (jnp.tile(prm['ev_q_norm'][0], A_HEADS) * (A_HEAD_DIM ** -0.5)).reshape(1, A_W)
    k_gain = jnp.tile(prm['ev_k_norm'][0], A_HEADS).reshape(1, A_W)
    w2 = jnp.zeros((V7X_LANES, B_KW), _F32).at[
        _EV_AB_OFF - 256:_EV_AB_OFF - 256 + B_GATE_RANK].set(prm['ev_gate_w2'][0])
    qn, k_new, v_new, k16, v16, idx, qb, kb, vb, rb, la = _even_in_proj(
        xf, prm['mix_norm'][0], _prep_even_w_in(prm['ev_w_in'][0]), q_gain, k_gain, head_mean,
        w2, prm['ev_gate_b2'][0].reshape(1, B_KW))
    r3 = lambda a: a.reshape(bsz, lq, a.shape[-1])
    ki_new = r3(idx)[:, :, 256:256 + IDX_DIM]
    if past is None:
        past_len = 0
        k_all, v_all, ki_all = r3(k16), r3(v16), ki_new
        s0_gla = jnp.zeros((bsz, B_HEADS, B_KEY_DIM, B_VAL_DIM), _F32)
    else:
        past_len = past['k'].shape[2]
        k_all = jnp.concatenate([past['k'][0].reshape(bsz, past_len, A_W).astype(_BF16), r3(k16)], axis=1)
        v_all = jnp.concatenate([past['v'][0].reshape(bsz, past_len, A_W).astype(_BF16), r3(v16)], axis=1)
        ki_all = jnp.concatenate([past['ki'][0], ki_new], axis=1)
        s0_gla = past['gla'][0]
    o_a = _dsa_attention(r3(qn), r3(idx), k_all, v_all, ki_all, prm['rel_bias'], past_len)
    o_b, s_gla = _gla(r3(qb), r3(kb), r3(la), r3(vb), r3(rb), s0_gla, prm['ev_gla_norm'][0])
    w_out = prm['ev_w_out'][0].astype(_BF16)
    xf = ffn(xf, 0, 1, [o_a.reshape(t, A_W), o_b.reshape(t, B_VW)], [w_out[:A_W], w_out[A_W:]])

    xf = ffn(xf, 1, 0)
    a_log = _lane_pad(prm['od_a_log'][0])
    dt_bias = _lane_pad(prm['od_dt_bias'][0])
    if past is None:
        conv_prev = jnp.zeros((bsz, CONV_WIDTH - 1, CONV_CH), _F32)
        s0_delta = jnp.zeros((bsz, C_HEADS, C_HEAD_DIM, C_HEAD_DIM), _F32)
    else:
        conv_prev, s0_delta = past['conv'][0], past['delta'][0]
    conv_prev8 = jnp.pad(conv_prev, ((0, 0), (V7X_SUBLANES - (CONV_WIDTH - 1), 0), (0, 0)))
    act, z, gb, tail = _odd_in_proj(xf, prm['mix_norm'][1], _prep_odd_w_in(prm['od_w_in'][0]), a_log, dt_bias,
                                    prm['od_conv_w'][0], conv_prev8, lq)
    o_c, s_delta = _delta(r3(act), r3(z), r3(gb), s0_delta, prm['od_norm'][0])
    conv_new = tail[:, V7X_SUBLANES - (CONV_WIDTH - 1):]
    xf = ffn(xf, 1, 1, [o_c.reshape(t, C_W)], [prm['od_w_out'][0].astype(_BF16)])

    y = xf.reshape(bsz, lq, d)
    k_out = r3(k_new).reshape(1, bsz, lq, A_HEADS, A_HEAD_DIM)
    v_out = r3(v_new).reshape(1, bsz, lq, A_HEADS, A_HEAD_DIM)
    return y, k_out, v_out, ki_new[None], s_gla[None], s_delta[None], conv_new[None]


def kernel(x_prompt, x_sample, cache_attn_k, cache_attn_v, cache_idx_k, state_gla, state_delta, state_conv,
           ffn_norm, ffn_w_gate, ffn_w_up, ffn_w_down, mix_norm, ev_w_in, ev_q_norm, ev_k_norm, rel_bias,
           ev_gate_w2, ev_gate_b2, ev_gla_norm, ev_w_out, od_w_in, od_conv_w, od_a_log, od_dt_bias, od_norm,
           od_w_out):
    prm = {'ffn_norm': ffn_norm, 'ffn_w_gate': ffn_w_gate, 'ffn_w_up': ffn_w_up, 'ffn_w_down': ffn_w_down,
           'mix_norm': mix_norm, 'ev_w_in': ev_w_in, 'ev_q_norm': ev_q_norm, 'ev_k_norm': ev_k_norm,
           'rel_bias': rel_bias, 'ev_gate_w2': ev_gate_w2, 'ev_gate_b2': ev_gate_b2, 'ev_gla_norm': ev_gla_norm,
           'ev_w_out': ev_w_out, 'od_w_in': od_w_in, 'od_conv_w': od_conv_w, 'od_a_log': od_a_log,
           'od_dt_bias': od_dt_bias, 'od_norm': od_norm, 'od_w_out': od_w_out}
    past = {'k': cache_attn_k, 'v': cache_attn_v, 'ki': cache_idx_k, 'gla': state_gla,
            'delta': state_delta, 'conv': state_conv}
    y_prompt, p_k, p_v, p_ki, p_gla, p_delta, p_conv = _trunk(x_prompt, None, prm)
    y_sample, s_k, s_v, s_ki, s_gla, s_delta, s_conv = _trunk(x_sample, past, prm)
    return (y_prompt, y_sample, p_k, p_v, p_ki, p_gla, p_delta, p_conv,
            s_k, s_v, s_ki, s_gla, s_delta, s_conv)
```

```python
import functools
import math

import jax
import jax.numpy as jnp
from jax import lax
from jax.experimental import pallas as pl
from jax.experimental.pallas import tpu as pltpu

D_MODEL = 1024
DEPTH = 2
CHUNK = 64
EPS = 1e-6

A_HEADS = 8
A_HEAD_DIM = 64
A_W = A_HEADS * A_HEAD_DIM
IDX_HEADS = 4
IDX_DIM = 64
TOPK_MAX = 256
N_BUCKETS = 32
MAX_DISTANCE = 128

B_HEADS = 4
B_KEY_DIM = 64
B_VAL_DIM = 128
B_KW = B_HEADS * B_KEY_DIM
B_VW = B_HEADS * B_VAL_DIM
B_GATE_RANK = 16
B_GATE_TAU = 16.0

C_HEADS = 8
C_HEAD_DIM = 128
C_W = C_HEADS * C_HEAD_DIM
CONV_WIDTH = 4
CONV_CH = 3 * C_W

D_FF = 2816

V7X_LANES = 128
V7X_SUBLANES = 8
V7X_VMEM_LIMIT_BYTES = 56 * 1024 * 1024

_F32 = jnp.float32
_BF16 = jnp.bfloat16
_HI = lax.Precision.HIGHEST
_NEG = -1e30
_NN = (((1,), (0,)), ((), ()))
_NT = (((1,), (1,)), ((), ()))
_TN = (((0,), (0,)), ((), ()))

_EV_Q, _EV_K, _EV_V = 0, 512, 1024
_EV_IDX = 1536
_EV_IDX_W = 384
_EV_WI_OFF = 256 + IDX_DIM
_EV_AB_OFF = _EV_WI_OFF + IDX_HEADS
_EV_QB, _EV_KB, _EV_VB, _EV_RB = 1920, 2176, 2432, 2944
_EV_COLS = 3456
_OD_QKV, _OD_Z, _OD_AB = 0, 3072, 4096
_OD_COLS = 4224


def _params(n_axes):
    return pltpu.CompilerParams(dimension_semantics=("arbitrary",) * n_axes,
                                vmem_limit_bytes=V7X_VMEM_LIMIT_BYTES)


def _rms(x, g):
    return x * lax.rsqrt(jnp.mean(x * x, axis=-1, keepdims=True) + EPS) * g


def _silu(x):
    return x * jax.nn.sigmoid(x)


def _dot(a, b, dims=_NN):
    return lax.dot_general(a.astype(_BF16), b.astype(_BF16), dims, preferred_element_type=_F32)


def _dot_hi(a, b, dims=_NN):
    return lax.dot_general(a, b, dims, precision=_HI, preferred_element_type=_F32)


def _full(shape):
    return pl.BlockSpec(shape, lambda *_: (0,) * len(shape))


_FFN_COLS = 256


def _ffn_kernel(*refs, n_proj):
    x_ref = refs[0]
    g_ref, wg_ref, wu_ref, wd_ref, o_ref, act_ref = refs[1 + 2 * n_proj:]
    x = x_ref[...]
    for i in range(n_proj):
        x = x + jnp.dot(refs[1 + i][...], refs[1 + n_proj + i][...], preferred_element_type=_F32)
    h = _rms(x, g_ref[...]).astype(_BF16)
    for c in range(wg_ref.shape[1] // _FFN_COLS):
        cols = slice(c * _FFN_COLS, (c + 1) * _FFN_COLS)
        gate = jnp.dot(h, wg_ref[:, cols], preferred_element_type=_F32)
        up = jnp.dot(h, wu_ref[:, cols], preferred_element_type=_F32)
        act_ref[:, cols] = (_silu(gate) * up).astype(_BF16)
    o_ref[...] = x + 0.5 * jnp.dot(act_ref[...], wd_ref[...], preferred_element_type=_F32)


def _ffn_half(x, g, wg, wu, wd, acts=(), proj_w=()):
    t, d = x.shape
    ff = wg.shape[1]
    tm = min(t, 512)
    resident = lambda shape: pl.BlockSpec(shape, lambda i: (0, 0), pipeline_mode=pl.Buffered(1))
    return pl.pallas_call(
        functools.partial(_ffn_kernel, n_proj=len(acts)),
        grid=(t // tm,),
        in_specs=([pl.BlockSpec((tm, d), lambda i: (i, 0))]
                  + [pl.BlockSpec((tm, a.shape[1]), lambda i: (i, 0)) for a in acts]
                  + [resident(w.shape) for w in proj_w]
                  + [resident((1, d)), resident((d, ff)), resident((d, ff)), resident((ff, d))]),
        out_specs=pl.BlockSpec((tm, d), lambda i: (i, 0)),
        out_shape=jax.ShapeDtypeStruct((t, d), _F32),
        scratch_shapes=[pltpu.VMEM((tm, ff), _BF16)],
        compiler_params=_params(1),
        name="ffn_half",
    )(x, *acts, *proj_w, g.reshape(1, d), wg, wu, wd)


def _even_in_kernel(x_ref, g_ref, w_ref, qg_ref, kg_ref, hm_ref, w2_ref, b2_ref,
                    qn_ref, k_ref, v_ref, k16_ref, v16_ref, idx_ref, qb_ref, kb_ref, vb_ref, rb_ref, la_ref):
    h = _rms(x_ref[...], g_ref[...]).astype(_BF16)

    def proj(lo, width):
        return jnp.dot(h, w_ref[:, lo:lo + width], preferred_element_type=_F32)

    def headnorm(t, gain):
        t2 = t * t
        hi = t2.astype(_BF16)
        lo = (t2 - hi.astype(_F32)).astype(_BF16)
        ms = (jnp.dot(hi, hm_ref[...], preferred_element_type=_F32)
              + jnp.dot(lo, hm_ref[...], preferred_element_type=_F32))
        return t * lax.rsqrt(ms + EPS) * gain

    qn_ref[...] = headnorm(proj(_EV_Q, A_W), qg_ref[...]).astype(_BF16)
    k = headnorm(proj(_EV_K, A_W), kg_ref[...])
    v = proj(_EV_V, A_W)
    k_ref[...] = k
    v_ref[...] = v
    k16_ref[...] = k.astype(_BF16)
    v16_ref[...] = v.astype(_BF16)
    idx = proj(_EV_IDX, _EV_IDX_W)
    idx_ref[...] = idx
    qb_ref[...] = proj(_EV_QB, B_KW) * (B_KEY_DIM ** -0.5)
    kb_ref[...] = proj(_EV_KB, B_KW)
    vb_ref[...] = proj(_EV_VB, B_VW)
    rb_ref[...] = proj(_EV_RB, B_VW)
    z = _dot_hi(idx[:, 256:384], w2_ref[...]) + b2_ref[...]
    la_ref[...] = (jnp.minimum(z, 0.0) - jnp.log1p(jnp.exp(-jnp.abs(z)))) * (1.0 / B_GATE_TAU)


def _even_in_proj(x, g, w, q_gain, k_gain, head_mean, w2, b2):
    t, d = x.shape
    tm = min(t, 512)
    row = lambda n: pl.BlockSpec((tm, n), lambda i: (i, 0))
    outs = [(A_W, _BF16), (A_W, _F32), (A_W, _F32), (A_W, _BF16), (A_W, _BF16), (_EV_IDX_W, _F32), (B_KW, _F32), (B_KW, _F32),
            (B_VW, _F32), (B_VW, _F32), (B_KW, _F32)]
    w_spec = pl.BlockSpec((d, _EV_COLS), lambda i: (0, 0), pipeline_mode=pl.Buffered(1))
    return pl.pallas_call(
        _even_in_kernel,
        grid=(t // tm,),
        in_specs=[row(d), _full((1, d)), w_spec, _full((1, A_W)), _full((1, A_W)),
                  _full((A_W, A_W)), _full((V7X_LANES, B_KW)), _full((1, B_KW))],
        out_specs=[row(n) for n, _ in outs],
        out_shape=[jax.ShapeDtypeStruct((t, n), dt) for n, dt in outs],
        compiler_params=_params(1),
        name="even_in_proj",
    )(x, g.reshape(1, d), w, q_gain, k_gain, head_mean, w2, b2)


def _dsa_kernel(rb_ref, qt_ref, qit_ref, wit_ref, k_ref, vt_ref, ki_ref, bkt_ref, o_ref,
                key_ref, mask_ref, nb_ref, ki3_ref, acc_ref, lt_ref, p_ref, *, past, tq, kc, topk, r0_off, chunk_counts):
    t = pl.program_id(1)

    @pl.when((pl.program_id(0) == 0) & (t == 0))
    def _():
        bk = bkt_ref[...]
        for h in range(A_HEADS):
            acc = jnp.zeros(bk.shape, _F32)
            for bb in range(N_BUCKETS):
                acc = jnp.where(bk == bb, rb_ref[bb, h], acc)
            nb_ref[h] = acc - rb_ref[N_BUCKETS // 2 - 1, h]

    q0 = past + t * tq
    n_chunks = (q0 + tq + kc - 1) // kc
    q_limit = ((q0 + lax.broadcasted_iota(jnp.int32, (1, tq), 1)) // CHUNK + 1) * CHUNK
    k_iota = lax.broadcasted_iota(jnp.int32, (kc, tq), 0)

    def admissible(r0):
        return k_iota < (q_limit - r0)

    @pl.when(t == 0)
    def _():
        def split_body(j, carry):
            r0 = pl.multiple_of(j * kc, kc)
            x = ki_ref[0, pl.ds(r0, kc), :]
            hi = x.astype(_BF16)
            ki3_ref[pl.ds(r0, kc), 0:V7X_LANES] = hi
            ki3_ref[pl.ds(r0, kc), V7X_LANES:2 * V7X_LANES] = (x - hi.astype(_F32)).astype(_BF16)
            return carry
        lax.fori_loop(0, ki3_ref.shape[0] // kc, split_body, 0)

    qit = qit_ref[0]
    q_cols = []
    for h in range(IDX_HEADS):
        x = qit[h * IDX_DIM:(h + 1) * IDX_DIM, :]
        hi = x.astype(_BF16)
        lo = (x - hi.astype(_F32)).astype(_BF16)
        q_cols.append(jnp.concatenate([hi, lo, hi, jnp.zeros_like(hi)], axis=0))
    q3 = jnp.concatenate(q_cols, axis=1)
    wit = wit_ref[0]

    def score_body(j, carry):
        r0 = pl.multiple_of(j * kc, kc)
        d = jnp.dot(ki3_ref[pl.ds(r0, kc), :], q3, preferred_element_type=_F32)
        s = jnp.zeros((kc, tq), _F32)
        for h in range(IDX_HEADS):
            s = s + wit[h:h + 1, :] * jnp.maximum(d[:, h * tq:(h + 1) * tq], 0.0)
        s = jnp.where(s == 0.0, 0.0, s)
        s = jnp.where(admissible(r0), s, -jnp.inf)
        bits = pltpu.bitcast(s, jnp.int32)
        key = bits ^ ((bits >> 31) & jnp.int32(0x7FFFFFFF))
        key_ref[pl.ds(r0, kc), :] = key
        return carry

    lax.fori_loop(0, n_chunks, score_body, 0)

    int_min = jnp.int32(-2 ** 31)

    def count_keys(pred, n):
        parts = []
        for j in range(n):
            m = jnp.where(pred(key_ref[j * kc:(j + 1) * kc, :]), 1, 0)
            parts.append(m.reshape(kc // V7X_SUBLANES, V7X_SUBLANES, tq).sum(axis=0))
        while len(parts) > 1:
            parts = [sum(parts[i:i + 2]) for i in range(0, len(parts), 2)]
        return parts[0].sum(axis=0, keepdims=True)

    def radix_select(n):
        def bit_body(i, carry):
            tu, n_ge = carry
            cu = tu | jnp.left_shift(jnp.int32(1), 31 - i)
            cs = cu ^ int_min
            cnt = count_keys(lambda blk: blk >= cs, n)
            keep = cnt >= topk
            return jnp.where(keep, cu, tu), jnp.where(keep, cnt, n_ge)

        return lax.fori_loop(0, 32, bit_body, (jnp.zeros((1, tq), jnp.int32),
                                               jnp.full((1, tq), n * kc, jnp.int32)))

    tu, n_ge = lax.switch(n_chunks - chunk_counts[0], [functools.partial(radix_select, n) for n in chunk_counts])
    ts = tu ^ int_min

    tri = (lax.broadcasted_iota(jnp.int32, (kc, kc), 0)
           >= lax.broadcasted_iota(jnp.int32, (kc, kc), 1)).astype(_BF16)

    def tie_masks():
        def gt_body(j, acc):
            r0 = pl.multiple_of(j * kc, kc)
            m = jnp.where(key_ref[pl.ds(r0, kc), :] > ts, 1, 0)
            return acc + m.reshape(kc // V7X_SUBLANES, V7X_SUBLANES, tq).sum(axis=0)

        n_greater = lax.fori_loop(0, n_chunks, gt_body, jnp.zeros((V7X_SUBLANES, tq), jnp.int32))
        room = (topk - n_greater.sum(axis=0, keepdims=True)).astype(_F32)

        def body(j, seen):
            r0 = pl.multiple_of(j * kc, kc)
            blk = key_ref[pl.ds(r0, kc), :]
            eq = blk == ts
            rank = jnp.dot(tri, jnp.where(eq, 1.0, 0.0).astype(_BF16), preferred_element_type=_F32) + seen
            val = jnp.where(blk > ts, 0.0, jnp.where(eq, jnp.where(rank <= room, 0.0, _NEG), _NEG))
            mask_ref[pl.ds(r0, kc), :] = jnp.where(admissible(r0), val, _NEG)
            return rank[kc - 1:kc, :]

        return lax.fori_loop(0, n_chunks, body, jnp.zeros((1, tq), _F32))

    def plain_masks():
        def body(j, carry):
            r0 = pl.multiple_of(j * kc, kc)
            val = jnp.where(key_ref[pl.ds(r0, kc), :] >= ts, 0.0, _NEG)
            mask_ref[pl.ds(r0, kc), :] = jnp.where(admissible(r0), val, _NEG)
            return carry

        return lax.fori_loop(0, n_chunks, body, jnp.zeros((1, tq), _F32))

    lax.cond(jnp.max(n_ge) > topk, tie_masks, plain_masks)

    upper_rows = lax.broadcasted_iota(jnp.int32, (V7X_LANES, tq), 0) >= A_HEAD_DIM
    q_pairs = []
    for pr in range(A_HEADS // 2):
        blk = qt_ref[0, pr * V7X_LANES:(pr + 1) * V7X_LANES, :].astype(_F32)
        q_pairs.append(jnp.concatenate([jnp.where(upper_rows, 0.0, blk), jnp.where(upper_rows, blk, 0.0)],
                                       axis=1).astype(_BF16))
    acc_ref[...] = jnp.zeros_like(acc_ref)

    def attn_body(j, carry):
        ms, ls = carry
        r0 = pl.multiple_of(j * kc, kc)
        msk = mask_ref[pl.ds(r0, kc), :]
        off = pl.multiple_of(jnp.maximum(r0 - q0 + r0_off, 0), CHUNK)
        for pr in range(A_HEADS // 2):
            kblk = k_ref[0, pl.ds(r0, kc), pr * V7X_LANES:(pr + 1) * V7X_LANES]
            lt2 = jnp.dot(kblk, q_pairs[pr], preferred_element_type=_F32)
            for hh in range(2):
                h = 2 * pr + hh
                lt_ref[h] = lt2[:, hh * tq:(hh + 1) * tq] + msk + nb_ref[h, pl.ds(off, kc), :]
        new_ms, new_ls, alphas = [], [], []
        for h in range(A_HEADS):
            lt = lt_ref[h]
            m_new = jnp.maximum(ms[h], lt.max(axis=0, keepdims=True))
            p = jnp.exp(lt - m_new)
            alpha = jnp.exp(ms[h] - m_new)
            new_ms.append(m_new)
            new_ls.append(alpha * ls[h] + p.sum(axis=0, keepdims=True))
            alphas.append(alpha)
            p_ref[h] = p.astype(_BF16)
        for h in range(A_HEADS):
            rows = slice(h * A_HEAD_DIM, (h + 1) * A_HEAD_DIM)
            pv = jnp.dot(vt_ref[0, rows, pl.ds(r0, kc)], p_ref[h], preferred_element_type=_F32)
            acc_ref[rows, :] = alphas[h] * acc_ref[rows, :] + pv
        return tuple(new_ms), tuple(new_ls)

    init = (tuple(jnp.full((1, tq), _NEG, _F32) for _ in range(A_HEADS)),
            tuple(jnp.zeros((1, tq), _F32) for _ in range(A_HEADS)))
    _, ls = lax.fori_loop(0, n_chunks, attn_body, init)

    eye = (lax.broadcasted_iota(jnp.int32, (tq, tq), 0)
           == lax.broadcasted_iota(jnp.int32, (tq, tq), 1)).astype(_BF16)
    for pair in range(A_HEADS // 2):
        lanes = slice(pair * V7X_LANES, (pair + 1) * V7X_LANES)
        o_t = jnp.concatenate(
            [acc_ref[(2 * pair + hh) * A_HEAD_DIM:(2 * pair + hh + 1) * A_HEAD_DIM, :] / ls[2 * pair + hh]
             for hh in range(2)], axis=0).astype(_BF16)
        o_ref[0, :, lanes] = lax.dot_general(eye, o_t, _NT, preferred_element_type=_F32).astype(_BF16)


def _t5_bucket_table(rel):
    half = N_BUCKETS // 2
    max_exact = half // 2
    n = jnp.abs(rel)
    nf = jnp.maximum(n, 1).astype(jnp.float32)
    large = max_exact + (jnp.log(nf / max_exact) / math.log(MAX_DISTANCE / max_exact)
                         * (half - max_exact)).astype(jnp.int32)
    large = jnp.minimum(large, half - 1)
    return jnp.where(rel > 0, half, 0) + jnp.where(n < max_exact, n, large)


def _dsa_attention(qn, idx, k_all, v_all, ki_all, rel_bias, past):
    bsz, lq, _ = qn.shape
    nk = k_all.shape[1]
    topk = min(TOPK_MAX, nk // 4)
    tq = min(V7X_LANES, lq)
    kc = 256
    nk_pad = -(-nk // kc) * kc
    pad = ((0, 0), (0, nk_pad - nk), (0, 0))
    k16 = jnp.pad(k_all, pad)
    vt16 = jnp.swapaxes(jnp.pad(v_all, pad), 1, 2)
    ki2 = jnp.pad(jnp.concatenate([ki_all, ki_all], axis=-1), pad)
    wit = jnp.swapaxes(idx[:, :, _EV_WI_OFF:_EV_WI_OFF + V7X_SUBLANES], 1, 2)
    qt = jnp.swapaxes(qn, 1, 2)
    qit = jnp.swapaxes(idx[:, :, :IDX_HEADS * IDX_DIM], 1, 2)
    r0_off = kc + MAX_DISTANCE
    nbr = r0_off + kc + tq
    rel = (jnp.arange(nbr, dtype=jnp.int32)[:, None] - r0_off) - jnp.arange(tq, dtype=jnp.int32)[None, :]
    bkt = _t5_bucket_table(rel)
    chunk_counts = sorted({(past + (t + 1) * tq + kc - 1) // kc for t in range(lq // tq)})
    assert chunk_counts == list(range(chunk_counts[0], chunk_counts[-1] + 1))
    kern = functools.partial(_dsa_kernel, past=past, tq=tq, kc=kc, topk=topk, r0_off=r0_off,
                             chunk_counts=tuple(chunk_counts))
    return pl.pallas_call(
        kern,
        grid=(bsz, lq // tq),
        in_specs=[pl.BlockSpec(memory_space=pltpu.SMEM),
                  pl.BlockSpec((1, A_W, tq), lambda b, t: (b, 0, t)),
                  pl.BlockSpec((1, IDX_HEADS * IDX_DIM, tq), lambda b, t: (b, 0, t)),
                  pl.BlockSpec((1, V7X_SUBLANES, tq), lambda b, t: (b, 0, t)),
                  pl.BlockSpec((1, nk_pad, A_W), lambda b, t: (b, 0, 0)),
                  pl.BlockSpec((1, A_W, nk_pad), lambda b, t: (b, 0, 0)),
                  pl.BlockSpec((1, nk_pad, V7X_LANES), lambda b, t: (b, 0, 0)),
                  pl.BlockSpec((nbr, tq), lambda b, t: (0, 0))],
        out_specs=pl.BlockSpec((1, tq, A_W), lambda b, t: (b, t, 0)),
        out_shape=jax.ShapeDtypeStruct((bsz, lq, A_W), _BF16),
        scratch_shapes=[pltpu.VMEM((nk_pad, tq), jnp.int32), pltpu.VMEM((nk_pad, tq), _F32),
                        pltpu.VMEM((A_HEADS, nbr, tq), _F32),
                        pltpu.VMEM((nk_pad, 2 * V7X_LANES), _BF16), pltpu.VMEM((A_W, tq), _F32),
                        pltpu.VMEM((A_HEADS, kc, tq), _F32), pltpu.VMEM((A_HEADS, kc, tq), _BF16)],
        compiler_params=_params(2),
        name="dsa_attention",
    )(rel_bias, qt, qit, wit, k16, vt16, ki2, bkt)


def _split3_rows(x):
    hi = x.astype(_BF16)
    r1 = x - hi.astype(_F32)
    mid = r1.astype(_BF16)
    lo = (r1 - mid.astype(_F32)).astype(_BF16)
    return jnp.concatenate([hi, mid, lo], axis=0)


def _gla_kernel(q_ref, k_ref, g_ref, v_ref, r_ref, s0_ref, ng_ref, o_ref, s_out_ref, st_ref, *, n_sub):
    c = pl.program_id(1)
    causal = (lax.broadcasted_iota(jnp.int32, (CHUNK, CHUNK), 0)
              >= lax.broadcasted_iota(jnp.int32, (CHUNK, CHUNK), 1))
    causal3 = jnp.concatenate([causal.astype(_BF16)] * 3, axis=1)
    upper_half = lax.broadcasted_iota(jnp.int32, (CHUNK, V7X_LANES), 1) >= B_KEY_DIM
    diag = (lax.broadcasted_iota(jnp.int32, (V7X_LANES, V7X_LANES), 0)
            == lax.broadcasted_iota(jnp.int32, (V7X_LANES, V7X_LANES), 1))
    pairs = range(B_HEADS // 2)
    heads = range(B_HEADS)

    @pl.when(c == 0)
    def _():
        st_ref[...] = s0_ref[0]

    def pair_lanes(x, p):
        return x[:, p * V7X_LANES:(p + 1) * V7X_LANES]

    def head_half(x, h):
        keep = upper_half if h % 2 == 1 else jnp.logical_not(upper_half)
        return jnp.where(keep, pair_lanes(x, h // 2), 0.0)

    parts = []
    for s in range(n_sub):
        rows = slice(s * CHUNK, (s + 1) * CHUNK)
        b = jnp.dot(causal3, _split3_rows(g_ref[0, rows, :]), preferred_element_type=_F32)
        mid = CHUNK // 2
        b_mid = b[mid:mid + 1, :]
        b_last = b[CHUNK - 1:CHUNK, :]
        q = q_ref[0, rows, :]
        k = k_ref[0, rows, :]
        qe = q * jnp.exp(b)
        qm = q * jnp.exp(b - b_mid)
        km = k * jnp.exp(b_mid - b)
        kl = k * jnp.exp(b_last - b)
        vs = [v_ref[0, rows, h * B_VAL_DIM:(h + 1) * B_VAL_DIM] for h in heads]
        a = [jnp.where(causal, _dot(head_half(qm, h), pair_lanes(km, h // 2), _NT), 0.0) for h in heads]
        av = [_dot(a[h], vs[h]) for h in heads]
        upd = [_dot(head_half(kl, h), vs[h], _TN) for h in heads]
        dcol = [jnp.sum(jnp.where(diag, jnp.exp(pair_lanes(b_last, p)), 0.0), axis=1, keepdims=True)
                for p in pairs]
        parts.append((qe, av, upd, dcol))

    st = [st_ref[p * V7X_LANES:(p + 1) * V7X_LANES, :] for p in pairs]
    for s in range(n_sub):
        rows = slice(s * CHUNK, (s + 1) * CHUNK)
        qe, av, upd, dcol = parts[s]
        o = [_dot(head_half(qe, h), st[h // 2]) + av[h] for h in heads]
        st = [dcol[p] * st[p] + upd[2 * p] + upd[2 * p + 1] for p in pairs]
        for h in heads:
            cols = slice(h * B_VAL_DIM, (h + 1) * B_VAL_DIM)
            o_ref[0, rows, cols] = (_rms(o[h], ng_ref[...]) * _silu(r_ref[0, rows, cols])).astype(_BF16)
    for p in pairs:
        st_ref[p * V7X_LANES:(p + 1) * V7X_LANES, :] = st[p]

    @pl.when(c == pl.num_programs(1) - 1)
    def _():
        s_out_ref[0] = st_ref[...]


def _gla(qb, kb, la, vb, rb, s0, norm_g):
    bsz, lq, _ = qb.shape
    n_sub = min(4, lq // CHUNK)
    tok = lambda n: pl.BlockSpec((1, n_sub * CHUNK, n), lambda b, c: (b, c, 0))
    state = pl.BlockSpec((1, B_KW, B_VAL_DIM), lambda b, c: (b, 0, 0))
    o, s_new = pl.pallas_call(
        functools.partial(_gla_kernel, n_sub=n_sub),
        grid=(bsz, lq // (n_sub * CHUNK)),
        in_specs=[tok(B_KW), tok(B_KW), tok(B_KW), tok(B_VW), tok(B_VW), state,
                  pl.BlockSpec((1, B_VAL_DIM), lambda b, c: (0, 0))],
        out_specs=[tok(B_VW), state],
        out_shape=[jax.ShapeDtypeStruct((bsz, lq, B_VW), _BF16),
                   jax.ShapeDtypeStruct((bsz, B_KW, B_VAL_DIM), _F32)],
        scratch_shapes=[pltpu.VMEM((B_KW, B_VAL_DIM), _F32)],
        compiler_params=_params(2),
        name="gla",
    )(qb, kb, la, vb, rb, s0.reshape(bsz, B_KW, B_VAL_DIM), norm_g.reshape(1, B_VAL_DIM))
    return o, s_new.reshape(s0.shape)


def _odd_in_kernel(x_ref, g_ref, w_ref, alog_ref, dt_ref, cw_ref, cprev_ref, act_ref, z_ref, gb_ref, tail_ref,
                   xbuf_ref, *, seg, tiles_per_batch):
    halo = V7X_SUBLANES
    h = _rms(x_ref[...], g_ref[...]).astype(_BF16)
    z_ref[...] = jnp.dot(h, w_ref[:, _OD_Z:_OD_Z + C_W], preferred_element_type=_F32)
    ab = jnp.dot(h, w_ref[:, _OD_AB:_OD_AB + V7X_LANES], preferred_element_type=_F32)
    xa = ab + dt_ref[...]
    g = -jnp.exp(alog_ref[...]) * (jnp.maximum(xa, 0.0) + jnp.log1p(jnp.exp(-jnp.abs(xa))))
    lane = lax.broadcasted_iota(jnp.int32, ab.shape, 1)
    gb_ref[...] = jnp.where(lane < C_HEADS, g, jax.nn.sigmoid(ab))

    for s in range(x_ref.shape[0] // seg):
        rows = slice(s * seg, (s + 1) * seg)
        if tiles_per_batch > 1:
            @pl.when(pl.program_id(0) % tiles_per_batch == 0)
            def _():
                xbuf_ref[0:halo, :] = cprev_ref[0]
        else:
            xbuf_ref[0:halo, :] = cprev_ref[s]
        xbuf_ref[halo:halo + seg, :] = jnp.dot(h[rows], w_ref[:, _OD_QKV:_OD_QKV + CONV_CH],
                                               preferred_element_type=_F32)
        conv = xbuf_ref[halo:halo + seg, :] * cw_ref[CONV_WIDTH - 1:CONV_WIDTH, :]
        for j in range(CONV_WIDTH - 1):
            sh = CONV_WIDTH - 1 - j
            conv = conv + xbuf_ref[halo - sh:halo - sh + seg, :] * cw_ref[j:j + 1, :]
        tail = xbuf_ref[seg:seg + halo, :]
        xbuf_ref[0:halo, :] = tail
        tail_ref[s] = tail
        act = _silu(conv)
        for hd in range(C_HEADS):
            for part, scale in ((0, C_HEAD_DIM ** -0.5), (1, 1.0)):
                cols = slice(part * C_W + hd * C_HEAD_DIM, part * C_W + (hd + 1) * C_HEAD_DIM)
                a = act[:, cols]
                act_ref[rows, cols] = a * (lax.rsqrt(jnp.sum(a * a, axis=-1, keepdims=True) + EPS) * scale)
        act_ref[rows, 2 * C_W:] = act[:, 2 * C_W:]


def _odd_in_proj(x, g, w, a_log, dt_bias, conv_w, conv_prev8, lq):
    t, d = x.shape
    tm = min(t, 256)
    seg = min(tm, lq)
    batches_per_tile = tm // seg
    tiles_per_batch = lq // seg
    row = lambda n: pl.BlockSpec((tm, n), lambda i: (i, 0))
    per_batch = pl.BlockSpec((batches_per_tile, V7X_SUBLANES, CONV_CH), lambda i: (i // tiles_per_batch, 0, 0))
    outs = [CONV_CH, C_W, V7X_LANES]
    return pl.pallas_call(
        functools.partial(_odd_in_kernel, seg=seg, tiles_per_batch=tiles_per_batch),
        grid=(t // tm,),
        in_specs=[row(d), _full((1, d)), _full((d, _OD_COLS)), _full((1, V7X_LANES)), _full((1, V7X_LANES)),
                  _full((CONV_WIDTH, CONV_CH)), per_batch],
        out_specs=[row(n) for n in outs] + [per_batch],
        out_shape=([jax.ShapeDtypeStruct((t, n), _F32) for n in outs]
                   + [jax.ShapeDtypeStruct((t // lq, V7X_SUBLANES, CONV_CH), _F32)]),
        scratch_shapes=[pltpu.VMEM((seg + V7X_SUBLANES, CONV_CH), _F32)],
        compiler_params=_params(1),
        name="odd_in_proj",
    )(x, g.reshape(1, d), w, a_log, dt_bias, conv_w, conv_prev8)


def _delta_kernel(act_ref, z_ref, gb_ref, s0_ref, ng_ref, o_ref, s_out_ref, st_ref, *, n_sub):
    c = pl.program_id(1)

    @pl.when(c == 0)
    def _():
        st_ref[...] = s0_ref[0]

    t_i = lax.broadcasted_iota(jnp.int32, (CHUNK, V7X_LANES), 0)
    s_i = lax.broadcasted_iota(jnp.int32, (CHUNK, V7X_LANES), 1)
    lower_half = s_i < CHUNK
    s_mod = jnp.where(lower_half, s_i, s_i - CHUNK)
    incl = t_i >= s_mod
    strict = t_i > s_mod
    eye = (t_i == s_mod).astype(_F32)
    incl64 = (lax.broadcasted_iota(jnp.int32, (CHUNK, CHUNK), 0)
              >= lax.broadcasted_iota(jnp.int32, (CHUNK, CHUNK), 1)).astype(_F32)
    eye128 = (lax.broadcasted_iota(jnp.int32, (V7X_LANES, V7X_LANES), 0)
              == lax.broadcasted_iota(jnp.int32, (V7X_LANES, V7X_LANES), 1)).astype(_F32)

    def split3(x):
        hi16 = x.astype(_BF16)
        hi = hi16.astype(_F32)
        lo = x - hi
        lhs = jnp.concatenate([jnp.where(lower_half, hi, lo).astype(_BF16), hi16], axis=1)
        lo16 = lo.astype(_BF16)
        rhs = jnp.concatenate([hi16, hi16, lo16, jnp.zeros_like(lo16)], axis=0)
        return lhs, rhs

    def mm(lhs, rhs):
        return jnp.dot(lhs, rhs, preferred_element_type=_F32)

    heads = range(C_HEADS)
    units = [(s, h) for s in range(n_sub) for h in heads]
    g_lasts = []
    qs, ks, kbs, decays, rhss, qgs, kdecs = [], [], [], [], [], [], []
    for s in range(n_sub):
        rows = slice(s * CHUNK, (s + 1) * CHUNK)
        gb = gb_ref[0, rows, :]
        gcum_all = _dot_hi(incl64, gb)
        gcum_rows = _dot_hi(eye128, jnp.concatenate([gcum_all, gcum_all], axis=0), _NT)
        g_last = gcum_all[CHUNK - 1:CHUNK, :]
        g_lasts.append(g_last)
        for h in heads:
            q = act_ref[0, rows, h * C_HEAD_DIM:(h + 1) * C_HEAD_DIM]
            k = act_ref[0, rows, C_W + h * C_HEAD_DIM:C_W + (h + 1) * C_HEAD_DIM]
            v = act_ref[0, rows, 2 * C_W + h * C_HEAD_DIM:2 * C_W + (h + 1) * C_HEAD_DIM]
            beta = gb[:, C_HEADS + h:C_HEADS + h + 1]
            gcum = gcum_all[:, h:h + 1]
            kb = k * beta
            qs.append(q)
            ks.append(k)
            kbs.append(kb)
            decays.append(jnp.exp(jnp.where(incl, gcum - gcum_rows[h:h + 1, :], _NEG)))
            rhss.append(jnp.concatenate([v * beta, kb * jnp.exp(gcum)], axis=-1))
            qgs.append(q * jnp.exp(gcum))
            kdecs.append(k * jnp.exp(g_last[:, h:h + 1] - gcum))

    n_units = range(len(units))
    lowers, attns = [], []
    for u in n_units:
        prod = _dot(jnp.concatenate([kbs[u], qs[u]], axis=0),
                    jnp.concatenate([ks[u], ks[u]], axis=0), _NT)
        lowers.append(jnp.where(strict, prod[:CHUNK] * decays[u], 0.0))
        attns.append(jnp.where(incl, prod[CHUNK:] * decays[u], 0.0))

    pws = [-lowers[u] for u in n_units]
    invs = [eye + pws[u] for u in n_units]
    pw_l, pw_r = zip(*[split3(pws[u]) for u in n_units])
    pws = [mm(pw_l[u], pw_r[u]) for u in n_units]
    for _ in range(4):
        pw_l, pw_r = zip(*[split3(pws[u]) for u in n_units])
        both = [mm(jnp.concatenate([split3(invs[u])[0], pw_l[u]], axis=0), pw_r[u]) for u in n_units]
        invs = [invs[u] + both[u][:CHUNK] for u in n_units]
        pws = [both[u][CHUNK:] for u in n_units]
    invs = [invs[u] + mm(split3(invs[u])[0], split3(pws[u])[1]) for u in n_units]
    sols = []
    for u in n_units:
        r_hi16 = rhss[u].astype(_BF16)
        r_lo16 = (rhss[u] - r_hi16.astype(_F32)).astype(_BF16)
        sols.append(mm(split3(invs[u])[0], jnp.concatenate([r_hi16, r_hi16, r_lo16, jnp.zeros_like(r_lo16)], axis=0)))

    st = [st_ref[h] for h in heads]
    for u, (s, h) in enumerate(units):
        rows = slice(s * CHUNK, (s + 1) * CHUNK)
        w_u = sols[u][:, :C_HEAD_DIM]
        w_w = sols[u][:, C_HEAD_DIM:]
        ws = _dot(jnp.concatenate([w_w, qgs[u]], axis=0), st[h])
        v_new = w_u - ws[:CHUNK]
        o = ws[CHUNK:] + _dot(attns[u][:, :CHUNK], v_new)
        st[h] = jnp.exp(g_lasts[s][:, h:h + 1]) * st[h] + _dot(kdecs[u], v_new, _TN)
        cols = slice(h * C_HEAD_DIM, (h + 1) * C_HEAD_DIM)
        o_ref[0, rows, cols] = (_rms(o, ng_ref[...]) * _silu(z_ref[0, rows, cols])).astype(_BF16)
    for h in heads:
        st_ref[h] = st[h]

    @pl.when(c == pl.num_programs(1) - 1)
    def _():
        s_out_ref[0] = st_ref[...]


def _delta(act, z, gb, s0, norm_g):
    bsz, lq, _ = act.shape
    n_sub = min(2, lq // CHUNK)
    rows_all = n_sub * CHUNK
    tok = lambda n: pl.BlockSpec((1, rows_all, n), lambda b, c: (b, c, 0))
    state = pl.BlockSpec((1, C_HEADS, C_HEAD_DIM, C_HEAD_DIM), lambda b, c: (b, 0, 0, 0))
    return pl.pallas_call(
        functools.partial(_delta_kernel, n_sub=n_sub),
        grid=(bsz, lq // rows_all),
        in_specs=[tok(CONV_CH), tok(C_W), tok(V7X_LANES), state,
                  pl.BlockSpec((1, C_HEAD_DIM), lambda b, c: (0, 0))],
        out_specs=[tok(C_W), state],
        out_shape=[jax.ShapeDtypeStruct((bsz, lq, C_W), _BF16),
                   jax.ShapeDtypeStruct(s0.shape, _F32)],
        scratch_shapes=[pltpu.VMEM((C_HEADS, C_HEAD_DIM, C_HEAD_DIM), _F32)],
        compiler_params=_params(2),
        name="gated_delta",
    )(act, z, gb, s0, norm_g.reshape(1, C_HEAD_DIM))


def _prep_even_w_in(w):
    sizes = (A_W, A_W, A_W, IDX_HEADS * IDX_DIM, IDX_DIM, IDX_HEADS, B_KW, B_KW, B_VW, B_VW, B_GATE_RANK)
    offs = [0]
    for s in sizes:
        offs.append(offs[-1] + s)
    part = lambda i: w[:, offs[i]:offs[i + 1]]
    q, k, v, qi, ki, wi, qb, kb, vb, rb, ab = [part(i) for i in range(len(sizes))]
    pad = jnp.zeros((w.shape[0], _EV_IDX_W - (256 + IDX_DIM + IDX_HEADS + B_GATE_RANK)), w.dtype)
    return jnp.concatenate([q, k, v, qi, ki, wi, ab, pad, qb, kb, vb, rb], axis=1).astype(_BF16)


def _prep_odd_w_in(w):
    pad = jnp.zeros((w.shape[0], _OD_COLS - w.shape[1]), w.dtype)
    return jnp.concatenate([w, pad], axis=1).astype(_BF16)


def _lane_pad(v, n=V7X_LANES):
    return jnp.pad(v, (0, n - v.shape[0])).reshape(1, n)


def _trunk(x, past, prm):
    bsz, lq, d = x.shape
    t = bsz * lq
    xf = x.reshape(t, d)

    def ffn(xf, i, j, acts=(), proj_w=()):
        return _ffn_half(xf, prm['ffn_norm'][i, j], prm['ffn_w_gate'][i, j].astype(_BF16),
                         prm['ffn_w_up'][i, j].astype(_BF16), prm['ffn_w_down'][i, j].astype(_BF16),
                         acts, proj_w)

    xf = ffn(xf, 0, 0)
    head_mean = jnp.kron(jnp.eye(A_HEADS, dtype=_F32),
                         jnp.full((A_HEAD_DIM, A_HEAD_DIM), 1.0 / A_HEAD_DIM, _F32)).astype(_BF16)
    q_gain = (jnp.tile(prm['ev_q_norm'][0], A_HEADS) * (A_HEAD_DIM ** -0.5)).reshape(1, A_W)
    k_gain = jnp.tile(prm['ev_k_norm'][0], A_HEADS).reshape(1, A_W)
    w2 = jnp.zeros((V7X_LANES, B_KW), _F32).at[
        _EV_AB_OFF - 256:_EV_AB_OFF - 256 + B_GATE_RANK].set(prm['ev_gate_w2'][0])
    qn, k_new, v_new, k16, v16, idx, qb, kb, vb, rb, la = _even_in_proj(
        xf, prm['mix_norm'][0], _prep_even_w_in(prm['ev_w_in'][0]), q_gain, k_gain, head_mean,
        w2, prm['ev_gate_b2'][0].reshape(1, B_KW))
    r3 = lambda a: a.reshape(bsz, lq, a.shape[-1])
    ki_new = r3(idx)[:, :, 256:256 + IDX_DIM]
    if past is None:
        past_len = 0
        k_all, v_all, ki_all = r3(k16), r3(v16), ki_new
        s0_gla = jnp.zeros((bsz, B_HEADS, B_KEY_DIM, B_VAL_DIM), _F32)
    else:
        past_len = past['k'].shape[2]
        k_all = jnp.concatenate([past['k'][0].reshape(bsz, past_len, A_W).astype(_BF16), r3(k16)], axis=1)
        v_all = jnp.concatenate([past['v'][0].reshape(bsz, past_len, A_W).astype(_BF16), r3(v16)], axis=1)
        ki_all = jnp.concatenate([past['ki'][0], ki_new], axis=1)
        s0_gla = past['gla'][0]
    o_a = _dsa_attention(r3(qn), r3(idx), k_all, v_all, ki_all, prm['rel_bias'], past_len)
    o_b, s_gla = _gla(r3(qb), r3(kb), r3(la), r3(vb), r3(rb), s0_gla, prm['ev_gla_norm'][0])
    w_out = prm['ev_w_out'][0].astype(_BF16)
    xf = ffn(xf, 0, 1, [o_a.reshape(t, A_W), o_b.reshape(t, B_VW)], [w_out[:A_W], w_out[A_W:]])

    xf = ffn(xf, 1, 0)
    a_log = _lane_pad(prm['od_a_log'][0])
    dt_bias = _lane_pad(prm['od_dt_bias'][0])
    if past is None:
        conv_prev = jnp.zeros((bsz, CONV_WIDTH - 1, CONV_CH), _F32)
        s0_delta = jnp.zeros((bsz, C_HEADS, C_HEAD_DIM, C_HEAD_DIM), _F32)
    else:
        conv_prev, s0_delta = past['conv'][0], past['delta'][0]
    conv_prev8 = jnp.pad(conv_prev, ((0, 0), (V7X_SUBLANES - (CONV_WIDTH - 1), 0), (0, 0)))
    act, z, gb, tail = _odd_in_proj(xf, prm['mix_norm'][1], _prep_odd_w_in(prm['od_w_in'][0]), a_log, dt_bias,
                                    prm['od_conv_w'][0], conv_prev8, lq)
    o_c, s_delta = _delta(r3(act), r3(z), r3(gb), s0_delta, prm['od_norm'][0])
    conv_new = tail[:, V7X_SUBLANES - (CONV_WIDTH - 1):]
    xf = ffn(xf, 1, 1, [o_c.reshape(t, C_W)], [prm['od_w_out'][0].astype(_BF16)])

    y = xf.reshape(bsz, lq, d)
    k_out = r3(k_new).reshape(1, bsz, lq, A_HEADS, A_HEAD_DIM)
    v_out = r3(v_new).reshape(1, bsz, lq, A_HEADS, A_HEAD_DIM)
    return y, k_out, v_out, ki_new[None], s_gla[None], s_delta[None], conv_new[None]


def kernel(x_prompt, x_sample, cache_attn_k, cache_attn_v, cache_idx_k, state_gla, state_delta, state_conv,
           ffn_norm, ffn_w_gate, ffn_w_up, ffn_w_down, mix_norm, ev_w_in, ev_q_norm, ev_k_norm, rel_bias,
           ev_gate_w2, ev_gate_b2, ev_gla_norm, ev_w_out, od_w_in, od_conv_w, od_a_log, od_dt_bias, od_norm,
           od_w_out):
    prm = {'ffn_norm': ffn_norm, 'ffn_w_gate': ffn_w_gate, 'ffn_w_up': ffn_w_up, 'ffn_w_down': ffn_w_down,
           'mix_norm': mix_norm, 'ev_w_in': ev_w_in, 'ev_q_norm': ev_q_norm, 'ev_k_norm': ev_k_norm,
           'rel_bias': rel_bias, 'ev_gate_w2': ev_gate_w2, 'ev_gate_b2': ev_gate_b2, 'ev_gla_norm': ev_gla_norm,
           'ev_w_out': ev_w_out, 'od_w_in': od_w_in, 'od_conv_w': od_conv_w, 'od_a_log': od_a_log,
           'od_dt_bias': od_dt_bias, 'od_norm': od_norm, 'od_w_out': od_w_out}
    past = {'k': cache_attn_k, 'v': cache_attn_v, 'ki': cache_idx_k, 'gla': state_gla,
            'delta': state_delta, 'conv': state_conv}
    y_prompt, p_k, p_v, p_ki, p_gla, p_delta, p_conv = _trunk(x_prompt, None, prm)
    y_sample, s_k, s_v, s_ki, s_gla, s_delta, s_conv = _trunk(x_sample, past, prm)
    return (y_prompt, y_sample, p_k, p_v, p_ki, p_gla, p_delta, p_conv,
            s_k, s_v, s_ki, s_gla, s_delta, s_conv)
```

```python
import functools
import math

import jax
import jax.numpy as jnp
from jax import lax
from jax.experimental import pallas as pl
from jax.experimental.pallas import tpu as pltpu

D_MODEL = 1024
DEPTH = 2
CHUNK = 64
EPS = 1e-6

A_HEADS = 8
A_HEAD_DIM = 64
A_W = A_HEADS * A_HEAD_DIM
IDX_HEADS = 4
IDX_DIM = 64
TOPK_MAX = 256
N_BUCKETS = 32
MAX_DISTANCE = 128

B_HEADS = 4
B_KEY_DIM = 64
B_VAL_DIM = 128
B_KW = B_HEADS * B_KEY_DIM
B_VW = B_HEADS * B_VAL_DIM
B_GATE_RANK = 16
B_GATE_TAU = 16.0

C_HEADS = 8
C_HEAD_DIM = 128
C_W = C_HEADS * C_HEAD_DIM
CONV_WIDTH = 4
CONV_CH = 3 * C_W

D_FF = 2816

V7X_LANES = 128
V7X_SUBLANES = 8
V7X_VMEM_LIMIT_BYTES = 56 * 1024 * 1024

_F32 = jnp.float32
_BF16 = jnp.bfloat16
_HI = lax.Precision.HIGHEST
_NEG = -1e30
_LOG2E = math.log2(math.e)
_NN = (((1,), (0,)), ((), ()))
_NT = (((1,), (1,)), ((), ()))
_TN = (((0,), (0,)), ((), ()))

_EV_Q, _EV_K, _EV_V = 0, 512, 1024
_EV_IDX = 1536
_EV_IDX_W = 384
_EV_WI_OFF = 256 + IDX_DIM
_EV_AB_OFF = _EV_WI_OFF + IDX_HEADS
_EV_QB, _EV_KB, _EV_VB, _EV_RB = 1920, 2176, 2432, 2944
_EV_COLS = 3456
_OD_QKV, _OD_Z, _OD_AB = 0, 3072, 4096
_OD_COLS = 4224


def _params(n_axes):
    return pltpu.CompilerParams(dimension_semantics=("arbitrary",) * n_axes,
                                vmem_limit_bytes=V7X_VMEM_LIMIT_BYTES)


def _rms(x, g):
    return x * lax.rsqrt(jnp.mean(x * x, axis=-1, keepdims=True) + EPS) * g


def _silu(x):
    return x * jax.nn.sigmoid(x)


def _dot(a, b, dims=_NN):
    return lax.dot_general(a.astype(_BF16), b.astype(_BF16), dims, preferred_element_type=_F32)


def _dot_hi(a, b, dims=_NN):
    return lax.dot_general(a, b, dims, precision=_HI, preferred_element_type=_F32)


def _full(shape):
    return pl.BlockSpec(shape, lambda *_: (0,) * len(shape))


_FFN_COLS = 256


def _ffn_kernel(*refs, n_proj):
    x_ref = refs[0]
    g_ref, wg_ref, wu_ref, wd_ref, o_ref, act_ref = refs[1 + 2 * n_proj:]
    x = x_ref[...]
    for i in range(n_proj):
        x = x + jnp.dot(refs[1 + i][...], refs[1 + n_proj + i][...], preferred_element_type=_F32)
    h = _rms(x, g_ref[...]).astype(_BF16)
    for c in range(wg_ref.shape[1] // _FFN_COLS):
        cols = slice(c * _FFN_COLS, (c + 1) * _FFN_COLS)
        gate = jnp.dot(h, wg_ref[:, cols], preferred_element_type=_F32)
        up = jnp.dot(h, wu_ref[:, cols], preferred_element_type=_F32)
        act_ref[:, cols] = (_silu(gate) * up).astype(_BF16)
    o_ref[...] = x + 0.5 * jnp.dot(act_ref[...], wd_ref[...], preferred_element_type=_F32)


def _ffn_half(x, g, wg, wu, wd, acts=(), proj_w=()):
    t, d = x.shape
    ff = wg.shape[1]
    tm = min(t, 512)
    resident = lambda shape: pl.BlockSpec(shape, lambda i: (0, 0), pipeline_mode=pl.Buffered(1))
    return pl.pallas_call(
        functools.partial(_ffn_kernel, n_proj=len(acts)),
        grid=(t // tm,),
        in_specs=([pl.BlockSpec((tm, d), lambda i: (i, 0))]
                  + [pl.BlockSpec((tm, a.shape[1]), lambda i: (i, 0)) for a in acts]
                  + [resident(w.shape) for w in proj_w]
                  + [resident((1, d)), resident((d, ff)), resident((d, ff)), resident((ff, d))]),
        out_specs=pl.BlockSpec((tm, d), lambda i: (i, 0)),
        out_shape=jax.ShapeDtypeStruct((t, d), _F32),
        scratch_shapes=[pltpu.VMEM((tm, ff), _BF16)],
        compiler_params=_params(1),
        name="ffn_half",
    )(x, *acts, *proj_w, g.reshape(1, d), wg, wu, wd)


def _even_in_kernel(x_ref, g_ref, w_ref, qg_ref, kg_ref, hm_ref, w2_ref, b2_ref,
                    qn_ref, k_ref, v_ref, k16_ref, v16_ref, idx_ref, qb_ref, kb_ref, vb_ref, rb_ref, la_ref):
    h = _rms(x_ref[...], g_ref[...]).astype(_BF16)

    def proj(lo, width):
        return jnp.dot(h, w_ref[:, lo:lo + width], preferred_element_type=_F32)

    def headnorm(t, gain):
        t2 = t * t
        hi = t2.astype(_BF16)
        lo = (t2 - hi.astype(_F32)).astype(_BF16)
        ms = (jnp.dot(hi, hm_ref[...], preferred_element_type=_F32)
              + jnp.dot(lo, hm_ref[...], preferred_element_type=_F32))
        return t * lax.rsqrt(ms + EPS) * gain

    qn_ref[...] = headnorm(proj(_EV_Q, A_W), qg_ref[...]).astype(_BF16)
    k = headnorm(proj(_EV_K, A_W), kg_ref[...])
    v = proj(_EV_V, A_W)
    k_ref[...] = k
    v_ref[...] = v
    k16_ref[...] = k.astype(_BF16)
    v16_ref[...] = v.astype(_BF16)
    idx = proj(_EV_IDX, _EV_IDX_W)
    idx_ref[...] = idx
    qb_ref[...] = proj(_EV_QB, B_KW) * (B_KEY_DIM ** -0.5)
    kb_ref[...] = proj(_EV_KB, B_KW)
    vb_ref[...] = proj(_EV_VB, B_VW)
    rb_ref[...] = proj(_EV_RB, B_VW)
    z = _dot_hi(idx[:, 256:384], w2_ref[...]) + b2_ref[...]
    la_ref[...] = (jnp.minimum(z, 0.0) - jnp.log1p(jnp.exp(-jnp.abs(z)))) * (1.0 / B_GATE_TAU)


def _even_in_proj(x, g, w, q_gain, k_gain, head_mean, w2, b2):
    t, d = x.shape
    tm = min(t, 512)
    row = lambda n: pl.BlockSpec((tm, n), lambda i: (i, 0))
    outs = [(A_W, _BF16), (A_W, _F32), (A_W, _F32), (A_W, _BF16), (A_W, _BF16), (_EV_IDX_W, _F32), (B_KW, _F32), (B_KW, _F32),
            (B_VW, _F32), (B_VW, _F32), (B_KW, _F32)]
    w_spec = pl.BlockSpec((d, _EV_COLS), lambda i: (0, 0), pipeline_mode=pl.Buffered(1))
    return pl.pallas_call(
        _even_in_kernel,
        grid=(t // tm,),
        in_specs=[row(d), _full((1, d)), w_spec, _full((1, A_W)), _full((1, A_W)),
                  _full((A_W, A_W)), _full((V7X_LANES, B_KW)), _full((1, B_KW))],
        out_specs=[row(n) for n, _ in outs],
        out_shape=[jax.ShapeDtypeStruct((t, n), dt) for n, dt in outs],
        compiler_params=_params(1),
        name="even_in_proj",
    )(x, g.reshape(1, d), w, q_gain, k_gain, head_mean, w2, b2)


def _dsa_kernel(rb_ref, qt_ref, qit_ref, wit_ref, k_ref, vt_ref, ki_ref, bkt_ref, o_ref,
                key_ref, mask_ref, nb_ref, ki3_ref, acc_ref, lt_ref, p_ref, *, past, tq, kc, topk, r0_off, chunk_counts):
    t = pl.program_id(1)

    @pl.when((pl.program_id(0) == 0) & (t == 0))
    def _():
        bk = bkt_ref[...]
        for h in range(A_HEADS):
            acc = jnp.zeros(bk.shape, _F32)
            for bb in range(N_BUCKETS):
                acc = jnp.where(bk == bb, rb_ref[bb, h], acc)
            nb_ref[h] = (acc - rb_ref[N_BUCKETS // 2 - 1, h]) * _LOG2E

    q0 = past + t * tq
    n_chunks = (q0 + tq + kc - 1) // kc
    q_limit = ((q0 + lax.broadcasted_iota(jnp.int32, (1, tq), 1)) // CHUNK + 1) * CHUNK
    k_iota = lax.broadcasted_iota(jnp.int32, (kc, tq), 0)

    def admissible(r0):
        return k_iota < (q_limit - r0)

    @pl.when(t == 0)
    def _():
        def split_body(j, carry):
            r0 = pl.multiple_of(j * kc, kc)
            x = ki_ref[0, pl.ds(r0, kc), :]
            hi = x.astype(_BF16)
            ki3_ref[pl.ds(r0, kc), 0:V7X_LANES] = hi
            ki3_ref[pl.ds(r0, kc), V7X_LANES:2 * V7X_LANES] = (x - hi.astype(_F32)).astype(_BF16)
            return carry
        lax.fori_loop(0, ki3_ref.shape[0] // kc, split_body, 0)

    qit = qit_ref[0]
    q_cols = []
    for h in range(IDX_HEADS):
        x = qit[h * IDX_DIM:(h + 1) * IDX_DIM, :]
        hi = x.astype(_BF16)
        lo = (x - hi.astype(_F32)).astype(_BF16)
        q_cols.append(jnp.concatenate([hi, lo, hi, jnp.zeros_like(hi)], axis=0))
    q3 = jnp.concatenate(q_cols, axis=1)
    wit = wit_ref[0]

    upper_rows = lax.broadcasted_iota(jnp.int32, (V7X_LANES, tq), 0) >= A_HEAD_DIM
    q_pairs = []
    for pr in range(A_HEADS // 2):
        blk = qt_ref[0, pr * V7X_LANES:(pr + 1) * V7X_LANES, :].astype(_F32)
        q_pairs.append(jnp.concatenate([jnp.where(upper_rows, 0.0, blk), jnp.where(upper_rows, blk, 0.0)],
                                       axis=1).astype(_BF16))

    def scores_and_logits(n):
        d_all = jnp.dot(ki3_ref[0:n * kc, :], q3, preferred_element_type=_F32)
        for j in range(n):
            d = d_all[j * kc:(j + 1) * kc]
            s = jnp.zeros((kc, tq), _F32)
            for h in range(IDX_HEADS):
                s = s + wit[h:h + 1, :] * jnp.maximum(d[:, h * tq:(h + 1) * tq], 0.0)
            s = jnp.where(s == 0.0, 0.0, s)
            s = jnp.where(admissible(j * kc), s, -jnp.inf)
            bits = pltpu.bitcast(s, jnp.int32)
            key_ref[j * kc:(j + 1) * kc, :] = bits ^ ((bits >> 31) & jnp.int32(0x7FFFFFFF))
        for pr in range(A_HEADS // 2):
            lt2 = jnp.dot(k_ref[0, 0:n * kc, pr * V7X_LANES:(pr + 1) * V7X_LANES], q_pairs[pr],
                          preferred_element_type=_F32)
            for hh in range(2):
                lt_ref[2 * pr + hh, 0:n * kc, :] = lt2[:, hh * tq:(hh + 1) * tq] * _LOG2E
        for j in range(max(n - 2, 0), n):
            off = pl.multiple_of(jnp.maximum(j * kc - q0 + r0_off, 0), CHUNK)
            for h in range(A_HEADS):
                lt_ref[h, j * kc:(j + 1) * kc, :] += nb_ref[h, pl.ds(off, kc), :]

    int_min = jnp.int32(-2 ** 31)

    def count_keys(pred, n):
        parts = []
        for j in range(n):
            m = jnp.where(pred(key_ref[j * kc:(j + 1) * kc, :]), 1, 0)
            parts.append(m.reshape(kc // V7X_SUBLANES, V7X_SUBLANES, tq).sum(axis=0))
        while len(parts) > 1:
            parts = [sum(parts[i:i + 2]) for i in range(0, len(parts), 2)]
        return parts[0].sum(axis=0, keepdims=True)

    def radix_select(n):
        def bit_body(i, carry):
            tu, n_ge = carry
            cu = tu | jnp.left_shift(jnp.int32(1), 31 - i)
            cs = cu ^ int_min
            cnt = count_keys(lambda blk: blk >= cs, n)
            keep = cnt >= topk
            return jnp.where(keep, cu, tu), jnp.where(keep, cnt, n_ge)

        return lax.fori_loop(0, 32, bit_body, (jnp.zeros((1, tq), jnp.int32),
                                               jnp.full((1, tq), n * kc, jnp.int32)))

    def tile_select(n):
        scores_and_logits(n)
        return radix_select(n)

    tu, n_ge = lax.switch(n_chunks - chunk_counts[0], [functools.partial(tile_select, n) for n in chunk_counts])
    ts = tu ^ int_min

    tri = (lax.broadcasted_iota(jnp.int32, (kc, kc), 0)
           >= lax.broadcasted_iota(jnp.int32, (kc, kc), 1)).astype(_BF16)

    def tie_masks():
        def gt_body(j, acc):
            r0 = pl.multiple_of(j * kc, kc)
            m = jnp.where(key_ref[pl.ds(r0, kc), :] > ts, 1, 0)
            return acc + m.reshape(kc // V7X_SUBLANES, V7X_SUBLANES, tq).sum(axis=0)

        n_greater = lax.fori_loop(0, n_chunks, gt_body, jnp.zeros((V7X_SUBLANES, tq), jnp.int32))
        room = (topk - n_greater.sum(axis=0, keepdims=True)).astype(_F32)

        def body(j, seen):
            r0 = pl.multiple_of(j * kc, kc)
            blk = key_ref[pl.ds(r0, kc), :]
            eq = blk == ts
            rank = jnp.dot(tri, jnp.where(eq, 1.0, 0.0).astype(_BF16), preferred_element_type=_F32) + seen
            val = jnp.where(blk > ts, 0.0, jnp.where(eq, jnp.where(rank <= room, 0.0, _NEG), _NEG))
            mask_ref[pl.ds(r0, kc), :] = jnp.where(admissible(r0), val, _NEG)
            return rank[kc - 1:kc, :]

        return lax.fori_loop(0, n_chunks, body, jnp.zeros((1, tq), _F32))

    def plain_masks():
        def body(j, carry):
            r0 = pl.multiple_of(j * kc, kc)
            val = jnp.where(key_ref[pl.ds(r0, kc), :] >= ts, 0.0, _NEG)
            mask_ref[pl.ds(r0, kc), :] = jnp.where(admissible(r0), val, _NEG)
            return carry

        return lax.fori_loop(0, n_chunks, body, jnp.zeros((1, tq), _F32))

    lax.cond(jnp.max(n_ge) > topk, tie_masks, plain_masks)

    acc_ref[...] = jnp.zeros_like(acc_ref)

    def attn_body(j, carry):
        ms, ls = carry
        r0 = pl.multiple_of(j * kc, kc)
        msk = mask_ref[pl.ds(r0, kc), :]
        new_ms, new_ls, alphas = [], [], []
        for h in range(A_HEADS):
            lt = lt_ref[h, pl.ds(r0, kc), :] + msk
            m_new = jnp.maximum(ms[h], lt.max(axis=0, keepdims=True))
            p = jnp.exp2(lt - m_new)
            alpha = jnp.exp2(ms[h] - m_new)
            new_ms.append(m_new)
            new_ls.append(alpha * ls[h] + p.sum(axis=0, keepdims=True))
            alphas.append(alpha)
            p_ref[h] = p.astype(_BF16)
        for h in range(A_HEADS):
            rows = slice(h * A_HEAD_DIM, (h + 1) * A_HEAD_DIM)
            pv = jnp.dot(vt_ref[0, rows, pl.ds(r0, kc)], p_ref[h], preferred_element_type=_F32)
            acc_ref[rows, :] = alphas[h] * acc_ref[rows, :] + pv
        return tuple(new_ms), tuple(new_ls)

    init = (tuple(jnp.full((1, tq), _NEG, _F32) for _ in range(A_HEADS)),
            tuple(jnp.zeros((1, tq), _F32) for _ in range(A_HEADS)))
    _, ls = lax.fori_loop(0, n_chunks, attn_body, init)

    eye = (lax.broadcasted_iota(jnp.int32, (tq, tq), 0)
           == lax.broadcasted_iota(jnp.int32, (tq, tq), 1)).astype(_BF16)
    for pair in range(A_HEADS // 2):
        lanes = slice(pair * V7X_LANES, (pair + 1) * V7X_LANES)
        o_t = jnp.concatenate(
            [acc_ref[(2 * pair + hh) * A_HEAD_DIM:(2 * pair + hh + 1) * A_HEAD_DIM, :] / ls[2 * pair + hh]
             for hh in range(2)], axis=0).astype(_BF16)
        o_ref[0, :, lanes] = lax.dot_general(eye, o_t, _NT, preferred_element_type=_F32).astype(_BF16)


def _t5_bucket_table(rel):
    half = N_BUCKETS // 2
    max_exact = half // 2
    n = jnp.abs(rel)
    nf = jnp.maximum(n, 1).astype(jnp.float32)
    large = max_exact + (jnp.log(nf / max_exact) / math.log(MAX_DISTANCE / max_exact)
                         * (half - max_exact)).astype(jnp.int32)
    large = jnp.minimum(large, half - 1)
    return jnp.where(rel > 0, half, 0) + jnp.where(n < max_exact, n, large)


def _dsa_attention(qn, idx, k_all, v_all, ki_all, rel_bias, past):
    bsz, lq, _ = qn.shape
    nk = k_all.shape[1]
    topk = min(TOPK_MAX, nk // 4)
    tq = min(V7X_LANES, lq)
    kc = 256
    nk_pad = -(-nk // kc) * kc
    pad = ((0, 0), (0, nk_pad - nk), (0, 0))
    k16 = jnp.pad(k_all, pad)
    vt16 = jnp.swapaxes(jnp.pad(v_all, pad), 1, 2)
    ki2 = jnp.pad(jnp.concatenate([ki_all, ki_all], axis=-1), pad)
    wit = jnp.swapaxes(idx[:, :, _EV_WI_OFF:_EV_WI_OFF + V7X_SUBLANES], 1, 2)
    qt = jnp.swapaxes(qn, 1, 2)
    qit = jnp.swapaxes(idx[:, :, :IDX_HEADS * IDX_DIM], 1, 2)
    r0_off = kc + MAX_DISTANCE
    nbr = r0_off + kc + tq
    rel = (jnp.arange(nbr, dtype=jnp.int32)[:, None] - r0_off) - jnp.arange(tq, dtype=jnp.int32)[None, :]
    bkt = _t5_bucket_table(rel)
    chunk_counts = sorted({(past + (t + 1) * tq + kc - 1) // kc for t in range(lq // tq)})
    assert chunk_counts == list(range(chunk_counts[0], chunk_counts[-1] + 1))
    kern = functools.partial(_dsa_kernel, past=past, tq=tq, kc=kc, topk=topk, r0_off=r0_off,
                             chunk_counts=tuple(chunk_counts))
    return pl.pallas_call(
        kern,
        grid=(bsz, lq // tq),
        in_specs=[pl.BlockSpec(memory_space=pltpu.SMEM),
                  pl.BlockSpec((1, A_W, tq), lambda b, t: (b, 0, t)),
                  pl.BlockSpec((1, IDX_HEADS * IDX_DIM, tq), lambda b, t: (b, 0, t)),
                  pl.BlockSpec((1, V7X_SUBLANES, tq), lambda b, t: (b, 0, t)),
                  pl.BlockSpec((1, nk_pad, A_W), lambda b, t: (b, 0, 0)),
                  pl.BlockSpec((1, A_W, nk_pad), lambda b, t: (b, 0, 0)),
                  pl.BlockSpec((1, nk_pad, V7X_LANES), lambda b, t: (b, 0, 0)),
                  pl.BlockSpec((nbr, tq), lambda b, t: (0, 0))],
        out_specs=pl.BlockSpec((1, tq, A_W), lambda b, t: (b, t, 0)),
        out_shape=jax.ShapeDtypeStruct((bsz, lq, A_W), _BF16),
        scratch_shapes=[pltpu.VMEM((nk_pad, tq), jnp.int32), pltpu.VMEM((nk_pad, tq), _F32),
                        pltpu.VMEM((A_HEADS, nbr, tq), _F32),
                        pltpu.VMEM((nk_pad, 2 * V7X_LANES), _BF16), pltpu.VMEM((A_W, tq), _F32),
                        pltpu.VMEM((A_HEADS, nk_pad, tq), _F32), pltpu.VMEM((A_HEADS, kc, tq), _BF16)],
        compiler_params=_params(2),
        name="dsa_attention",
    )(rel_bias, qt, qit, wit, k16, vt16, ki2, bkt)


def _split3_rows(x):
    hi = x.astype(_BF16)
    r1 = x - hi.astype(_F32)
    mid = r1.astype(_BF16)
    lo = (r1 - mid.astype(_F32)).astype(_BF16)
    return jnp.concatenate([hi, mid, lo], axis=0)


def _gla_kernel(q_ref, k_ref, g_ref, v_ref, r_ref, s0_ref, ng_ref, o_ref, s_out_ref, st_ref, *, n_sub):
    c = pl.program_id(1)
    causal = (lax.broadcasted_iota(jnp.int32, (CHUNK, CHUNK), 0)
              >= lax.broadcasted_iota(jnp.int32, (CHUNK, CHUNK), 1))
    causal3 = jnp.concatenate([causal.astype(_BF16)] * 3, axis=1)
    upper_half = lax.broadcasted_iota(jnp.int32, (CHUNK, V7X_LANES), 1) >= B_KEY_DIM
    diag = (lax.broadcasted_iota(jnp.int32, (V7X_LANES, V7X_LANES), 0)
            == lax.broadcasted_iota(jnp.int32, (V7X_LANES, V7X_LANES), 1))
    pairs = range(B_HEADS // 2)
    heads = range(B_HEADS)

    @pl.when(c == 0)
    def _():
        st_ref[...] = s0_ref[0]

    def pair_lanes(x, p):
        return x[:, p * V7X_LANES:(p + 1) * V7X_LANES]

    def head_half(x, h):
        keep = upper_half if h % 2 == 1 else jnp.logical_not(upper_half)
        return jnp.where(keep, pair_lanes(x, h // 2), 0.0)

    parts = []
    for s in range(n_sub):
        rows = slice(s * CHUNK, (s + 1) * CHUNK)
        b = jnp.dot(causal3, _split3_rows(g_ref[0, rows, :]), preferred_element_type=_F32)
        mid = CHUNK // 2
        b_mid = b[mid:mid + 1, :]
        b_last = b[CHUNK - 1:CHUNK, :]
        q = q_ref[0, rows, :]
        k = k_ref[0, rows, :]
        qe = q * jnp.exp(b)
        qm = q * jnp.exp(b - b_mid)
        km = k * jnp.exp(b_mid - b)
        kl = k * jnp.exp(b_last - b)
        vs = [v_ref[0, rows, h * B_VAL_DIM:(h + 1) * B_VAL_DIM] for h in heads]
        a = [jnp.where(causal, _dot(head_half(qm, h), pair_lanes(km, h // 2), _NT), 0.0) for h in heads]
        av = [_dot(a[h], vs[h]) for h in heads]
        upd = [_dot(head_half(kl, h), vs[h], _TN) for h in heads]
        dcol = [jnp.sum(jnp.where(diag, jnp.exp(pair_lanes(b_last, p)), 0.0), axis=1, keepdims=True)
                for p in pairs]
        parts.append((qe, av, upd, dcol))

    st = [st_ref[p * V7X_LANES:(p + 1) * V7X_LANES, :] for p in pairs]
    for s in range(n_sub):
        rows = slice(s * CHUNK, (s + 1) * CHUNK)
        qe, av, upd, dcol = parts[s]
        o = [_dot(head_half(qe, h), st[h // 2]) + av[h] for h in heads]
        st = [dcol[p] * st[p] + upd[2 * p] + upd[2 * p + 1] for p in pairs]
        for h in heads:
            cols = slice(h * B_VAL_DIM, (h + 1) * B_VAL_DIM)
            o_ref[0, rows, cols] = (_rms(o[h], ng_ref[...]) * _silu(r_ref[0, rows, cols])).astype(_BF16)
    for p in pairs:
        st_ref[p * V7X_LANES:(p + 1) * V7X_LANES, :] = st[p]

    @pl.when(c == pl.num_programs(1) - 1)
    def _():
        s_out_ref[0] = st_ref[...]


def _gla(qb, kb, la, vb, rb, s0, norm_g):
    bsz, lq, _ = qb.shape
    n_sub = min(4, lq // CHUNK)
    tok = lambda n: pl.BlockSpec((1, n_sub * CHUNK, n), lambda b, c: (b, c, 0))
    state = pl.BlockSpec((1, B_KW, B_VAL_DIM), lambda b, c: (b, 0, 0))
    o, s_new = pl.pallas_call(
        functools.partial(_gla_kernel, n_sub=n_sub),
        grid=(bsz, lq // (n_sub * CHUNK)),
        in_specs=[tok(B_KW), tok(B_KW), tok(B_KW), tok(B_VW), tok(B_VW), state,
                  pl.BlockSpec((1, B_VAL_DIM), lambda b, c: (0, 0))],
        out_specs=[tok(B_VW), state],
        out_shape=[jax.ShapeDtypeStruct((bsz, lq, B_VW), _BF16),
                   jax.ShapeDtypeStruct((bsz, B_KW, B_VAL_DIM), _F32)],
        scratch_shapes=[pltpu.VMEM((B_KW, B_VAL_DIM), _F32)],
        compiler_params=_params(2),
        name="gla",
    )(qb, kb, la, vb, rb, s0.reshape(bsz, B_KW, B_VAL_DIM), norm_g.reshape(1, B_VAL_DIM))
    return o, s_new.reshape(s0.shape)


def _odd_in_kernel(x_ref, g_ref, w_ref, alog_ref, dt_ref, cw_ref, cprev_ref, act_ref, z_ref, gb_ref, tail_ref,
                   xbuf_ref, *, seg, tiles_per_batch):
    halo = V7X_SUBLANES
    h = _rms(x_ref[...], g_ref[...]).astype(_BF16)
    z_ref[...] = jnp.dot(h, w_ref[:, _OD_Z:_OD_Z + C_W], preferred_element_type=_F32)
    ab = jnp.dot(h, w_ref[:, _OD_AB:_OD_AB + V7X_LANES], preferred_element_type=_F32)
    xa = ab + dt_ref[...]
    g = -jnp.exp(alog_ref[...]) * (jnp.maximum(xa, 0.0) + jnp.log1p(jnp.exp(-jnp.abs(xa))))
    lane = lax.broadcasted_iota(jnp.int32, ab.shape, 1)
    gb_ref[...] = jnp.where(lane < C_HEADS, g, jax.nn.sigmoid(ab))

    for s in range(x_ref.shape[0] // seg):
        rows = slice(s * seg, (s + 1) * seg)
        if tiles_per_batch > 1:
            @pl.when(pl.program_id(0) % tiles_per_batch == 0)
            def _():
                xbuf_ref[0:halo, :] = cprev_ref[0]
        else:
            xbuf_ref[0:halo, :] = cprev_ref[s]
        xbuf_ref[halo:halo + seg, :] = jnp.dot(h[rows], w_ref[:, _OD_QKV:_OD_QKV + CONV_CH],
                                               preferred_element_type=_F32)
        conv = xbuf_ref[halo:halo + seg, :] * cw_ref[CONV_WIDTH - 1:CONV_WIDTH, :]
        for j in range(CONV_WIDTH - 1):
            sh = CONV_WIDTH - 1 - j
            conv = conv + xbuf_ref[halo - sh:halo - sh + seg, :] * cw_ref[j:j + 1, :]
        tail = xbuf_ref[seg:seg + halo, :]
        xbuf_ref[0:halo, :] = tail
        tail_ref[s] = tail
        act = _silu(conv)
        for hd in range(C_HEADS):
            for part, scale in ((0, C_HEAD_DIM ** -0.5), (1, 1.0)):
                cols = slice(part * C_W + hd * C_HEAD_DIM, part * C_W + (hd + 1) * C_HEAD_DIM)
                a = act[:, cols]
                act_ref[rows, cols] = a * (lax.rsqrt(jnp.sum(a * a, axis=-1, keepdims=True) + EPS) * scale)
        act_ref[rows, 2 * C_W:] = act[:, 2 * C_W:]


def _odd_in_proj(x, g, w, a_log, dt_bias, conv_w, conv_prev8, lq):
    t, d = x.shape
    tm = min(t, 256)
    seg = min(tm, lq)
    batches_per_tile = tm // seg
    tiles_per_batch = lq // seg
    row = lambda n: pl.BlockSpec((tm, n), lambda i: (i, 0))
    per_batch = pl.BlockSpec((batches_per_tile, V7X_SUBLANES, CONV_CH), lambda i: (i // tiles_per_batch, 0, 0))
    outs = [CONV_CH, C_W, V7X_LANES]
    return pl.pallas_call(
        functools.partial(_odd_in_kernel, seg=seg, tiles_per_batch=tiles_per_batch),
        grid=(t // tm,),
        in_specs=[row(d), _full((1, d)), _full((d, _OD_COLS)), _full((1, V7X_LANES)), _full((1, V7X_LANES)),
                  _full((CONV_WIDTH, CONV_CH)), per_batch],
        out_specs=[row(n) for n in outs] + [per_batch],
        out_shape=([jax.ShapeDtypeStruct((t, n), _F32) for n in outs]
                   + [jax.ShapeDtypeStruct((t // lq, V7X_SUBLANES, CONV_CH), _F32)]),
        scratch_shapes=[pltpu.VMEM((seg + V7X_SUBLANES, CONV_CH), _F32)],
        compiler_params=_params(1),
        name="odd_in_proj",
    )(x, g.reshape(1, d), w, a_log, dt_bias, conv_w, conv_prev8)


def _delta_kernel(act_ref, z_ref, gb_ref, s0_ref, ng_ref, o_ref, s_out_ref, st_ref, *, n_sub):
    c = pl.program_id(1)

    @pl.when(c == 0)
    def _():
        st_ref[...] = s0_ref[0]

    t_i = lax.broadcasted_iota(jnp.int32, (CHUNK, V7X_LANES), 0)
    s_i = lax.broadcasted_iota(jnp.int32, (CHUNK, V7X_LANES), 1)
    lower_half = s_i < CHUNK
    s_mod = jnp.where(lower_half, s_i, s_i - CHUNK)
    incl = t_i >= s_mod
    strict = t_i > s_mod
    eye = (t_i == s_mod).astype(_F32)
    incl64 = (lax.broadcasted_iota(jnp.int32, (CHUNK, CHUNK), 0)
              >= lax.broadcasted_iota(jnp.int32, (CHUNK, CHUNK), 1)).astype(_F32)
    eye128 = (lax.broadcasted_iota(jnp.int32, (V7X_LANES, V7X_LANES), 0)
              == lax.broadcasted_iota(jnp.int32, (V7X_LANES, V7X_LANES), 1)).astype(_F32)

    def split3(x):
        hi16 = x.astype(_BF16)
        hi = hi16.astype(_F32)
        lo = x - hi
        lhs = jnp.concatenate([jnp.where(lower_half, hi, lo).astype(_BF16), hi16], axis=1)
        lo16 = lo.astype(_BF16)
        rhs = jnp.concatenate([hi16, hi16, lo16, jnp.zeros_like(lo16)], axis=0)
        return lhs, rhs

    def mm(lhs, rhs):
        return jnp.dot(lhs, rhs, preferred_element_type=_F32)

    heads = range(C_HEADS)
    units = [(s, h) for s in range(n_sub) for h in heads]
    g_lasts = []
    qs, ks, kbs, decays, rhss, qgs, kdecs = [], [], [], [], [], [], []
    for s in range(n_sub):
        rows = slice(s * CHUNK, (s + 1) * CHUNK)
        gb = gb_ref[0, rows, :]
        gcum_all = _dot_hi(incl64, gb)
        gcum_rows = _dot_hi(eye128, jnp.concatenate([gcum_all, gcum_all], axis=0), _NT)
        g_last = gcum_all[CHUNK - 1:CHUNK, :]
        g_lasts.append(g_last)
        for h in heads:
            q = act_ref[0, rows, h * C_HEAD_DIM:(h + 1) * C_HEAD_DIM]
            k = act_ref[0, rows, C_W + h * C_HEAD_DIM:C_W + (h + 1) * C_HEAD_DIM]
            v = act_ref[0, rows, 2 * C_W + h * C_HEAD_DIM:2 * C_W + (h + 1) * C_HEAD_DIM]
            beta = gb[:, C_HEADS + h:C_HEADS + h + 1]
            gcum = gcum_all[:, h:h + 1]
            kb = k * beta
            qs.append(q)
            ks.append(k)
            kbs.append(kb)
            decays.append(jnp.exp(jnp.where(incl, gcum - gcum_rows[h:h + 1, :], _NEG)))
            rhss.append(jnp.concatenate([v * beta, kb * jnp.exp(gcum)], axis=-1))
            qgs.append(q * jnp.exp(gcum))
            kdecs.append(k * jnp.exp(g_last[:, h:h + 1] - gcum))

    n_units = range(len(units))
    lowers, attns = [], []
    for u in n_units:
        prod = _dot(jnp.concatenate([kbs[u], qs[u]], axis=0),
                    jnp.concatenate([ks[u], ks[u]], axis=0), _NT)
        lowers.append(jnp.where(strict, prod[:CHUNK] * decays[u], 0.0))
        attns.append(jnp.where(incl, prod[CHUNK:] * decays[u], 0.0))

    pws = [-lowers[u] for u in n_units]
    invs = [eye + pws[u] for u in n_units]
    pw_l, pw_r = zip(*[split3(pws[u]) for u in n_units])
    pws = [mm(pw_l[u], pw_r[u]) for u in n_units]
    for _ in range(4):
        pw_l, pw_r = zip(*[split3(pws[u]) for u in n_units])
        both = [mm(jnp.concatenate([split3(invs[u])[0], pw_l[u]], axis=0), pw_r[u]) for u in n_units]
        invs = [invs[u] + both[u][:CHUNK] for u in n_units]
        pws = [both[u][CHUNK:] for u in n_units]
    invs = [invs[u] + mm(split3(invs[u])[0], split3(pws[u])[1]) for u in n_units]
    sols = []
    for u in n_units:
        r_hi16 = rhss[u].astype(_BF16)
        r_lo16 = (rhss[u] - r_hi16.astype(_F32)).astype(_BF16)
        sols.append(mm(split3(invs[u])[0], jnp.concatenate([r_hi16, r_hi16, r_lo16, jnp.zeros_like(r_lo16)], axis=0)))

    st = [st_ref[h] for h in heads]
    for u, (s, h) in enumerate(units):
        rows = slice(s * CHUNK, (s + 1) * CHUNK)
        w_u = sols[u][:, :C_HEAD_DIM]
        w_w = sols[u][:, C_HEAD_DIM:]
        ws = _dot(jnp.concatenate([w_w, qgs[u]], axis=0), st[h])
        v_new = w_u - ws[:CHUNK]
        o = ws[CHUNK:] + _dot(attns[u][:, :CHUNK], v_new)
        st[h] = jnp.exp(g_lasts[s][:, h:h + 1]) * st[h] + _dot(kdecs[u], v_new, _TN)
        cols = slice(h * C_HEAD_DIM, (h + 1) * C_HEAD_DIM)
        o_ref[0, rows, cols] = (_rms(o, ng_ref[...]) * _silu(z_ref[0, rows, cols])).astype(_BF16)
    for h in heads:
        st_ref[h] = st[h]

    @pl.when(c == pl.num_programs(1) - 1)
    def _():
        s_out_ref[0] = st_ref[...]


def _delta(act, z, gb, s0, norm_g):
    bsz, lq, _ = act.shape
    n_sub = min(2, lq // CHUNK)
    rows_all = n_sub * CHUNK
    tok = lambda n: pl.BlockSpec((1, rows_all, n), lambda b, c: (b, c, 0))
    state = pl.BlockSpec((1, C_HEADS, C_HEAD_DIM, C_HEAD_DIM), lambda b, c: (b, 0, 0, 0))
    return pl.pallas_call(
        functools.partial(_delta_kernel, n_sub=n_sub),
        grid=(bsz, lq // rows_all),
        in_specs=[tok(CONV_CH), tok(C_W), tok(V7X_LANES), state,
                  pl.BlockSpec((1, C_HEAD_DIM), lambda b, c: (0, 0))],
        out_specs=[tok(C_W), state],
        out_shape=[jax.ShapeDtypeStruct((bsz, lq, C_W), _BF16),
                   jax.ShapeDtypeStruct(s0.shape, _F32)],
        scratch_shapes=[pltpu.VMEM((C_HEADS, C_HEAD_DIM, C_HEAD_DIM), _F32)],
        compiler_params=_params(2),
        name="gated_delta",
    )(act, z, gb, s0, norm_g.reshape(1, C_HEAD_DIM))


def _prep_even_w_in(w):
    sizes = (A_W, A_W, A_W, IDX_HEADS * IDX_DIM, IDX_DIM, IDX_HEADS, B_KW, B_KW, B_VW, B_VW, B_GATE_RANK)
    offs = [0]
    for s in sizes:
        offs.append(offs[-1] + s)
    part = lambda i: w[:, offs[i]:offs[i + 1]]
    q, k, v, qi, ki, wi, qb, kb, vb, rb, ab = [part(i) for i in range(len(sizes))]
    pad = jnp.zeros((w.shape[0], _EV_IDX_W - (256 + IDX_DIM + IDX_HEADS + B_GATE_RANK)), w.dtype)
    return jnp.concatenate([q, k, v, qi, ki, wi, ab, pad, qb, kb, vb, rb], axis=1).astype(_BF16)


def _prep_odd_w_in(w):
    pad = jnp.zeros((w.shape[0], _OD_COLS - w.shape[1]), w.dtype)
    return jnp.concatenate([w, pad], axis=1).astype(_BF16)


def _lane_pad(v, n=V7X_LANES):
    return jnp.pad(v, (0, n - v.shape[0])).reshape(1, n)


def _trunk(x, past, prm):
    bsz, lq, d = x.shape
    t = bsz * lq
    xf = x.reshape(t, d)

    def ffn(xf, i, j, acts=(), proj_w=()):
        return _ffn_half(xf, prm['ffn_norm'][i, j], prm['ffn_w_gate'][i, j].astype(_BF16),
                         prm['ffn_w_up'][i, j].astype(_BF16), prm['ffn_w_down'][i, j].astype(_BF16),
                         acts, proj_w)

    xf = ffn(xf, 0, 0)
    head_mean = jnp.kron(jnp.eye(A_HEADS, dtype=_F32),
                         jnp.full((A_HEAD_DIM, A_HEAD_DIM), 1.0 / A_HEAD_DIM, _F32)).astype(_BF16)
    q_gain = (jnp.tile(prm['ev_q_norm'][0], A_HEADS) * (A_HEAD_DIM ** -0.5)).reshape(1, A_W)
    k_gain = jnp.tile(prm['ev_k_norm'][0], A_HEADS).reshape(1, A_W)
    w2 = jnp.zeros((V7X_LANES, B_KW), _F32).at[
        _EV_AB_OFF - 256:_EV_AB_OFF - 256 + B_GATE_RANK].set(prm['ev_gate_w2'][0])
    qn, k_new, v_new, k16, v16, idx, qb, kb, vb, rb, la = _even_in_proj(
        xf, prm['mix_norm'][0], _prep_even_w_in(prm['ev_w_in'][0]), q_gain, k_gain, head_mean,
        w2, prm['ev_gate_b2'][0].reshape(1, B_KW))
    r3 = lambda a: a.reshape(bsz, lq, a.shape[-1])
    ki_new = r3(idx)[:, :, 256:256 + IDX_DIM]
    if past is None:
        past_len = 0
        k_all, v_all, ki_all = r3(k16), r3(v16), ki_new
        s0_gla = jnp.zeros((bsz, B_HEADS, B_KEY_DIM, B_VAL_DIM), _F32)
    else:
        past_len = past['k'].shape[2]
        k_all = jnp.concatenate([past['k'][0].reshape(bsz, past_len, A_W).astype(_BF16), r3(k16)], axis=1)
        v_all = jnp.concatenate([past['v'][0].reshape(bsz, past_len, A_W).astype(_BF16), r3(v16)], axis=1)
        ki_all = jnp.concatenate([past['ki'][0], ki_new], axis=1)
        s0_gla = past['gla'][0]
    o_a = _dsa_attention(r3(qn), r3(idx), k_all, v_all, ki_all, prm['rel_bias'], past_len)
    o_b, s_gla = _gla(r3(qb), r3(kb), r3(la), r3(vb), r3(rb), s0_gla, prm['ev_gla_norm'][0])
    w_out = prm['ev_w_out'][0].astype(_BF16)
    xf = ffn(xf, 0, 1, [o_a.reshape(t, A_W), o_b.reshape(t, B_VW)], [w_out[:A_W], w_out[A_W:]])

    xf = ffn(xf, 1, 0)
    a_log = _lane_pad(prm['od_a_log'][0])
    dt_bias = _lane_pad(prm['od_dt_bias'][0])
    if past is None:
        conv_prev = jnp.zeros((bsz, CONV_WIDTH - 1, CONV_CH), _F32)
        s0_delta = jnp.zeros((bsz, C_HEADS, C_HEAD_DIM, C_HEAD_DIM), _F32)
    else:
        conv_prev, s0_delta = past['conv'][0], past['delta'][0]
    conv_prev8 = jnp.pad(conv_prev, ((0, 0), (V7X_SUBLANES - (CONV_WIDTH - 1), 0), (0, 0)))
    act, z, gb, tail = _odd_in_proj(xf, prm['mix_norm'][1], _prep_odd_w_in(prm['od_w_in'][0]), a_log, dt_bias,
                                    prm['od_conv_w'][0], conv_prev8, lq)
    o_c, s_delta = _delta(r3(act), r3(z), r3(gb), s0_delta, prm['od_norm'][0])
    conv_new = tail[:, V7X_SUBLANES - (CONV_WIDTH - 1):]
    xf = ffn(xf, 1, 1, [o_c.reshape(t, C_W)], [prm['od_w_out'][0].astype(_BF16)])

    y = xf.reshape(bsz, lq, d)
    k_out = r3(k_new).reshape(1, bsz, lq, A_HEADS, A_HEAD_DIM)
    v_out = r3(v_new).reshape(1, bsz, lq, A_HEADS, A_HEAD_DIM)
    return y, k_out, v_out, ki_new[None], s_gla[None], s_delta[None], conv_new[None]


def kernel(x_prompt, x_sample, cache_attn_k, cache_attn_v, cache_idx_k, state_gla, state_delta, state_conv,
           ffn_norm, ffn_w_gate, ffn_w_up, ffn_w_down, mix_norm, ev_w_in, ev_q_norm, ev_k_norm, rel_bias,
           ev_gate_w2, ev_gate_b2, ev_gla_norm, ev_w_out, od_w_in, od_conv_w, od_a_log, od_dt_bias, od_norm,
           od_w_out):
    prm = {'ffn_norm': ffn_norm, 'ffn_w_gate': ffn_w_gate, 'ffn_w_up': ffn_w_up, 'ffn_w_down': ffn_w_down,
           'mix_norm': mix_norm, 'ev_w_in': ev_w_in, 'ev_q_norm': ev_q_norm, 'ev_k_norm': ev_k_norm,
           'rel_bias': rel_bias, 'ev_gate_w2': ev_gate_w2, 'ev_gate_b2': ev_gate_b2, 'ev_gla_norm': ev_gla_norm,
           'ev_w_out': ev_w_out, 'od_w_in': od_w_in, 'od_conv_w': od_conv_w, 'od_a_log': od_a_log,
           'od_dt_bias': od_dt_bias, 'od_norm': od_norm, 'od_w_out': od_w_out}
    past = {'k': cache_attn_k, 'v': cache_attn_v, 'ki': cache_idx_k, 'gla': state_gla,
            'delta': state_delta, 'conv': state_conv}
    y_prompt, p_k, p_v, p_ki, p_gla, p_delta, p_conv = _trunk(x_prompt, None, prm)
    y_sample, s_k, s_v, s_ki, s_gla, s_delta, s_conv = _trunk(x_sample, past, prm)
    return (y_prompt, y_sample, p_k, p_v, p_ki, p_gla, p_delta, p_conv,
            s_k, s_v, s_ki, s_gla, s_delta, s_conv)
```

```python
import functools
import math

import jax
import jax.numpy as jnp
from jax import lax
from jax.experimental import pallas as pl
from jax.experimental.pallas import tpu as pltpu

CHUNK = 64
EPS = 1e-6

A_HEADS = 8
A_HEAD_DIM = 64
A_W = A_HEADS * A_HEAD_DIM
IDX_HEADS = 4
IDX_DIM = 64
TOPK_MAX = 256
N_BUCKETS = 32
MAX_DISTANCE = 128

B_HEADS = 4
B_KEY_DIM = 64
B_VAL_DIM = 128
B_KW = B_HEADS * B_KEY_DIM
B_VW = B_HEADS * B_VAL_DIM
B_GATE_RANK = 16
B_GATE_TAU = 16.0

C_HEADS = 8
C_HEAD_DIM = 128
C_W = C_HEADS * C_HEAD_DIM
CONV_WIDTH = 4
CONV_CH = 3 * C_W

V7X_LANES = 128
V7X_SUBLANES = 8
V7X_VMEM_LIMIT_BYTES = 56 * 1024 * 1024

_F32 = jnp.float32
_BF16 = jnp.bfloat16
_HI = lax.Precision.HIGHEST
_NEG = -1e30
_LOG2E = math.log2(math.e)
_NN = (((1,), (0,)), ((), ()))
_NT = (((1,), (1,)), ((), ()))
_TN = (((0,), (0,)), ((), ()))

_EV_Q, _EV_K, _EV_V = 0, 512, 1024
_EV_IDX = 1536
_EV_IDX_W = 384
_EV_WI_OFF = 256 + IDX_DIM
_EV_AB_OFF = _EV_WI_OFF + IDX_HEADS
_EV_QB, _EV_KB, _EV_VB, _EV_RB = 1920, 2176, 2432, 2944
_EV_COLS = 3456
_OD_QKV, _OD_Z, _OD_AB = 0, 3072, 4096
_OD_COLS = 4224


def _params(n_axes):
    return pltpu.CompilerParams(dimension_semantics=("arbitrary",) * n_axes,
                                vmem_limit_bytes=V7X_VMEM_LIMIT_BYTES)


def _rms(x, g):
    return x * lax.rsqrt(jnp.mean(x * x, axis=-1, keepdims=True) + EPS) * g


def _silu(x):
    return x * jax.nn.sigmoid(x)


def _dot(a, b, dims=_NN):
    return lax.dot_general(a.astype(_BF16), b.astype(_BF16), dims, preferred_element_type=_F32)


def _dot_hi(a, b, dims=_NN):
    return lax.dot_general(a, b, dims, precision=_HI, preferred_element_type=_F32)


def _full(shape):
    return pl.BlockSpec(shape, lambda *_: (0,) * len(shape))


_FFN_COLS = 256


def _ffn_kernel(*refs, n_proj):
    x_ref = refs[0]
    g_ref, wg_ref, wu_ref, wd_ref, o_ref, act_ref = refs[1 + 2 * n_proj:]
    x = x_ref[...]
    for i in range(n_proj):
        x = x + jnp.dot(refs[1 + i][...], refs[1 + n_proj + i][...], preferred_element_type=_F32)
    h = _rms(x, g_ref[...]).astype(_BF16)
    for c in range(wg_ref.shape[1] // _FFN_COLS):
        cols = slice(c * _FFN_COLS, (c + 1) * _FFN_COLS)
        gate = jnp.dot(h, wg_ref[:, cols], preferred_element_type=_F32)
        up = jnp.dot(h, wu_ref[:, cols], preferred_element_type=_F32)
        act_ref[:, cols] = (_silu(gate) * up).astype(_BF16)
    o_ref[...] = x + 0.5 * jnp.dot(act_ref[...], wd_ref[...], preferred_element_type=_F32)


def _ffn_half(x, g, wg, wu, wd, acts=(), proj_w=()):
    t, d = x.shape
    ff = wg.shape[1]
    tm = min(t, 512)
    resident = lambda shape: pl.BlockSpec(shape, lambda i: (0, 0), pipeline_mode=pl.Buffered(1))
    return pl.pallas_call(
        functools.partial(_ffn_kernel, n_proj=len(acts)),
        grid=(t // tm,),
        in_specs=([pl.BlockSpec((tm, d), lambda i: (i, 0))]
                  + [pl.BlockSpec((tm, a.shape[1]), lambda i: (i, 0)) for a in acts]
                  + [resident(w.shape) for w in proj_w]
                  + [resident((1, d)), resident((d, ff)), resident((d, ff)), resident((ff, d))]),
        out_specs=pl.BlockSpec((tm, d), lambda i: (i, 0)),
        out_shape=jax.ShapeDtypeStruct((t, d), _F32),
        scratch_shapes=[pltpu.VMEM((tm, ff), _BF16)],
        compiler_params=_params(1),
        name="ffn_half",
    )(x, *acts, *proj_w, g.reshape(1, d), wg, wu, wd)


def _even_in_kernel(x_ref, g_ref, w_ref, qg_ref, kg_ref, hm_ref, w2_ref, b2_ref,
                    qn_ref, k_ref, v_ref, k16_ref, v16_ref, idx_ref, qb_ref, kb_ref, vb_ref, rb_ref, la_ref):
    h = _rms(x_ref[...], g_ref[...]).astype(_BF16)

    def proj(lo, width):
        return jnp.dot(h, w_ref[:, lo:lo + width], preferred_element_type=_F32)

    def headnorm(t, gain):
        t2 = t * t
        hi = t2.astype(_BF16)
        lo = (t2 - hi.astype(_F32)).astype(_BF16)
        ms = (jnp.dot(hi, hm_ref[...], preferred_element_type=_F32)
              + jnp.dot(lo, hm_ref[...], preferred_element_type=_F32))
        return t * lax.rsqrt(ms + EPS) * gain

    qn_ref[...] = headnorm(proj(_EV_Q, A_W), qg_ref[...]).astype(_BF16)
    k = headnorm(proj(_EV_K, A_W), kg_ref[...])
    v = proj(_EV_V, A_W)
    k_ref[...] = k
    v_ref[...] = v
    k16_ref[...] = k.astype(_BF16)
    v16_ref[...] = v.astype(_BF16)
    idx = proj(_EV_IDX, _EV_IDX_W)
    idx_ref[...] = idx
    qb_ref[...] = proj(_EV_QB, B_KW) * (B_KEY_DIM ** -0.5)
    kb_ref[...] = proj(_EV_KB, B_KW)
    vb_ref[...] = proj(_EV_VB, B_VW)
    rb_ref[...] = proj(_EV_RB, B_VW)
    z = _dot_hi(idx[:, 256:384], w2_ref[...]) + b2_ref[...]
    la_ref[...] = (jnp.minimum(z, 0.0) - jnp.log1p(jnp.exp(-jnp.abs(z)))) * (1.0 / B_GATE_TAU)


def _even_in_proj(x, g, w, q_gain, k_gain, head_mean, w2, b2):
    t, d = x.shape
    tm = min(t, 512)
    row = lambda n: pl.BlockSpec((tm, n), lambda i: (i, 0))
    outs = [(A_W, _BF16), (A_W, _F32), (A_W, _F32), (A_W, _BF16), (A_W, _BF16), (_EV_IDX_W, _F32), (B_KW, _F32), (B_KW, _F32),
            (B_VW, _F32), (B_VW, _F32), (B_KW, _F32)]
    w_spec = pl.BlockSpec((d, _EV_COLS), lambda i: (0, 0), pipeline_mode=pl.Buffered(1))
    return pl.pallas_call(
        _even_in_kernel,
        grid=(t // tm,),
        in_specs=[row(d), _full((1, d)), w_spec, _full((1, A_W)), _full((1, A_W)),
                  _full((A_W, A_W)), _full((V7X_LANES, B_KW)), _full((1, B_KW))],
        out_specs=[row(n) for n, _ in outs],
        out_shape=[jax.ShapeDtypeStruct((t, n), dt) for n, dt in outs],
        compiler_params=_params(1),
        name="even_in_proj",
    )(x, g.reshape(1, d), w, q_gain, k_gain, head_mean, w2, b2)


def _dsa_kernel(rb_ref, qt_ref, qit_ref, wit_ref, k_ref, vt_ref, ki_ref, bkt_ref, o_ref,
                key_ref, mask_ref, nb_ref, ki3_ref, acc_ref, lt_ref, p_ref, *, past, tq, kc, topk, r0_off, chunk_counts):
    t = pl.program_id(1)

    @pl.when((pl.program_id(0) == 0) & (t == 0))
    def _():
        bk = bkt_ref[...]
        for h in range(A_HEADS):
            acc = jnp.zeros(bk.shape, _F32)
            for bb in range(N_BUCKETS):
                acc = jnp.where(bk == bb, rb_ref[bb, h], acc)
            nb_ref[h] = (acc - rb_ref[N_BUCKETS // 2 - 1, h]) * _LOG2E

    q0 = past + t * tq
    n_chunks = (q0 + tq + kc - 1) // kc
    q_limit = ((q0 + lax.broadcasted_iota(jnp.int32, (1, tq), 1)) // CHUNK + 1) * CHUNK
    k_iota = lax.broadcasted_iota(jnp.int32, (kc, tq), 0)

    def admissible(r0):
        return k_iota < (q_limit - r0)

    @pl.when(t == 0)
    def _():
        def split_body(j, carry):
            r0 = pl.multiple_of(j * kc, kc)
            x = ki_ref[0, pl.ds(r0, kc), :]
            hi = x.astype(_BF16)
            ki3_ref[pl.ds(r0, kc), 0:V7X_LANES] = hi
            ki3_ref[pl.ds(r0, kc), V7X_LANES:2 * V7X_LANES] = (x - hi.astype(_F32)).astype(_BF16)
            return carry
        lax.fori_loop(0, ki3_ref.shape[0] // kc, split_body, 0)

    qit = qit_ref[0]
    q_cols = []
    for h in range(IDX_HEADS):
        x = qit[h * IDX_DIM:(h + 1) * IDX_DIM, :]
        hi = x.astype(_BF16)
        lo = (x - hi.astype(_F32)).astype(_BF16)
        q_cols.append(jnp.concatenate([hi, lo, hi, jnp.zeros_like(hi)], axis=0))
    q3 = jnp.concatenate(q_cols, axis=1)
    wit = wit_ref[0]

    upper_rows = lax.broadcasted_iota(jnp.int32, (V7X_LANES, tq), 0) >= A_HEAD_DIM
    q_pairs = []
    for pr in range(A_HEADS // 2):
        blk = qt_ref[0, pr * V7X_LANES:(pr + 1) * V7X_LANES, :].astype(_F32)
        q_pairs.append(jnp.concatenate([jnp.where(upper_rows, 0.0, blk), jnp.where(upper_rows, blk, 0.0)],
                                       axis=1).astype(_BF16))

    def scores_and_logits(n):
        d_all = jnp.dot(ki3_ref[0:n * kc, :], q3, preferred_element_type=_F32)
        for j in range(n):
            d = d_all[j * kc:(j + 1) * kc]
            s = jnp.zeros((kc, tq), _F32)
            for h in range(IDX_HEADS):
                s = s + wit[h:h + 1, :] * jnp.maximum(d[:, h * tq:(h + 1) * tq], 0.0)
            s = jnp.where(s == 0.0, 0.0, s)
            s = jnp.where(admissible(j * kc), s, -jnp.inf)
            bits = pltpu.bitcast(s, jnp.int32)
            key_ref[j * kc:(j + 1) * kc, :] = bits ^ ((bits >> 31) & jnp.int32(0x7FFFFFFF))
        for pr in range(A_HEADS // 2):
            lt2 = jnp.dot(k_ref[0, 0:n * kc, pr * V7X_LANES:(pr + 1) * V7X_LANES], q_pairs[pr],
                          preferred_element_type=_F32)
            for hh in range(2):
                lt_ref[2 * pr + hh, 0:n * kc, :] = lt2[:, hh * tq:(hh + 1) * tq] * _LOG2E
        for j in range(max(n - 2, 0), n):
            off = pl.multiple_of(jnp.maximum(j * kc - q0 + r0_off, 0), CHUNK)
            for h in range(A_HEADS):
                lt_ref[h, j * kc:(j + 1) * kc, :] += nb_ref[h, pl.ds(off, kc), :]

    int_min = jnp.int32(-2 ** 31)

    def count_keys(pred, n):
        parts = []
        for j in range(n):
            m = jnp.where(pred(key_ref[j * kc:(j + 1) * kc, :]), 1, 0)
            parts.append(m.reshape(kc // V7X_SUBLANES, V7X_SUBLANES, tq).sum(axis=0))
        while len(parts) > 1:
            parts = [sum(parts[i:i + 2]) for i in range(0, len(parts), 2)]
        return parts[0].sum(axis=0, keepdims=True)

    def radix_select(n):
        def bit_body(i, carry):
            tu, n_ge = carry
            cu = tu | jnp.left_shift(jnp.int32(1), 31 - i)
            cs = cu ^ int_min
            cnt = count_keys(lambda blk: blk >= cs, n)
            keep = cnt >= topk
            return jnp.where(keep, cu, tu), jnp.where(keep, cnt, n_ge)

        start = (jnp.zeros((1, tq), jnp.int32), jnp.full((1, tq), n * kc, jnp.int32))
        if n * kc <= topk:
            return start
        return lax.fori_loop(0, 32, bit_body, start)

    def tile_select(n):
        scores_and_logits(n)
        return radix_select(n)

    tu, n_ge = lax.switch(n_chunks - chunk_counts[0], [functools.partial(tile_select, n) for n in chunk_counts])
    ts = tu ^ int_min

    tri = (lax.broadcasted_iota(jnp.int32, (kc, kc), 0)
           >= lax.broadcasted_iota(jnp.int32, (kc, kc), 1)).astype(_BF16)

    def tie_masks():
        def gt_body(j, acc):
            r0 = pl.multiple_of(j * kc, kc)
            m = jnp.where(key_ref[pl.ds(r0, kc), :] > ts, 1, 0)
            return acc + m.reshape(kc // V7X_SUBLANES, V7X_SUBLANES, tq).sum(axis=0)

        n_greater = lax.fori_loop(0, n_chunks, gt_body, jnp.zeros((V7X_SUBLANES, tq), jnp.int32))
        room = (topk - n_greater.sum(axis=0, keepdims=True)).astype(_F32)

        def body(j, seen):
            r0 = pl.multiple_of(j * kc, kc)
            blk = key_ref[pl.ds(r0, kc), :]
            eq = blk == ts
            rank = jnp.dot(tri, jnp.where(eq, 1.0, 0.0).astype(_BF16), preferred_element_type=_F32) + seen
            val = jnp.where(blk > ts, 0.0, jnp.where(eq, jnp.where(rank <= room, 0.0, _NEG), _NEG))
            mask_ref[pl.ds(r0, kc), :] = jnp.where(admissible(r0), val, _NEG)
            return rank[kc - 1:kc, :]

        return lax.fori_loop(0, n_chunks, body, jnp.zeros((1, tq), _F32))

    def plain_masks():
        def body(j, carry):
            r0 = pl.multiple_of(j * kc, kc)
            val = jnp.where(key_ref[pl.ds(r0, kc), :] >= ts, 0.0, _NEG)
            mask_ref[pl.ds(r0, kc), :] = jnp.where(admissible(r0), val, _NEG)
            return carry

        return lax.fori_loop(0, n_chunks, body, jnp.zeros((1, tq), _F32))

    lax.cond(jnp.max(n_ge) > topk, tie_masks, plain_masks)

    acc_ref[...] = jnp.zeros_like(acc_ref)

    def attn_body(j, carry):
        ms, ls = carry
        r0 = pl.multiple_of(j * kc, kc)
        msk = mask_ref[pl.ds(r0, kc), :]
        new_ms, new_ls, alphas = [], [], []
        for h in range(A_HEADS):
            lt = lt_ref[h, pl.ds(r0, kc), :] + msk
            m_new = jnp.maximum(ms[h], lt.max(axis=0, keepdims=True))
            p = jnp.exp2(lt - m_new)
            alpha = jnp.exp2(ms[h] - m_new)
            new_ms.append(m_new)
            new_ls.append(alpha * ls[h] + p.sum(axis=0, keepdims=True))
            alphas.append(alpha)
            p_ref[h] = p.astype(_BF16)
        for h in range(A_HEADS):
            rows = slice(h * A_HEAD_DIM, (h + 1) * A_HEAD_DIM)
            pv = jnp.dot(vt_ref[0, rows, pl.ds(r0, kc)], p_ref[h], preferred_element_type=_F32)
            acc_ref[rows, :] = alphas[h] * acc_ref[rows, :] + pv
        return tuple(new_ms), tuple(new_ls)

    init = (tuple(jnp.full((1, tq), _NEG, _F32) for _ in range(A_HEADS)),
            tuple(jnp.zeros((1, tq), _F32) for _ in range(A_HEADS)))
    _, ls = lax.fori_loop(0, n_chunks, attn_body, init)

    eye = (lax.broadcasted_iota(jnp.int32, (tq, tq), 0)
           == lax.broadcasted_iota(jnp.int32, (tq, tq), 1)).astype(_BF16)
    for pair in range(A_HEADS // 2):
        lanes = slice(pair * V7X_LANES, (pair + 1) * V7X_LANES)
        o_t = jnp.concatenate(
            [acc_ref[(2 * pair + hh) * A_HEAD_DIM:(2 * pair + hh + 1) * A_HEAD_DIM, :] / ls[2 * pair + hh]
             for hh in range(2)], axis=0).astype(_BF16)
        o_ref[0, :, lanes] = lax.dot_general(eye, o_t, _NT, preferred_element_type=_F32).astype(_BF16)


def _t5_bucket_table(rel):
    half = N_BUCKETS // 2
    max_exact = half // 2
    n = jnp.abs(rel)
    nf = jnp.maximum(n, 1).astype(jnp.float32)
    large = max_exact + (jnp.log(nf / max_exact) / math.log(MAX_DISTANCE / max_exact)
                         * (half - max_exact)).astype(jnp.int32)
    large = jnp.minimum(large, half - 1)
    return jnp.where(rel > 0, half, 0) + jnp.where(n < max_exact, n, large)


def _dsa_attention(qn, idx, k_all, v_all, ki_all, rel_bias, past):
    bsz, lq, _ = qn.shape
    nk = k_all.shape[1]
    topk = min(TOPK_MAX, nk // 4)
    tq = min(V7X_LANES, lq)
    kc = 256
    nk_pad = -(-nk // kc) * kc
    pad = ((0, 0), (0, nk_pad - nk), (0, 0))
    k16 = jnp.pad(k_all, pad)
    vt16 = jnp.swapaxes(jnp.pad(v_all, pad), 1, 2)
    ki2 = jnp.pad(jnp.concatenate([ki_all, ki_all], axis=-1), pad)
    wit = jnp.swapaxes(idx[:, :, _EV_WI_OFF:_EV_WI_OFF + V7X_SUBLANES], 1, 2)
    qt = jnp.swapaxes(qn, 1, 2)
    qit = jnp.swapaxes(idx[:, :, :IDX_HEADS * IDX_DIM], 1, 2)
    r0_off = kc + MAX_DISTANCE
    nbr = r0_off + kc + tq
    rel = (jnp.arange(nbr, dtype=jnp.int32)[:, None] - r0_off) - jnp.arange(tq, dtype=jnp.int32)[None, :]
    bkt = _t5_bucket_table(rel)
    chunk_counts = sorted({(past + (t + 1) * tq + kc - 1) // kc for t in range(lq // tq)})
    assert chunk_counts == list(range(chunk_counts[0], chunk_counts[-1] + 1))
    kern = functools.partial(_dsa_kernel, past=past, tq=tq, kc=kc, topk=topk, r0_off=r0_off,
                             chunk_counts=tuple(chunk_counts))
    return pl.pallas_call(
        kern,
        grid=(bsz, lq // tq),
        in_specs=[pl.BlockSpec(memory_space=pltpu.SMEM),
                  pl.BlockSpec((1, A_W, tq), lambda b, t: (b, 0, t)),
                  pl.BlockSpec((1, IDX_HEADS * IDX_DIM, tq), lambda b, t: (b, 0, t)),
                  pl.BlockSpec((1, V7X_SUBLANES, tq), lambda b, t: (b, 0, t)),
                  pl.BlockSpec((1, nk_pad, A_W), lambda b, t: (b, 0, 0)),
                  pl.BlockSpec((1, A_W, nk_pad), lambda b, t: (b, 0, 0)),
                  pl.BlockSpec((1, nk_pad, V7X_LANES), lambda b, t: (b, 0, 0)),
                  pl.BlockSpec((nbr, tq), lambda b, t: (0, 0))],
        out_specs=pl.BlockSpec((1, tq, A_W), lambda b, t: (b, t, 0)),
        out_shape=jax.ShapeDtypeStruct((bsz, lq, A_W), _BF16),
        scratch_shapes=[pltpu.VMEM((nk_pad, tq), jnp.int32), pltpu.VMEM((nk_pad, tq), _F32),
                        pltpu.VMEM((A_HEADS, nbr, tq), _F32),
                        pltpu.VMEM((nk_pad, 2 * V7X_LANES), _BF16), pltpu.VMEM((A_W, tq), _F32),
                        pltpu.VMEM((A_HEADS, nk_pad, tq), _F32), pltpu.VMEM((A_HEADS, kc, tq), _BF16)],
        compiler_params=_params(2),
        name="dsa_attention",
    )(rel_bias, qt, qit, wit, k16, vt16, ki2, bkt)


def _split3_rows(x):
    hi = x.astype(_BF16)
    r1 = x - hi.astype(_F32)
    mid = r1.astype(_BF16)
    lo = (r1 - mid.astype(_F32)).astype(_BF16)
    return jnp.concatenate([hi, mid, lo], axis=0)


def _gla_kernel(q_ref, k_ref, g_ref, v_ref, r_ref, s0_ref, ng_ref, o_ref, s_out_ref, st_ref, *, n_sub):
    c = pl.program_id(1)
    causal = (lax.broadcasted_iota(jnp.int32, (CHUNK, CHUNK), 0)
              >= lax.broadcasted_iota(jnp.int32, (CHUNK, CHUNK), 1))
    causal3 = jnp.concatenate([causal.astype(_BF16)] * 3, axis=1)
    upper_half = lax.broadcasted_iota(jnp.int32, (CHUNK, V7X_LANES), 1) >= B_KEY_DIM
    diag = (lax.broadcasted_iota(jnp.int32, (V7X_LANES, V7X_LANES), 0)
            == lax.broadcasted_iota(jnp.int32, (V7X_LANES, V7X_LANES), 1))
    pairs = range(B_HEADS // 2)
    heads = range(B_HEADS)

    @pl.when(c == 0)
    def _():
        st_ref[...] = s0_ref[0]

    def pair_lanes(x, p):
        return x[:, p * V7X_LANES:(p + 1) * V7X_LANES]

    def head_half(x, h):
        keep = upper_half if h % 2 == 1 else jnp.logical_not(upper_half)
        return jnp.where(keep, pair_lanes(x, h // 2), 0.0)

    parts = []
    for s in range(n_sub):
        rows = slice(s * CHUNK, (s + 1) * CHUNK)
        b = jnp.dot(causal3, _split3_rows(g_ref[0, rows, :]), preferred_element_type=_F32)
        mid = CHUNK // 2
        b_mid = b[mid:mid + 1, :]
        b_last = b[CHUNK - 1:CHUNK, :]
        q = q_ref[0, rows, :]
        k = k_ref[0, rows, :]
        qe = q * jnp.exp(b)
        qm = q * jnp.exp(b - b_mid)
        km = k * jnp.exp(b_mid - b)
        kl = k * jnp.exp(b_last - b)
        vs = [v_ref[0, rows, h * B_VAL_DIM:(h + 1) * B_VAL_DIM] for h in heads]
        a = [jnp.where(causal, _dot(head_half(qm, h), pair_lanes(km, h // 2), _NT), 0.0) for h in heads]
        av = [_dot(a[h], vs[h]) for h in heads]
        upd = [_dot(head_half(kl, h), vs[h], _TN) for h in heads]
        dcol = [jnp.sum(jnp.where(diag, jnp.exp(pair_lanes(b_last, p)), 0.0), axis=1, keepdims=True)
                for p in pairs]
        parts.append((qe, av, upd, dcol))

    st = [st_ref[p * V7X_LANES:(p + 1) * V7X_LANES, :] for p in pairs]
    for s in range(n_sub):
        rows = slice(s * CHUNK, (s + 1) * CHUNK)
        qe, av, upd, dcol = parts[s]
        o = [_dot(head_half(qe, h), st[h // 2]) + av[h] for h in heads]
        st = [dcol[p] * st[p] + upd[2 * p] + upd[2 * p + 1] for p in pairs]
        for h in heads:
            cols = slice(h * B_VAL_DIM, (h + 1) * B_VAL_DIM)
            o_ref[0, rows, cols] = (_rms(o[h], ng_ref[...]) * _silu(r_ref[0, rows, cols])).astype(_BF16)
    for p in pairs:
        st_ref[p * V7X_LANES:(p + 1) * V7X_LANES, :] = st[p]

    @pl.when(c == pl.num_programs(1) - 1)
    def _():
        s_out_ref[0] = st_ref[...]


def _gla(qb, kb, la, vb, rb, s0, norm_g):
    bsz, lq, _ = qb.shape
    n_sub = min(4, lq // CHUNK)
    tok = lambda n: pl.BlockSpec((1, n_sub * CHUNK, n), lambda b, c: (b, c, 0))
    state = pl.BlockSpec((1, B_KW, B_VAL_DIM), lambda b, c: (b, 0, 0))
    o, s_new = pl.pallas_call(
        functools.partial(_gla_kernel, n_sub=n_sub),
        grid=(bsz, lq // (n_sub * CHUNK)),
        in_specs=[tok(B_KW), tok(B_KW), tok(B_KW), tok(B_VW), tok(B_VW), state,
                  pl.BlockSpec((1, B_VAL_DIM), lambda b, c: (0, 0))],
        out_specs=[tok(B_VW), state],
        out_shape=[jax.ShapeDtypeStruct((bsz, lq, B_VW), _BF16),
                   jax.ShapeDtypeStruct((bsz, B_KW, B_VAL_DIM), _F32)],
        scratch_shapes=[pltpu.VMEM((B_KW, B_VAL_DIM), _F32)],
        compiler_params=_params(2),
        name="gla",
    )(qb, kb, la, vb, rb, s0.reshape(bsz, B_KW, B_VAL_DIM), norm_g.reshape(1, B_VAL_DIM))
    return o, s_new.reshape(s0.shape)


def _odd_in_kernel(x_ref, g_ref, w_ref, alog_ref, dt_ref, cw_ref, cprev_ref, act_ref, z_ref, gb_ref, tail_ref,
                   xbuf_ref, *, seg, tiles_per_batch):
    halo = V7X_SUBLANES
    h = _rms(x_ref[...], g_ref[...]).astype(_BF16)
    z_ref[...] = jnp.dot(h, w_ref[:, _OD_Z:_OD_Z + C_W], preferred_element_type=_F32)
    ab = jnp.dot(h, w_ref[:, _OD_AB:_OD_AB + V7X_LANES], preferred_element_type=_F32)
    xa = ab + dt_ref[...]
    g = -jnp.exp(alog_ref[...]) * (jnp.maximum(xa, 0.0) + jnp.log1p(jnp.exp(-jnp.abs(xa))))
    lane = lax.broadcasted_iota(jnp.int32, ab.shape, 1)
    gb_ref[...] = jnp.where(lane < C_HEADS, g, jax.nn.sigmoid(ab))

    for s in range(x_ref.shape[0] // seg):
        rows = slice(s * seg, (s + 1) * seg)
        if tiles_per_batch > 1:
            @pl.when(pl.program_id(0) % tiles_per_batch == 0)
            def _():
                xbuf_ref[0:halo, :] = cprev_ref[0]
        else:
            xbuf_ref[0:halo, :] = cprev_ref[s]
        xbuf_ref[halo:halo + seg, :] = jnp.dot(h[rows], w_ref[:, _OD_QKV:_OD_QKV + CONV_CH],
                                               preferred_element_type=_F32)
        conv = xbuf_ref[halo:halo + seg, :] * cw_ref[CONV_WIDTH - 1:CONV_WIDTH, :]
        for j in range(CONV_WIDTH - 1):
            sh = CONV_WIDTH - 1 - j
            conv = conv + xbuf_ref[halo - sh:halo - sh + seg, :] * cw_ref[j:j + 1, :]
        tail = xbuf_ref[seg:seg + halo, :]
        xbuf_ref[0:halo, :] = tail
        tail_ref[s] = tail
        act = _silu(conv)
        for hd in range(C_HEADS):
            for part, scale in ((0, C_HEAD_DIM ** -0.5), (1, 1.0)):
                cols = slice(part * C_W + hd * C_HEAD_DIM, part * C_W + (hd + 1) * C_HEAD_DIM)
                a = act[:, cols]
                act_ref[rows, cols] = a * (lax.rsqrt(jnp.sum(a * a, axis=-1, keepdims=True) + EPS) * scale)
        act_ref[rows, 2 * C_W:] = act[:, 2 * C_W:]


def _odd_in_proj(x, g, w, a_log, dt_bias, conv_w, conv_prev8, lq):
    t, d = x.shape
    tm = min(t, 256)
    seg = min(tm, lq)
    batches_per_tile = tm // seg
    tiles_per_batch = lq // seg
    row = lambda n: pl.BlockSpec((tm, n), lambda i: (i, 0))
    per_batch = pl.BlockSpec((batches_per_tile, V7X_SUBLANES, CONV_CH), lambda i: (i // tiles_per_batch, 0, 0))
    outs = [CONV_CH, C_W, V7X_LANES]
    return pl.pallas_call(
        functools.partial(_odd_in_kernel, seg=seg, tiles_per_batch=tiles_per_batch),
        grid=(t // tm,),
        in_specs=[row(d), _full((1, d)), _full((d, _OD_COLS)), _full((1, V7X_LANES)), _full((1, V7X_LANES)),
                  _full((CONV_WIDTH, CONV_CH)), per_batch],
        out_specs=[row(n) for n in outs] + [per_batch],
        out_shape=([jax.ShapeDtypeStruct((t, n), _F32) for n in outs]
                   + [jax.ShapeDtypeStruct((t // lq, V7X_SUBLANES, CONV_CH), _F32)]),
        scratch_shapes=[pltpu.VMEM((seg + V7X_SUBLANES, CONV_CH), _F32)],
        compiler_params=_params(1),
        name="odd_in_proj",
    )(x, g.reshape(1, d), w, a_log, dt_bias, conv_w, conv_prev8)


def _delta_kernel(act_ref, z_ref, gb_ref, s0_ref, ng_ref, o_ref, s_out_ref, st_ref, *, n_sub):
    c = pl.program_id(1)

    @pl.when(c == 0)
    def _():
        st_ref[...] = s0_ref[0]

    t_i = lax.broadcasted_iota(jnp.int32, (CHUNK, V7X_LANES), 0)
    s_i = lax.broadcasted_iota(jnp.int32, (CHUNK, V7X_LANES), 1)
    lower_half = s_i < CHUNK
    s_mod = jnp.where(lower_half, s_i, s_i - CHUNK)
    incl = t_i >= s_mod
    strict = t_i > s_mod
    eye = (t_i == s_mod).astype(_F32)
    incl64 = (lax.broadcasted_iota(jnp.int32, (CHUNK, CHUNK), 0)
              >= lax.broadcasted_iota(jnp.int32, (CHUNK, CHUNK), 1)).astype(_F32)
    eye128 = (lax.broadcasted_iota(jnp.int32, (V7X_LANES, V7X_LANES), 0)
              == lax.broadcasted_iota(jnp.int32, (V7X_LANES, V7X_LANES), 1)).astype(_F32)

    def split3(x):
        hi16 = x.astype(_BF16)
        hi = hi16.astype(_F32)
        lo = x - hi
        lhs = jnp.concatenate([jnp.where(lower_half, hi, lo).astype(_BF16), hi16], axis=1)
        lo16 = lo.astype(_BF16)
        rhs = jnp.concatenate([hi16, hi16, lo16, jnp.zeros_like(lo16)], axis=0)
        return lhs, rhs

    def mm(lhs, rhs):
        return jnp.dot(lhs, rhs, preferred_element_type=_F32)

    heads = range(C_HEADS)
    units = [(s, h) for s in range(n_sub) for h in heads]
    g_lasts = []
    qs, ks, kbs, decays, rhss, qgs, kdecs = [], [], [], [], [], [], []
    for s in range(n_sub):
        rows = slice(s * CHUNK, (s + 1) * CHUNK)
        gb = gb_ref[0, rows, :]
        gcum_all = _dot_hi(incl64, gb)
        gcum_rows = _dot_hi(eye128, jnp.concatenate([gcum_all, gcum_all], axis=0), _NT)
        g_last = gcum_all[CHUNK - 1:CHUNK, :]
        g_lasts.append(g_last)
        for h in heads:
            q = act_ref[0, rows, h * C_HEAD_DIM:(h + 1) * C_HEAD_DIM]
            k = act_ref[0, rows, C_W + h * C_HEAD_DIM:C_W + (h + 1) * C_HEAD_DIM]
            v = act_ref[0, rows, 2 * C_W + h * C_HEAD_DIM:2 * C_W + (h + 1) * C_HEAD_DIM]
            beta = gb[:, C_HEADS + h:C_HEADS + h + 1]
            gcum = gcum_all[:, h:h + 1]
            kb = k * beta
            qs.append(q)
            ks.append(k)
            kbs.append(kb)
            decays.append(jnp.exp(jnp.where(incl, gcum - gcum_rows[h:h + 1, :], _NEG)))
            rhss.append(jnp.concatenate([v * beta, kb * jnp.exp(gcum)], axis=-1))
            qgs.append(q * jnp.exp(gcum))
            kdecs.append(k * jnp.exp(g_last[:, h:h + 1] - gcum))

    n_units = range(len(units))
    lowers, attns = [], []
    for u in n_units:
        prod = _dot(jnp.concatenate([kbs[u], qs[u]], axis=0),
                    jnp.concatenate([ks[u], ks[u]], axis=0), _NT)
        lowers.append(jnp.where(strict, prod[:CHUNK] * decays[u], 0.0))
        attns.append(jnp.where(incl, prod[CHUNK:] * decays[u], 0.0))

    pws = [-lowers[u] for u in n_units]
    invs = [eye + pws[u] for u in n_units]
    pw_l, pw_r = zip(*[split3(pws[u]) for u in n_units])
    pws = [mm(pw_l[u], pw_r[u]) for u in n_units]
    for _ in range(4):
        pw_l, pw_r = zip(*[split3(pws[u]) for u in n_units])
        both = [mm(jnp.concatenate([split3(invs[u])[0], pw_l[u]], axis=0), pw_r[u]) for u in n_units]
        invs = [invs[u] + both[u][:CHUNK] for u in n_units]
        pws = [both[u][CHUNK:] for u in n_units]
    invs = [invs[u] + mm(split3(invs[u])[0], split3(pws[u])[1]) for u in n_units]
    sols = []
    for u in n_units:
        r_hi16 = rhss[u].astype(_BF16)
        r_lo16 = (rhss[u] - r_hi16.astype(_F32)).astype(_BF16)
        sols.append(mm(split3(invs[u])[0], jnp.concatenate([r_hi16, r_hi16, r_lo16, jnp.zeros_like(r_lo16)], axis=0)))

    st = [st_ref[h] for h in heads]
    for u, (s, h) in enumerate(units):
        rows = slice(s * CHUNK, (s + 1) * CHUNK)
        w_u = sols[u][:, :C_HEAD_DIM]
        w_w = sols[u][:, C_HEAD_DIM:]
        ws = _dot(jnp.concatenate([w_w, qgs[u]], axis=0), st[h])
        v_new = w_u - ws[:CHUNK]
        o = ws[CHUNK:] + _dot(attns[u][:, :CHUNK], v_new)
        st[h] = jnp.exp(g_lasts[s][:, h:h + 1]) * st[h] + _dot(kdecs[u], v_new, _TN)
        cols = slice(h * C_HEAD_DIM, (h + 1) * C_HEAD_DIM)
        o_ref[0, rows, cols] = (_rms(o, ng_ref[...]) * _silu(z_ref[0, rows, cols])).astype(_BF16)
    for h in heads:
        st_ref[h] = st[h]

    @pl.when(c == pl.num_programs(1) - 1)
    def _():
        s_out_ref[0] = st_ref[...]


def _delta(act, z, gb, s0, norm_g):
    bsz, lq, _ = act.shape
    n_sub = min(4, lq // CHUNK)
    rows_all = n_sub * CHUNK
    tok = lambda n: pl.BlockSpec((1, rows_all, n), lambda b, c: (b, c, 0))
    state = pl.BlockSpec((1, C_HEADS, C_HEAD_DIM, C_HEAD_DIM), lambda b, c: (b, 0, 0, 0))
    return pl.pallas_call(
        functools.partial(_delta_kernel, n_sub=n_sub),
        grid=(bsz, lq // rows_all),
        in_specs=[tok(CONV_CH), tok(C_W), tok(V7X_LANES), state,
                  pl.BlockSpec((1, C_HEAD_DIM), lambda b, c: (0, 0))],
        out_specs=[tok(C_W), state],
        out_shape=[jax.ShapeDtypeStruct((bsz, lq, C_W), _BF16),
                   jax.ShapeDtypeStruct(s0.shape, _F32)],
        scratch_shapes=[pltpu.VMEM((C_HEADS, C_HEAD_DIM, C_HEAD_DIM), _F32)],
        compiler_params=_params(2),
        name="gated_delta",
    )(act, z, gb, s0, norm_g.reshape(1, C_HEAD_DIM))


def _prep_even_w_in(w):
    sizes = (A_W, A_W, A_W, IDX_HEADS * IDX_DIM, IDX_DIM, IDX_HEADS, B_KW, B_KW, B_VW, B_VW, B_GATE_RANK)
    offs = [0]
    for s in sizes:
        offs.append(offs[-1] + s)
    part = lambda i: w[:, offs[i]:offs[i + 1]]
    q, k, v, qi, ki, wi, qb, kb, vb, rb, ab = [part(i) for i in range(len(sizes))]
    pad = jnp.zeros((w.shape[0], _EV_IDX_W - (256 + IDX_DIM + IDX_HEADS + B_GATE_RANK)), w.dtype)
    return jnp.concatenate([q, k, v, qi, ki, wi, ab, pad, qb, kb, vb, rb], axis=1).astype(_BF16)


def _prep_odd_w_in(w):
    pad = jnp.zeros((w.shape[0], _OD_COLS - w.shape[1]), w.dtype)
    return jnp.concatenate([w, pad], axis=1).astype(_BF16)


def _lane_pad(v, n=V7X_LANES):
    return jnp.pad(v, (0, n - v.shape[0])).reshape(1, n)


def _trunk(x, past, prm):
    bsz, lq, d = x.shape
    t = bsz * lq
    xf = x.reshape(t, d)

    def ffn(xf, i, j, acts=(), proj_w=()):
        return _ffn_half(xf, prm['ffn_norm'][i, j], prm['ffn_w_gate'][i, j].astype(_BF16),
                         prm['ffn_w_up'][i, j].astype(_BF16), prm['ffn_w_down'][i, j].astype(_BF16),
                         acts, proj_w)

    xf = ffn(xf, 0, 0)
    head_mean = jnp.kron(jnp.eye(A_HEADS, dtype=_F32),
                         jnp.full((A_HEAD_DIM, A_HEAD_DIM), 1.0 / A_HEAD_DIM, _F32)).astype(_BF16)
    q_gain = (jnp.tile(prm['ev_q_norm'][0], A_HEADS) * (A_HEAD_DIM ** -0.5)).reshape(1, A_W)
    k_gain = jnp.tile(prm['ev_k_norm'][0], A_HEADS).reshape(1, A_W)
    w2 = jnp.zeros((V7X_LANES, B_KW), _F32).at[
        _EV_AB_OFF - 256:_EV_AB_OFF - 256 + B_GATE_RANK].set(prm['ev_gate_w2'][0])
    qn, k_new, v_new, k16, v16, idx, qb, kb, vb, rb, la = _even_in_proj(
        xf, prm['mix_norm'][0], _prep_even_w_in(prm['ev_w_in'][0]), q_gain, k_gain, head_mean,
        w2, prm['ev_gate_b2'][0].reshape(1, B_KW))
    r3 = lambda a: a.reshape(bsz, lq, a.shape[-1])
    ki_new = r3(idx)[:, :, 256:256 + IDX_DIM]
    if past is None:
        past_len = 0
        k_all, v_all, ki_all = r3(k16), r3(v16), ki_new
        s0_gla = jnp.zeros((bsz, B_HEADS, B_KEY_DIM, B_VAL_DIM), _F32)
    else:
        past_len = past['k'].shape[2]
        k_all = jnp.concatenate([past['k'][0].reshape(bsz, past_len, A_W).astype(_BF16), r3(k16)], axis=1)
        v_all = jnp.concatenate([past['v'][0].reshape(bsz, past_len, A_W).astype(_BF16), r3(v16)], axis=1)
        ki_all = jnp.concatenate([past['ki'][0], ki_new], axis=1)
        s0_gla = past['gla'][0]
    o_a = _dsa_attention(r3(qn), r3(idx), k_all, v_all, ki_all, prm['rel_bias'], past_len)
    o_b, s_gla = _gla(r3(qb), r3(kb), r3(la), r3(vb), r3(rb), s0_gla, prm['ev_gla_norm'][0])
    w_out = prm['ev_w_out'][0].astype(_BF16)
    xf = ffn(xf, 0, 1, [o_a.reshape(t, A_W), o_b.reshape(t, B_VW)], [w_out[:A_W], w_out[A_W:]])

    xf = ffn(xf, 1, 0)
    a_log = _lane_pad(prm['od_a_log'][0])
    dt_bias = _lane_pad(prm['od_dt_bias'][0])
    if past is None:
        conv_prev = jnp.zeros((bsz, CONV_WIDTH - 1, CONV_CH), _F32)
        s0_delta = jnp.zeros((bsz, C_HEADS, C_HEAD_DIM, C_HEAD_DIM), _F32)
    else:
        conv_prev, s0_delta = past['conv'][0], past['delta'][0]
    conv_prev8 = jnp.pad(conv_prev, ((0, 0), (V7X_SUBLANES - (CONV_WIDTH - 1), 0), (0, 0)))
    act, z, gb, tail = _odd_in_proj(xf, prm['mix_norm'][1], _prep_odd_w_in(prm['od_w_in'][0]), a_log, dt_bias,
                                    prm['od_conv_w'][0], conv_prev8, lq)
    o_c, s_delta = _delta(r3(act), r3(z), r3(gb), s0_delta, prm['od_norm'][0])
    conv_new = tail[:, V7X_SUBLANES - (CONV_WIDTH - 1):]
    xf = ffn(xf, 1, 1, [o_c.reshape(t, C_W)], [prm['od_w_out'][0].astype(_BF16)])

    y = xf.reshape(bsz, lq, d)
    k_out = r3(k_new).reshape(1, bsz, lq, A_HEADS, A_HEAD_DIM)
    v_out = r3(v_new).reshape(1, bsz, lq, A_HEADS, A_HEAD_DIM)
    return y, k_out, v_out, ki_new[None], s_gla[None], s_delta[None], conv_new[None]


def kernel(x_prompt, x_sample, cache_attn_k, cache_attn_v, cache_idx_k, state_gla, state_delta, state_conv,
           ffn_norm, ffn_w_gate, ffn_w_up, ffn_w_down, mix_norm, ev_w_in, ev_q_norm, ev_k_norm, rel_bias,
           ev_gate_w2, ev_gate_b2, ev_gla_norm, ev_w_out, od_w_in, od_conv_w, od_a_log, od_dt_bias, od_norm,
           od_w_out):
    prm = {'ffn_norm': ffn_norm, 'ffn_w_gate': ffn_w_gate, 'ffn_w_up': ffn_w_up, 'ffn_w_down': ffn_w_down,
           'mix_norm': mix_norm, 'ev_w_in': ev_w_in, 'ev_q_norm': ev_q_norm, 'ev_k_norm': ev_k_norm,
           'rel_bias': rel_bias, 'ev_gate_w2': ev_gate_w2, 'ev_gate_b2': ev_gate_b2, 'ev_gla_norm': ev_gla_norm,
           'ev_w_out': ev_w_out, 'od_w_in': od_w_in, 'od_conv_w': od_conv_w, 'od_a_log': od_a_log,
           'od_dt_bias': od_dt_bias, 'od_norm': od_norm, 'od_w_out': od_w_out}
    past = {'k': cache_attn_k, 'v': cache_attn_v, 'ki': cache_idx_k, 'gla': state_gla,
            'delta': state_delta, 'conv': state_conv}
    y_prompt, p_k, p_v, p_ki, p_gla, p_delta, p_conv = _trunk(x_prompt, None, prm)
    y_sample, s_k, s_v, s_ki, s_gla, s_delta, s_conv = _trunk(x_sample, past, prm)
    return (y_prompt, y_sample, p_k, p_v, p_ki, p_gla, p_delta, p_conv,
            s_k, s_v, s_ki, s_gla, s_delta, s_conv)
```

```python
import functools
import math

import jax
import jax.numpy as jnp
from jax import lax
from jax.experimental import pallas as pl
from jax.experimental.pallas import tpu as pltpu

CHUNK = 64
EPS = 1e-6

A_HEADS = 8
A_HEAD_DIM = 64
A_W = A_HEADS * A_HEAD_DIM
IDX_HEADS = 4
IDX_DIM = 64
TOPK_MAX = 256
N_BUCKETS = 32
MAX_DISTANCE = 128

B_HEADS = 4
B_KEY_DIM = 64
B_VAL_DIM = 128
B_KW = B_HEADS * B_KEY_DIM
B_VW = B_HEADS * B_VAL_DIM
B_GATE_RANK = 16
B_GATE_TAU = 16.0

C_HEADS = 8
C_HEAD_DIM = 128
C_W = C_HEADS * C_HEAD_DIM
CONV_WIDTH = 4
CONV_CH = 3 * C_W

V7X_LANES = 128
V7X_SUBLANES = 8
V7X_VMEM_LIMIT_BYTES = 56 * 1024 * 1024
V7X_MXU_DIM = 256

ROW_TILE = 2 * V7X_MXU_DIM
ODD_ROW_TILE = V7X_MXU_DIM
DSA_KEY_CHUNK = V7X_MXU_DIM
GLA_SUBCHUNKS = 4
DELTA_SUBCHUNKS = 4

_F32 = jnp.float32
_BF16 = jnp.bfloat16
_NEG = -1e30
_LOG2E = math.log2(math.e)
_NN = (((1,), (0,)), ((), ()))
_NT = (((1,), (1,)), ((), ()))
_TN = (((0,), (0,)), ((), ()))

_EV_Q, _EV_K, _EV_V = 0, 512, 1024
_EV_IDX = 1536
_EV_IDX_W = 384
_EV_WI_OFF = 256 + IDX_DIM
_EV_AB_OFF = _EV_WI_OFF + IDX_HEADS
_EV_QB, _EV_KB, _EV_VB, _EV_RB = 1920, 2176, 2432, 2944
_EV_COLS = 3456
_OD_QKV, _OD_Z, _OD_AB = 0, 3072, 4096
_OD_COLS = 4224


def _params(n_axes):
    return pltpu.CompilerParams(dimension_semantics=("arbitrary",) * n_axes,
                                vmem_limit_bytes=V7X_VMEM_LIMIT_BYTES)


def _rms(x, g):
    return x * lax.rsqrt(jnp.mean(x * x, axis=-1, keepdims=True) + EPS) * g


def _silu(x):
    return x * jax.nn.sigmoid(x)


def _dot(a, b, dims=_NN):
    return lax.dot_general(a.astype(_BF16), b.astype(_BF16), dims, preferred_element_type=_F32)


def _full(shape):
    return pl.BlockSpec(shape, lambda *_: (0,) * len(shape))


_FFN_COLS = V7X_MXU_DIM


def _ffn_kernel(*refs, n_proj):
    x_ref = refs[0]
    g_ref, wg_ref, wu_ref, wd_ref, o_ref, act_ref = refs[1 + 2 * n_proj:]
    x = x_ref[...]
    for i in range(n_proj):
        x = x + jnp.dot(refs[1 + i][...], refs[1 + n_proj + i][...], preferred_element_type=_F32)
    h = _rms(x, g_ref[...]).astype(_BF16)
    for c in range(wg_ref.shape[1] // _FFN_COLS):
        cols = slice(c * _FFN_COLS, (c + 1) * _FFN_COLS)
        gate = jnp.dot(h, wg_ref[:, cols], preferred_element_type=_F32)
        up = jnp.dot(h, wu_ref[:, cols], preferred_element_type=_F32)
        act_ref[:, cols] = (_silu(gate) * up).astype(_BF16)
    o_ref[...] = x + 0.5 * jnp.dot(act_ref[...], wd_ref[...], preferred_element_type=_F32)


def _ffn_half(x, g, wg, wu, wd, acts=(), proj_w=()):
    t, d = x.shape
    ff = wg.shape[1]
    tm = min(t, ROW_TILE)
    resident = lambda shape: pl.BlockSpec(shape, lambda i: (0, 0), pipeline_mode=pl.Buffered(1))
    return pl.pallas_call(
        functools.partial(_ffn_kernel, n_proj=len(acts)),
        grid=(t // tm,),
        in_specs=([pl.BlockSpec((tm, d), lambda i: (i, 0))]
                  + [pl.BlockSpec((tm, a.shape[1]), lambda i: (i, 0)) for a in acts]
                  + [resident(w.shape) for w in proj_w]
                  + [resident((1, d)), resident((d, ff)), resident((d, ff)), resident((ff, d))]),
        out_specs=pl.BlockSpec((tm, d), lambda i: (i, 0)),
        out_shape=jax.ShapeDtypeStruct((t, d), _F32),
        scratch_shapes=[pltpu.VMEM((tm, ff), _BF16)],
        compiler_params=_params(1),
        name="ffn_half",
    )(x, *acts, *proj_w, g.reshape(1, d), wg, wu, wd)


def _even_in_kernel(x_ref, g_ref, w_ref, qg_ref, kg_ref, hm_ref, w2_ref, b2_ref,
                    k_ref, v_ref, k16_ref, idx_ref, qb_ref, kb_ref, vb_ref, rb_ref, la_ref,
                    qt_ref, vt_ref, qit_ref, wit_ref, *, seg):
    h = _rms(x_ref[...], g_ref[...]).astype(_BF16)

    def proj(lo, width):
        return jnp.dot(h, w_ref[:, lo:lo + width], preferred_element_type=_F32)

    def headnorm(t, gain):
        t2 = t * t
        hi = t2.astype(_BF16)
        lo = (t2 - hi.astype(_F32)).astype(_BF16)
        ms = (jnp.dot(hi, hm_ref[...], preferred_element_type=_F32)
              + jnp.dot(lo, hm_ref[...], preferred_element_type=_F32))
        return t * lax.rsqrt(ms + EPS) * gain

    qn = headnorm(proj(_EV_Q, A_W), qg_ref[...]).astype(_BF16)
    k = headnorm(proj(_EV_K, A_W), kg_ref[...])
    v = proj(_EV_V, A_W)
    k_ref[...] = k
    v_ref[...] = v
    k16_ref[...] = k.astype(_BF16)
    v16 = v.astype(_BF16)
    idx = proj(_EV_IDX, _EV_IDX_W)
    idx_ref[...] = idx
    for s in range(x_ref.shape[0] // seg):
        rows = slice(s * seg, (s + 1) * seg)
        qt_ref[s] = qn[rows].T
        vt_ref[s] = v16[rows].T
        qit_ref[s] = idx[rows, 0:IDX_HEADS * IDX_DIM].T
        wit_ref[s] = idx[rows, IDX_HEADS * IDX_DIM:_EV_IDX_W].T[IDX_DIM:IDX_DIM + V7X_SUBLANES]
    qb_ref[...] = proj(_EV_QB, B_KW) * (B_KEY_DIM ** -0.5)
    kb_ref[...] = proj(_EV_KB, B_KW)
    vb_ref[...] = proj(_EV_VB, B_VW)
    rb_ref[...] = proj(_EV_RB, B_VW)
    z = _dot(idx[:, 256:384], w2_ref[...]) + b2_ref[...]
    la_ref[...] = (jnp.minimum(z, 0.0) - jnp.log1p(jnp.exp(-jnp.abs(z)))) * (1.0 / B_GATE_TAU)


def _even_in_proj(x, g, w, q_gain, k_gain, head_mean, w2, b2, lq):
    t, d = x.shape
    tm = min(t, ROW_TILE)
    seg = min(tm, lq)
    tiles_per_batch = lq // seg
    row = lambda n: pl.BlockSpec((tm, n), lambda i: (i, 0))
    col = lambda n: pl.BlockSpec((tm // seg, n, seg), lambda i: (i // tiles_per_batch, 0, i % tiles_per_batch))
    outs = [(A_W, _F32), (A_W, _F32), (A_W, _BF16), (_EV_IDX_W, _F32), (B_KW, _F32), (B_KW, _F32),
            (B_VW, _F32), (B_VW, _F32), (B_KW, _F32)]
    outs_t = [(A_W, _BF16), (A_W, _BF16), (IDX_HEADS * IDX_DIM, _F32), (V7X_SUBLANES, _F32)]
    w_spec = pl.BlockSpec((d, _EV_COLS), lambda i: (0, 0), pipeline_mode=pl.Buffered(1))
    return pl.pallas_call(
        functools.partial(_even_in_kernel, seg=seg),
        grid=(t // tm,),
        in_specs=[row(d), _full((1, d)), w_spec, _full((1, A_W)), _full((1, A_W)),
                  _full((A_W, A_W)), _full((V7X_LANES, B_KW)), _full((1, B_KW))],
        out_specs=[row(n) for n, _ in outs] + [col(n) for n, _ in outs_t],
        out_shape=([jax.ShapeDtypeStruct((t, n), dt) for n, dt in outs]
                   + [jax.ShapeDtypeStruct((t // lq, n, lq), dt) for n, dt in outs_t]),
        compiler_params=_params(1),
        name="even_in_proj",
    )(x, g.reshape(1, d), w, q_gain, k_gain, head_mean, w2, b2)


def _dsa_kernel(rb_ref, qt_ref, qit_ref, wit_ref, k_ref, vt_ref, ki_ref, bkt_ref, o_ref,
                key_ref, mask_ref, nb_ref, ki3_ref, acc_ref, lt_ref, p_ref, *, past, tq, kc, topk, r0_off, chunk_counts):
    t = pl.program_id(1)

    @pl.when((pl.program_id(0) == 0) & (t == 0))
    def _():
        bk = bkt_ref[...]
        for h in range(A_HEADS):
            acc = jnp.zeros(bk.shape, _F32)
            for bb in range(N_BUCKETS):
                acc = jnp.where(bk == bb, rb_ref[bb, h], acc)
            nb_ref[h] = (acc - rb_ref[N_BUCKETS // 2 - 1, h]) * _LOG2E

    q0 = past + t * tq
    n_chunks = (q0 + tq + kc - 1) // kc
    q_limit = ((q0 + lax.broadcasted_iota(jnp.int32, (1, tq), 1)) // CHUNK + 1) * CHUNK
    k_iota = lax.broadcasted_iota(jnp.int32, (kc, tq), 0)

    def admissible(r0):
        return k_iota < (q_limit - r0)

    @pl.when(t == 0)
    def _():
        def split_body(j, carry):
            r0 = pl.multiple_of(j * kc, kc)
            x = ki_ref[0, pl.ds(r0, kc), :]
            hi = x.astype(_BF16)
            ki3_ref[pl.ds(r0, kc), 0:V7X_LANES] = hi
            ki3_ref[pl.ds(r0, kc), V7X_LANES:2 * V7X_LANES] = (x - hi.astype(_F32)).astype(_BF16)
            return carry
        lax.fori_loop(0, ki3_ref.shape[0] // kc, split_body, 0)

    qit = qit_ref[0]
    q_cols = []
    for h in range(IDX_HEADS):
        x = qit[h * IDX_DIM:(h + 1) * IDX_DIM, :]
        hi = x.astype(_BF16)
        lo = (x - hi.astype(_F32)).astype(_BF16)
        q_cols.append(jnp.concatenate([hi, lo, hi, jnp.zeros_like(hi)], axis=0))
    q3 = jnp.concatenate(q_cols, axis=1)
    wit = wit_ref[0]

    upper_rows = lax.broadcasted_iota(jnp.int32, (V7X_LANES, tq), 0) >= A_HEAD_DIM
    q_pairs = []
    for pr in range(A_HEADS // 2):
        blk = qt_ref[0, pr * V7X_LANES:(pr + 1) * V7X_LANES, :].astype(_F32)
        q_pairs.append(jnp.concatenate([jnp.where(upper_rows, 0.0, blk), jnp.where(upper_rows, blk, 0.0)],
                                       axis=1).astype(_BF16))

    def scores_and_logits(n):
        d_all = jnp.dot(ki3_ref[0:n * kc, :], q3, preferred_element_type=_F32)
        for j in range(n):
            d = d_all[j * kc:(j + 1) * kc]
            s = jnp.zeros((kc, tq), _F32)
            for h in range(IDX_HEADS):
                s = s + wit[h:h + 1, :] * jnp.maximum(d[:, h * tq:(h + 1) * tq], 0.0)
            s = jnp.where(s == 0.0, 0.0, s)
            s = jnp.where(admissible(j * kc), s, -jnp.inf)
            bits = pltpu.bitcast(s, jnp.int32)
            key_ref[j * kc:(j + 1) * kc, :] = bits ^ ((bits >> 31) & jnp.int32(0x7FFFFFFF))
        for pr in range(A_HEADS // 2):
            lt2 = jnp.dot(k_ref[0, 0:n * kc, pr * V7X_LANES:(pr + 1) * V7X_LANES], q_pairs[pr],
                          preferred_element_type=_F32)
            for hh in range(2):
                lt_ref[2 * pr + hh, 0:n * kc, :] = lt2[:, hh * tq:(hh + 1) * tq] * _LOG2E
        for j in range(max(n - 2, 0), n):
            off = pl.multiple_of(jnp.maximum(j * kc - q0 + r0_off, 0), CHUNK)
            for h in range(A_HEADS):
                lt_ref[h, j * kc:(j + 1) * kc, :] += nb_ref[h, pl.ds(off, kc), :]

    int_min = jnp.int32(-2 ** 31)

    def count_keys(pred, n):
        parts = []
        for j in range(n):
            m = jnp.where(pred(key_ref[j * kc:(j + 1) * kc, :]), 1, 0)
            parts.append(m.reshape(kc // V7X_SUBLANES, V7X_SUBLANES, tq).sum(axis=0))
        while len(parts) > 1:
            parts = [sum(parts[i:i + 2]) for i in range(0, len(parts), 2)]
        return parts[0].sum(axis=0, keepdims=True)

    def radix_select(n):
        def bit_body(i, carry):
            tu, n_ge = carry
            cu = tu | jnp.left_shift(jnp.int32(1), 31 - i)
            cs = cu ^ int_min
            cnt = count_keys(lambda blk: blk >= cs, n)
            keep = cnt >= topk
            return jnp.where(keep, cu, tu), jnp.where(keep, cnt, n_ge)

        start = (jnp.zeros((1, tq), jnp.int32), jnp.full((1, tq), n * kc, jnp.int32))
        if n * kc <= topk:
            return start
        return lax.fori_loop(0, 32, bit_body, start)

    def tile_select(n):
        scores_and_logits(n)
        return radix_select(n)

    tu, n_ge = lax.switch(n_chunks - chunk_counts[0], [functools.partial(tile_select, n) for n in chunk_counts])
    ts = tu ^ int_min

    tri = (lax.broadcasted_iota(jnp.int32, (kc, kc), 0)
           >= lax.broadcasted_iota(jnp.int32, (kc, kc), 1)).astype(_BF16)

    def tie_masks():
        def gt_body(j, acc):
            r0 = pl.multiple_of(j * kc, kc)
            m = jnp.where(key_ref[pl.ds(r0, kc), :] > ts, 1, 0)
            return acc + m.reshape(kc // V7X_SUBLANES, V7X_SUBLANES, tq).sum(axis=0)

        n_greater = lax.fori_loop(0, n_chunks, gt_body, jnp.zeros((V7X_SUBLANES, tq), jnp.int32))
        room = (topk - n_greater.sum(axis=0, keepdims=True)).astype(_F32)

        def body(j, seen):
            r0 = pl.multiple_of(j * kc, kc)
            blk = key_ref[pl.ds(r0, kc), :]
            eq = blk == ts
            rank = jnp.dot(tri, jnp.where(eq, 1.0, 0.0).astype(_BF16), preferred_element_type=_F32) + seen
            val = jnp.where(blk > ts, 0.0, jnp.where(eq, jnp.where(rank <= room, 0.0, _NEG), _NEG))
            mask_ref[pl.ds(r0, kc), :] = jnp.where(admissible(r0), val, _NEG)
            return rank[kc - 1:kc, :]

        return lax.fori_loop(0, n_chunks, body, jnp.zeros((1, tq), _F32))

    def plain_masks():
        def body(j, carry):
            r0 = pl.multiple_of(j * kc, kc)
            val = jnp.where(key_ref[pl.ds(r0, kc), :] >= ts, 0.0, _NEG)
            mask_ref[pl.ds(r0, kc), :] = jnp.where(admissible(r0), val, _NEG)
            return carry

        return lax.fori_loop(0, n_chunks, body, jnp.zeros((1, tq), _F32))

    lax.cond(jnp.max(n_ge) > topk, tie_masks, plain_masks)

    acc_ref[...] = jnp.zeros_like(acc_ref)

    def attn_body(j, carry):
        ms, ls = carry
        r0 = pl.multiple_of(j * kc, kc)
        msk = mask_ref[pl.ds(r0, kc), :]
        new_ms, new_ls, alphas = [], [], []
        for h in range(A_HEADS):
            lt = lt_ref[h, pl.ds(r0, kc), :] + msk
            m_new = jnp.maximum(ms[h], lt.max(axis=0, keepdims=True))
            p = jnp.exp2(lt - m_new)
            alpha = jnp.exp2(ms[h] - m_new)
            new_ms.append(m_new)
            new_ls.append(alpha * ls[h] + p.sum(axis=0, keepdims=True))
            alphas.append(alpha)
            p_ref[h] = p.astype(_BF16)
        for h in range(A_HEADS):
            rows = slice(h * A_HEAD_DIM, (h + 1) * A_HEAD_DIM)
            pv = jnp.dot(vt_ref[0, rows, pl.ds(r0, kc)], p_ref[h], preferred_element_type=_F32)
            acc_ref[rows, :] = alphas[h] * acc_ref[rows, :] + pv
        return tuple(new_ms), tuple(new_ls)

    init = (tuple(jnp.full((1, tq), _NEG, _F32) for _ in range(A_HEADS)),
            tuple(jnp.zeros((1, tq), _F32) for _ in range(A_HEADS)))
    _, ls = lax.fori_loop(0, n_chunks, attn_body, init)

    eye = (lax.broadcasted_iota(jnp.int32, (tq, tq), 0)
           == lax.broadcasted_iota(jnp.int32, (tq, tq), 1)).astype(_BF16)
    for pair in range(A_HEADS // 2):
        lanes = slice(pair * V7X_LANES, (pair + 1) * V7X_LANES)
        o_t = jnp.concatenate(
            [acc_ref[(2 * pair + hh) * A_HEAD_DIM:(2 * pair + hh + 1) * A_HEAD_DIM, :] / ls[2 * pair + hh]
             for hh in range(2)], axis=0).astype(_BF16)
        o_ref[0, :, lanes] = lax.dot_general(eye, o_t, _NT, preferred_element_type=_F32).astype(_BF16)


def _t5_bucket_table(rel):
    half = N_BUCKETS // 2
    max_exact = half // 2
    n = jnp.abs(rel)
    nf = jnp.maximum(n, 1).astype(jnp.float32)
    large = max_exact + (jnp.log(nf / max_exact) / math.log(MAX_DISTANCE / max_exact)
                         * (half - max_exact)).astype(jnp.int32)
    large = jnp.minimum(large, half - 1)
    return jnp.where(rel > 0, half, 0) + jnp.where(n < max_exact, n, large)


def _dsa_attention(qt, qit, wit, k_all, vt_all, ki_all, rel_bias, past):
    bsz, _, lq = qt.shape
    nk = k_all.shape[1]
    topk = min(TOPK_MAX, nk // 4)
    tq = min(V7X_LANES, lq)
    kc = DSA_KEY_CHUNK
    nk_pad = -(-nk // kc) * kc
    pad = ((0, 0), (0, nk_pad - nk), (0, 0))
    k16 = jnp.pad(k_all, pad)
    vt16 = jnp.pad(vt_all, ((0, 0), (0, 0), (0, nk_pad - nk)))
    ki2 = jnp.pad(jnp.concatenate([ki_all, ki_all], axis=-1), pad)
    r0_off = kc + MAX_DISTANCE
    nbr = r0_off + kc + tq
    rel = (jnp.arange(nbr, dtype=jnp.int32)[:, None] - r0_off) - jnp.arange(tq, dtype=jnp.int32)[None, :]
    bkt = _t5_bucket_table(rel)
    chunk_counts = sorted({(past + (t + 1) * tq + kc - 1) // kc for t in range(lq // tq)})
    assert chunk_counts == list(range(chunk_counts[0], chunk_counts[-1] + 1))
    kern = functools.partial(_dsa_kernel, past=past, tq=tq, kc=kc, topk=topk, r0_off=r0_off,
                             chunk_counts=tuple(chunk_counts))
    return pl.pallas_call(
        kern,
        grid=(bsz, lq // tq),
        in_specs=[pl.BlockSpec(memory_space=pltpu.SMEM),
                  pl.BlockSpec((1, A_W, tq), lambda b, t: (b, 0, t)),
                  pl.BlockSpec((1, IDX_HEADS * IDX_DIM, tq), lambda b, t: (b, 0, t)),
                  pl.BlockSpec((1, V7X_SUBLANES, tq), lambda b, t: (b, 0, t)),
                  pl.BlockSpec((1, nk_pad, A_W), lambda b, t: (b, 0, 0)),
                  pl.BlockSpec((1, A_W, nk_pad), lambda b, t: (b, 0, 0)),
                  pl.BlockSpec((1, nk_pad, V7X_LANES), lambda b, t: (b, 0, 0)),
                  pl.BlockSpec((nbr, tq), lambda b, t: (0, 0))],
        out_specs=pl.BlockSpec((1, tq, A_W), lambda b, t: (b, t, 0)),
        out_shape=jax.ShapeDtypeStruct((bsz, lq, A_W), _BF16),
        scratch_shapes=[pltpu.VMEM((nk_pad, tq), jnp.int32), pltpu.VMEM((nk_pad, tq), _F32),
                        pltpu.VMEM((A_HEADS, nbr, tq), _F32),
                        pltpu.VMEM((nk_pad, 2 * V7X_LANES), _BF16), pltpu.VMEM((A_W, tq), _F32),
                        pltpu.VMEM((A_HEADS, nk_pad, tq), _F32), pltpu.VMEM((A_HEADS, kc, tq), _BF16)],
        compiler_params=_params(2),
        name="dsa_attention",
    )(rel_bias, qt, qit, wit, k16, vt16, ki2, bkt)


def _split3_rows(x):
    hi = x.astype(_BF16)
    r1 = x - hi.astype(_F32)
    mid = r1.astype(_BF16)
    lo = (r1 - mid.astype(_F32)).astype(_BF16)
    return jnp.concatenate([hi, mid, lo], axis=0)


def _gla_kernel(q_ref, k_ref, g_ref, v_ref, r_ref, s0_ref, ng_ref, o_ref, s_out_ref, st_ref, *, n_sub):
    c = pl.program_id(1)
    causal = (lax.broadcasted_iota(jnp.int32, (CHUNK, CHUNK), 0)
              >= lax.broadcasted_iota(jnp.int32, (CHUNK, CHUNK), 1))
    causal3 = jnp.concatenate([causal.astype(_BF16)] * 3, axis=1)
    upper_half = lax.broadcasted_iota(jnp.int32, (CHUNK, V7X_LANES), 1) >= B_KEY_DIM
    diag = (lax.broadcasted_iota(jnp.int32, (V7X_LANES, V7X_LANES), 0)
            == lax.broadcasted_iota(jnp.int32, (V7X_LANES, V7X_LANES), 1))
    pairs = range(B_HEADS // 2)
    heads = range(B_HEADS)

    @pl.when(c == 0)
    def _():
        st_ref[...] = s0_ref[0]

    def pair_lanes(x, p):
        return x[:, p * V7X_LANES:(p + 1) * V7X_LANES]

    def head_half(x, h):
        keep = upper_half if h % 2 == 1 else jnp.logical_not(upper_half)
        return jnp.where(keep, pair_lanes(x, h // 2), 0.0)

    parts = []
    for s in range(n_sub):
        rows = slice(s * CHUNK, (s + 1) * CHUNK)
        b = jnp.dot(causal3, _split3_rows(g_ref[0, rows, :]), preferred_element_type=_F32)
        mid = CHUNK // 2
        b_mid = b[mid:mid + 1, :]
        b_last = b[CHUNK - 1:CHUNK, :]
        q = q_ref[0, rows, :]
        k = k_ref[0, rows, :]
        qe = q * jnp.exp(b)
        qm = q * jnp.exp(b - b_mid)
        km = k * jnp.exp(b_mid - b)
        kl = k * jnp.exp(b_last - b)
        vs = [v_ref[0, rows, h * B_VAL_DIM:(h + 1) * B_VAL_DIM] for h in heads]
        a = [jnp.where(causal, _dot(head_half(qm, h), pair_lanes(km, h // 2), _NT), 0.0) for h in heads]
        av = [_dot(a[h], vs[h]) for h in heads]
        upd = [_dot(head_half(kl, h), vs[h], _TN) for h in heads]
        dcol = [jnp.sum(jnp.where(diag, jnp.exp(pair_lanes(b_last, p)), 0.0), axis=1, keepdims=True)
                for p in pairs]
        parts.append((qe, av, upd, dcol))

    st = [st_ref[p * V7X_LANES:(p + 1) * V7X_LANES, :] for p in pairs]
    for s in range(n_sub):
        rows = slice(s * CHUNK, (s + 1) * CHUNK)
        qe, av, upd, dcol = parts[s]
        o = [_dot(head_half(qe, h), st[h // 2]) + av[h] for h in heads]
        st = [dcol[p] * st[p] + upd[2 * p] + upd[2 * p + 1] for p in pairs]
        for h in heads:
            cols = slice(h * B_VAL_DIM, (h + 1) * B_VAL_DIM)
            o_ref[0, rows, cols] = (_rms(o[h], ng_ref[...]) * _silu(r_ref[0, rows, cols])).astype(_BF16)
    for p in pairs:
        st_ref[p * V7X_LANES:(p + 1) * V7X_LANES, :] = st[p]

    @pl.when(c == pl.num_programs(1) - 1)
    def _():
        s_out_ref[0] = st_ref[...]


def _gla(qb, kb, la, vb, rb, s0, norm_g):
    bsz, lq, _ = qb.shape
    n_sub = min(GLA_SUBCHUNKS, lq // CHUNK)
    tok = lambda n: pl.BlockSpec((1, n_sub * CHUNK, n), lambda b, c: (b, c, 0))
    state = pl.BlockSpec((1, B_KW, B_VAL_DIM), lambda b, c: (b, 0, 0))
    o, s_new = pl.pallas_call(
        functools.partial(_gla_kernel, n_sub=n_sub),
        grid=(bsz, lq // (n_sub * CHUNK)),
        in_specs=[tok(B_KW), tok(B_KW), tok(B_KW), tok(B_VW), tok(B_VW), state,
                  pl.BlockSpec((1, B_VAL_DIM), lambda b, c: (0, 0))],
        out_specs=[tok(B_VW), state],
        out_shape=[jax.ShapeDtypeStruct((bsz, lq, B_VW), _BF16),
                   jax.ShapeDtypeStruct((bsz, B_KW, B_VAL_DIM), _F32)],
        scratch_shapes=[pltpu.VMEM((B_KW, B_VAL_DIM), _F32)],
        compiler_params=_params(2),
        name="gla",
    )(qb, kb, la, vb, rb, s0.reshape(bsz, B_KW, B_VAL_DIM), norm_g.reshape(1, B_VAL_DIM))
    return o, s_new.reshape(s0.shape)


def _odd_in_kernel(x_ref, g_ref, w_ref, alog_ref, dt_ref, cw_ref, cprev_ref, act_ref, z_ref, gb_ref, tail_ref,
                   xbuf_ref, *, seg, tiles_per_batch):
    halo = V7X_SUBLANES
    h = _rms(x_ref[...], g_ref[...]).astype(_BF16)
    z_ref[...] = jnp.dot(h, w_ref[:, _OD_Z:_OD_Z + C_W], preferred_element_type=_F32)
    ab = jnp.dot(h, w_ref[:, _OD_AB:_OD_AB + V7X_LANES], preferred_element_type=_F32)
    xa = ab + dt_ref[...]
    g = -jnp.exp(alog_ref[...]) * (jnp.maximum(xa, 0.0) + jnp.log1p(jnp.exp(-jnp.abs(xa))))
    lane = lax.broadcasted_iota(jnp.int32, ab.shape, 1)
    gb_ref[...] = jnp.where(lane < C_HEADS, g, jax.nn.sigmoid(ab))

    for s in range(x_ref.shape[0] // seg):
        rows = slice(s * seg, (s + 1) * seg)
        if tiles_per_batch > 1:
            @pl.when(pl.program_id(0) % tiles_per_batch == 0)
            def _():
                xbuf_ref[0:halo, :] = cprev_ref[0]
        else:
            xbuf_ref[0:halo, :] = cprev_ref[s]
        xbuf_ref[halo:halo + seg, :] = jnp.dot(h[rows], w_ref[:, _OD_QKV:_OD_QKV + CONV_CH],
                                               preferred_element_type=_F32)
        conv = xbuf_ref[halo:halo + seg, :] * cw_ref[CONV_WIDTH - 1:CONV_WIDTH, :]
        for j in range(CONV_WIDTH - 1):
            sh = CONV_WIDTH - 1 - j
            conv = conv + xbuf_ref[halo - sh:halo - sh + seg, :] * cw_ref[j:j + 1, :]
        tail = xbuf_ref[seg:seg + halo, :]
        xbuf_ref[0:halo, :] = tail
        tail_ref[s] = tail
        act = _silu(conv)
        for hd in range(C_HEADS):
            for part, scale in ((0, C_HEAD_DIM ** -0.5), (1, 1.0)):
                cols = slice(part * C_W + hd * C_HEAD_DIM, part * C_W + (hd + 1) * C_HEAD_DIM)
                a = act[:, cols]
                act_ref[rows, cols] = a * (lax.rsqrt(jnp.sum(a * a, axis=-1, keepdims=True) + EPS) * scale)
        act_ref[rows, 2 * C_W:] = act[:, 2 * C_W:]


def _odd_in_proj(x, g, w, a_log, dt_bias, conv_w, conv_prev8, lq):
    t, d = x.shape
    tm = min(t, ODD_ROW_TILE)
    seg = min(tm, lq)
    batches_per_tile = tm // seg
    tiles_per_batch = lq // seg
    row = lambda n: pl.BlockSpec((tm, n), lambda i: (i, 0))
    per_batch = pl.BlockSpec((batches_per_tile, V7X_SUBLANES, CONV_CH), lambda i: (i // tiles_per_batch, 0, 0))
    outs = [CONV_CH, C_W, V7X_LANES]
    return pl.pallas_call(
        functools.partial(_odd_in_kernel, seg=seg, tiles_per_batch=tiles_per_batch),
        grid=(t // tm,),
        in_specs=[row(d), _full((1, d)), _full((d, _OD_COLS)), _full((1, V7X_LANES)), _full((1, V7X_LANES)),
                  _full((CONV_WIDTH, CONV_CH)), per_batch],
        out_specs=[row(n) for n in outs] + [per_batch],
        out_shape=([jax.ShapeDtypeStruct((t, n), _F32) for n in outs]
                   + [jax.ShapeDtypeStruct((t // lq, V7X_SUBLANES, CONV_CH), _F32)]),
        scratch_shapes=[pltpu.VMEM((seg + V7X_SUBLANES, CONV_CH), _F32)],
        compiler_params=_params(1),
        name="odd_in_proj",
    )(x, g.reshape(1, d), w, a_log, dt_bias, conv_w, conv_prev8)


def _delta_kernel(act_ref, z_ref, gb_ref, s0_ref, ng_ref, o_ref, s_out_ref, st_ref, *, n_sub):
    c = pl.program_id(1)

    @pl.when(c == 0)
    def _():
        st_ref[...] = s0_ref[0]

    t_i = lax.broadcasted_iota(jnp.int32, (CHUNK, V7X_LANES), 0)
    s_i = lax.broadcasted_iota(jnp.int32, (CHUNK, V7X_LANES), 1)
    lower_half = s_i < CHUNK
    s_mod = jnp.where(lower_half, s_i, s_i - CHUNK)
    incl = t_i >= s_mod
    strict = t_i > s_mod
    eye = (t_i == s_mod).astype(_F32)
    incl3 = jnp.concatenate([(lax.broadcasted_iota(jnp.int32, (CHUNK, CHUNK), 0)
                              >= lax.broadcasted_iota(jnp.int32, (CHUNK, CHUNK), 1)).astype(_BF16)] * 3, axis=1)

    def split3(x):
        hi16 = x.astype(_BF16)
        hi = hi16.astype(_F32)
        lo = x - hi
        lhs = jnp.concatenate([jnp.where(lower_half, hi, lo).astype(_BF16), hi16], axis=1)
        lo16 = lo.astype(_BF16)
        rhs = jnp.concatenate([hi16, hi16, lo16, jnp.zeros_like(lo16)], axis=0)
        return lhs, rhs

    def mm(lhs, rhs):
        return jnp.dot(lhs, rhs, preferred_element_type=_F32)

    heads = range(C_HEADS)
    units = [(s, h) for s in range(n_sub) for h in heads]
    g_lasts = []
    qs, ks, kbs, decays, rhss, qgs, kdecs = [], [], [], [], [], [], []
    for s in range(n_sub):
        rows = slice(s * CHUNK, (s + 1) * CHUNK)
        gb = gb_ref[0, rows, :]
        gcum_all = jnp.dot(incl3, _split3_rows(gb), preferred_element_type=_F32)
        gcum_rows = jnp.concatenate([gcum_all, gcum_all], axis=0).T
        g_last = gcum_all[CHUNK - 1:CHUNK, :]
        g_lasts.append(g_last)
        for h in heads:
            q = act_ref[0, rows, h * C_HEAD_DIM:(h + 1) * C_HEAD_DIM]
            k = act_ref[0, rows, C_W + h * C_HEAD_DIM:C_W + (h + 1) * C_HEAD_DIM]
            v = act_ref[0, rows, 2 * C_W + h * C_HEAD_DIM:2 * C_W + (h + 1) * C_HEAD_DIM]
            beta = gb[:, C_HEADS + h:C_HEADS + h + 1]
            gcum = gcum_all[:, h:h + 1]
            kb = k * beta
            qs.append(q)
            ks.append(k)
            kbs.append(kb)
            decays.append(jnp.exp(jnp.where(incl, gcum - gcum_rows[h:h + 1, :], _NEG)))
            rhss.append(jnp.concatenate([v * beta, kb * jnp.exp(gcum)], axis=-1))
            qgs.append(q * jnp.exp(gcum))
            kdecs.append(k * jnp.exp(g_last[:, h:h + 1] - gcum))

    n_units = range(len(units))
    lowers, attns = [], []
    for u in n_units:
        prod = _dot(jnp.concatenate([kbs[u], qs[u]], axis=0),
                    jnp.concatenate([ks[u], ks[u]], axis=0), _NT)
        lowers.append(jnp.where(strict, prod[:CHUNK] * decays[u], 0.0))
        attns.append(jnp.where(incl, prod[CHUNK:] * decays[u], 0.0))

    pws = [-lowers[u] for u in n_units]
    invs = [eye + pws[u] for u in n_units]
    pw_l, pw_r = zip(*[split3(pws[u]) for u in n_units])
    pws = [mm(pw_l[u], pw_r[u]) for u in n_units]
    for _ in range(4):
        pw_l, pw_r = zip(*[split3(pws[u]) for u in n_units])
        both = [mm(jnp.concatenate([split3(invs[u])[0], pw_l[u]], axis=0), pw_r[u]) for u in n_units]
        invs = [invs[u] + both[u][:CHUNK] for u in n_units]
        pws = [both[u][CHUNK:] for u in n_units]
    invs = [invs[u] + mm(split3(invs[u])[0], split3(pws[u])[1]) for u in n_units]
    sols = []
    for u in n_units:
        r_hi16 = rhss[u].astype(_BF16)
        r_lo16 = (rhss[u] - r_hi16.astype(_F32)).astype(_BF16)
        sols.append(mm(split3(invs[u])[0], jnp.concatenate([r_hi16, r_hi16, r_lo16, jnp.zeros_like(r_lo16)], axis=0)))

    st = [st_ref[h] for h in heads]
    for u, (s, h) in enumerate(units):
        rows = slice(s * CHUNK, (s + 1) * CHUNK)
        w_u = sols[u][:, :C_HEAD_DIM]
        w_w = sols[u][:, C_HEAD_DIM:]
        ws = _dot(jnp.concatenate([w_w, qgs[u]], axis=0), st[h])
        v_new = w_u - ws[:CHUNK]
        o = ws[CHUNK:] + _dot(attns[u][:, :CHUNK], v_new)
        st[h] = jnp.exp(g_lasts[s][:, h:h + 1]) * st[h] + _dot(kdecs[u], v_new, _TN)
        cols = slice(h * C_HEAD_DIM, (h + 1) * C_HEAD_DIM)
        o_ref[0, rows, cols] = (_rms(o, ng_ref[...]) * _silu(z_ref[0, rows, cols])).astype(_BF16)
    for h in heads:
        st_ref[h] = st[h]

    @pl.when(c == pl.num_programs(1) - 1)
    def _():
        s_out_ref[0] = st_ref[...]


def _delta(act, z, gb, s0, norm_g):
    bsz, lq, _ = act.shape
    n_sub = min(DELTA_SUBCHUNKS, lq // CHUNK)
    rows_all = n_sub * CHUNK
    tok = lambda n: pl.BlockSpec((1, rows_all, n), lambda b, c: (b, c, 0))
    state = pl.BlockSpec((1, C_HEADS, C_HEAD_DIM, C_HEAD_DIM), lambda b, c: (b, 0, 0, 0))
    return pl.pallas_call(
        functools.partial(_delta_kernel, n_sub=n_sub),
        grid=(bsz, lq // rows_all),
        in_specs=[tok(CONV_CH), tok(C_W), tok(V7X_LANES), state,
                  pl.BlockSpec((1, C_HEAD_DIM), lambda b, c: (0, 0))],
        out_specs=[tok(C_W), state],
        out_shape=[jax.ShapeDtypeStruct((bsz, lq, C_W), _BF16),
                   jax.ShapeDtypeStruct(s0.shape, _F32)],
        scratch_shapes=[pltpu.VMEM((C_HEADS, C_HEAD_DIM, C_HEAD_DIM), _F32)],
        compiler_params=_params(2),
        name="gated_delta",
    )(act, z, gb, s0, norm_g.reshape(1, C_HEAD_DIM))


def _prep_even_w_in(w):
    sizes = (A_W, A_W, A_W, IDX_HEADS * IDX_DIM, IDX_DIM, IDX_HEADS, B_KW, B_KW, B_VW, B_VW, B_GATE_RANK)
    offs = [0]
    for s in sizes:
        offs.append(offs[-1] + s)
    part = lambda i: w[:, offs[i]:offs[i + 1]]
    q, k, v, qi, ki, wi, qb, kb, vb, rb, ab = [part(i) for i in range(len(sizes))]
    pad = jnp.zeros((w.shape[0], _EV_IDX_W - (256 + IDX_DIM + IDX_HEADS + B_GATE_RANK)), w.dtype)
    return jnp.concatenate([q, k, v, qi, ki, wi, ab, pad, qb, kb, vb, rb], axis=1).astype(_BF16)


def _prep_odd_w_in(w):
    pad = jnp.zeros((w.shape[0], _OD_COLS - w.shape[1]), w.dtype)
    return jnp.concatenate([w, pad], axis=1).astype(_BF16)


def _lane_pad(v, n=V7X_LANES):
    return jnp.pad(v, (0, n - v.shape[0])).reshape(1, n)


def _trunk(x, past, prm):
    bsz, lq, d = x.shape
    t = bsz * lq
    xf = x.reshape(t, d)

    def ffn(xf, i, j, acts=(), proj_w=()):
        return _ffn_half(xf, prm['ffn_norm'][i, j], prm['ffn_w_gate'][i, j].astype(_BF16),
                         prm['ffn_w_up'][i, j].astype(_BF16), prm['ffn_w_down'][i, j].astype(_BF16),
                         acts, proj_w)

    xf = ffn(xf, 0, 0)
    head_mean = jnp.kron(jnp.eye(A_HEADS, dtype=_F32),
                         jnp.full((A_HEAD_DIM, A_HEAD_DIM), 1.0 / A_HEAD_DIM, _F32)).astype(_BF16)
    q_gain = (jnp.tile(prm['ev_q_norm'][0], A_HEADS) * (A_HEAD_DIM ** -0.5)).reshape(1, A_W)
    k_gain = jnp.tile(prm['ev_k_norm'][0], A_HEADS).reshape(1, A_W)
    w2 = jnp.zeros((V7X_LANES, B_KW), _F32).at[
        _EV_AB_OFF - 256:_EV_AB_OFF - 256 + B_GATE_RANK].set(prm['ev_gate_w2'][0]).astype(_BF16)
    k_new, v_new, k16, idx, qb, kb, vb, rb, la, qt, vt, qit, wit = _even_in_proj(
        xf, prm['mix_norm'][0], _prep_even_w_in(prm['ev_w_in'][0]), q_gain, k_gain, head_mean,
        w2, prm['ev_gate_b2'][0].reshape(1, B_KW), lq)
    r3 = lambda a: a.reshape(bsz, lq, a.shape[-1])
    ki_new = r3(idx)[:, :, 256:256 + IDX_DIM]
    if past is None:
        past_len = 0
        k_all, vt_all, ki_all = r3(k16), vt, ki_new
        s0_gla = jnp.zeros((bsz, B_HEADS, B_KEY_DIM, B_VAL_DIM), _F32)
    else:
        past_len = past['k'].shape[2]
        k_all = jnp.concatenate([past['k'][0].reshape(bsz, past_len, A_W).astype(_BF16), r3(k16)], axis=1)
        vt_all = jnp.concatenate([jnp.swapaxes(past['v'][0].reshape(bsz, past_len, A_W).astype(_BF16), 1, 2), vt],
                                 axis=2)
        ki_all = jnp.concatenate([past['ki'][0], ki_new], axis=1)
        s0_gla = past['gla'][0]
    o_a = _dsa_attention(qt, qit, wit, k_all, vt_all, ki_all, prm['rel_bias'], past_len)
    o_b, s_gla = _gla(r3(qb), r3(kb), r3(la), r3(vb), r3(rb), s0_gla, prm['ev_gla_norm'][0])
    w_out = prm['ev_w_out'][0].astype(_BF16)
    xf = ffn(xf, 0, 1, [o_a.reshape(t, A_W), o_b.reshape(t, B_VW)], [w_out[:A_W], w_out[A_W:]])

    xf = ffn(xf, 1, 0)
    a_log = _lane_pad(prm['od_a_log'][0])
    dt_bias = _lane_pad(prm['od_dt_bias'][0])
    if past is None:
        conv_prev = jnp.zeros((bsz, CONV_WIDTH - 1, CONV_CH), _F32)
        s0_delta = jnp.zeros((bsz, C_HEADS, C_HEAD_DIM, C_HEAD_DIM), _F32)
    else:
        conv_prev, s0_delta = past['conv'][0], past['delta'][0]
    conv_prev8 = jnp.pad(conv_prev, ((0, 0), (V7X_SUBLANES - (CONV_WIDTH - 1), 0), (0, 0)))
    act, z, gb, tail = _odd_in_proj(xf, prm['mix_norm'][1], _prep_odd_w_in(prm['od_w_in'][0]), a_log, dt_bias,
                                    prm['od_conv_w'][0], conv_prev8, lq)
    o_c, s_delta = _delta(r3(act), r3(z), r3(gb), s0_delta, prm['od_norm'][0])
    conv_new = tail[:, V7X_SUBLANES - (CONV_WIDTH - 1):]
    xf = ffn(xf, 1, 1, [o_c.reshape(t, C_W)], [prm['od_w_out'][0].astype(_BF16)])

    y = xf.reshape(bsz, lq, d)
    k_out = r3(k_new).reshape(1, bsz, lq, A_HEADS, A_HEAD_DIM)
    v_out = r3(v_new).reshape(1, bsz, lq, A_HEADS, A_HEAD_DIM)
    return y, k_out, v_out, ki_new[None], s_gla[None], s_delta[None], conv_new[None]


def kernel(x_prompt, x_sample, cache_attn_k, cache_attn_v, cache_idx_k, state_gla, state_delta, state_conv,
           ffn_norm, ffn_w_gate, ffn_w_up, ffn_w_down, mix_norm, ev_w_in, ev_q_norm, ev_k_norm, rel_bias,
           ev_gate_w2, ev_gate_b2, ev_gla_norm, ev_w_out, od_w_in, od_conv_w, od_a_log, od_dt_bias, od_norm,
           od_w_out):
    prm = {'ffn_norm': ffn_norm, 'ffn_w_gate': ffn_w_gate, 'ffn_w_up': ffn_w_up, 'ffn_w_down': ffn_w_down,
           'mix_norm': mix_norm, 'ev_w_in': ev_w_in, 'ev_q_norm': ev_q_norm, 'ev_k_norm': ev_k_norm,
           'rel_bias': rel_bias, 'ev_gate_w2': ev_gate_w2, 'ev_gate_b2': ev_gate_b2, 'ev_gla_norm': ev_gla_norm,
           'ev_w_out': ev_w_out, 'od_w_in': od_w_in, 'od_conv_w': od_conv_w, 'od_a_log': od_a_log,
           'od_dt_bias': od_dt_bias, 'od_norm': od_norm, 'od_w_out': od_w_out}
    past = {'k': cache_attn_k, 'v': cache_attn_v, 'ki': cache_idx_k, 'gla': state_gla,
            'delta': state_delta, 'conv': state_conv}
    y_prompt, p_k, p_v, p_ki, p_gla, p_delta, p_conv = _trunk(x_prompt, None, prm)
    y_sample, s_k, s_v, s_ki, s_gla, s_delta, s_conv = _trunk(x_sample, past, prm)
    return (y_prompt, y_sample, p_k, p_v, p_ki, p_gla, p_delta, p_conv,
            s_k, s_v, s_ki, s_gla, s_delta, s_conv)
```

```python
import functools
import math

import jax
import jax.numpy as jnp
from jax import lax
from jax.experimental import pallas as pl
from jax.experimental.pallas import tpu as pltpu

CHUNK = 64
EPS = 1e-6

A_HEADS = 8
A_HEAD_DIM = 64
A_W = A_HEADS * A_HEAD_DIM
IDX_HEADS = 4
IDX_DIM = 64
TOPK_MAX = 256
N_BUCKETS = 32
MAX_DISTANCE = 128

B_HEADS = 4
B_KEY_DIM = 64
B_VAL_DIM = 128
B_KW = B_HEADS * B_KEY_DIM
B_VW = B_HEADS * B_VAL_DIM
B_GATE_RANK = 16
B_GATE_TAU = 16.0

C_HEADS = 8
C_HEAD_DIM = 128
C_W = C_HEADS * C_HEAD_DIM
CONV_WIDTH = 4
CONV_CH = 3 * C_W

V7X_LANES = 128
V7X_SUBLANES = 8
V7X_VMEM_LIMIT_BYTES = 56 * 1024 * 1024
V7X_MXU_DIM = 256

ROW_TILE = 2 * V7X_MXU_DIM
ODD_ROW_TILE = V7X_MXU_DIM
DSA_KEY_CHUNK = V7X_MXU_DIM
GLA_SUBCHUNKS = 4
DELTA_SUBCHUNKS = 4

_F32 = jnp.float32
_BF16 = jnp.bfloat16
_NEG = -1e30
_LOG2E = math.log2(math.e)
_NN = (((1,), (0,)), ((), ()))
_NT = (((1,), (1,)), ((), ()))
_TN = (((0,), (0,)), ((), ()))

_EV_Q, _EV_K, _EV_V = 0, 512, 1024
_EV_IDX = 1536
_EV_IDX_W = 384
_EV_WI_OFF = 256 + IDX_DIM
_EV_AB_OFF = _EV_WI_OFF + IDX_HEADS
_EV_QB, _EV_KB, _EV_VB, _EV_RB = 1920, 2176, 2432, 2944
_EV_COLS = 3456
_OD_QKV, _OD_Z, _OD_AB = 0, 3072, 4096
_OD_COLS = 4224


def _params(n_axes):
    return pltpu.CompilerParams(dimension_semantics=("arbitrary",) * n_axes,
                                vmem_limit_bytes=V7X_VMEM_LIMIT_BYTES)


def _rms(x, g):
    return x * lax.rsqrt(jnp.mean(x * x, axis=-1, keepdims=True) + EPS) * g


def _silu(x):
    return x * jax.nn.sigmoid(x)


def _dot(a, b, dims=_NN):
    return lax.dot_general(a.astype(_BF16), b.astype(_BF16), dims, preferred_element_type=_F32)


def _full(shape):
    return pl.BlockSpec(shape, lambda *_: (0,) * len(shape))


_FFN_COLS = V7X_MXU_DIM


def _ffn_kernel(*refs, n_proj):
    x_ref = refs[0]
    g_ref, wg_ref, wu_ref, wd_ref, o_ref, act_ref = refs[1 + 2 * n_proj:]
    x = x_ref[...]
    for i in range(n_proj):
        x = x + jnp.dot(refs[1 + i][...], refs[1 + n_proj + i][...], preferred_element_type=_F32)
    h = _rms(x, g_ref[...]).astype(_BF16)
    for c in range(wg_ref.shape[1] // _FFN_COLS):
        cols = slice(c * _FFN_COLS, (c + 1) * _FFN_COLS)
        gate = jnp.dot(h, wg_ref[:, cols], preferred_element_type=_F32)
        up = jnp.dot(h, wu_ref[:, cols], preferred_element_type=_F32)
        act_ref[:, cols] = (_silu(gate) * up).astype(_BF16)
    o_ref[...] = x + 0.5 * jnp.dot(act_ref[...], wd_ref[...], preferred_element_type=_F32)


def _ffn_half(x, g, wg, wu, wd, acts=(), proj_w=()):
    t, d = x.shape
    ff = wg.shape[1]
    tm = min(t, ROW_TILE)
    resident = lambda shape: pl.BlockSpec(shape, lambda i: (0, 0), pipeline_mode=pl.Buffered(1))
    return pl.pallas_call(
        functools.partial(_ffn_kernel, n_proj=len(acts)),
        grid=(t // tm,),
        in_specs=([pl.BlockSpec((tm, d), lambda i: (i, 0))]
                  + [pl.BlockSpec((tm, a.shape[1]), lambda i: (i, 0)) for a in acts]
                  + [resident(w.shape) for w in proj_w]
                  + [resident((1, d)), resident((d, ff)), resident((d, ff)), resident((ff, d))]),
        out_specs=pl.BlockSpec((tm, d), lambda i: (i, 0)),
        out_shape=jax.ShapeDtypeStruct((t, d), _F32),
        scratch_shapes=[pltpu.VMEM((tm, ff), _BF16)],
        compiler_params=_params(1),
        name="ffn_half",
    )(x, *acts, *proj_w, g.reshape(1, d), wg, wu, wd)


def _even_in_kernel(x_ref, g_ref, w_ref, qg_ref, kg_ref, hm_ref, w2_ref, b2_ref,
                    k_ref, v_ref, k16_ref, ki_ref, ki2_ref, qb_ref, kb_ref, vb_ref, rb_ref, la_ref,
                    qt_ref, vt_ref, qit_ref, wit_ref, *, seg):
    h = _rms(x_ref[...], g_ref[...]).astype(_BF16)

    def proj(lo, width):
        return jnp.dot(h, w_ref[:, lo:lo + width], preferred_element_type=_F32)

    def headnorm(t, gain):
        t2 = t * t
        hi = t2.astype(_BF16)
        lo = (t2 - hi.astype(_F32)).astype(_BF16)
        ms = (jnp.dot(hi, hm_ref[...], preferred_element_type=_F32)
              + jnp.dot(lo, hm_ref[...], preferred_element_type=_F32))
        return t * lax.rsqrt(ms + EPS) * gain

    qn = headnorm(proj(_EV_Q, A_W), qg_ref[...]).astype(_BF16)
    k = headnorm(proj(_EV_K, A_W), kg_ref[...])
    v = proj(_EV_V, A_W)
    k_ref[...] = k
    v_ref[...] = v
    k16_ref[...] = k.astype(_BF16)
    v16 = v.astype(_BF16)
    idx = proj(_EV_IDX, _EV_IDX_W)
    ki = idx[:, IDX_HEADS * IDX_DIM:IDX_HEADS * IDX_DIM + IDX_DIM]
    ki_ref[...] = ki
    ki2_ref[...] = jnp.concatenate([ki, ki], axis=1)
    for s in range(x_ref.shape[0] // seg):
        rows = slice(s * seg, (s + 1) * seg)
        qt_ref[s] = qn[rows].T
        vt_ref[s] = v16[rows].T
        qit_ref[s] = idx[rows, 0:IDX_HEADS * IDX_DIM].T
        wit_ref[s] = idx[rows, IDX_HEADS * IDX_DIM:_EV_IDX_W].T[IDX_DIM:IDX_DIM + V7X_SUBLANES]
    qb_ref[...] = proj(_EV_QB, B_KW) * (B_KEY_DIM ** -0.5)
    kb_ref[...] = proj(_EV_KB, B_KW)
    vb_ref[...] = proj(_EV_VB, B_VW)
    rb_ref[...] = proj(_EV_RB, B_VW)
    z = _dot(idx[:, 256:384], w2_ref[...]) + b2_ref[...]
    la_ref[...] = (jnp.minimum(z, 0.0) - jnp.log1p(jnp.exp(-jnp.abs(z)))) * (1.0 / B_GATE_TAU)


def _even_in_proj(x, g, w, q_gain, k_gain, head_mean, w2, b2, lq):
    t, d = x.shape
    tm = min(t, ROW_TILE)
    seg = min(tm, lq)
    tiles_per_batch = lq // seg
    row = lambda n: pl.BlockSpec((tm, n), lambda i: (i, 0))
    col = lambda n: pl.BlockSpec((tm // seg, n, seg), lambda i: (i // tiles_per_batch, 0, i % tiles_per_batch))
    outs = [(A_W, _F32), (A_W, _F32), (A_W, _BF16), (IDX_DIM, _F32), (2 * IDX_DIM, _F32), (B_KW, _F32), (B_KW, _F32),
            (B_VW, _F32), (B_VW, _F32), (B_KW, _F32)]
    outs_t = [(A_W, _BF16), (A_W, _BF16), (IDX_HEADS * IDX_DIM, _F32), (V7X_SUBLANES, _F32)]
    w_spec = pl.BlockSpec((d, _EV_COLS), lambda i: (0, 0), pipeline_mode=pl.Buffered(1))
    return pl.pallas_call(
        functools.partial(_even_in_kernel, seg=seg),
        grid=(t // tm,),
        in_specs=[row(d), _full((1, d)), w_spec, _full((1, A_W)), _full((1, A_W)),
                  _full((A_W, A_W)), _full((V7X_LANES, B_KW)), _full((1, B_KW))],
        out_specs=[row(n) for n, _ in outs] + [col(n) for n, _ in outs_t],
        out_shape=([jax.ShapeDtypeStruct((t, n), dt) for n, dt in outs]
                   + [jax.ShapeDtypeStruct((t // lq, n, lq), dt) for n, dt in outs_t]),
        compiler_params=_params(1),
        name="even_in_proj",
    )(x, g.reshape(1, d), w, q_gain, k_gain, head_mean, w2, b2)


def _dsa_kernel(rb_ref, qt_ref, qit_ref, wit_ref, k_ref, vt_ref, ki_ref, bkt_ref, o_ref,
                key_ref, mask_ref, nb_ref, ki3_ref, acc_ref, lt_ref, p_ref, *, past, tq, kc, topk, r0_off, chunk_counts):
    t = pl.program_id(1)

    @pl.when((pl.program_id(0) == 0) & (t == 0))
    def _():
        bk = bkt_ref[...]
        for h in range(A_HEADS):
            acc = jnp.zeros(bk.shape, _F32)
            for bb in range(N_BUCKETS):
                acc = jnp.where(bk == bb, rb_ref[bb, h], acc)
            nb_ref[h] = (acc - rb_ref[N_BUCKETS // 2 - 1, h]) * _LOG2E

    q0 = past + t * tq
    n_chunks = (q0 + tq + kc - 1) // kc
    q_limit = ((q0 + lax.broadcasted_iota(jnp.int32, (1, tq), 1)) // CHUNK + 1) * CHUNK
    k_iota = lax.broadcasted_iota(jnp.int32, (kc, tq), 0)

    def admissible(r0):
        return k_iota < (q_limit - r0)

    @pl.when(t == 0)
    def _():
        def split_body(j, carry):
            r0 = pl.multiple_of(j * kc, kc)
            x = ki_ref[0, pl.ds(r0, kc), :]
            hi = x.astype(_BF16)
            ki3_ref[pl.ds(r0, kc), 0:V7X_LANES] = hi
            ki3_ref[pl.ds(r0, kc), V7X_LANES:2 * V7X_LANES] = (x - hi.astype(_F32)).astype(_BF16)
            return carry
        lax.fori_loop(0, ki3_ref.shape[0] // kc, split_body, 0)

    qit = qit_ref[0]
    q_cols = []
    for h in range(IDX_HEADS):
        x = qit[h * IDX_DIM:(h + 1) * IDX_DIM, :]
        hi = x.astype(_BF16)
        lo = (x - hi.astype(_F32)).astype(_BF16)
        q_cols.append(jnp.concatenate([hi, lo, hi, jnp.zeros_like(hi)], axis=0))
    q3 = jnp.concatenate(q_cols, axis=1)
    wit = wit_ref[0]

    upper_rows = lax.broadcasted_iota(jnp.int32, (V7X_LANES, tq), 0) >= A_HEAD_DIM
    q_pairs = []
    for pr in range(A_HEADS // 2):
        blk = qt_ref[0, pr * V7X_LANES:(pr + 1) * V7X_LANES, :].astype(_F32)
        q_pairs.append(jnp.concatenate([jnp.where(upper_rows, 0.0, blk), jnp.where(upper_rows, blk, 0.0)],
                                       axis=1).astype(_BF16))

    def scores_and_logits(n):
        d_all = jnp.dot(ki3_ref[0:n * kc, :], q3, preferred_element_type=_F32)
        for j in range(n):
            d = d_all[j * kc:(j + 1) * kc]
            s = jnp.zeros((kc, tq), _F32)
            for h in range(IDX_HEADS):
                s = s + wit[h:h + 1, :] * jnp.maximum(d[:, h * tq:(h + 1) * tq], 0.0)
            s = jnp.where(s == 0.0, 0.0, s)
            s = jnp.where(admissible(j * kc), s, -jnp.inf)
            bits = pltpu.bitcast(s, jnp.int32)
            key_ref[j * kc:(j + 1) * kc, :] = bits ^ ((bits >> 31) & jnp.int32(0x7FFFFFFF))
        for pr in range(A_HEADS // 2):
            lt2 = jnp.dot(k_ref[0, 0:n * kc, pr * V7X_LANES:(pr + 1) * V7X_LANES], q_pairs[pr],
                          preferred_element_type=_F32)
            for hh in range(2):
                lt_ref[2 * pr + hh, 0:n * kc, :] = lt2[:, hh * tq:(hh + 1) * tq] * _LOG2E
        for j in range(max(n - 2, 0), n):
            off = pl.multiple_of(jnp.maximum(j * kc - q0 + r0_off, 0), CHUNK)
            for h in range(A_HEADS):
                lt_ref[h, j * kc:(j + 1) * kc, :] += nb_ref[h, pl.ds(off, kc), :]

    int_min = jnp.int32(-2 ** 31)

    def count_keys(pred, n):
        parts = []
        for j in range(n):
            m = jnp.where(pred(key_ref[j * kc:(j + 1) * kc, :]), 1, 0)
            parts.append(m.reshape(kc // V7X_SUBLANES, V7X_SUBLANES, tq).sum(axis=0))
        while len(parts) > 1:
            parts = [sum(parts[i:i + 2]) for i in range(0, len(parts), 2)]
        return parts[0].sum(axis=0, keepdims=True)

    def radix_select(n):
        def bit_body(i, carry):
            tu, n_ge = carry
            cu = tu | jnp.left_shift(jnp.int32(1), 31 - i)
            cs = cu ^ int_min
            cnt = count_keys(lambda blk: blk >= cs, n)
            keep = cnt >= topk
            return jnp.where(keep, cu, tu), jnp.where(keep, cnt, n_ge)

        start = (jnp.zeros((1, tq), jnp.int32), jnp.full((1, tq), n * kc, jnp.int32))
        if n * kc <= topk:
            return start
        return lax.fori_loop(0, 32, bit_body, start)

    def tile_select(n):
        scores_and_logits(n)
        return radix_select(n)

    tu, n_ge = lax.switch(n_chunks - chunk_counts[0], [functools.partial(tile_select, n) for n in chunk_counts])
    ts = tu ^ int_min

    tri = (lax.broadcasted_iota(jnp.int32, (kc, kc), 0)
           >= lax.broadcasted_iota(jnp.int32, (kc, kc), 1)).astype(_BF16)

    def tie_masks():
        def gt_body(j, acc):
            r0 = pl.multiple_of(j * kc, kc)
            m = jnp.where(key_ref[pl.ds(r0, kc), :] > ts, 1, 0)
            return acc + m.reshape(kc // V7X_SUBLANES, V7X_SUBLANES, tq).sum(axis=0)

        n_greater = lax.fori_loop(0, n_chunks, gt_body, jnp.zeros((V7X_SUBLANES, tq), jnp.int32))
        room = (topk - n_greater.sum(axis=0, keepdims=True)).astype(_F32)

        def body(j, seen):
            r0 = pl.multiple_of(j * kc, kc)
            blk = key_ref[pl.ds(r0, kc), :]
            eq = blk == ts
            rank = jnp.dot(tri, jnp.where(eq, 1.0, 0.0).astype(_BF16), preferred_element_type=_F32) + seen
            val = jnp.where(blk > ts, 0.0, jnp.where(eq, jnp.where(rank <= room, 0.0, _NEG), _NEG))
            mask_ref[pl.ds(r0, kc), :] = jnp.where(admissible(r0), val, _NEG)
            return rank[kc - 1:kc, :]

        return lax.fori_loop(0, n_chunks, body, jnp.zeros((1, tq), _F32))

    def plain_masks():
        def body(j, carry):
            r0 = pl.multiple_of(j * kc, kc)
            val = jnp.where(key_ref[pl.ds(r0, kc), :] >= ts, 0.0, _NEG)
            mask_ref[pl.ds(r0, kc), :] = jnp.where(admissible(r0), val, _NEG)
            return carry

        return lax.fori_loop(0, n_chunks, body, jnp.zeros((1, tq), _F32))

    lax.cond(jnp.max(n_ge) > topk, tie_masks, plain_masks)

    acc_ref[...] = jnp.zeros_like(acc_ref)

    def attn_body(j, carry):
        ms, ls = carry
        r0 = pl.multiple_of(j * kc, kc)
        msk = mask_ref[pl.ds(r0, kc), :]
        new_ms, new_ls, alphas = [], [], []
        for h in range(A_HEADS):
            lt = lt_ref[h, pl.ds(r0, kc), :] + msk
            m_new = jnp.maximum(ms[h], lt.max(axis=0, keepdims=True))
            p = jnp.exp2(lt - m_new)
            alpha = jnp.exp2(ms[h] - m_new)
            new_ms.append(m_new)
            new_ls.append(alpha * ls[h] + p.sum(axis=0, keepdims=True))
            alphas.append(alpha)
            p_ref[h] = p.astype(_BF16)
        for h in range(A_HEADS):
            rows = slice(h * A_HEAD_DIM, (h + 1) * A_HEAD_DIM)
            pv = jnp.dot(vt_ref[0, rows, pl.ds(r0, kc)], p_ref[h], preferred_element_type=_F32)
            acc_ref[rows, :] = alphas[h] * acc_ref[rows, :] + pv
        return tuple(new_ms), tuple(new_ls)

    init = (tuple(jnp.full((1, tq), _NEG, _F32) for _ in range(A_HEADS)),
            tuple(jnp.zeros((1, tq), _F32) for _ in range(A_HEADS)))
    _, ls = lax.fori_loop(0, n_chunks, attn_body, init)

    eye = (lax.broadcasted_iota(jnp.int32, (tq, tq), 0)
           == lax.broadcasted_iota(jnp.int32, (tq, tq), 1)).astype(_BF16)
    for pair in range(A_HEADS // 2):
        lanes = slice(pair * V7X_LANES, (pair + 1) * V7X_LANES)
        o_t = jnp.concatenate(
            [acc_ref[(2 * pair + hh) * A_HEAD_DIM:(2 * pair + hh + 1) * A_HEAD_DIM, :] / ls[2 * pair + hh]
             for hh in range(2)], axis=0).astype(_BF16)
        o_ref[0, :, lanes] = lax.dot_general(eye, o_t, _NT, preferred_element_type=_F32).astype(_BF16)


def _t5_bucket_table(rel):
    half = N_BUCKETS // 2
    max_exact = half // 2
    n = jnp.abs(rel)
    nf = jnp.maximum(n, 1).astype(jnp.float32)
    large = max_exact + (jnp.log(nf / max_exact) / math.log(MAX_DISTANCE / max_exact)
                         * (half - max_exact)).astype(jnp.int32)
    large = jnp.minimum(large, half - 1)
    return jnp.where(rel > 0, half, 0) + jnp.where(n < max_exact, n, large)


def _dsa_attention(qt, qit, wit, k_all, vt_all, ki2_all, rel_bias, past):
    bsz, _, lq = qt.shape
    nk = k_all.shape[1]
    topk = min(TOPK_MAX, nk // 4)
    tq = min(V7X_LANES, lq)
    kc = DSA_KEY_CHUNK
    nk_pad = -(-nk // kc) * kc
    pad = ((0, 0), (0, nk_pad - nk), (0, 0))
    k16 = jnp.pad(k_all, pad)
    vt16 = jnp.pad(vt_all, ((0, 0), (0, 0), (0, nk_pad - nk)))
    ki2 = jnp.pad(ki2_all, pad)
    r0_off = kc + MAX_DISTANCE
    nbr = r0_off + kc + tq
    rel = (jnp.arange(nbr, dtype=jnp.int32)[:, None] - r0_off) - jnp.arange(tq, dtype=jnp.int32)[None, :]
    bkt = _t5_bucket_table(rel)
    chunk_counts = sorted({(past + (t + 1) * tq + kc - 1) // kc for t in range(lq // tq)})
    assert chunk_counts == list(range(chunk_counts[0], chunk_counts[-1] + 1))
    kern = functools.partial(_dsa_kernel, past=past, tq=tq, kc=kc, topk=topk, r0_off=r0_off,
                             chunk_counts=tuple(chunk_counts))
    return pl.pallas_call(
        kern,
        grid=(bsz, lq // tq),
        in_specs=[pl.BlockSpec(memory_space=pltpu.SMEM),
                  pl.BlockSpec((1, A_W, tq), lambda b, t: (b, 0, t)),
                  pl.BlockSpec((1, IDX_HEADS * IDX_DIM, tq), lambda b, t: (b, 0, t)),
                  pl.BlockSpec((1, V7X_SUBLANES, tq), lambda b, t: (b, 0, t)),
                  pl.BlockSpec((1, nk_pad, A_W), lambda b, t: (b, 0, 0)),
                  pl.BlockSpec((1, A_W, nk_pad), lambda b, t: (b, 0, 0)),
                  pl.BlockSpec((1, nk_pad, V7X_LANES), lambda b, t: (b, 0, 0)),
                  pl.BlockSpec((nbr, tq), lambda b, t: (0, 0))],
        out_specs=pl.BlockSpec((1, tq, A_W), lambda b, t: (b, t, 0)),
        out_shape=jax.ShapeDtypeStruct((bsz, lq, A_W), _BF16),
        scratch_shapes=[pltpu.VMEM((nk_pad, tq), jnp.int32), pltpu.VMEM((nk_pad, tq), _F32),
                        pltpu.VMEM((A_HEADS, nbr, tq), _F32),
                        pltpu.VMEM((nk_pad, 2 * V7X_LANES), _BF16), pltpu.VMEM((A_W, tq), _F32),
                        pltpu.VMEM((A_HEADS, nk_pad, tq), _F32), pltpu.VMEM((A_HEADS, kc, tq), _BF16)],
        compiler_params=_params(2),
        name="dsa_attention",
    )(rel_bias, qt, qit, wit, k16, vt16, ki2, bkt)


def _split3_rows(x):
    hi = x.astype(_BF16)
    r1 = x - hi.astype(_F32)
    mid = r1.astype(_BF16)
    lo = (r1 - mid.astype(_F32)).astype(_BF16)
    return jnp.concatenate([hi, mid, lo], axis=0)


def _gla_kernel(q_ref, k_ref, g_ref, v_ref, r_ref, s0_ref, ng_ref, o_ref, s_out_ref, st_ref, *, n_sub):
    c = pl.program_id(1)
    causal = (lax.broadcasted_iota(jnp.int32, (CHUNK, CHUNK), 0)
              >= lax.broadcasted_iota(jnp.int32, (CHUNK, CHUNK), 1))
    causal3 = jnp.concatenate([causal.astype(_BF16)] * 3, axis=1)
    upper_half = lax.broadcasted_iota(jnp.int32, (CHUNK, V7X_LANES), 1) >= B_KEY_DIM
    diag = (lax.broadcasted_iota(jnp.int32, (V7X_LANES, V7X_LANES), 0)
            == lax.broadcasted_iota(jnp.int32, (V7X_LANES, V7X_LANES), 1))
    pairs = range(B_HEADS // 2)
    heads = range(B_HEADS)

    @pl.when(c == 0)
    def _():
        st_ref[...] = s0_ref[0]

    def pair_lanes(x, p):
        return x[:, p * V7X_LANES:(p + 1) * V7X_LANES]

    def head_half(x, h):
        keep = upper_half if h % 2 == 1 else jnp.logical_not(upper_half)
        return jnp.where(keep, pair_lanes(x, h // 2), 0.0)

    parts = []
    for s in range(n_sub):
        rows = slice(s * CHUNK, (s + 1) * CHUNK)
        b = jnp.dot(causal3, _split3_rows(g_ref[0, rows, :]), preferred_element_type=_F32)
        mid = CHUNK // 2
        b_mid = b[mid:mid + 1, :]
        b_last = b[CHUNK - 1:CHUNK, :]
        q = q_ref[0, rows, :]
        k = k_ref[0, rows, :]
        qe = q * jnp.exp(b)
        qm = q * jnp.exp(b - b_mid)
        km = k * jnp.exp(b_mid - b)
        kl = k * jnp.exp(b_last - b)
        vs = [v_ref[0, rows, h * B_VAL_DIM:(h + 1) * B_VAL_DIM] for h in heads]
        a = [jnp.where(causal, _dot(head_half(qm, h), pair_lanes(km, h // 2), _NT), 0.0) for h in heads]
        av = [_dot(a[h], vs[h]) for h in heads]
        upd = [_dot(head_half(kl, h), vs[h], _TN) for h in heads]
        dcol = [jnp.sum(jnp.where(diag, jnp.exp(pair_lanes(b_last, p)), 0.0), axis=1, keepdims=True)
                for p in pairs]
        parts.append((qe, av, upd, dcol))

    st = [st_ref[p * V7X_LANES:(p + 1) * V7X_LANES, :] for p in pairs]
    for s in range(n_sub):
        rows = slice(s * CHUNK, (s + 1) * CHUNK)
        qe, av, upd, dcol = parts[s]
        o = [_dot(head_half(qe, h), st[h // 2]) + av[h] for h in heads]
        st = [dcol[p] * st[p] + upd[2 * p] + upd[2 * p + 1] for p in pairs]
        for h in heads:
            cols = slice(h * B_VAL_DIM, (h + 1) * B_VAL_DIM)
            o_ref[0, rows, cols] = (_rms(o[h], ng_ref[...]) * _silu(r_ref[0, rows, cols])).astype(_BF16)
    for p in pairs:
        st_ref[p * V7X_LANES:(p + 1) * V7X_LANES, :] = st[p]

    @pl.when(c == pl.num_programs(1) - 1)
    def _():
        s_out_ref[0] = st_ref[...]


def _gla(qb, kb, la, vb, rb, s0, norm_g):
    bsz, lq, _ = qb.shape
    n_sub = min(GLA_SUBCHUNKS, lq // CHUNK)
    tok = lambda n: pl.BlockSpec((1, n_sub * CHUNK, n), lambda b, c: (b, c, 0))
    state = pl.BlockSpec((1, B_KW, B_VAL_DIM), lambda b, c: (b, 0, 0))
    o, s_new = pl.pallas_call(
        functools.partial(_gla_kernel, n_sub=n_sub),
        grid=(bsz, lq // (n_sub * CHUNK)),
        in_specs=[tok(B_KW), tok(B_KW), tok(B_KW), tok(B_VW), tok(B_VW), state,
                  pl.BlockSpec((1, B_VAL_DIM), lambda b, c: (0, 0))],
        out_specs=[tok(B_VW), state],
        out_shape=[jax.ShapeDtypeStruct((bsz, lq, B_VW), _BF16),
                   jax.ShapeDtypeStruct((bsz, B_KW, B_VAL_DIM), _F32)],
        scratch_shapes=[pltpu.VMEM((B_KW, B_VAL_DIM), _F32)],
        compiler_params=_params(2),
        name="gla",
    )(qb, kb, la, vb, rb, s0.reshape(bsz, B_KW, B_VAL_DIM), norm_g.reshape(1, B_VAL_DIM))
    return o, s_new.reshape(s0.shape)


def _odd_in_kernel(x_ref, g_ref, w_ref, alog_ref, dt_ref, cw_ref, cprev_ref, act_ref, z_ref, gb_ref, tail_ref,
                   xbuf_ref, *, seg, tiles_per_batch):
    halo = V7X_SUBLANES
    h = _rms(x_ref[...], g_ref[...]).astype(_BF16)
    z_ref[...] = jnp.dot(h, w_ref[:, _OD_Z:_OD_Z + C_W], preferred_element_type=_F32)
    ab = jnp.dot(h, w_ref[:, _OD_AB:_OD_AB + V7X_LANES], preferred_element_type=_F32)
    xa = ab + dt_ref[...]
    g = -jnp.exp(alog_ref[...]) * (jnp.maximum(xa, 0.0) + jnp.log1p(jnp.exp(-jnp.abs(xa))))
    lane = lax.broadcasted_iota(jnp.int32, ab.shape, 1)
    gb_ref[...] = jnp.where(lane < C_HEADS, g, jax.nn.sigmoid(ab))

    for s in range(x_ref.shape[0] // seg):
        rows = slice(s * seg, (s + 1) * seg)
        if tiles_per_batch > 1:
            @pl.when(pl.program_id(0) % tiles_per_batch == 0)
            def _():
                xbuf_ref[0:halo, :] = cprev_ref[0]
        else:
            xbuf_ref[0:halo, :] = cprev_ref[s]
        xbuf_ref[halo:halo + seg, :] = jnp.dot(h[rows], w_ref[:, _OD_QKV:_OD_QKV + CONV_CH],
                                               preferred_element_type=_F32)
        conv = xbuf_ref[halo:halo + seg, :] * cw_ref[CONV_WIDTH - 1:CONV_WIDTH, :]
        for j in range(CONV_WIDTH - 1):
            sh = CONV_WIDTH - 1 - j
            conv = conv + xbuf_ref[halo - sh:halo - sh + seg, :] * cw_ref[j:j + 1, :]
        tail = xbuf_ref[seg:seg + halo, :]
        xbuf_ref[0:halo, :] = tail
        tail_ref[s] = tail
        act = _silu(conv)
        for hd in range(C_HEADS):
            for part, scale in ((0, C_HEAD_DIM ** -0.5), (1, 1.0)):
                cols = slice(part * C_W + hd * C_HEAD_DIM, part * C_W + (hd + 1) * C_HEAD_DIM)
                a = act[:, cols]
                act_ref[rows, cols] = a * (lax.rsqrt(jnp.sum(a * a, axis=-1, keepdims=True) + EPS) * scale)
        act_ref[rows, 2 * C_W:] = act[:, 2 * C_W:]


def _odd_in_proj(x, g, w, a_log, dt_bias, conv_w, conv_prev8, lq):
    t, d = x.shape
    tm = min(t, ODD_ROW_TILE)
    seg = min(tm, lq)
    batches_per_tile = tm // seg
    tiles_per_batch = lq // seg
    row = lambda n: pl.BlockSpec((tm, n), lambda i: (i, 0))
    per_batch = pl.BlockSpec((batches_per_tile, V7X_SUBLANES, CONV_CH), lambda i: (i // tiles_per_batch, 0, 0))
    outs = [CONV_CH, C_W, V7X_LANES]
    return pl.pallas_call(
        functools.partial(_odd_in_kernel, seg=seg, tiles_per_batch=tiles_per_batch),
        grid=(t // tm,),
        in_specs=[row(d), _full((1, d)), _full((d, _OD_COLS)), _full((1, V7X_LANES)), _full((1, V7X_LANES)),
                  _full((CONV_WIDTH, CONV_CH)), per_batch],
        out_specs=[row(n) for n in outs] + [per_batch],
        out_shape=([jax.ShapeDtypeStruct((t, n), _F32) for n in outs]
                   + [jax.ShapeDtypeStruct((t // lq, V7X_SUBLANES, CONV_CH), _F32)]),
        scratch_shapes=[pltpu.VMEM((seg + V7X_SUBLANES, CONV_CH), _F32)],
        compiler_params=_params(1),
        name="odd_in_proj",
    )(x, g.reshape(1, d), w, a_log, dt_bias, conv_w, conv_prev8)


def _delta_kernel(act_ref, z_ref, gb_ref, s0_ref, ng_ref, o_ref, s_out_ref, st_ref, *, n_sub):
    c = pl.program_id(1)

    @pl.when(c == 0)
    def _():
        st_ref[...] = s0_ref[0]

    t_i = lax.broadcasted_iota(jnp.int32, (CHUNK, V7X_LANES), 0)
    s_i = lax.broadcasted_iota(jnp.int32, (CHUNK, V7X_LANES), 1)
    lower_half = s_i < CHUNK
    s_mod = jnp.where(lower_half, s_i, s_i - CHUNK)
    incl = t_i >= s_mod
    strict = t_i > s_mod
    eye = (t_i == s_mod).astype(_F32)
    incl3 = jnp.concatenate([(lax.broadcasted_iota(jnp.int32, (CHUNK, CHUNK), 0)
                              >= lax.broadcasted_iota(jnp.int32, (CHUNK, CHUNK), 1)).astype(_BF16)] * 3, axis=1)

    def split3(x):
        hi16 = x.astype(_BF16)
        hi = hi16.astype(_F32)
        lo = x - hi
        lhs = jnp.concatenate([jnp.where(lower_half, hi, lo).astype(_BF16), hi16], axis=1)
        lo16 = lo.astype(_BF16)
        rhs = jnp.concatenate([hi16, hi16, lo16, jnp.zeros_like(lo16)], axis=0)
        return lhs, rhs

    def mm(lhs, rhs):
        return jnp.dot(lhs, rhs, preferred_element_type=_F32)

    heads = range(C_HEADS)
    units = [(s, h) for s in range(n_sub) for h in heads]
    g_lasts = []
    qs, ks, kbs, decays, rhss, qgs, kdecs = [], [], [], [], [], [], []
    for s in range(n_sub):
        rows = slice(s * CHUNK, (s + 1) * CHUNK)
        gb = gb_ref[0, rows, :]
        gcum_all = jnp.dot(incl3, _split3_rows(gb), preferred_element_type=_F32)
        gcum_rows = jnp.concatenate([gcum_all, gcum_all], axis=0).T
        g_last = gcum_all[CHUNK - 1:CHUNK, :]
        g_lasts.append(g_last)
        for h in heads:
            q = act_ref[0, rows, h * C_HEAD_DIM:(h + 1) * C_HEAD_DIM]
            k = act_ref[0, rows, C_W + h * C_HEAD_DIM:C_W + (h + 1) * C_HEAD_DIM]
            v = act_ref[0, rows, 2 * C_W + h * C_HEAD_DIM:2 * C_W + (h + 1) * C_HEAD_DIM]
            beta = gb[:, C_HEADS + h:C_HEADS + h + 1]
            gcum = gcum_all[:, h:h + 1]
            kb = k * beta
            qs.append(q)
            ks.append(k)
            kbs.append(kb)
            decays.append(jnp.exp(jnp.where(incl, gcum - gcum_rows[h:h + 1, :], _NEG)))
            rhss.append(jnp.concatenate([v * beta, kb * jnp.exp(gcum)], axis=-1))
            qgs.append(q * jnp.exp(gcum))
            kdecs.append(k * jnp.exp(g_last[:, h:h + 1] - gcum))

    n_units = range(len(units))
    lowers, attns = [], []
    for u in n_units:
        prod = _dot(jnp.concatenate([kbs[u], qs[u]], axis=0),
                    jnp.concatenate([ks[u], ks[u]], axis=0), _NT)
        lowers.append(jnp.where(strict, prod[:CHUNK] * decays[u], 0.0))
        attns.append(jnp.where(incl, prod[CHUNK:] * decays[u], 0.0))

    pws = [-lowers[u] for u in n_units]
    invs = [eye + pws[u] for u in n_units]
    pw_l, pw_r = zip(*[split3(pws[u]) for u in n_units])
    pws = [mm(pw_l[u], pw_r[u]) for u in n_units]
    for _ in range(4):
        pw_l, pw_r = zip(*[split3(pws[u]) for u in n_units])
        both = [mm(jnp.concatenate([split3(invs[u])[0], pw_l[u]], axis=0), pw_r[u]) for u in n_units]
        invs = [invs[u] + both[u][:CHUNK] for u in n_units]
        pws = [both[u][CHUNK:] for u in n_units]
    invs = [invs[u] + mm(split3(invs[u])[0], split3(pws[u])[1]) for u in n_units]
    sols = []
    for u in n_units:
        r_hi16 = rhss[u].astype(_BF16)
        r_lo16 = (rhss[u] - r_hi16.astype(_F32)).astype(_BF16)
        sols.append(mm(split3(invs[u])[0], jnp.concatenate([r_hi16, r_hi16, r_lo16, jnp.zeros_like(r_lo16)], axis=0)))

    st = [st_ref[h] for h in heads]
    for u, (s, h) in enumerate(units):
        rows = slice(s * CHUNK, (s + 1) * CHUNK)
        w_u = sols[u][:, :C_HEAD_DIM]
        w_w = sols[u][:, C_HEAD_DIM:]
        ws = _dot(jnp.concatenate([w_w, qgs[u]], axis=0), st[h])
        v_new = w_u - ws[:CHUNK]
        o = ws[CHUNK:] + _dot(attns[u][:, :CHUNK], v_new)
        st[h] = jnp.exp(g_lasts[s][:, h:h + 1]) * st[h] + _dot(kdecs[u], v_new, _TN)
        cols = slice(h * C_HEAD_DIM, (h + 1) * C_HEAD_DIM)
        o_ref[0, rows, cols] = (_rms(o, ng_ref[...]) * _silu(z_ref[0, rows, cols])).astype(_BF16)
    for h in heads:
        st_ref[h] = st[h]

    @pl.when(c == pl.num_programs(1) - 1)
    def _():
        s_out_ref[0] = st_ref[...]


def _delta(act, z, gb, s0, norm_g):
    bsz, lq, _ = act.shape
    n_sub = min(DELTA_SUBCHUNKS, lq // CHUNK)
    rows_all = n_sub * CHUNK
    tok = lambda n: pl.BlockSpec((1, rows_all, n), lambda b, c: (b, c, 0))
    state = pl.BlockSpec((1, C_HEADS, C_HEAD_DIM, C_HEAD_DIM), lambda b, c: (b, 0, 0, 0))
    return pl.pallas_call(
        functools.partial(_delta_kernel, n_sub=n_sub),
        grid=(bsz, lq // rows_all),
        in_specs=[tok(CONV_CH), tok(C_W), tok(V7X_LANES), state,
                  pl.BlockSpec((1, C_HEAD_DIM), lambda b, c: (0, 0))],
        out_specs=[tok(C_W), state],
        out_shape=[jax.ShapeDtypeStruct((bsz, lq, C_W), _BF16),
                   jax.ShapeDtypeStruct(s0.shape, _F32)],
        scratch_shapes=[pltpu.VMEM((C_HEADS, C_HEAD_DIM, C_HEAD_DIM), _F32)],
        compiler_params=_params(2),
        name="gated_delta",
    )(act, z, gb, s0, norm_g.reshape(1, C_HEAD_DIM))


def _prep_even_w_in(w):
    sizes = (A_W, A_W, A_W, IDX_HEADS * IDX_DIM, IDX_DIM, IDX_HEADS, B_KW, B_KW, B_VW, B_VW, B_GATE_RANK)
    offs = [0]
    for s in sizes:
        offs.append(offs[-1] + s)
    part = lambda i: w[:, offs[i]:offs[i + 1]]
    q, k, v, qi, ki, wi, qb, kb, vb, rb, ab = [part(i) for i in range(len(sizes))]
    pad = jnp.zeros((w.shape[0], _EV_IDX_W - (256 + IDX_DIM + IDX_HEADS + B_GATE_RANK)), w.dtype)
    return jnp.concatenate([q, k, v, qi, ki, wi, ab, pad, qb, kb, vb, rb], axis=1).astype(_BF16)


def _prep_odd_w_in(w):
    pad = jnp.zeros((w.shape[0], _OD_COLS - w.shape[1]), w.dtype)
    return jnp.concatenate([w, pad], axis=1).astype(_BF16)


def _lane_pad(v, n=V7X_LANES):
    return jnp.pad(v, (0, n - v.shape[0])).reshape(1, n)


def _trunk(x, past, prm):
    bsz, lq, d = x.shape
    t = bsz * lq
    xf = x.reshape(t, d)

    def ffn(xf, i, j, acts=(), proj_w=()):
        return _ffn_half(xf, prm['ffn_norm'][i, j], prm['ffn_w_gate'][i, j].astype(_BF16),
                         prm['ffn_w_up'][i, j].astype(_BF16), prm['ffn_w_down'][i, j].astype(_BF16),
                         acts, proj_w)

    xf = ffn(xf, 0, 0)
    head_mean = jnp.kron(jnp.eye(A_HEADS, dtype=_F32),
                         jnp.full((A_HEAD_DIM, A_HEAD_DIM), 1.0 / A_HEAD_DIM, _F32)).astype(_BF16)
    q_gain = (jnp.tile(prm['ev_q_norm'][0], A_HEADS) * (A_HEAD_DIM ** -0.5)).reshape(1, A_W)
    k_gain = jnp.tile(prm['ev_k_norm'][0], A_HEADS).reshape(1, A_W)
    w2 = jnp.zeros((V7X_LANES, B_KW), _F32).at[
        _EV_AB_OFF - 256:_EV_AB_OFF - 256 + B_GATE_RANK].set(prm['ev_gate_w2'][0]).astype(_BF16)
    k_new, v_new, k16, ki, ki2, qb, kb, vb, rb, la, qt, vt, qit, wit = _even_in_proj(
        xf, prm['mix_norm'][0], _prep_even_w_in(prm['ev_w_in'][0]), q_gain, k_gain, head_mean,
        w2, prm['ev_gate_b2'][0].reshape(1, B_KW), lq)
    r3 = lambda a: a.reshape(bsz, lq, a.shape[-1])
    ki_new = r3(ki)
    if past is None:
        past_len = 0
        k_all, vt_all, ki2_all = r3(k16), vt, r3(ki2)
        s0_gla = jnp.zeros((bsz, B_HEADS, B_KEY_DIM, B_VAL_DIM), _F32)
    else:
        past_len = past['k'].shape[2]
        k_all = jnp.concatenate([past['k'][0].reshape(bsz, past_len, A_W).astype(_BF16), r3(k16)], axis=1)
        vt_all = jnp.concatenate([jnp.swapaxes(past['v'][0].reshape(bsz, past_len, A_W).astype(_BF16), 1, 2), vt],
                                 axis=2)
        ki2_all = jnp.concatenate([jnp.concatenate([past['ki'][0], past['ki'][0]], axis=-1), r3(ki2)], axis=1)
        s0_gla = past['gla'][0]
    o_a = _dsa_attention(qt, qit, wit, k_all, vt_all, ki2_all, prm['rel_bias'], past_len)
    o_b, s_gla = _gla(r3(qb), r3(kb), r3(la), r3(vb), r3(rb), s0_gla, prm['ev_gla_norm'][0])
    w_out = prm['ev_w_out'][0].astype(_BF16)
    xf = ffn(xf, 0, 1, [o_a.reshape(t, A_W), o_b.reshape(t, B_VW)], [w_out[:A_W], w_out[A_W:]])

    xf = ffn(xf, 1, 0)
    a_log = _lane_pad(prm['od_a_log'][0])
    dt_bias = _lane_pad(prm['od_dt_bias'][0])
    if past is None:
        conv_prev = jnp.zeros((bsz, CONV_WIDTH - 1, CONV_CH), _F32)
        s0_delta = jnp.zeros((bsz, C_HEADS, C_HEAD_DIM, C_HEAD_DIM), _F32)
    else:
        conv_prev, s0_delta = past['conv'][0], past['delta'][0]
    conv_prev8 = jnp.pad(conv_prev, ((0, 0), (V7X_SUBLANES - (CONV_WIDTH - 1), 0), (0, 0)))
    act, z, gb, tail = _odd_in_proj(xf, prm['mix_norm'][1], _prep_odd_w_in(prm['od_w_in'][0]), a_log, dt_bias,
                                    prm['od_conv_w'][0], conv_prev8, lq)
    o_c, s_delta = _delta(r3(act), r3(z), r3(gb), s0_delta, prm['od_norm'][0])
    conv_new = tail[:, V7X_SUBLANES - (CONV_WIDTH - 1):]
    xf = ffn(xf, 1, 1, [o_c.reshape(t, C_W)], [prm['od_w_out'][0].astype(_BF16)])

    y = xf.reshape(bsz, lq, d)
    k_out = r3(k_new).reshape(1, bsz, lq, A_HEADS, A_HEAD_DIM)
    v_out = r3(v_new).reshape(1, bsz, lq, A_HEADS, A_HEAD_DIM)
    return y, k_out, v_out, ki_new[None], s_gla[None], s_delta[None], conv_new[None]


def kernel(x_prompt, x_sample, cache_attn_k, cache_attn_v, cache_idx_k, state_gla, state_delta, state_conv,
           ffn_norm, ffn_w_gate, ffn_w_up, ffn_w_down, mix_norm, ev_w_in, ev_q_norm, ev_k_norm, rel_bias,
           ev_gate_w2, ev_gate_b2, ev_gla_norm, ev_w_out, od_w_in, od_conv_w, od_a_log, od_dt_bias, od_norm,
           od_w_out):
    prm = {'ffn_norm': ffn_norm, 'ffn_w_gate': ffn_w_gate, 'ffn_w_up': ffn_w_up, 'ffn_w_down': ffn_w_down,
           'mix_norm': mix_norm, 'ev_w_in': ev_w_in, 'ev_q_norm': ev_q_norm, 'ev_k_norm': ev_k_norm,
           'rel_bias': rel_bias, 'ev_gate_w2': ev_gate_w2, 'ev_gate_b2': ev_gate_b2, 'ev_gla_norm': ev_gla_norm,
           'ev_w_out': ev_w_out, 'od_w_in': od_w_in, 'od_conv_w': od_conv_w, 'od_a_log': od_a_log,
           'od_dt_bias': od_dt_bias, 'od_norm': od_norm, 'od_w_out': od_w_out}
    past = {'k': cache_attn_k, 'v': cache_attn_v, 'ki': cache_idx_k, 'gla': state_gla,
            'delta': state_delta, 'conv': state_conv}
    y_prompt, p_k, p_v, p_ki, p_gla, p_delta, p_conv = _trunk(x_prompt, None, prm)
    y_sample, s_k, s_v, s_ki, s_gla, s_delta, s_conv = _trunk(x_sample, past, prm)
    return (y_prompt, y_sample, p_k, p_v, p_ki, p_gla, p_delta, p_conv,
            s_k, s_v, s_ki, s_gla, s_delta, s_conv)
```

```python
import functools
import math

import jax
import jax.numpy as jnp
from jax import lax
from jax.experimental import pallas as pl
from jax.experimental.pallas import tpu as pltpu

CHUNK = 64
EPS = 1e-6

A_HEADS = 8
A_HEAD_DIM = 64
A_W = A_HEADS * A_HEAD_DIM
IDX_HEADS = 4
IDX_DIM = 64
TOPK_MAX = 256
N_BUCKETS = 32
MAX_DISTANCE = 128

B_HEADS = 4
B_KEY_DIM = 64
B_VAL_DIM = 128
B_KW = B_HEADS * B_KEY_DIM
B_VW = B_HEADS * B_VAL_DIM
B_GATE_RANK = 16
B_GATE_TAU = 16.0

C_HEADS = 8
C_HEAD_DIM = 128
C_W = C_HEADS * C_HEAD_DIM
CONV_WIDTH = 4
CONV_CH = 3 * C_W

V7X_LANES = 128
V7X_SUBLANES = 8
V7X_VMEM_LIMIT_BYTES = 56 * 1024 * 1024
V7X_MXU_DIM = 256

ROW_TILE = 2 * V7X_MXU_DIM
ODD_ROW_TILE = V7X_MXU_DIM
DSA_KEY_CHUNK = V7X_MXU_DIM
GLA_SUBCHUNKS = 4
DELTA_SUBCHUNKS = 4

_F32 = jnp.float32
_BF16 = jnp.bfloat16
_NEG = -1e30
_LOG2E = math.log2(math.e)
_NN = (((1,), (0,)), ((), ()))
_NT = (((1,), (1,)), ((), ()))
_TN = (((0,), (0,)), ((), ()))

_EV_Q, _EV_K, _EV_V = 0, 512, 1024
_EV_IDX = 1536
_EV_IDX_W = 384
_EV_WI_OFF = 256 + IDX_DIM
_EV_AB_OFF = _EV_WI_OFF + IDX_HEADS
_EV_QB, _EV_KB, _EV_VB, _EV_RB = 1920, 2176, 2432, 2944
_EV_COLS = 3456
_OD_QKV, _OD_Z, _OD_AB = 0, 3072, 4096
_OD_COLS = 4224


def _params(n_axes):
    return pltpu.CompilerParams(dimension_semantics=("arbitrary",) * n_axes,
                                vmem_limit_bytes=V7X_VMEM_LIMIT_BYTES)


def _rms(x, g):
    return x * lax.rsqrt(jnp.mean(x * x, axis=-1, keepdims=True) + EPS) * g


def _silu(x):
    return x * jax.nn.sigmoid(x)


def _dot(a, b, dims=_NN):
    return lax.dot_general(a.astype(_BF16), b.astype(_BF16), dims, preferred_element_type=_F32)


def _full(shape):
    return pl.BlockSpec(shape, lambda *_: (0,) * len(shape))


_FFN_COLS = V7X_MXU_DIM


def _ffn_kernel(*refs, n_proj):
    x_ref = refs[0]
    g_ref, wg_ref, wu_ref, wd_ref, o_ref, act_ref = refs[1 + 2 * n_proj:]
    x = x_ref[...]
    for i in range(n_proj):
        x = x + jnp.dot(refs[1 + i][...], refs[1 + n_proj + i][...], preferred_element_type=_F32)
    h = _rms(x, g_ref[...]).astype(_BF16)
    for c in range(wg_ref.shape[1] // _FFN_COLS):
        cols = slice(c * _FFN_COLS, (c + 1) * _FFN_COLS)
        gate = jnp.dot(h, wg_ref[:, cols], preferred_element_type=_F32)
        up = jnp.dot(h, wu_ref[:, cols], preferred_element_type=_F32)
        act_ref[:, cols] = (_silu(gate) * up).astype(_BF16)
    o_ref[...] = x + 0.5 * jnp.dot(act_ref[...], wd_ref[...], preferred_element_type=_F32)


def _ffn_half(x, g, wg, wu, wd, acts=(), proj_w=()):
    t, d = x.shape
    ff = wg.shape[1]
    tm = min(t, ROW_TILE)
    resident = lambda shape: pl.BlockSpec(shape, lambda i: (0, 0), pipeline_mode=pl.Buffered(1))
    return pl.pallas_call(
        functools.partial(_ffn_kernel, n_proj=len(acts)),
        grid=(t // tm,),
        in_specs=([pl.BlockSpec((tm, d), lambda i: (i, 0))]
                  + [pl.BlockSpec((tm, a.shape[1]), lambda i: (i, 0)) for a in acts]
                  + [resident(w.shape) for w in proj_w]
                  + [resident((1, d)), resident((d, ff)), resident((d, ff)), resident((ff, d))]),
        out_specs=pl.BlockSpec((tm, d), lambda i: (i, 0)),
        out_shape=jax.ShapeDtypeStruct((t, d), _F32),
        scratch_shapes=[pltpu.VMEM((tm, ff), _BF16)],
        compiler_params=_params(1),
        name="ffn_half",
    )(x, *acts, *proj_w, g.reshape(1, d), wg, wu, wd)


def _even_in_kernel(x_ref, g_ref, w_ref, qg_ref, kg_ref, hm_ref, w2_ref, b2_ref,
                    k_ref, v_ref, k16_ref, ki_ref, ki2_ref, qb_ref, kb_ref, vb_ref, rb_ref, la_ref,
                    qt_ref, vt_ref, qit_ref, wit_ref, *, seg):
    h = _rms(x_ref[...], g_ref[...]).astype(_BF16)

    def proj(lo, width):
        return jnp.dot(h, w_ref[:, lo:lo + width], preferred_element_type=_F32)

    def headnorm(t, gain):
        t2 = t * t
        hi = t2.astype(_BF16)
        lo = (t2 - hi.astype(_F32)).astype(_BF16)
        ms = (jnp.dot(hi, hm_ref[...], preferred_element_type=_F32)
              + jnp.dot(lo, hm_ref[...], preferred_element_type=_F32))
        return t * lax.rsqrt(ms + EPS) * gain

    qn = headnorm(proj(_EV_Q, A_W), qg_ref[...]).astype(_BF16)
    k = headnorm(proj(_EV_K, A_W), kg_ref[...])
    v = proj(_EV_V, A_W)
    k_ref[...] = k
    v_ref[...] = v
    k16_ref[...] = k.astype(_BF16)
    v16 = v.astype(_BF16)
    idx = proj(_EV_IDX, _EV_IDX_W)
    ki = idx[:, IDX_HEADS * IDX_DIM:IDX_HEADS * IDX_DIM + IDX_DIM]
    ki_ref[...] = ki
    ki2_ref[...] = jnp.concatenate([ki, ki], axis=1)
    for s in range(x_ref.shape[0] // seg):
        rows = slice(s * seg, (s + 1) * seg)
        qt_ref[s] = qn[rows].T
        vt_ref[s] = v16[rows].T
        qit_ref[s] = idx[rows, 0:IDX_HEADS * IDX_DIM].T
        wit_ref[s] = idx[rows, IDX_HEADS * IDX_DIM:_EV_IDX_W].T[IDX_DIM:IDX_DIM + V7X_SUBLANES]
    qb_ref[...] = proj(_EV_QB, B_KW) * (B_KEY_DIM ** -0.5)
    kb_ref[...] = proj(_EV_KB, B_KW)
    vb_ref[...] = proj(_EV_VB, B_VW)
    rb_ref[...] = proj(_EV_RB, B_VW)
    z = _dot(idx[:, 256:384], w2_ref[...]) + b2_ref[...]
    la_ref[...] = (jnp.minimum(z, 0.0) - jnp.log1p(jnp.exp(-jnp.abs(z)))) * (1.0 / B_GATE_TAU)


def _even_in_proj(x, g, w, q_gain, k_gain, head_mean, w2, b2, lq):
    t, d = x.shape
    tm = min(t, ROW_TILE)
    seg = min(tm, lq)
    tiles_per_batch = lq // seg
    row = lambda n: pl.BlockSpec((tm, n), lambda i: (i, 0))
    col = lambda n: pl.BlockSpec((tm // seg, n, seg), lambda i: (i // tiles_per_batch, 0, i % tiles_per_batch))
    outs = [(A_W, _F32), (A_W, _F32), (A_W, _BF16), (IDX_DIM, _F32), (2 * IDX_DIM, _F32), (B_KW, _F32), (B_KW, _F32),
            (B_VW, _F32), (B_VW, _F32), (B_KW, _F32)]
    outs_t = [(A_W, _BF16), (A_W, _BF16), (IDX_HEADS * IDX_DIM, _F32), (V7X_SUBLANES, _F32)]
    w_spec = pl.BlockSpec((d, _EV_COLS), lambda i: (0, 0), pipeline_mode=pl.Buffered(1))
    return pl.pallas_call(
        functools.partial(_even_in_kernel, seg=seg),
        grid=(t // tm,),
        in_specs=[row(d), _full((1, d)), w_spec, _full((1, A_W)), _full((1, A_W)),
                  _full((A_W, A_W)), _full((V7X_LANES, B_KW)), _full((1, B_KW))],
        out_specs=[row(n) for n, _ in outs] + [col(n) for n, _ in outs_t],
        out_shape=([jax.ShapeDtypeStruct((t, n), dt) for n, dt in outs]
                   + [jax.ShapeDtypeStruct((t // lq, n, lq), dt) for n, dt in outs_t]),
        compiler_params=_params(1),
        name="even_in_proj",
    )(x, g.reshape(1, d), w, q_gain, k_gain, head_mean, w2, b2)


def _dsa_kernel(rb_ref, qt_ref, qit_ref, wit_ref, k_ref, vt_ref, ki_ref, bkt_ref, o_ref,
                key_ref, mask_ref, nb_ref, ki3_ref, acc_ref, lt_ref, p_ref, *, past, tq, kc, topk, r0_off, chunk_counts):
    t = pl.program_id(1)

    @pl.when((pl.program_id(0) == 0) & (t == 0))
    def _():
        bk = bkt_ref[...]
        for h in range(A_HEADS):
            acc = jnp.zeros(bk.shape, _F32)
            for bb in range(N_BUCKETS):
                acc = jnp.where(bk == bb, rb_ref[bb, h], acc)
            nb_ref[h] = (acc - rb_ref[N_BUCKETS // 2 - 1, h]) * _LOG2E

    q0 = past + t * tq
    n_chunks = (q0 + tq + kc - 1) // kc
    q_limit = ((q0 + lax.broadcasted_iota(jnp.int32, (1, tq), 1)) // CHUNK + 1) * CHUNK
    k_iota = lax.broadcasted_iota(jnp.int32, (kc, tq), 0)

    def admissible(r0):
        return k_iota < (q_limit - r0)

    @pl.when(t == 0)
    def _():
        def split_body(j, carry):
            r0 = pl.multiple_of(j * kc, kc)
            x = ki_ref[0, pl.ds(r0, kc), :]
            hi = x.astype(_BF16)
            ki3_ref[pl.ds(r0, kc), 0:V7X_LANES] = hi
            ki3_ref[pl.ds(r0, kc), V7X_LANES:2 * V7X_LANES] = (x - hi.astype(_F32)).astype(_BF16)
            return carry
        lax.fori_loop(0, ki3_ref.shape[0] // kc, split_body, 0)

    qit = qit_ref[0]
    q_cols = []
    for h in range(IDX_HEADS):
        x = qit[h * IDX_DIM:(h + 1) * IDX_DIM, :]
        hi = x.astype(_BF16)
        lo = (x - hi.astype(_F32)).astype(_BF16)
        q_cols.append(jnp.concatenate([hi, lo, hi, jnp.zeros_like(hi)], axis=0))
    q3 = jnp.concatenate(q_cols, axis=1)
    wit = wit_ref[0]

    upper_rows = lax.broadcasted_iota(jnp.int32, (V7X_LANES, tq), 0) >= A_HEAD_DIM
    q_pairs = []
    for pr in range(A_HEADS // 2):
        blk = qt_ref[0, pr * V7X_LANES:(pr + 1) * V7X_LANES, :].astype(_F32)
        q_pairs.append(jnp.concatenate([jnp.where(upper_rows, 0.0, blk), jnp.where(upper_rows, blk, 0.0)],
                                       axis=1).astype(_BF16))

    def scores_and_logits(n):
        d_all = jnp.dot(ki3_ref[0:n * kc, :], q3, preferred_element_type=_F32)
        for j in range(n):
            d = d_all[j * kc:(j + 1) * kc]
            s = jnp.zeros((kc, tq), _F32)
            for h in range(IDX_HEADS):
                s = s + wit[h:h + 1, :] * jnp.maximum(d[:, h * tq:(h + 1) * tq], 0.0)
            s = jnp.where(s == 0.0, 0.0, s)
            s = jnp.where(admissible(j * kc), s, -jnp.inf)
            bits = pltpu.bitcast(s, jnp.int32)
            key_ref[j * kc:(j + 1) * kc, :] = bits ^ ((bits >> 31) & jnp.int32(0x7FFFFFFF))
        for pr in range(A_HEADS // 2):
            lt2 = jnp.dot(k_ref[0, 0:n * kc, pr * V7X_LANES:(pr + 1) * V7X_LANES], q_pairs[pr],
                          preferred_element_type=_F32)
            for hh in range(2):
                lt_ref[2 * pr + hh, 0:n * kc, :] = lt2[:, hh * tq:(hh + 1) * tq] * _LOG2E
        for j in range(max(n - 2, 0), n):
            off = pl.multiple_of(jnp.maximum(j * kc - q0 + r0_off, 0), CHUNK)
            for h in range(A_HEADS):
                lt_ref[h, j * kc:(j + 1) * kc, :] += nb_ref[h, pl.ds(off, kc), :]

    int_min = jnp.int32(-2 ** 31)

    def count_keys(pred, n):
        parts = []
        for j in range(n):
            m = jnp.where(pred(key_ref[j * kc:(j + 1) * kc, :]), 1, 0)
            parts.append(m.reshape(kc // V7X_SUBLANES, V7X_SUBLANES, tq).sum(axis=0))
        while len(parts) > 1:
            parts = [sum(parts[i:i + 2]) for i in range(0, len(parts), 2)]
        return parts[0].sum(axis=0, keepdims=True)

    def radix_select(n):
        def bit_body(i, carry):
            tu, n_ge = carry
            cu = tu | jnp.left_shift(jnp.int32(1), 31 - i)
            cs = cu ^ int_min
            cnt = count_keys(lambda blk: blk >= cs, n)
            keep = cnt >= topk
            return jnp.where(keep, cu, tu), jnp.where(keep, cnt, n_ge)

        start = (jnp.zeros((1, tq), jnp.int32), jnp.full((1, tq), n * kc, jnp.int32))
        if n * kc <= topk:
            return start
        return lax.fori_loop(0, 32, bit_body, start)

    def tile_select(n):
        scores_and_logits(n)
        return radix_select(n)

    tu, n_ge = lax.switch(n_chunks - chunk_counts[0], [functools.partial(tile_select, n) for n in chunk_counts])
    ts = tu ^ int_min

    tri = (lax.broadcasted_iota(jnp.int32, (kc, kc), 0)
           >= lax.broadcasted_iota(jnp.int32, (kc, kc), 1)).astype(_BF16)

    def tie_masks():
        def gt_body(j, acc):
            r0 = pl.multiple_of(j * kc, kc)
            m = jnp.where(key_ref[pl.ds(r0, kc), :] > ts, 1, 0)
            return acc + m.reshape(kc // V7X_SUBLANES, V7X_SUBLANES, tq).sum(axis=0)

        n_greater = lax.fori_loop(0, n_chunks, gt_body, jnp.zeros((V7X_SUBLANES, tq), jnp.int32))
        room = (topk - n_greater.sum(axis=0, keepdims=True)).astype(_F32)

        def body(j, seen):
            r0 = pl.multiple_of(j * kc, kc)
            blk = key_ref[pl.ds(r0, kc), :]
            eq = blk == ts
            rank = jnp.dot(tri, jnp.where(eq, 1.0, 0.0).astype(_BF16), preferred_element_type=_F32) + seen
            val = jnp.where(blk > ts, 0.0, jnp.where(eq, jnp.where(rank <= room, 0.0, _NEG), _NEG))
            mask_ref[pl.ds(r0, kc), :] = jnp.where(admissible(r0), val, _NEG)
            return rank[kc - 1:kc, :]

        return lax.fori_loop(0, n_chunks, body, jnp.zeros((1, tq), _F32))

    def plain_masks():
        def body(j, carry):
            r0 = pl.multiple_of(j * kc, kc)
            val = jnp.where(key_ref[pl.ds(r0, kc), :] >= ts, 0.0, _NEG)
            mask_ref[pl.ds(r0, kc), :] = jnp.where(admissible(r0), val, _NEG)
            return carry

        return lax.fori_loop(0, n_chunks, body, jnp.zeros((1, tq), _F32))

    lax.cond(jnp.max(n_ge) > topk, tie_masks, plain_masks)

    acc_ref[...] = jnp.zeros_like(acc_ref)

    def attn_body(j, carry):
        ms, ls = carry
        r0 = pl.multiple_of(j * kc, kc)
        msk = mask_ref[pl.ds(r0, kc), :]
        new_ms, new_ls, alphas = [], [], []
        for h in range(A_HEADS):
            lt = lt_ref[h, pl.ds(r0, kc), :] + msk
            m_new = jnp.maximum(ms[h], lt.max(axis=0, keepdims=True))
            p = jnp.exp2(lt - m_new)
            alpha = jnp.exp2(ms[h] - m_new)
            new_ms.append(m_new)
            new_ls.append(alpha * ls[h] + p.sum(axis=0, keepdims=True))
            alphas.append(alpha)
            p_ref[h] = p.astype(_BF16)
        for h in range(A_HEADS):
            rows = slice(h * A_HEAD_DIM, (h + 1) * A_HEAD_DIM)
            pv = jnp.dot(vt_ref[0, rows, pl.ds(r0, kc)], p_ref[h], preferred_element_type=_F32)
            acc_ref[rows, :] = alphas[h] * acc_ref[rows, :] + pv
        return tuple(new_ms), tuple(new_ls)

    init = (tuple(jnp.full((1, tq), _NEG, _F32) for _ in range(A_HEADS)),
            tuple(jnp.zeros((1, tq), _F32) for _ in range(A_HEADS)))

    def attend(n):
        carry = init
        for j in range(n):
            carry = attn_body(j, carry)
        return carry

    _, ls = lax.switch(n_chunks - chunk_counts[0], [functools.partial(attend, n) for n in chunk_counts])

    for pair in range(A_HEADS // 2):
        lanes = slice(pair * V7X_LANES, (pair + 1) * V7X_LANES)
        o_t = jnp.concatenate(
            [acc_ref[(2 * pair + hh) * A_HEAD_DIM:(2 * pair + hh + 1) * A_HEAD_DIM, :] / ls[2 * pair + hh]
             for hh in range(2)], axis=0)
        o_ref[0, :, lanes] = o_t.T.astype(_BF16)


def _t5_bucket_table(rel):
    half = N_BUCKETS // 2
    max_exact = half // 2
    n = jnp.abs(rel)
    nf = jnp.maximum(n, 1).astype(jnp.float32)
    large = max_exact + (jnp.log(nf / max_exact) / math.log(MAX_DISTANCE / max_exact)
                         * (half - max_exact)).astype(jnp.int32)
    large = jnp.minimum(large, half - 1)
    return jnp.where(rel > 0, half, 0) + jnp.where(n < max_exact, n, large)


def _dsa_attention(qt, qit, wit, k_all, vt_all, ki2_all, rel_bias, past):
    bsz, _, lq = qt.shape
    nk = k_all.shape[1]
    topk = min(TOPK_MAX, nk // 4)
    tq = min(V7X_LANES, lq)
    kc = DSA_KEY_CHUNK
    nk_pad = -(-nk // kc) * kc
    pad = ((0, 0), (0, nk_pad - nk), (0, 0))
    k16 = jnp.pad(k_all, pad)
    vt16 = jnp.pad(vt_all, ((0, 0), (0, 0), (0, nk_pad - nk)))
    ki2 = jnp.pad(ki2_all, pad)
    r0_off = kc + MAX_DISTANCE
    nbr = r0_off + kc + tq
    rel = (jnp.arange(nbr, dtype=jnp.int32)[:, None] - r0_off) - jnp.arange(tq, dtype=jnp.int32)[None, :]
    bkt = _t5_bucket_table(rel)
    chunk_counts = sorted({(past + (t + 1) * tq + kc - 1) // kc for t in range(lq // tq)})
    assert chunk_counts == list(range(chunk_counts[0], chunk_counts[-1] + 1))
    kern = functools.partial(_dsa_kernel, past=past, tq=tq, kc=kc, topk=topk, r0_off=r0_off,
                             chunk_counts=tuple(chunk_counts))
    return pl.pallas_call(
        kern,
        grid=(bsz, lq // tq),
        in_specs=[pl.BlockSpec(memory_space=pltpu.SMEM),
                  pl.BlockSpec((1, A_W, tq), lambda b, t: (b, 0, t)),
                  pl.BlockSpec((1, IDX_HEADS * IDX_DIM, tq), lambda b, t: (b, 0, t)),
                  pl.BlockSpec((1, V7X_SUBLANES, tq), lambda b, t: (b, 0, t)),
                  pl.BlockSpec((1, nk_pad, A_W), lambda b, t: (b, 0, 0)),
                  pl.BlockSpec((1, A_W, nk_pad), lambda b, t: (b, 0, 0)),
                  pl.BlockSpec((1, nk_pad, V7X_LANES), lambda b, t: (b, 0, 0)),
                  pl.BlockSpec((nbr, tq), lambda b, t: (0, 0))],
        out_specs=pl.BlockSpec((1, tq, A_W), lambda b, t: (b, t, 0)),
        out_shape=jax.ShapeDtypeStruct((bsz, lq, A_W), _BF16),
        scratch_shapes=[pltpu.VMEM((nk_pad, tq), jnp.int32), pltpu.VMEM((nk_pad, tq), _F32),
                        pltpu.VMEM((A_HEADS, nbr, tq), _F32),
                        pltpu.VMEM((nk_pad, 2 * V7X_LANES), _BF16), pltpu.VMEM((A_W, tq), _F32),
                        pltpu.VMEM((A_HEADS, nk_pad, tq), _F32), pltpu.VMEM((A_HEADS, kc, tq), _BF16)],
        compiler_params=_params(2),
        name="dsa_attention",
    )(rel_bias, qt, qit, wit, k16, vt16, ki2, bkt)


def _split3_rows(x):
    hi = x.astype(_BF16)
    r1 = x - hi.astype(_F32)
    mid = r1.astype(_BF16)
    lo = (r1 - mid.astype(_F32)).astype(_BF16)
    return jnp.concatenate([hi, mid, lo], axis=0)


def _gla_kernel(q_ref, k_ref, g_ref, v_ref, r_ref, s0_ref, ng_ref, o_ref, s_out_ref, st_ref, *, n_sub):
    c = pl.program_id(1)
    causal = (lax.broadcasted_iota(jnp.int32, (CHUNK, CHUNK), 0)
              >= lax.broadcasted_iota(jnp.int32, (CHUNK, CHUNK), 1))
    causal3 = jnp.concatenate([causal.astype(_BF16)] * 3, axis=1)
    upper_half = lax.broadcasted_iota(jnp.int32, (CHUNK, V7X_LANES), 1) >= B_KEY_DIM
    diag = (lax.broadcasted_iota(jnp.int32, (V7X_LANES, V7X_LANES), 0)
            == lax.broadcasted_iota(jnp.int32, (V7X_LANES, V7X_LANES), 1))
    pairs = range(B_HEADS // 2)
    heads = range(B_HEADS)

    @pl.when(c == 0)
    def _():
        st_ref[...] = s0_ref[0]

    def pair_lanes(x, p):
        return x[:, p * V7X_LANES:(p + 1) * V7X_LANES]

    def head_half(x, h):
        keep = upper_half if h % 2 == 1 else jnp.logical_not(upper_half)
        return jnp.where(keep, pair_lanes(x, h // 2), 0.0)

    parts = []
    for s in range(n_sub):
        rows = slice(s * CHUNK, (s + 1) * CHUNK)
        b = jnp.dot(causal3, _split3_rows(g_ref[0, rows, :]), preferred_element_type=_F32)
        mid = CHUNK // 2
        b_mid = b[mid:mid + 1, :]
        b_last = b[CHUNK - 1:CHUNK, :]
        q = q_ref[0, rows, :]
        k = k_ref[0, rows, :]
        qe = q * jnp.exp(b)
        qm = q * jnp.exp(b - b_mid)
        km = k * jnp.exp(b_mid - b)
        kl = k * jnp.exp(b_last - b)
        vs = [v_ref[0, rows, h * B_VAL_DIM:(h + 1) * B_VAL_DIM] for h in heads]
        a = [jnp.where(causal, _dot(head_half(qm, h), pair_lanes(km, h // 2), _NT), 0.0) for h in heads]
        av = [_dot(a[h], vs[h]) for h in heads]
        upd = [_dot(head_half(kl, h), vs[h], _TN) for h in heads]
        dcol = [jnp.sum(jnp.where(diag, jnp.exp(pair_lanes(b_last, p)), 0.0), axis=1, keepdims=True)
                for p in pairs]
        parts.append((qe, av, upd, dcol))

    st = [st_ref[p * V7X_LANES:(p + 1) * V7X_LANES, :] for p in pairs]
    for s in range(n_sub):
        rows = slice(s * CHUNK, (s + 1) * CHUNK)
        qe, av, upd, dcol = parts[s]
        o = [_dot(head_half(qe, h), st[h // 2]) + av[h] for h in heads]
        st = [dcol[p] * st[p] + upd[2 * p] + upd[2 * p + 1] for p in pairs]
        for h in heads:
            cols = slice(h * B_VAL_DIM, (h + 1) * B_VAL_DIM)
            o_ref[0, rows, cols] = (_rms(o[h], ng_ref[...]) * _silu(r_ref[0, rows, cols])).astype(_BF16)
    for p in pairs:
        st_ref[p * V7X_LANES:(p + 1) * V7X_LANES, :] = st[p]

    @pl.when(c == pl.num_programs(1) - 1)
    def _():
        s_out_ref[0] = st_ref[...]


def _gla(qb, kb, la, vb, rb, s0, norm_g):
    bsz, lq, _ = qb.shape
    n_sub = min(GLA_SUBCHUNKS, lq // CHUNK)
    tok = lambda n: pl.BlockSpec((1, n_sub * CHUNK, n), lambda b, c: (b, c, 0))
    state = pl.BlockSpec((1, B_KW, B_VAL_DIM), lambda b, c: (b, 0, 0))
    o, s_new = pl.pallas_call(
        functools.partial(_gla_kernel, n_sub=n_sub),
        grid=(bsz, lq // (n_sub * CHUNK)),
        in_specs=[tok(B_KW), tok(B_KW), tok(B_KW), tok(B_VW), tok(B_VW), state,
                  pl.BlockSpec((1, B_VAL_DIM), lambda b, c: (0, 0))],
        out_specs=[tok(B_VW), state],
        out_shape=[jax.ShapeDtypeStruct((bsz, lq, B_VW), _BF16),
                   jax.ShapeDtypeStruct((bsz, B_KW, B_VAL_DIM), _F32)],
        scratch_shapes=[pltpu.VMEM((B_KW, B_VAL_DIM), _F32)],
        compiler_params=_params(2),
        name="gla",
    )(qb, kb, la, vb, rb, s0.reshape(bsz, B_KW, B_VAL_DIM), norm_g.reshape(1, B_VAL_DIM))
    return o, s_new.reshape(s0.shape)


def _odd_in_kernel(x_ref, g_ref, w_ref, alog_ref, dt_ref, cw_ref, cprev_ref, act_ref, z_ref, gb_ref, tail_ref,
                   xbuf_ref, *, seg, tiles_per_batch):
    halo = V7X_SUBLANES
    h = _rms(x_ref[...], g_ref[...]).astype(_BF16)
    z_ref[...] = jnp.dot(h, w_ref[:, _OD_Z:_OD_Z + C_W], preferred_element_type=_F32)
    ab = jnp.dot(h, w_ref[:, _OD_AB:_OD_AB + V7X_LANES], preferred_element_type=_F32)
    xa = ab + dt_ref[...]
    g = -jnp.exp(alog_ref[...]) * (jnp.maximum(xa, 0.0) + jnp.log1p(jnp.exp(-jnp.abs(xa))))
    lane = lax.broadcasted_iota(jnp.int32, ab.shape, 1)
    gb_ref[...] = jnp.where(lane < C_HEADS, g, jax.nn.sigmoid(ab))

    for s in range(x_ref.shape[0] // seg):
        rows = slice(s * seg, (s + 1) * seg)
        if tiles_per_batch > 1:
            @pl.when(pl.program_id(0) % tiles_per_batch == 0)
            def _():
                xbuf_ref[0:halo, :] = cprev_ref[0]
        else:
            xbuf_ref[0:halo, :] = cprev_ref[s]
        xbuf_ref[halo:halo + seg, :] = jnp.dot(h[rows], w_ref[:, _OD_QKV:_OD_QKV + CONV_CH],
                                               preferred_element_type=_F32)
        conv = xbuf_ref[halo:halo + seg, :] * cw_ref[CONV_WIDTH - 1:CONV_WIDTH, :]
        for j in range(CONV_WIDTH - 1):
            sh = CONV_WIDTH - 1 - j
            conv = conv + xbuf_ref[halo - sh:halo - sh + seg, :] * cw_ref[j:j + 1, :]
        tail = xbuf_ref[seg:seg + halo, :]
        xbuf_ref[0:halo, :] = tail
        tail_ref[s] = tail
        act = _silu(conv)
        for hd in range(C_HEADS):
            for part, scale in ((0, C_HEAD_DIM ** -0.5), (1, 1.0)):
                cols = slice(part * C_W + hd * C_HEAD_DIM, part * C_W + (hd + 1) * C_HEAD_DIM)
                a = act[:, cols]
                act_ref[rows, cols] = a * (lax.rsqrt(jnp.sum(a * a, axis=-1, keepdims=True) + EPS) * scale)
        act_ref[rows, 2 * C_W:] = act[:, 2 * C_W:]


def _odd_in_proj(x, g, w, a_log, dt_bias, conv_w, conv_prev8, lq):
    t, d = x.shape
    tm = min(t, ODD_ROW_TILE)
    seg = min(tm, lq)
    batches_per_tile = tm // seg
    tiles_per_batch = lq // seg
    row = lambda n: pl.BlockSpec((tm, n), lambda i: (i, 0))
    per_batch = pl.BlockSpec((batches_per_tile, V7X_SUBLANES, CONV_CH), lambda i: (i // tiles_per_batch, 0, 0))
    outs = [CONV_CH, C_W, V7X_LANES]
    return pl.pallas_call(
        functools.partial(_odd_in_kernel, seg=seg, tiles_per_batch=tiles_per_batch),
        grid=(t // tm,),
        in_specs=[row(d), _full((1, d)), _full((d, _OD_COLS)), _full((1, V7X_LANES)), _full((1, V7X_LANES)),
                  _full((CONV_WIDTH, CONV_CH)), per_batch],
        out_specs=[row(n) for n in outs] + [per_batch],
        out_shape=([jax.ShapeDtypeStruct((t, n), _F32) for n in outs]
                   + [jax.ShapeDtypeStruct((t // lq, V7X_SUBLANES, CONV_CH), _F32)]),
        scratch_shapes=[pltpu.VMEM((seg + V7X_SUBLANES, CONV_CH), _F32)],
        compiler_params=_params(1),
        name="odd_in_proj",
    )(x, g.reshape(1, d), w, a_log, dt_bias, conv_w, conv_prev8)


def _delta_kernel(act_ref, z_ref, gb_ref, s0_ref, ng_ref, o_ref, s_out_ref, st_ref, *, n_sub):
    c = pl.program_id(1)

    @pl.when(c == 0)
    def _():
        st_ref[...] = s0_ref[0]

    t_i = lax.broadcasted_iota(jnp.int32, (CHUNK, V7X_LANES), 0)
    s_i = lax.broadcasted_iota(jnp.int32, (CHUNK, V7X_LANES), 1)
    lower_half = s_i < CHUNK
    s_mod = jnp.where(lower_half, s_i, s_i - CHUNK)
    incl = t_i >= s_mod
    strict = t_i > s_mod
    eye = (t_i == s_mod).astype(_F32)
    incl3 = jnp.concatenate([(lax.broadcasted_iota(jnp.int32, (CHUNK, CHUNK), 0)
                              >= lax.broadcasted_iota(jnp.int32, (CHUNK, CHUNK), 1)).astype(_BF16)] * 3, axis=1)

    def split3(x):
        hi16 = x.astype(_BF16)
        hi = hi16.astype(_F32)
        lo = x - hi
        lhs = jnp.concatenate([jnp.where(lower_half, hi, lo).astype(_BF16), hi16], axis=1)
        lo16 = lo.astype(_BF16)
        rhs = jnp.concatenate([hi16, hi16, lo16, jnp.zeros_like(lo16)], axis=0)
        return lhs, rhs

    def mm(lhs, rhs):
        return jnp.dot(lhs, rhs, preferred_element_type=_F32)

    heads = range(C_HEADS)
    units = [(s, h) for s in range(n_sub) for h in heads]
    g_lasts = []
    qs, ks, kbs, decays, rhss, qgs, kdecs = [], [], [], [], [], [], []
    for s in range(n_sub):
        rows = slice(s * CHUNK, (s + 1) * CHUNK)
        gb = gb_ref[0, rows, :]
        gcum_all = jnp.dot(incl3, _split3_rows(gb), preferred_element_type=_F32)
        gcum_rows = jnp.concatenate([gcum_all, gcum_all], axis=0).T
        g_last = gcum_all[CHUNK - 1:CHUNK, :]
        g_lasts.append(g_last)
        for h in heads:
            q = act_ref[0, rows, h * C_HEAD_DIM:(h + 1) * C_HEAD_DIM]
            k = act_ref[0, rows, C_W + h * C_HEAD_DIM:C_W + (h + 1) * C_HEAD_DIM]
            v = act_ref[0, rows, 2 * C_W + h * C_HEAD_DIM:2 * C_W + (h + 1) * C_HEAD_DIM]
            beta = gb[:, C_HEADS + h:C_HEADS + h + 1]
            gcum = gcum_all[:, h:h + 1]
            kb = k * beta
            qs.append(q)
            ks.append(k)
            kbs.append(kb)
            decays.append(jnp.exp(jnp.where(incl, gcum - gcum_rows[h:h + 1, :], _NEG)))
            rhss.append(jnp.concatenate([v * beta, kb * jnp.exp(gcum)], axis=-1))
            qgs.append(q * jnp.exp(gcum))
            kdecs.append(k * jnp.exp(g_last[:, h:h + 1] - gcum))

    n_units = range(len(units))
    lowers, attns = [], []
    for u in n_units:
        prod = _dot(jnp.concatenate([kbs[u], qs[u]], axis=0),
                    jnp.concatenate([ks[u], ks[u]], axis=0), _NT)
        lowers.append(jnp.where(strict, prod[:CHUNK] * decays[u], 0.0))
        attns.append(jnp.where(incl, prod[CHUNK:] * decays[u], 0.0))

    pws = [-lowers[u] for u in n_units]
    invs = [eye + pws[u] for u in n_units]
    pw_l, pw_r = zip(*[split3(pws[u]) for u in n_units])
    pws = [mm(pw_l[u], pw_r[u]) for u in n_units]
    for _ in range(4):
        pw_l, pw_r = zip(*[split3(pws[u]) for u in n_units])
        both = [mm(jnp.concatenate([split3(invs[u])[0], pw_l[u]], axis=0), pw_r[u]) for u in n_units]
        invs = [invs[u] + both[u][:CHUNK] for u in n_units]
        pws = [both[u][CHUNK:] for u in n_units]
    invs = [invs[u] + mm(split3(invs[u])[0], split3(pws[u])[1]) for u in n_units]
    sols = []
    for u in n_units:
        r_hi16 = rhss[u].astype(_BF16)
        r_lo16 = (rhss[u] - r_hi16.astype(_F32)).astype(_BF16)
        sols.append(mm(split3(invs[u])[0], jnp.concatenate([r_hi16, r_hi16, r_lo16, jnp.zeros_like(r_lo16)], axis=0)))

    st = [st_ref[h] for h in heads]
    for u, (s, h) in enumerate(units):
        rows = slice(s * CHUNK, (s + 1) * CHUNK)
        w_u = sols[u][:, :C_HEAD_DIM]
        w_w = sols[u][:, C_HEAD_DIM:]
        ws = _dot(jnp.concatenate([w_w, qgs[u]], axis=0), st[h])
        v_new = w_u - ws[:CHUNK]
        o = ws[CHUNK:] + _dot(attns[u][:, :CHUNK], v_new)
        st[h] = jnp.exp(g_lasts[s][:, h:h + 1]) * st[h] + _dot(kdecs[u], v_new, _TN)
        cols = slice(h * C_HEAD_DIM, (h + 1) * C_HEAD_DIM)
        o_ref[0, rows, cols] = (_rms(o, ng_ref[...]) * _silu(z_ref[0, rows, cols])).astype(_BF16)
    for h in heads:
        st_ref[h] = st[h]

    @pl.when(c == pl.num_programs(1) - 1)
    def _():
        s_out_ref[0] = st_ref[...]


def _delta(act, z, gb, s0, norm_g):
    bsz, lq, _ = act.shape
    n_sub = min(DELTA_SUBCHUNKS, lq // CHUNK)
    rows_all = n_sub * CHUNK
    tok = lambda n: pl.BlockSpec((1, rows_all, n), lambda b, c: (b, c, 0))
    state = pl.BlockSpec((1, C_HEADS, C_HEAD_DIM, C_HEAD_DIM), lambda b, c: (b, 0, 0, 0))
    return pl.pallas_call(
        functools.partial(_delta_kernel, n_sub=n_sub),
        grid=(bsz, lq // rows_all),
        in_specs=[tok(CONV_CH), tok(C_W), tok(V7X_LANES), state,
                  pl.BlockSpec((1, C_HEAD_DIM), lambda b, c: (0, 0))],
        out_specs=[tok(C_W), state],
        out_shape=[jax.ShapeDtypeStruct((bsz, lq, C_W), _BF16),
                   jax.ShapeDtypeStruct(s0.shape, _F32)],
        scratch_shapes=[pltpu.VMEM((C_HEADS, C_HEAD_DIM, C_HEAD_DIM), _F32)],
        compiler_params=_params(2),
        name="gated_delta",
    )(act, z, gb, s0, norm_g.reshape(1, C_HEAD_DIM))


def _prep_even_w_in(w):
    sizes = (A_W, A_W, A_W, IDX_HEADS * IDX_DIM, IDX_DIM, IDX_HEADS, B_KW, B_KW, B_VW, B_VW, B_GATE_RANK)
    offs = [0]
    for s in sizes:
        offs.append(offs[-1] + s)
    part = lambda i: w[:, offs[i]:offs[i + 1]]
    q, k, v, qi, ki, wi, qb, kb, vb, rb, ab = [part(i) for i in range(len(sizes))]
    pad = jnp.zeros((w.shape[0], _EV_IDX_W - (256 + IDX_DIM + IDX_HEADS + B_GATE_RANK)), w.dtype)
    return jnp.concatenate([q, k, v, qi, ki, wi, ab, pad, qb, kb, vb, rb], axis=1).astype(_BF16)


def _prep_odd_w_in(w):
    pad = jnp.zeros((w.shape[0], _OD_COLS - w.shape[1]), w.dtype)
    return jnp.concatenate([w, pad], axis=1).astype(_BF16)


def _lane_pad(v, n=V7X_LANES):
    return jnp.pad(v, (0, n - v.shape[0])).reshape(1, n)


def _trunk(x, past, prm):
    bsz, lq, d = x.shape
    t = bsz * lq
    xf = x.reshape(t, d)

    def ffn(xf, i, j, acts=(), proj_w=()):
        return _ffn_half(xf, prm['ffn_norm'][i, j], prm['ffn_w_gate'][i, j].astype(_BF16),
                         prm['ffn_w_up'][i, j].astype(_BF16), prm['ffn_w_down'][i, j].astype(_BF16),
                         acts, proj_w)

    xf = ffn(xf, 0, 0)
    head_mean = jnp.kron(jnp.eye(A_HEADS, dtype=_F32),
                         jnp.full((A_HEAD_DIM, A_HEAD_DIM), 1.0 / A_HEAD_DIM, _F32)).astype(_BF16)
    q_gain = (jnp.tile(prm['ev_q_norm'][0], A_HEADS) * (A_HEAD_DIM ** -0.5)).reshape(1, A_W)
    k_gain = jnp.tile(prm['ev_k_norm'][0], A_HEADS).reshape(1, A_W)
    w2 = jnp.zeros((V7X_LANES, B_KW), _F32).at[
        _EV_AB_OFF - 256:_EV_AB_OFF - 256 + B_GATE_RANK].set(prm['ev_gate_w2'][0]).astype(_BF16)
    k_new, v_new, k16, ki, ki2, qb, kb, vb, rb, la, qt, vt, qit, wit = _even_in_proj(
        xf, prm['mix_norm'][0], _prep_even_w_in(prm['ev_w_in'][0]), q_gain, k_gain, head_mean,
        w2, prm['ev_gate_b2'][0].reshape(1, B_KW), lq)
    r3 = lambda a: a.reshape(bsz, lq, a.shape[-1])
    ki_new = r3(ki)
    if past is None:
        past_len = 0
        k_all, vt_all, ki2_all = r3(k16), vt, r3(ki2)
        s0_gla = jnp.zeros((bsz, B_HEADS, B_KEY_DIM, B_VAL_DIM), _F32)
    else:
        past_len = past['k'].shape[2]
        k_all = jnp.concatenate([past['k'][0].reshape(bsz, past_len, A_W).astype(_BF16), r3(k16)], axis=1)
        vt_all = jnp.concatenate([jnp.swapaxes(past['v'][0].reshape(bsz, past_len, A_W).astype(_BF16), 1, 2), vt],
                                 axis=2)
        ki2_all = jnp.concatenate([jnp.concatenate([past['ki'][0], past['ki'][0]], axis=-1), r3(ki2)], axis=1)
        s0_gla = past['gla'][0]
    o_a = _dsa_attention(qt, qit, wit, k_all, vt_all, ki2_all, prm['rel_bias'], past_len)
    o_b, s_gla = _gla(r3(qb), r3(kb), r3(la), r3(vb), r3(rb), s0_gla, prm['ev_gla_norm'][0])
    w_out = prm['ev_w_out'][0].astype(_BF16)
    xf = ffn(xf, 0, 1, [o_a.reshape(t, A_W), o_b.reshape(t, B_VW)], [w_out[:A_W], w_out[A_W:]])

    xf = ffn(xf, 1, 0)
    a_log = _lane_pad(prm['od_a_log'][0])
    dt_bias = _lane_pad(prm['od_dt_bias'][0])
    if past is None:
        conv_prev = jnp.zeros((bsz, CONV_WIDTH - 1, CONV_CH), _F32)
        s0_delta = jnp.zeros((bsz, C_HEADS, C_HEAD_DIM, C_HEAD_DIM), _F32)
    else:
        conv_prev, s0_delta = past['conv'][0], past['delta'][0]
    conv_prev8 = jnp.pad(conv_prev, ((0, 0), (V7X_SUBLANES - (CONV_WIDTH - 1), 0), (0, 0)))
    act, z, gb, tail = _odd_in_proj(xf, prm['mix_norm'][1], _prep_odd_w_in(prm['od_w_in'][0]), a_log, dt_bias,
                                    prm['od_conv_w'][0], conv_prev8, lq)
    o_c, s_delta = _delta(r3(act), r3(z), r3(gb), s0_delta, prm['od_norm'][0])
    conv_new = tail[:, V7X_SUBLANES - (CONV_WIDTH - 1):]
    xf = ffn(xf, 1, 1, [o_c.reshape(t, C_W)], [prm['od_w_out'][0].astype(_BF16)])

    y = xf.reshape(bsz, lq, d)
    k_out = r3(k_new).reshape(1, bsz, lq, A_HEADS, A_HEAD_DIM)
    v_out = r3(v_new).reshape(1, bsz, lq, A_HEADS, A_HEAD_DIM)
    return y, k_out, v_out, ki_new[None], s_gla[None], s_delta[None], conv_new[None]


def kernel(x_prompt, x_sample, cache_attn_k, cache_attn_v, cache_idx_k, state_gla, state_delta, state_conv,
           ffn_norm, ffn_w_gate, ffn_w_up, ffn_w_down, mix_norm, ev_w_in, ev_q_norm, ev_k_norm, rel_bias,
           ev_gate_w2, ev_gate_b2, ev_gla_norm, ev_w_out, od_w_in, od_conv_w, od_a_log, od_dt_bias, od_norm,
           od_w_out):
    prm = {'ffn_norm': ffn_norm, 'ffn_w_gate': ffn_w_gate, 'ffn_w_up': ffn_w_up, 'ffn_w_down': ffn_w_down,
           'mix_norm': mix_norm, 'ev_w_in': ev_w_in, 'ev_q_norm': ev_q_norm, 'ev_k_norm': ev_k_norm,
           'rel_bias': rel_bias, 'ev_gate_w2': ev_gate_w2, 'ev_gate_b2': ev_gate_b2, 'ev_gla_norm': ev_gla_norm,
           'ev_w_out': ev_w_out, 'od_w_in': od_w_in, 'od_conv_w': od_conv_w, 'od_a_log': od_a_log,
           'od_dt_bias': od_dt_bias, 'od_norm': od_norm, 'od_w_out': od_w_out}
    past = {'k': cache_attn_k, 'v': cache_attn_v, 'ki': cache_idx_k, 'gla': state_gla,
            'delta': state_delta, 'conv': state_conv}
    y_prompt, p_k, p_v, p_ki, p_gla, p_delta, p_conv = _trunk(x_prompt, None, prm)
    y_sample, s_k, s_v, s_ki, s_gla, s_delta, s_conv = _trunk(x_sample, past, prm)
    return (y_prompt, y_sample, p_k, p_v, p_ki, p_gla, p_delta, p_conv,
            s_k, s_v, s_ki, s_gla, s_delta, s_conv)
```

```python
import functools
import math

import jax
import jax.numpy as jnp
from jax import lax
from jax.experimental import pallas as pl
from jax.experimental.pallas import tpu as pltpu

CHUNK = 64
EPS = 1e-6

A_HEADS = 8
A_HEAD_DIM = 64
A_W = A_HEADS * A_HEAD_DIM
IDX_HEADS = 4
IDX_DIM = 64
TOPK_MAX = 256
N_BUCKETS = 32
MAX_DISTANCE = 128

B_HEADS = 4
B_KEY_DIM = 64
B_VAL_DIM = 128
B_KW = B_HEADS * B_KEY_DIM
B_VW = B_HEADS * B_VAL_DIM
B_GATE_RANK = 16
B_GATE_TAU = 16.0

C_HEADS = 8
C_HEAD_DIM = 128
C_W = C_HEADS * C_HEAD_DIM
CONV_WIDTH = 4
CONV_CH = 3 * C_W

V7X_LANES = 128
V7X_SUBLANES = 8
V7X_VMEM_LIMIT_BYTES = 56 * 1024 * 1024
V7X_MXU_DIM = 256

ROW_TILE = 2 * V7X_MXU_DIM
ODD_ROW_TILE = V7X_MXU_DIM
DSA_KEY_CHUNK = V7X_MXU_DIM
GLA_SUBCHUNKS = 8
DELTA_SUBCHUNKS = 8

_F32 = jnp.float32
_BF16 = jnp.bfloat16
_NEG = -1e30
_LOG2E = math.log2(math.e)
_NN = (((1,), (0,)), ((), ()))
_NT = (((1,), (1,)), ((), ()))
_TN = (((0,), (0,)), ((), ()))

_EV_Q, _EV_K, _EV_V = 0, 512, 1024
_EV_IDX = 1536
_EV_IDX_W = 384
_EV_WI_OFF = 256 + IDX_DIM
_EV_AB_OFF = _EV_WI_OFF + IDX_HEADS
_EV_QB, _EV_KB, _EV_VB, _EV_RB = 1920, 2176, 2432, 2944
_EV_COLS = 3456
_OD_QKV, _OD_Z, _OD_AB = 0, 3072, 4096
_OD_COLS = 4224


def _params(n_axes):
    return pltpu.CompilerParams(dimension_semantics=("arbitrary",) * n_axes,
                                vmem_limit_bytes=V7X_VMEM_LIMIT_BYTES)


def _rms(x, g):
    return x * lax.rsqrt(jnp.mean(x * x, axis=-1, keepdims=True) + EPS) * g


def _silu(x):
    return x * jax.nn.sigmoid(x)


def _dot(a, b, dims=_NN):
    return lax.dot_general(a.astype(_BF16), b.astype(_BF16), dims, preferred_element_type=_F32)


def _full(shape):
    return pl.BlockSpec(shape, lambda *_: (0,) * len(shape))


_FFN_COLS = V7X_MXU_DIM


def _ffn_kernel(*refs, n_proj):
    x_ref = refs[0]
    g_ref, wg_ref, wu_ref, wd_ref, o_ref, act_ref = refs[1 + 2 * n_proj:]
    x = x_ref[...]
    for i in range(n_proj):
        x = x + jnp.dot(refs[1 + i][...], refs[1 + n_proj + i][...], preferred_element_type=_F32)
    h = _rms(x, g_ref[...]).astype(_BF16)
    for c in range(wg_ref.shape[1] // _FFN_COLS):
        cols = slice(c * _FFN_COLS, (c + 1) * _FFN_COLS)
        gate = jnp.dot(h, wg_ref[:, cols], preferred_element_type=_F32)
        up = jnp.dot(h, wu_ref[:, cols], preferred_element_type=_F32)
        act_ref[:, cols] = (_silu(gate) * up).astype(_BF16)
    o_ref[...] = x + 0.5 * jnp.dot(act_ref[...], wd_ref[...], preferred_element_type=_F32)


def _ffn_half(x, g, wg, wu, wd, acts=(), proj_w=()):
    t, d = x.shape
    ff = wg.shape[1]
    tm = min(t, ROW_TILE)
    resident = lambda shape: pl.BlockSpec(shape, lambda i: (0, 0), pipeline_mode=pl.Buffered(1))
    return pl.pallas_call(
        functools.partial(_ffn_kernel, n_proj=len(acts)),
        grid=(t // tm,),
        in_specs=([pl.BlockSpec((tm, d), lambda i: (i, 0))]
                  + [pl.BlockSpec((tm, a.shape[1]), lambda i: (i, 0)) for a in acts]
                  + [resident(w.shape) for w in proj_w]
                  + [resident((1, d)), resident((d, ff)), resident((d, ff)), resident((ff, d))]),
        out_specs=pl.BlockSpec((tm, d), lambda i: (i, 0)),
        out_shape=jax.ShapeDtypeStruct((t, d), _F32),
        scratch_shapes=[pltpu.VMEM((tm, ff), _BF16)],
        compiler_params=_params(1),
        name="ffn_half",
    )(x, *acts, *proj_w, g.reshape(1, d), wg, wu, wd)


def _even_in_kernel(x_ref, g_ref, w_ref, qg_ref, kg_ref, hm_ref, w2_ref, b2_ref,
                    k_ref, v_ref, k16_ref, ki_ref, ki2_ref, qb_ref, kb_ref, vb_ref, rb_ref, la_ref,
                    qt_ref, vt_ref, qit_ref, wit_ref, *, seg):
    h = _rms(x_ref[...], g_ref[...]).astype(_BF16)

    def proj(lo, width):
        return jnp.dot(h, w_ref[:, lo:lo + width], preferred_element_type=_F32)

    def headnorm(t, gain):
        t2 = t * t
        hi = t2.astype(_BF16)
        lo = (t2 - hi.astype(_F32)).astype(_BF16)
        ms = (jnp.dot(hi, hm_ref[...], preferred_element_type=_F32)
              + jnp.dot(lo, hm_ref[...], preferred_element_type=_F32))
        return t * lax.rsqrt(ms + EPS) * gain

    qn = headnorm(proj(_EV_Q, A_W), qg_ref[...]).astype(_BF16)
    k = headnorm(proj(_EV_K, A_W), kg_ref[...])
    v = proj(_EV_V, A_W)
    k_ref[...] = k
    v_ref[...] = v
    k16_ref[...] = k.astype(_BF16)
    v16 = v.astype(_BF16)
    idx = proj(_EV_IDX, _EV_IDX_W)
    ki = idx[:, IDX_HEADS * IDX_DIM:IDX_HEADS * IDX_DIM + IDX_DIM]
    ki_ref[...] = ki
    ki2_ref[...] = jnp.concatenate([ki, ki], axis=1)
    for s in range(x_ref.shape[0] // seg):
        rows = slice(s * seg, (s + 1) * seg)
        qt_ref[s] = qn[rows].T
        vt_ref[s] = v16[rows].T
        qit_ref[s] = idx[rows, 0:IDX_HEADS * IDX_DIM].T
        wit_ref[s] = idx[rows, IDX_HEADS * IDX_DIM:_EV_IDX_W].T[IDX_DIM:IDX_DIM + V7X_SUBLANES]
    qb_ref[...] = proj(_EV_QB, B_KW) * (B_KEY_DIM ** -0.5)
    kb_ref[...] = proj(_EV_KB, B_KW)
    vb_ref[...] = proj(_EV_VB, B_VW)
    rb_ref[...] = proj(_EV_RB, B_VW)
    z = _dot(idx[:, 256:384], w2_ref[...]) + b2_ref[...]
    la_ref[...] = (jnp.minimum(z, 0.0) - jnp.log1p(jnp.exp(-jnp.abs(z)))) * (1.0 / B_GATE_TAU)


def _even_in_proj(x, g, w, q_gain, k_gain, head_mean, w2, b2, lq):
    t, d = x.shape
    tm = min(t, ROW_TILE)
    seg = min(tm, lq)
    tiles_per_batch = lq // seg
    row = lambda n: pl.BlockSpec((tm, n), lambda i: (i, 0))
    col = lambda n: pl.BlockSpec((tm // seg, n, seg), lambda i: (i // tiles_per_batch, 0, i % tiles_per_batch))
    outs = [(A_W, _F32), (A_W, _F32), (A_W, _BF16), (IDX_DIM, _F32), (2 * IDX_DIM, _F32), (B_KW, _F32), (B_KW, _F32),
            (B_VW, _F32), (B_VW, _F32), (B_KW, _F32)]
    outs_t = [(A_W, _BF16), (A_W, _BF16), (IDX_HEADS * IDX_DIM, _F32), (V7X_SUBLANES, _F32)]
    w_spec = pl.BlockSpec((d, _EV_COLS), lambda i: (0, 0), pipeline_mode=pl.Buffered(1))
    return pl.pallas_call(
        functools.partial(_even_in_kernel, seg=seg),
        grid=(t // tm,),
        in_specs=[row(d), _full((1, d)), w_spec, _full((1, A_W)), _full((1, A_W)),
                  _full((A_W, A_W)), _full((V7X_LANES, B_KW)), _full((1, B_KW))],
        out_specs=[row(n) for n, _ in outs] + [col(n) for n, _ in outs_t],
        out_shape=([jax.ShapeDtypeStruct((t, n), dt) for n, dt in outs]
                   + [jax.ShapeDtypeStruct((t // lq, n, lq), dt) for n, dt in outs_t]),
        compiler_params=_params(1),
        name="even_in_proj",
    )(x, g.reshape(1, d), w, q_gain, k_gain, head_mean, w2, b2)


def _dsa_kernel(rb_ref, qt_ref, qit_ref, wit_ref, k_ref, vt_ref, ki_ref, bkt_ref, o_ref,
                key_ref, mask_ref, nb_ref, ki3_ref, acc_ref, lt_ref, p_ref, *, past, tq, kc, topk, r0_off, chunk_counts):
    t = pl.program_id(1)

    @pl.when((pl.program_id(0) == 0) & (t == 0))
    def _():
        bk = bkt_ref[...]
        for h in range(A_HEADS):
            acc = jnp.zeros(bk.shape, _F32)
            for bb in range(N_BUCKETS):
                acc = jnp.where(bk == bb, rb_ref[bb, h], acc)
            nb_ref[h] = (acc - rb_ref[N_BUCKETS // 2 - 1, h]) * _LOG2E

    q0 = past + t * tq
    n_chunks = (q0 + tq + kc - 1) // kc
    q_limit = ((q0 + lax.broadcasted_iota(jnp.int32, (1, tq), 1)) // CHUNK + 1) * CHUNK
    k_iota = lax.broadcasted_iota(jnp.int32, (kc, tq), 0)

    def admissible(r0):
        return k_iota < (q_limit - r0)

    @pl.when(t == 0)
    def _():
        def split_body(j, carry):
            r0 = pl.multiple_of(j * kc, kc)
            x = ki_ref[0, pl.ds(r0, kc), :]
            hi = x.astype(_BF16)
            ki3_ref[pl.ds(r0, kc), 0:V7X_LANES] = hi
            ki3_ref[pl.ds(r0, kc), V7X_LANES:2 * V7X_LANES] = (x - hi.astype(_F32)).astype(_BF16)
            return carry
        lax.fori_loop(0, ki3_ref.shape[0] // kc, split_body, 0)

    qit = qit_ref[0]
    q_cols = []
    for h in range(IDX_HEADS):
        x = qit[h * IDX_DIM:(h + 1) * IDX_DIM, :]
        hi = x.astype(_BF16)
        lo = (x - hi.astype(_F32)).astype(_BF16)
        q_cols.append(jnp.concatenate([hi, lo, hi, jnp.zeros_like(hi)], axis=0))
    q3 = jnp.concatenate(q_cols, axis=1)
    wit = wit_ref[0]

    upper_rows = lax.broadcasted_iota(jnp.int32, (V7X_LANES, tq), 0) >= A_HEAD_DIM
    q_pairs = []
    for pr in range(A_HEADS // 2):
        blk = qt_ref[0, pr * V7X_LANES:(pr + 1) * V7X_LANES, :].astype(_F32)
        q_pairs.append(jnp.concatenate([jnp.where(upper_rows, 0.0, blk), jnp.where(upper_rows, blk, 0.0)],
                                       axis=1).astype(_BF16))

    def scores_and_logits(n):
        d_all = jnp.dot(ki3_ref[0:n * kc, :], q3, preferred_element_type=_F32)
        for j in range(n):
            d = d_all[j * kc:(j + 1) * kc]
            s = jnp.zeros((kc, tq), _F32)
            for h in range(IDX_HEADS):
                s = s + wit[h:h + 1, :] * jnp.maximum(d[:, h * tq:(h + 1) * tq], 0.0)
            s = jnp.where(s == 0.0, 0.0, s)
            s = jnp.where(admissible(j * kc), s, -jnp.inf)
            bits = pltpu.bitcast(s, jnp.int32)
            key_ref[j * kc:(j + 1) * kc, :] = bits ^ ((bits >> 31) & jnp.int32(0x7FFFFFFF))
        for pr in range(A_HEADS // 2):
            lt2 = jnp.dot(k_ref[0, 0:n * kc, pr * V7X_LANES:(pr + 1) * V7X_LANES], q_pairs[pr],
                          preferred_element_type=_F32)
            for hh in range(2):
                lt_ref[2 * pr + hh, 0:n * kc, :] = lt2[:, hh * tq:(hh + 1) * tq] * _LOG2E
        for j in range(max(n - 2, 0), n):
            off = pl.multiple_of(jnp.maximum(j * kc - q0 + r0_off, 0), CHUNK)
            for h in range(A_HEADS):
                lt_ref[h, j * kc:(j + 1) * kc, :] += nb_ref[h, pl.ds(off, kc), :]

    int_min = jnp.int32(-2 ** 31)

    def count_keys(pred, n):
        parts = []
        for j in range(n):
            m = jnp.where(pred(key_ref[j * kc:(j + 1) * kc, :]), 1, 0)
            parts.append(m.reshape(kc // V7X_SUBLANES, V7X_SUBLANES, tq).sum(axis=0))
        while len(parts) > 1:
            parts = [sum(parts[i:i + 2]) for i in range(0, len(parts), 2)]
        return parts[0].sum(axis=0, keepdims=True)

    def radix_select(n):
        def bit_body(i, carry):
            tu, n_ge = carry
            cu = tu | jnp.left_shift(jnp.int32(1), 31 - i)
            cs = cu ^ int_min
            cnt = count_keys(lambda blk: blk >= cs, n)
            keep = cnt >= topk
            return jnp.where(keep, cu, tu), jnp.where(keep, cnt, n_ge)

        start = (jnp.zeros((1, tq), jnp.int32), jnp.full((1, tq), n * kc, jnp.int32))
        if n * kc <= topk:
            return start
        return lax.fori_loop(0, 32, bit_body, start)

    def tile_select(n):
        scores_and_logits(n)
        return radix_select(n)

    tu, n_ge = lax.switch(n_chunks - chunk_counts[0], [functools.partial(tile_select, n) for n in chunk_counts])
    ts = tu ^ int_min

    tri = (lax.broadcasted_iota(jnp.int32, (kc, kc), 0)
           >= lax.broadcasted_iota(jnp.int32, (kc, kc), 1)).astype(_BF16)

    def tie_masks():
        def gt_body(j, acc):
            r0 = pl.multiple_of(j * kc, kc)
            m = jnp.where(key_ref[pl.ds(r0, kc), :] > ts, 1, 0)
            return acc + m.reshape(kc // V7X_SUBLANES, V7X_SUBLANES, tq).sum(axis=0)

        n_greater = lax.fori_loop(0, n_chunks, gt_body, jnp.zeros((V7X_SUBLANES, tq), jnp.int32))
        room = (topk - n_greater.sum(axis=0, keepdims=True)).astype(_F32)

        def body(j, seen):
            r0 = pl.multiple_of(j * kc, kc)
            blk = key_ref[pl.ds(r0, kc), :]
            eq = blk == ts
            rank = jnp.dot(tri, jnp.where(eq, 1.0, 0.0).astype(_BF16), preferred_element_type=_F32) + seen
            val = jnp.where(blk > ts, 0.0, jnp.where(eq, jnp.where(rank <= room, 0.0, _NEG), _NEG))
            mask_ref[pl.ds(r0, kc), :] = jnp.where(admissible(r0), val, _NEG)
            return rank[kc - 1:kc, :]

        return lax.fori_loop(0, n_chunks, body, jnp.zeros((1, tq), _F32))

    def plain_masks():
        def body(j, carry):
            r0 = pl.multiple_of(j * kc, kc)
            val = jnp.where(key_ref[pl.ds(r0, kc), :] >= ts, 0.0, _NEG)
            mask_ref[pl.ds(r0, kc), :] = jnp.where(admissible(r0), val, _NEG)
            return carry

        return lax.fori_loop(0, n_chunks, body, jnp.zeros((1, tq), _F32))

    lax.cond(jnp.max(n_ge) > topk, tie_masks, plain_masks)

    acc_ref[...] = jnp.zeros_like(acc_ref)

    def attn_body(j, carry):
        ms, ls = carry
        r0 = pl.multiple_of(j * kc, kc)
        msk = mask_ref[pl.ds(r0, kc), :]
        new_ms, new_ls, alphas = [], [], []
        for h in range(A_HEADS):
            lt = lt_ref[h, pl.ds(r0, kc), :] + msk
            m_new = jnp.maximum(ms[h], lt.max(axis=0, keepdims=True))
            p = jnp.exp2(lt - m_new)
            alpha = jnp.exp2(ms[h] - m_new)
            new_ms.append(m_new)
            new_ls.append(alpha * ls[h] + p.sum(axis=0, keepdims=True))
            alphas.append(alpha)
            p_ref[h] = p.astype(_BF16)
        for h in range(A_HEADS):
            rows = slice(h * A_HEAD_DIM, (h + 1) * A_HEAD_DIM)
            pv = jnp.dot(vt_ref[0, rows, pl.ds(r0, kc)], p_ref[h], preferred_element_type=_F32)
            acc_ref[rows, :] = alphas[h] * acc_ref[rows, :] + pv
        return tuple(new_ms), tuple(new_ls)

    init = (tuple(jnp.full((1, tq), _NEG, _F32) for _ in range(A_HEADS)),
            tuple(jnp.zeros((1, tq), _F32) for _ in range(A_HEADS)))

    def attend(n):
        carry = init
        for j in range(n):
            carry = attn_body(j, carry)
        return carry

    _, ls = lax.switch(n_chunks - chunk_counts[0], [functools.partial(attend, n) for n in chunk_counts])

    for pair in range(A_HEADS // 2):
        lanes = slice(pair * V7X_LANES, (pair + 1) * V7X_LANES)
        o_t = jnp.concatenate(
            [acc_ref[(2 * pair + hh) * A_HEAD_DIM:(2 * pair + hh + 1) * A_HEAD_DIM, :] / ls[2 * pair + hh]
             for hh in range(2)], axis=0)
        o_ref[0, :, lanes] = o_t.T.astype(_BF16)


def _t5_bucket_table(rel):
    half = N_BUCKETS // 2
    max_exact = half // 2
    n = jnp.abs(rel)
    nf = jnp.maximum(n, 1).astype(jnp.float32)
    large = max_exact + (jnp.log(nf / max_exact) / math.log(MAX_DISTANCE / max_exact)
                         * (half - max_exact)).astype(jnp.int32)
    large = jnp.minimum(large, half - 1)
    return jnp.where(rel > 0, half, 0) + jnp.where(n < max_exact, n, large)


def _dsa_attention(qt, qit, wit, k_all, vt_all, ki2_all, rel_bias, past):
    bsz, _, lq = qt.shape
    nk = k_all.shape[1]
    topk = min(TOPK_MAX, nk // 4)
    tq = min(V7X_LANES, lq)
    kc = DSA_KEY_CHUNK
    nk_pad = -(-nk // kc) * kc
    pad = ((0, 0), (0, nk_pad - nk), (0, 0))
    k16 = jnp.pad(k_all, pad)
    vt16 = jnp.pad(vt_all, ((0, 0), (0, 0), (0, nk_pad - nk)))
    ki2 = jnp.pad(ki2_all, pad)
    r0_off = kc + MAX_DISTANCE
    nbr = r0_off + kc + tq
    rel = (jnp.arange(nbr, dtype=jnp.int32)[:, None] - r0_off) - jnp.arange(tq, dtype=jnp.int32)[None, :]
    bkt = _t5_bucket_table(rel)
    chunk_counts = sorted({(past + (t + 1) * tq + kc - 1) // kc for t in range(lq // tq)})
    assert chunk_counts == list(range(chunk_counts[0], chunk_counts[-1] + 1))
    kern = functools.partial(_dsa_kernel, past=past, tq=tq, kc=kc, topk=topk, r0_off=r0_off,
                             chunk_counts=tuple(chunk_counts))
    return pl.pallas_call(
        kern,
        grid=(bsz, lq // tq),
        in_specs=[pl.BlockSpec(memory_space=pltpu.SMEM),
                  pl.BlockSpec((1, A_W, tq), lambda b, t: (b, 0, t)),
                  pl.BlockSpec((1, IDX_HEADS * IDX_DIM, tq), lambda b, t: (b, 0, t)),
                  pl.BlockSpec((1, V7X_SUBLANES, tq), lambda b, t: (b, 0, t)),
                  pl.BlockSpec((1, nk_pad, A_W), lambda b, t: (b, 0, 0)),
                  pl.BlockSpec((1, A_W, nk_pad), lambda b, t: (b, 0, 0)),
                  pl.BlockSpec((1, nk_pad, V7X_LANES), lambda b, t: (b, 0, 0)),
                  pl.BlockSpec((nbr, tq), lambda b, t: (0, 0))],
        out_specs=pl.BlockSpec((1, tq, A_W), lambda b, t: (b, t, 0)),
        out_shape=jax.ShapeDtypeStruct((bsz, lq, A_W), _BF16),
        scratch_shapes=[pltpu.VMEM((nk_pad, tq), jnp.int32), pltpu.VMEM((nk_pad, tq), _F32),
                        pltpu.VMEM((A_HEADS, nbr, tq), _F32),
                        pltpu.VMEM((nk_pad, 2 * V7X_LANES), _BF16), pltpu.VMEM((A_W, tq), _F32),
                        pltpu.VMEM((A_HEADS, nk_pad, tq), _F32), pltpu.VMEM((A_HEADS, kc, tq), _BF16)],
        compiler_params=_params(2),
        name="dsa_attention",
    )(rel_bias, qt, qit, wit, k16, vt16, ki2, bkt)


def _split3_rows(x):
    hi = x.astype(_BF16)
    r1 = x - hi.astype(_F32)
    mid = r1.astype(_BF16)
    lo = (r1 - mid.astype(_F32)).astype(_BF16)
    return jnp.concatenate([hi, mid, lo], axis=0)


def _gla_kernel(q_ref, k_ref, g_ref, v_ref, r_ref, s0_ref, ng_ref, o_ref, s_out_ref, st_ref, *, n_sub):
    c = pl.program_id(1)
    causal = (lax.broadcasted_iota(jnp.int32, (CHUNK, CHUNK), 0)
              >= lax.broadcasted_iota(jnp.int32, (CHUNK, CHUNK), 1))
    causal3 = jnp.concatenate([causal.astype(_BF16)] * 3, axis=1)
    upper_half = lax.broadcasted_iota(jnp.int32, (CHUNK, V7X_LANES), 1) >= B_KEY_DIM
    diag = (lax.broadcasted_iota(jnp.int32, (V7X_LANES, V7X_LANES), 0)
            == lax.broadcasted_iota(jnp.int32, (V7X_LANES, V7X_LANES), 1))
    pairs = range(B_HEADS // 2)
    heads = range(B_HEADS)

    @pl.when(c == 0)
    def _():
        st_ref[...] = s0_ref[0]

    def pair_lanes(x, p):
        return x[:, p * V7X_LANES:(p + 1) * V7X_LANES]

    def head_half(x, h):
        keep = upper_half if h % 2 == 1 else jnp.logical_not(upper_half)
        return jnp.where(keep, pair_lanes(x, h // 2), 0.0)

    parts = []
    for s in range(n_sub):
        rows = slice(s * CHUNK, (s + 1) * CHUNK)
        b = jnp.dot(causal3, _split3_rows(g_ref[0, rows, :]), preferred_element_type=_F32)
        mid = CHUNK // 2
        b_mid = b[mid:mid + 1, :]
        b_last = b[CHUNK - 1:CHUNK, :]
        q = q_ref[0, rows, :]
        k = k_ref[0, rows, :]
        qe = q * jnp.exp(b)
        qm = q * jnp.exp(b - b_mid)
        km = k * jnp.exp(b_mid - b)
        kl = k * jnp.exp(b_last - b)
        vs = [v_ref[0, rows, h * B_VAL_DIM:(h + 1) * B_VAL_DIM] for h in heads]
        a = [jnp.where(causal, _dot(head_half(qm, h), pair_lanes(km, h // 2), _NT), 0.0) for h in heads]
        av = [_dot(a[h], vs[h]) for h in heads]
        upd = [_dot(head_half(kl, h), vs[h], _TN) for h in heads]
        dcol = [jnp.sum(jnp.where(diag, jnp.exp(pair_lanes(b_last, p)), 0.0), axis=1, keepdims=True)
                for p in pairs]
        parts.append((qe, av, upd, dcol))

    st = [st_ref[p * V7X_LANES:(p + 1) * V7X_LANES, :] for p in pairs]
    for s in range(n_sub):
        rows = slice(s * CHUNK, (s + 1) * CHUNK)
        qe, av, upd, dcol = parts[s]
        o = [_dot(head_half(qe, h), st[h // 2]) + av[h] for h in heads]
        st = [dcol[p] * st[p] + upd[2 * p] + upd[2 * p + 1] for p in pairs]
        for h in heads:
            cols = slice(h * B_VAL_DIM, (h + 1) * B_VAL_DIM)
            o_ref[0, rows, cols] = (_rms(o[h], ng_ref[...]) * _silu(r_ref[0, rows, cols])).astype(_BF16)
    for p in pairs:
        st_ref[p * V7X_LANES:(p + 1) * V7X_LANES, :] = st[p]

    @pl.when(c == pl.num_programs(1) - 1)
    def _():
        s_out_ref[0] = st_ref[...]


def _gla(qb, kb, la, vb, rb, s0, norm_g):
    bsz, lq, _ = qb.shape
    n_sub = min(GLA_SUBCHUNKS, lq // CHUNK)
    tok = lambda n: pl.BlockSpec((1, n_sub * CHUNK, n), lambda b, c: (b, c, 0))
    state = pl.BlockSpec((1, B_KW, B_VAL_DIM), lambda b, c: (b, 0, 0))
    o, s_new = pl.pallas_call(
        functools.partial(_gla_kernel, n_sub=n_sub),
        grid=(bsz, lq // (n_sub * CHUNK)),
        in_specs=[tok(B_KW), tok(B_KW), tok(B_KW), tok(B_VW), tok(B_VW), state,
                  pl.BlockSpec((1, B_VAL_DIM), lambda b, c: (0, 0))],
        out_specs=[tok(B_VW), state],
        out_shape=[jax.ShapeDtypeStruct((bsz, lq, B_VW), _BF16),
                   jax.ShapeDtypeStruct((bsz, B_KW, B_VAL_DIM), _F32)],
        scratch_shapes=[pltpu.VMEM((B_KW, B_VAL_DIM), _F32)],
        compiler_params=_params(2),
        name="gla",
    )(qb, kb, la, vb, rb, s0.reshape(bsz, B_KW, B_VAL_DIM), norm_g.reshape(1, B_VAL_DIM))
    return o, s_new.reshape(s0.shape)


def _odd_in_kernel(x_ref, g_ref, w_ref, alog_ref, dt_ref, cw_ref, cprev_ref, act_ref, z_ref, gb_ref, tail_ref,
                   xbuf_ref, *, seg, tiles_per_batch):
    halo = V7X_SUBLANES
    h = _rms(x_ref[...], g_ref[...]).astype(_BF16)
    z_ref[...] = jnp.dot(h, w_ref[:, _OD_Z:_OD_Z + C_W], preferred_element_type=_F32)
    ab = jnp.dot(h, w_ref[:, _OD_AB:_OD_AB + V7X_LANES], preferred_element_type=_F32)
    xa = ab + dt_ref[...]
    g = -jnp.exp(alog_ref[...]) * (jnp.maximum(xa, 0.0) + jnp.log1p(jnp.exp(-jnp.abs(xa))))
    lane = lax.broadcasted_iota(jnp.int32, ab.shape, 1)
    gb_ref[...] = jnp.where(lane < C_HEADS, g, jax.nn.sigmoid(ab))

    for s in range(x_ref.shape[0] // seg):
        rows = slice(s * seg, (s + 1) * seg)
        if tiles_per_batch > 1:
            @pl.when(pl.program_id(0) % tiles_per_batch == 0)
            def _():
                xbuf_ref[0:halo, :] = cprev_ref[0]
        else:
            xbuf_ref[0:halo, :] = cprev_ref[s]
        xbuf_ref[halo:halo + seg, :] = jnp.dot(h[rows], w_ref[:, _OD_QKV:_OD_QKV + CONV_CH],
                                               preferred_element_type=_F32)
        conv = xbuf_ref[halo:halo + seg, :] * cw_ref[CONV_WIDTH - 1:CONV_WIDTH, :]
        for j in range(CONV_WIDTH - 1):
            sh = CONV_WIDTH - 1 - j
            conv = conv + xbuf_ref[halo - sh:halo - sh + seg, :] * cw_ref[j:j + 1, :]
        tail = xbuf_ref[seg:seg + halo, :]
        xbuf_ref[0:halo, :] = tail
        tail_ref[s] = tail
        act = _silu(conv)
        for hd in range(C_HEADS):
            for part, scale in ((0, C_HEAD_DIM ** -0.5), (1, 1.0)):
                cols = slice(part * C_W + hd * C_HEAD_DIM, part * C_W + (hd + 1) * C_HEAD_DIM)
                a = act[:, cols]
                act_ref[rows, cols] = a * (lax.rsqrt(jnp.sum(a * a, axis=-1, keepdims=True) + EPS) * scale)
        act_ref[rows, 2 * C_W:] = act[:, 2 * C_W:]


def _odd_in_proj(x, g, w, a_log, dt_bias, conv_w, conv_prev8, lq):
    t, d = x.shape
    tm = min(t, ODD_ROW_TILE)
    seg = min(tm, lq)
    batches_per_tile = tm // seg
    tiles_per_batch = lq // seg
    row = lambda n: pl.BlockSpec((tm, n), lambda i: (i, 0))
    per_batch = pl.BlockSpec((batches_per_tile, V7X_SUBLANES, CONV_CH), lambda i: (i // tiles_per_batch, 0, 0))
    outs = [CONV_CH, C_W, V7X_LANES]
    return pl.pallas_call(
        functools.partial(_odd_in_kernel, seg=seg, tiles_per_batch=tiles_per_batch),
        grid=(t // tm,),
        in_specs=[row(d), _full((1, d)), _full((d, _OD_COLS)), _full((1, V7X_LANES)), _full((1, V7X_LANES)),
                  _full((CONV_WIDTH, CONV_CH)), per_batch],
        out_specs=[row(n) for n in outs] + [per_batch],
        out_shape=([jax.ShapeDtypeStruct((t, n), _F32) for n in outs]
                   + [jax.ShapeDtypeStruct((t // lq, V7X_SUBLANES, CONV_CH), _F32)]),
        scratch_shapes=[pltpu.VMEM((seg + V7X_SUBLANES, CONV_CH), _F32)],
        compiler_params=_params(1),
        name="odd_in_proj",
    )(x, g.reshape(1, d), w, a_log, dt_bias, conv_w, conv_prev8)


def _delta_kernel(act_ref, z_ref, gb_ref, s0_ref, ng_ref, o_ref, s_out_ref, st_ref, *, n_sub):
    c = pl.program_id(1)

    @pl.when(c == 0)
    def _():
        st_ref[...] = s0_ref[0]

    t_i = lax.broadcasted_iota(jnp.int32, (CHUNK, V7X_LANES), 0)
    s_i = lax.broadcasted_iota(jnp.int32, (CHUNK, V7X_LANES), 1)
    lower_half = s_i < CHUNK
    s_mod = jnp.where(lower_half, s_i, s_i - CHUNK)
    incl = t_i >= s_mod
    strict = t_i > s_mod
    eye = (t_i == s_mod).astype(_F32)
    incl3 = jnp.concatenate([(lax.broadcasted_iota(jnp.int32, (CHUNK, CHUNK), 0)
                              >= lax.broadcasted_iota(jnp.int32, (CHUNK, CHUNK), 1)).astype(_BF16)] * 3, axis=1)

    def split3(x):
        hi16 = x.astype(_BF16)
        hi = hi16.astype(_F32)
        lo = x - hi
        lhs = jnp.concatenate([jnp.where(lower_half, hi, lo).astype(_BF16), hi16], axis=1)
        lo16 = lo.astype(_BF16)
        rhs = jnp.concatenate([hi16, hi16, lo16, jnp.zeros_like(lo16)], axis=0)
        return lhs, rhs

    def mm(lhs, rhs):
        return jnp.dot(lhs, rhs, preferred_element_type=_F32)

    heads = range(C_HEADS)
    units = [(s, h) for s in range(n_sub) for h in heads]
    g_lasts = []
    qs, ks, kbs, decays, rhss, qgs, kdecs = [], [], [], [], [], [], []
    for s in range(n_sub):
        rows = slice(s * CHUNK, (s + 1) * CHUNK)
        gb = gb_ref[0, rows, :]
        gcum_all = jnp.dot(incl3, _split3_rows(gb), preferred_element_type=_F32)
        gcum_rows = jnp.concatenate([gcum_all, gcum_all], axis=0).T
        g_last = gcum_all[CHUNK - 1:CHUNK, :]
        g_lasts.append(g_last)
        for h in heads:
            q = act_ref[0, rows, h * C_HEAD_DIM:(h + 1) * C_HEAD_DIM]
            k = act_ref[0, rows, C_W + h * C_HEAD_DIM:C_W + (h + 1) * C_HEAD_DIM]
            v = act_ref[0, rows, 2 * C_W + h * C_HEAD_DIM:2 * C_W + (h + 1) * C_HEAD_DIM]
            beta = gb[:, C_HEADS + h:C_HEADS + h + 1]
            gcum = gcum_all[:, h:h + 1]
            kb = k * beta
            qs.append(q)
            ks.append(k)
            kbs.append(kb)
            decays.append(jnp.exp(jnp.where(incl, gcum - gcum_rows[h:h + 1, :], _NEG)))
            rhss.append(jnp.concatenate([v * beta, kb * jnp.exp(gcum)], axis=-1))
            qgs.append(q * jnp.exp(gcum))
            kdecs.append(k * jnp.exp(g_last[:, h:h + 1] - gcum))

    n_units = range(len(units))
    lowers, attns = [], []
    for u in n_units:
        prod = _dot(jnp.concatenate([kbs[u], qs[u]], axis=0),
                    jnp.concatenate([ks[u], ks[u]], axis=0).T)
        lowers.append(jnp.where(strict, prod[:CHUNK] * decays[u], 0.0))
        attns.append(jnp.where(incl, prod[CHUNK:] * decays[u], 0.0))

    pws = [-lowers[u] for u in n_units]
    invs = [eye + pws[u] for u in n_units]
    pw_l, pw_r = zip(*[split3(pws[u]) for u in n_units])
    pws = [mm(pw_l[u], pw_r[u]) for u in n_units]
    for _ in range(4):
        pw_l, pw_r = zip(*[split3(pws[u]) for u in n_units])
        both = [mm(jnp.concatenate([split3(invs[u])[0], pw_l[u]], axis=0), pw_r[u]) for u in n_units]
        invs = [invs[u] + both[u][:CHUNK] for u in n_units]
        pws = [both[u][CHUNK:] for u in n_units]
    invs = [invs[u] + mm(split3(invs[u])[0], split3(pws[u])[1]) for u in n_units]
    sols = []
    for u in n_units:
        r_hi16 = rhss[u].astype(_BF16)
        r_lo16 = (rhss[u] - r_hi16.astype(_F32)).astype(_BF16)
        sols.append(mm(split3(invs[u])[0], jnp.concatenate([r_hi16, r_hi16, r_lo16, jnp.zeros_like(r_lo16)], axis=0)))

    st = [st_ref[h] for h in heads]
    for u, (s, h) in enumerate(units):
        rows = slice(s * CHUNK, (s + 1) * CHUNK)
        w_u = sols[u][:, :C_HEAD_DIM]
        w_w = sols[u][:, C_HEAD_DIM:]
        ws = _dot(jnp.concatenate([w_w, qgs[u]], axis=0), st[h])
        v_new = w_u - ws[:CHUNK]
        o = ws[CHUNK:] + _dot(attns[u][:, :CHUNK], v_new)
        st[h] = jnp.exp(g_lasts[s][:, h:h + 1]) * st[h] + _dot(kdecs[u], v_new, _TN)
        cols = slice(h * C_HEAD_DIM, (h + 1) * C_HEAD_DIM)
        o_ref[0, rows, cols] = (_rms(o, ng_ref[...]) * _silu(z_ref[0, rows, cols])).astype(_BF16)
    for h in heads:
        st_ref[h] = st[h]

    @pl.when(c == pl.num_programs(1) - 1)
    def _():
        s_out_ref[0] = st_ref[...]


def _delta(act, z, gb, s0, norm_g):
    bsz, lq, _ = act.shape
    n_sub = min(DELTA_SUBCHUNKS, lq // CHUNK)
    rows_all = n_sub * CHUNK
    tok = lambda n: pl.BlockSpec((1, rows_all, n), lambda b, c: (b, c, 0))
    state = pl.BlockSpec((1, C_HEADS, C_HEAD_DIM, C_HEAD_DIM), lambda b, c: (b, 0, 0, 0))
    return pl.pallas_call(
        functools.partial(_delta_kernel, n_sub=n_sub),
        grid=(bsz, lq // rows_all),
        in_specs=[tok(CONV_CH), tok(C_W), tok(V7X_LANES), state,
                  pl.BlockSpec((1, C_HEAD_DIM), lambda b, c: (0, 0))],
        out_specs=[tok(C_W), state],
        out_shape=[jax.ShapeDtypeStruct((bsz, lq, C_W), _BF16),
                   jax.ShapeDtypeStruct(s0.shape, _F32)],
        scratch_shapes=[pltpu.VMEM((C_HEADS, C_HEAD_DIM, C_HEAD_DIM), _F32)],
        compiler_params=_params(2),
        name="gated_delta",
    )(act, z, gb, s0, norm_g.reshape(1, C_HEAD_DIM))


def _prep_even_w_in(w):
    sizes = (A_W, A_W, A_W, IDX_HEADS * IDX_DIM, IDX_DIM, IDX_HEADS, B_KW, B_KW, B_VW, B_VW, B_GATE_RANK)
    offs = [0]
    for s in sizes:
        offs.append(offs[-1] + s)
    part = lambda i: w[:, offs[i]:offs[i + 1]]
    q, k, v, qi, ki, wi, qb, kb, vb, rb, ab = [part(i) for i in range(len(sizes))]
    pad = jnp.zeros((w.shape[0], _EV_IDX_W - (256 + IDX_DIM + IDX_HEADS + B_GATE_RANK)), w.dtype)
    return jnp.concatenate([q, k, v, qi, ki, wi, ab, pad, qb, kb, vb, rb], axis=1).astype(_BF16)


def _prep_odd_w_in(w):
    pad = jnp.zeros((w.shape[0], _OD_COLS - w.shape[1]), w.dtype)
    return jnp.concatenate([w, pad], axis=1).astype(_BF16)


def _lane_pad(v, n=V7X_LANES):
    return jnp.pad(v, (0, n - v.shape[0])).reshape(1, n)


def _trunk(x, past, prm):
    bsz, lq, d = x.shape
    t = bsz * lq
    xf = x.reshape(t, d)

    def ffn(xf, i, j, acts=(), proj_w=()):
        return _ffn_half(xf, prm['ffn_norm'][i, j], prm['ffn_w_gate'][i, j].astype(_BF16),
                         prm['ffn_w_up'][i, j].astype(_BF16), prm['ffn_w_down'][i, j].astype(_BF16),
                         acts, proj_w)

    xf = ffn(xf, 0, 0)
    head_mean = jnp.kron(jnp.eye(A_HEADS, dtype=_F32),
                         jnp.full((A_HEAD_DIM, A_HEAD_DIM), 1.0 / A_HEAD_DIM, _F32)).astype(_BF16)
    q_gain = (jnp.tile(prm['ev_q_norm'][0], A_HEADS) * (A_HEAD_DIM ** -0.5)).reshape(1, A_W)
    k_gain = jnp.tile(prm['ev_k_norm'][0], A_HEADS).reshape(1, A_W)
    w2 = jnp.zeros((V7X_LANES, B_KW), _F32).at[
        _EV_AB_OFF - 256:_EV_AB_OFF - 256 + B_GATE_RANK].set(prm['ev_gate_w2'][0]).astype(_BF16)
    k_new, v_new, k16, ki, ki2, qb, kb, vb, rb, la, qt, vt, qit, wit = _even_in_proj(
        xf, prm['mix_norm'][0], _prep_even_w_in(prm['ev_w_in'][0]), q_gain, k_gain, head_mean,
        w2, prm['ev_gate_b2'][0].reshape(1, B_KW), lq)
    r3 = lambda a: a.reshape(bsz, lq, a.shape[-1])
    ki_new = r3(ki)
    if past is None:
        past_len = 0
        k_all, vt_all, ki2_all = r3(k16), vt, r3(ki2)
        s0_gla = jnp.zeros((bsz, B_HEADS, B_KEY_DIM, B_VAL_DIM), _F32)
    else:
        past_len = past['k'].shape[2]
        k_all = jnp.concatenate([past['k'][0].reshape(bsz, past_len, A_W).astype(_BF16), r3(k16)], axis=1)
        vt_all = jnp.concatenate([jnp.swapaxes(past['v'][0].reshape(bsz, past_len, A_W).astype(_BF16), 1, 2), vt],
                                 axis=2)
        ki2_all = jnp.concatenate([jnp.concatenate([past['ki'][0], past['ki'][0]], axis=-1), r3(ki2)], axis=1)
        s0_gla = past['gla'][0]
    o_a = _dsa_attention(qt, qit, wit, k_all, vt_all, ki2_all, prm['rel_bias'], past_len)
    o_b, s_gla = _gla(r3(qb), r3(kb), r3(la), r3(vb), r3(rb), s0_gla, prm['ev_gla_norm'][0])
    w_out = prm['ev_w_out'][0].astype(_BF16)
    xf = ffn(xf, 0, 1, [o_a.reshape(t, A_W), o_b.reshape(t, B_VW)], [w_out[:A_W], w_out[A_W:]])

    xf = ffn(xf, 1, 0)
    a_log = _lane_pad(prm['od_a_log'][0])
    dt_bias = _lane_pad(prm['od_dt_bias'][0])
    if past is None:
        conv_prev = jnp.zeros((bsz, CONV_WIDTH - 1, CONV_CH), _F32)
        s0_delta = jnp.zeros((bsz, C_HEADS, C_HEAD_DIM, C_HEAD_DIM), _F32)
    else:
        conv_prev, s0_delta = past['conv'][0], past['delta'][0]
    conv_prev8 = jnp.pad(conv_prev, ((0, 0), (V7X_SUBLANES - (CONV_WIDTH - 1), 0), (0, 0)))
    act, z, gb, tail = _odd_in_proj(xf, prm['mix_norm'][1], _prep_odd_w_in(prm['od_w_in'][0]), a_log, dt_bias,
                                    prm['od_conv_w'][0], conv_prev8, lq)
    o_c, s_delta = _delta(r3(act), r3(z), r3(gb), s0_delta, prm['od_norm'][0])
    conv_new = tail[:, V7X_SUBLANES - (CONV_WIDTH - 1):]
    xf = ffn(xf, 1, 1, [o_c.reshape(t, C_W)], [prm['od_w_out'][0].astype(_BF16)])

    y = xf.reshape(bsz, lq, d)
    k_out = r3(k_new).reshape(1, bsz, lq, A_HEADS, A_HEAD_DIM)
    v_out = r3(v_new).reshape(1, bsz, lq, A_HEADS, A_HEAD_DIM)
    return y, k_out, v_out, ki_new[None], s_gla[None], s_delta[None], conv_new[None]


def kernel(x_prompt, x_sample, cache_attn_k, cache_attn_v, cache_idx_k, state_gla, state_delta, state_conv,
           ffn_norm, ffn_w_gate, ffn_w_up, ffn_w_down, mix_norm, ev_w_in, ev_q_norm, ev_k_norm, rel_bias,
           ev_gate_w2, ev_gate_b2, ev_gla_norm, ev_w_out, od_w_in, od_conv_w, od_a_log, od_dt_bias, od_norm,
           od_w_out):
    prm = {'ffn_norm': ffn_norm, 'ffn_w_gate': ffn_w_gate, 'ffn_w_up': ffn_w_up, 'ffn_w_down': ffn_w_down,
           'mix_norm': mix_norm, 'ev_w_in': ev_w_in, 'ev_q_norm': ev_q_norm, 'ev_k_norm': ev_k_norm,
           'rel_bias': rel_bias, 'ev_gate_w2': ev_gate_w2, 'ev_gate_b2': ev_gate_b2, 'ev_gla_norm': ev_gla_norm,
           'ev_w_out': ev_w_out, 'od_w_in': od_w_in, 'od_conv_w': od_conv_w, 'od_a_log': od_a_log,
           'od_dt_bias': od_dt_bias, 'od_norm': od_norm, 'od_w_out': od_w_out}
    past = {'k': cache_attn_k, 'v': cache_attn_v, 'ki': cache_idx_k, 'gla': state_gla,
            'delta': state_delta, 'conv': state_conv}
    y_prompt, p_k, p_v, p_ki, p_gla, p_delta, p_conv = _trunk(x_prompt, None, prm)
    y_sample, s_k, s_v, s_ki, s_gla, s_delta, s_conv = _trunk(x_sample, past, prm)
    return (y_prompt, y_sample, p_k, p_v, p_ki, p_gla, p_delta, p_conv,
            s_k, s_v, s_ki, s_gla, s_delta, s_conv)
```

```python
import functools
import math

import jax
import jax.numpy as jnp
from jax import lax
from jax.experimental import pallas as pl
from jax.experimental.pallas import tpu as pltpu

CHUNK = 64
EPS = 1e-6

A_HEADS = 8
A_HEAD_DIM = 64
A_W = A_HEADS * A_HEAD_DIM
IDX_HEADS = 4
IDX_DIM = 64
TOPK_MAX = 256
N_BUCKETS = 32
MAX_DISTANCE = 128

B_HEADS = 4
B_KEY_DIM = 64
B_VAL_DIM = 128
B_KW = B_HEADS * B_KEY_DIM
B_VW = B_HEADS * B_VAL_DIM
B_GATE_RANK = 16
B_GATE_TAU = 16.0

C_HEADS = 8
C_HEAD_DIM = 128
C_W = C_HEADS * C_HEAD_DIM
CONV_WIDTH = 4
CONV_CH = 3 * C_W

V7X_LANES = 128
V7X_SUBLANES = 8
V7X_VMEM_LIMIT_BYTES = 56 * 1024 * 1024
V7X_MXU_DIM = 256

ROW_TILE = 2 * V7X_MXU_DIM
ODD_ROW_TILE = V7X_MXU_DIM
DSA_KEY_CHUNK = V7X_MXU_DIM
GLA_SUBCHUNKS = 8
DELTA_SUBCHUNKS = 8

_F32 = jnp.float32
_BF16 = jnp.bfloat16
_NEG = -1e30
_LOG2E = math.log2(math.e)
_NN = (((1,), (0,)), ((), ()))
_NT = (((1,), (1,)), ((), ()))
_TN = (((0,), (0,)), ((), ()))

_EV_Q, _EV_K, _EV_V = 0, 512, 1024
_EV_IDX = 1536
_EV_IDX_W = 384
_EV_WI_OFF = 256 + IDX_DIM
_EV_AB_OFF = _EV_WI_OFF + IDX_HEADS
_EV_QB, _EV_KB, _EV_VB, _EV_RB = 1920, 2176, 2432, 2944
_EV_COLS = 3456
_OD_QKV, _OD_Z, _OD_AB = 0, 3072, 4096
_OD_COLS = 4224


def _params(n_axes):
    return pltpu.CompilerParams(dimension_semantics=("arbitrary",) * n_axes,
                                vmem_limit_bytes=V7X_VMEM_LIMIT_BYTES)


def _rms(x, g):
    return x * lax.rsqrt(jnp.mean(x * x, axis=-1, keepdims=True) + EPS) * g


def _silu(x):
    return x * jax.nn.sigmoid(x)


def _dot(a, b, dims=_NN):
    return lax.dot_general(a.astype(_BF16), b.astype(_BF16), dims, preferred_element_type=_F32)


def _full(shape):
    return pl.BlockSpec(shape, lambda *_: (0,) * len(shape))


_FFN_COLS = V7X_MXU_DIM


def _ffn_kernel(*refs, n_proj):
    x_ref = refs[0]
    g_ref, wg_ref, wu_ref, wd_ref, o_ref, act_ref = refs[1 + 2 * n_proj:]
    x = x_ref[...]
    for i in range(n_proj):
        x = x + jnp.dot(refs[1 + i][...], refs[1 + n_proj + i][...], preferred_element_type=_F32)
    h = _rms(x, g_ref[...]).astype(_BF16)
    for c in range(wg_ref.shape[1] // _FFN_COLS):
        cols = slice(c * _FFN_COLS, (c + 1) * _FFN_COLS)
        gate = jnp.dot(h, wg_ref[:, cols], preferred_element_type=_F32)
        up = jnp.dot(h, wu_ref[:, cols], preferred_element_type=_F32)
        act_ref[:, cols] = (_silu(gate) * up).astype(_BF16)
    o_ref[...] = x + 0.5 * jnp.dot(act_ref[...], wd_ref[...], preferred_element_type=_F32)


def _ffn_half(x, g, wg, wu, wd, acts=(), proj_w=()):
    t, d = x.shape
    ff = wg.shape[1]
    tm = min(t, ROW_TILE)
    resident = lambda shape: pl.BlockSpec(shape, lambda i: (0, 0), pipeline_mode=pl.Buffered(1))
    return pl.pallas_call(
        functools.partial(_ffn_kernel, n_proj=len(acts)),
        grid=(t // tm,),
        in_specs=([pl.BlockSpec((tm, d), lambda i: (i, 0))]
                  + [pl.BlockSpec((tm, a.shape[1]), lambda i: (i, 0)) for a in acts]
                  + [resident(w.shape) for w in proj_w]
                  + [resident((1, d)), resident((d, ff)), resident((d, ff)), resident((ff, d))]),
        out_specs=pl.BlockSpec((tm, d), lambda i: (i, 0)),
        out_shape=jax.ShapeDtypeStruct((t, d), _F32),
        scratch_shapes=[pltpu.VMEM((tm, ff), _BF16)],
        compiler_params=_params(1),
        name="ffn_half",
    )(x, *acts, *proj_w, g.reshape(1, d), wg, wu, wd)


def _even_in_kernel(x_ref, g_ref, w_ref, qg_ref, kg_ref, hm_ref, w2_ref, b2_ref,
                    k_ref, v_ref, k16_ref, ki_ref, ki2_ref, qb_ref, kb_ref, vb_ref, rb_ref, la_ref,
                    qt_ref, vt_ref, qit_ref, wit_ref, *, seg):
    h = _rms(x_ref[...], g_ref[...]).astype(_BF16)

    def proj(lo, width):
        return jnp.dot(h, w_ref[:, lo:lo + width], preferred_element_type=_F32)

    def headnorm(t, gain):
        t2 = t * t
        hi = t2.astype(_BF16)
        lo = (t2 - hi.astype(_F32)).astype(_BF16)
        ms = (jnp.dot(hi, hm_ref[...], preferred_element_type=_F32)
              + jnp.dot(lo, hm_ref[...], preferred_element_type=_F32))
        return t * lax.rsqrt(ms + EPS) * gain

    qn = headnorm(proj(_EV_Q, A_W), qg_ref[...]).astype(_BF16)
    k = headnorm(proj(_EV_K, A_W), kg_ref[...])
    v = proj(_EV_V, A_W)
    k_ref[...] = k
    v_ref[...] = v
    k16_ref[...] = k.astype(_BF16)
    v16 = v.astype(_BF16)
    idx = proj(_EV_IDX, _EV_IDX_W)
    ki = idx[:, IDX_HEADS * IDX_DIM:IDX_HEADS * IDX_DIM + IDX_DIM]
    ki_ref[...] = ki
    ki2_ref[...] = jnp.concatenate([ki, ki], axis=1)
    for s in range(x_ref.shape[0] // seg):
        rows = slice(s * seg, (s + 1) * seg)
        qt_ref[s] = qn[rows].T
        vt_ref[s] = v16[rows].T
        qit_ref[s] = idx[rows, 0:IDX_HEADS * IDX_DIM].T
        wit_ref[s] = idx[rows, IDX_HEADS * IDX_DIM:_EV_IDX_W].T[IDX_DIM:IDX_DIM + V7X_SUBLANES]
    qb_ref[...] = proj(_EV_QB, B_KW) * (B_KEY_DIM ** -0.5)
    kb_ref[...] = proj(_EV_KB, B_KW)
    vb_ref[...] = proj(_EV_VB, B_VW)
    rb_ref[...] = proj(_EV_RB, B_VW)
    z = _dot(idx[:, 256:384], w2_ref[...]) + b2_ref[...]
    la_ref[...] = (jnp.minimum(z, 0.0) - jnp.log1p(jnp.exp(-jnp.abs(z)))) * (1.0 / B_GATE_TAU)


def _even_in_proj(x, g, w, q_gain, k_gain, head_mean, w2, b2, lq):
    t, d = x.shape
    tm = min(t, ROW_TILE)
    seg = min(tm, lq)
    tiles_per_batch = lq // seg
    row = lambda n: pl.BlockSpec((tm, n), lambda i: (i, 0))
    col = lambda n: pl.BlockSpec((tm // seg, n, seg), lambda i: (i // tiles_per_batch, 0, i % tiles_per_batch))
    outs = [(A_W, _F32), (A_W, _F32), (A_W, _BF16), (IDX_DIM, _F32), (2 * IDX_DIM, _F32), (B_KW, _F32), (B_KW, _F32),
            (B_VW, _F32), (B_VW, _F32), (B_KW, _F32)]
    outs_t = [(A_W, _BF16), (A_W, _BF16), (IDX_HEADS * IDX_DIM, _F32), (V7X_SUBLANES, _F32)]
    w_spec = pl.BlockSpec((d, _EV_COLS), lambda i: (0, 0), pipeline_mode=pl.Buffered(1))
    return pl.pallas_call(
        functools.partial(_even_in_kernel, seg=seg),
        grid=(t // tm,),
        in_specs=[row(d), _full((1, d)), w_spec, _full((1, A_W)), _full((1, A_W)),
                  _full((A_W, A_W)), _full((V7X_LANES, B_KW)), _full((1, B_KW))],
        out_specs=[row(n) for n, _ in outs] + [col(n) for n, _ in outs_t],
        out_shape=([jax.ShapeDtypeStruct((t, n), dt) for n, dt in outs]
                   + [jax.ShapeDtypeStruct((t // lq, n, lq), dt) for n, dt in outs_t]),
        compiler_params=_params(1),
        name="even_in_proj",
    )(x, g.reshape(1, d), w, q_gain, k_gain, head_mean, w2, b2)


def _dsa_kernel(rb_ref, qt_ref, qit_ref, wit_ref, k_ref, vt_ref, ki_ref, bkt_ref, o_ref,
                key_ref, mask_ref, nb_ref, ki3_ref, acc_ref, lt_ref, p_ref, *, past, tq, kc, topk, r0_off, chunk_counts):
    t = pl.program_id(1)

    @pl.when((pl.program_id(0) == 0) & (t == 0))
    def _():
        bk = bkt_ref[...]
        for h in range(A_HEADS):
            acc = jnp.zeros(bk.shape, _F32)
            for bb in range(N_BUCKETS):
                acc = jnp.where(bk == bb, rb_ref[bb, h], acc)
            nb_ref[h] = (acc - rb_ref[N_BUCKETS // 2 - 1, h]) * _LOG2E

    q0 = past + t * tq
    n_chunks = (q0 + tq + kc - 1) // kc
    q_limit = ((q0 + lax.broadcasted_iota(jnp.int32, (1, tq), 1)) // CHUNK + 1) * CHUNK
    k_iota = lax.broadcasted_iota(jnp.int32, (kc, tq), 0)

    def admissible(r0):
        return k_iota < (q_limit - r0)

    @pl.when(t == 0)
    def _():
        def split_body(j, carry):
            r0 = pl.multiple_of(j * kc, kc)
            x = ki_ref[0, pl.ds(r0, kc), :]
            hi = x.astype(_BF16)
            ki3_ref[pl.ds(r0, kc), 0:V7X_LANES] = hi
            ki3_ref[pl.ds(r0, kc), V7X_LANES:2 * V7X_LANES] = (x - hi.astype(_F32)).astype(_BF16)
            return carry
        lax.fori_loop(0, ki3_ref.shape[0] // kc, split_body, 0)

    qit = qit_ref[0]
    q_cols = []
    for h in range(IDX_HEADS):
        x = qit[h * IDX_DIM:(h + 1) * IDX_DIM, :]
        hi = x.astype(_BF16)
        lo = (x - hi.astype(_F32)).astype(_BF16)
        q_cols.append(jnp.concatenate([hi, lo, hi, jnp.zeros_like(hi)], axis=0))
    q3 = jnp.concatenate(q_cols, axis=1)
    wit = wit_ref[0]

    upper_rows = lax.broadcasted_iota(jnp.int32, (V7X_LANES, tq), 0) >= A_HEAD_DIM
    q_pairs = []
    for pr in range(A_HEADS // 2):
        blk = qt_ref[0, pr * V7X_LANES:(pr + 1) * V7X_LANES, :].astype(_F32)
        q_pairs.append(jnp.concatenate([jnp.where(upper_rows, 0.0, blk), jnp.where(upper_rows, blk, 0.0)],
                                       axis=1).astype(_BF16))

    def scores_and_logits(n):
        d_all = jnp.dot(ki3_ref[0:n * kc, :], q3, preferred_element_type=_F32)
        for j in range(n):
            d = d_all[j * kc:(j + 1) * kc]
            s = jnp.zeros((kc, tq), _F32)
            for h in range(IDX_HEADS):
                s = s + wit[h:h + 1, :] * jnp.maximum(d[:, h * tq:(h + 1) * tq], 0.0)
            s = jnp.where(s == 0.0, 0.0, s)
            s = jnp.where(admissible(j * kc), s, -jnp.inf)
            bits = pltpu.bitcast(s, jnp.int32)
            key_ref[j * kc:(j + 1) * kc, :] = bits ^ ((bits >> 31) & jnp.int32(0x7FFFFFFF))
        for pr in range(A_HEADS // 2):
            lt2 = jnp.dot(k_ref[0, 0:n * kc, pr * V7X_LANES:(pr + 1) * V7X_LANES], q_pairs[pr],
                          preferred_element_type=_F32)
            for hh in range(2):
                lt_ref[2 * pr + hh, 0:n * kc, :] = lt2[:, hh * tq:(hh + 1) * tq] * _LOG2E
        for j in range(max(n - 2, 0), n):
            off = pl.multiple_of(jnp.maximum(j * kc - q0 + r0_off, 0), CHUNK)
            for h in range(A_HEADS):
                lt_ref[h, j * kc:(j + 1) * kc, :] += nb_ref[h, pl.ds(off, kc), :]

    int_min = jnp.int32(-2 ** 31)

    def count_keys(pred, n):
        parts = []
        for j in range(n):
            m = jnp.where(pred(key_ref[j * kc:(j + 1) * kc, :]), 1, 0)
            parts.append(m.reshape(kc // V7X_SUBLANES, V7X_SUBLANES, tq).sum(axis=0))
        while len(parts) > 1:
            parts = [sum(parts[i:i + 2]) for i in range(0, len(parts), 2)]
        return parts[0].sum(axis=0, keepdims=True)

    def radix_select(n):
        def bit_body(i, carry):
            tu, n_ge = carry
            cu = tu | jnp.left_shift(jnp.int32(1), 31 - i)
            cs = cu ^ int_min
            cnt = count_keys(lambda blk: blk >= cs, n)
            keep = cnt >= topk
            return jnp.where(keep, cu, tu), jnp.where(keep, cnt, n_ge)

        start = (jnp.zeros((1, tq), jnp.int32), jnp.full((1, tq), n * kc, jnp.int32))
        if n * kc <= topk:
            return start
        return lax.fori_loop(0, 32, bit_body, start)

    def tile_select(n):
        scores_and_logits(n)
        return radix_select(n)

    tu, n_ge = lax.switch(n_chunks - chunk_counts[0], [functools.partial(tile_select, n) for n in chunk_counts])
    ts = tu ^ int_min

    tri = (lax.broadcasted_iota(jnp.int32, (kc, kc), 0)
           >= lax.broadcasted_iota(jnp.int32, (kc, kc), 1)).astype(_BF16)

    def tie_masks():
        def gt_body(j, acc):
            r0 = pl.multiple_of(j * kc, kc)
            m = jnp.where(key_ref[pl.ds(r0, kc), :] > ts, 1, 0)
            return acc + m.reshape(kc // V7X_SUBLANES, V7X_SUBLANES, tq).sum(axis=0)

        n_greater = lax.fori_loop(0, n_chunks, gt_body, jnp.zeros((V7X_SUBLANES, tq), jnp.int32))
        room = (topk - n_greater.sum(axis=0, keepdims=True)).astype(_F32)

        def body(j, seen):
            r0 = pl.multiple_of(j * kc, kc)
            blk = key_ref[pl.ds(r0, kc), :]
            eq = blk == ts
            rank = jnp.dot(tri, jnp.where(eq, 1.0, 0.0).astype(_BF16), preferred_element_type=_F32) + seen
            val = jnp.where(blk > ts, 0.0, jnp.where(eq, jnp.where(rank <= room, 0.0, _NEG), _NEG))
            mask_ref[pl.ds(r0, kc), :] = jnp.where(admissible(r0), val, _NEG)
            return rank[kc - 1:kc, :]

        return lax.fori_loop(0, n_chunks, body, jnp.zeros((1, tq), _F32))

    def plain_masks():
        def body(j, carry):
            r0 = pl.multiple_of(j * kc, kc)
            val = jnp.where(key_ref[pl.ds(r0, kc), :] >= ts, 0.0, _NEG)
            mask_ref[pl.ds(r0, kc), :] = jnp.where(admissible(r0), val, _NEG)
            return carry

        return lax.fori_loop(0, n_chunks, body, jnp.zeros((1, tq), _F32))

    lax.cond(jnp.max(n_ge) > topk, tie_masks, plain_masks)

    acc_ref[...] = jnp.zeros_like(acc_ref)

    def attn_body(j, carry):
        ms, ls = carry
        r0 = pl.multiple_of(j * kc, kc)
        msk = mask_ref[pl.ds(r0, kc), :]
        new_ms, new_ls, alphas = [], [], []
        for h in range(A_HEADS):
            lt = lt_ref[h, pl.ds(r0, kc), :] + msk
            m_new = jnp.maximum(ms[h], lt.max(axis=0, keepdims=True))
            p = jnp.exp2(lt - m_new)
            alpha = jnp.exp2(ms[h] - m_new)
            new_ms.append(m_new)
            new_ls.append(alpha * ls[h] + p.sum(axis=0, keepdims=True))
            alphas.append(alpha)
            p_ref[h] = p.astype(_BF16)
        for h in range(A_HEADS):
            rows = slice(h * A_HEAD_DIM, (h + 1) * A_HEAD_DIM)
            pv = jnp.dot(vt_ref[0, rows, pl.ds(r0, kc)], p_ref[h], preferred_element_type=_F32)
            acc_ref[rows, :] = alphas[h] * acc_ref[rows, :] + pv
        return tuple(new_ms), tuple(new_ls)

    init = (tuple(jnp.full((1, tq), _NEG, _F32) for _ in range(A_HEADS)),
            tuple(jnp.zeros((1, tq), _F32) for _ in range(A_HEADS)))

    def attend(n):
        carry = init
        for j in range(n):
            carry = attn_body(j, carry)
        return carry

    _, ls = lax.switch(n_chunks - chunk_counts[0], [functools.partial(attend, n) for n in chunk_counts])

    for pair in range(A_HEADS // 2):
        lanes = slice(pair * V7X_LANES, (pair + 1) * V7X_LANES)
        o_t = jnp.concatenate(
            [acc_ref[(2 * pair + hh) * A_HEAD_DIM:(2 * pair + hh + 1) * A_HEAD_DIM, :] / ls[2 * pair + hh]
             for hh in range(2)], axis=0)
        o_ref[0, :, lanes] = o_t.T.astype(_BF16)


def _t5_bucket_table(rel):
    half = N_BUCKETS // 2
    max_exact = half // 2
    n = jnp.abs(rel)
    nf = jnp.maximum(n, 1).astype(jnp.float32)
    large = max_exact + (jnp.log(nf / max_exact) / math.log(MAX_DISTANCE / max_exact)
                         * (half - max_exact)).astype(jnp.int32)
    large = jnp.minimum(large, half - 1)
    return jnp.where(rel > 0, half, 0) + jnp.where(n < max_exact, n, large)


def _cache_keys_kernel(pk_ref, pv_ref, pki_ref, k16_ref, vt_ref, ki2_ref, ko_ref, vto_ref, kio_ref, *, past_tiles):
    j = pl.program_id(1)

    @pl.when(j < past_tiles)
    def _():
        ko_ref[0] = pk_ref[0].astype(_BF16)
        vto_ref[0] = pv_ref[0].astype(_BF16).T
        ki = pki_ref[0]
        kio_ref[0] = jnp.concatenate([ki, ki], axis=1)

    @pl.when(j == past_tiles)
    def _():
        n_new = k16_ref.shape[1]
        ko_ref[0] = jnp.zeros(ko_ref.shape[1:], _BF16)
        ko_ref[0, 0:n_new, :] = k16_ref[0]
        vto_ref[0] = jnp.zeros(vto_ref.shape[1:], _BF16)
        vto_ref[0, :, 0:n_new] = vt_ref[0]
        kio_ref[0] = jnp.zeros(kio_ref.shape[1:], _F32)
        kio_ref[0, 0:n_new, :] = ki2_ref[0]


def _cache_keys(past_k, past_v, past_ki, k16_new, vt_new, ki2_new):
    bsz, p_len, _ = past_k.shape
    lq = k16_new.shape[1]
    kc = DSA_KEY_CHUNK
    assert p_len % kc == 0 and lq <= kc
    past_tiles = p_len // kc
    nk_pad = p_len + kc
    old = lambda n: pl.BlockSpec((1, kc, n), lambda b, j: (b, jnp.minimum(j, past_tiles - 1), 0))
    return pl.pallas_call(
        functools.partial(_cache_keys_kernel, past_tiles=past_tiles),
        grid=(bsz, past_tiles + 1),
        in_specs=[old(A_W), old(A_W), old(IDX_DIM),
                  pl.BlockSpec((1, lq, A_W), lambda b, j: (b, 0, 0)),
                  pl.BlockSpec((1, A_W, lq), lambda b, j: (b, 0, 0)),
                  pl.BlockSpec((1, lq, 2 * IDX_DIM), lambda b, j: (b, 0, 0))],
        out_specs=[pl.BlockSpec((1, kc, A_W), lambda b, j: (b, j, 0)),
                   pl.BlockSpec((1, A_W, kc), lambda b, j: (b, 0, j)),
                   pl.BlockSpec((1, kc, 2 * IDX_DIM), lambda b, j: (b, j, 0))],
        out_shape=[jax.ShapeDtypeStruct((bsz, nk_pad, A_W), _BF16),
                   jax.ShapeDtypeStruct((bsz, A_W, nk_pad), _BF16),
                   jax.ShapeDtypeStruct((bsz, nk_pad, 2 * IDX_DIM), _F32)],
        compiler_params=_params(2),
        name="cache_keys",
    )(past_k, past_v, past_ki, k16_new, vt_new, ki2_new)


def _dsa_attention(qt, qit, wit, k_all, vt_all, ki2_all, rel_bias, past, nk):
    bsz, _, lq = qt.shape
    topk = min(TOPK_MAX, nk // 4)
    tq = min(V7X_LANES, lq)
    kc = DSA_KEY_CHUNK
    nk_pad = k_all.shape[1]
    assert nk_pad % kc == 0 and nk_pad - kc < nk <= nk_pad
    k16, vt16, ki2 = k_all, vt_all, ki2_all
    r0_off = kc + MAX_DISTANCE
    nbr = r0_off + kc + tq
    rel = (jnp.arange(nbr, dtype=jnp.int32)[:, None] - r0_off) - jnp.arange(tq, dtype=jnp.int32)[None, :]
    bkt = _t5_bucket_table(rel)
    chunk_counts = sorted({(past + (t + 1) * tq + kc - 1) // kc for t in range(lq // tq)})
    assert chunk_counts == list(range(chunk_counts[0], chunk_counts[-1] + 1))
    kern = functools.partial(_dsa_kernel, past=past, tq=tq, kc=kc, topk=topk, r0_off=r0_off,
                             chunk_counts=tuple(chunk_counts))
    return pl.pallas_call(
        kern,
        grid=(bsz, lq // tq),
        in_specs=[pl.BlockSpec(memory_space=pltpu.SMEM),
                  pl.BlockSpec((1, A_W, tq), lambda b, t: (b, 0, t)),
                  pl.BlockSpec((1, IDX_HEADS * IDX_DIM, tq), lambda b, t: (b, 0, t)),
                  pl.BlockSpec((1, V7X_SUBLANES, tq), lambda b, t: (b, 0, t)),
                  pl.BlockSpec((1, nk_pad, A_W), lambda b, t: (b, 0, 0)),
                  pl.BlockSpec((1, A_W, nk_pad), lambda b, t: (b, 0, 0)),
                  pl.BlockSpec((1, nk_pad, V7X_LANES), lambda b, t: (b, 0, 0)),
                  pl.BlockSpec((nbr, tq), lambda b, t: (0, 0))],
        out_specs=pl.BlockSpec((1, tq, A_W), lambda b, t: (b, t, 0)),
        out_shape=jax.ShapeDtypeStruct((bsz, lq, A_W), _BF16),
        scratch_shapes=[pltpu.VMEM((nk_pad, tq), jnp.int32), pltpu.VMEM((nk_pad, tq), _F32),
                        pltpu.VMEM((A_HEADS, nbr, tq), _F32),
                        pltpu.VMEM((nk_pad, 2 * V7X_LANES), _BF16), pltpu.VMEM((A_W, tq), _F32),
                        pltpu.VMEM((A_HEADS, nk_pad, tq), _F32), pltpu.VMEM((A_HEADS, kc, tq), _BF16)],
        compiler_params=_params(2),
        name="dsa_attention",
    )(rel_bias, qt, qit, wit, k16, vt16, ki2, bkt)


def _split3_rows(x):
    hi = x.astype(_BF16)
    r1 = x - hi.astype(_F32)
    mid = r1.astype(_BF16)
    lo = (r1 - mid.astype(_F32)).astype(_BF16)
    return jnp.concatenate([hi, mid, lo], axis=0)


def _gla_kernel(q_ref, k_ref, g_ref, v_ref, r_ref, s0_ref, ng_ref, o_ref, s_out_ref, st_ref, *, n_sub):
    c = pl.program_id(1)
    causal = (lax.broadcasted_iota(jnp.int32, (CHUNK, CHUNK), 0)
              >= lax.broadcasted_iota(jnp.int32, (CHUNK, CHUNK), 1))
    causal3 = jnp.concatenate([causal.astype(_BF16)] * 3, axis=1)
    upper_half = lax.broadcasted_iota(jnp.int32, (CHUNK, V7X_LANES), 1) >= B_KEY_DIM
    diag = (lax.broadcasted_iota(jnp.int32, (V7X_LANES, V7X_LANES), 0)
            == lax.broadcasted_iota(jnp.int32, (V7X_LANES, V7X_LANES), 1))
    pairs = range(B_HEADS // 2)
    heads = range(B_HEADS)

    @pl.when(c == 0)
    def _():
        st_ref[...] = s0_ref[0]

    def pair_lanes(x, p):
        return x[:, p * V7X_LANES:(p + 1) * V7X_LANES]

    def head_half(x, h):
        keep = upper_half if h % 2 == 1 else jnp.logical_not(upper_half)
        return jnp.where(keep, pair_lanes(x, h // 2), 0.0)

    parts = []
    for s in range(n_sub):
        rows = slice(s * CHUNK, (s + 1) * CHUNK)
        b = jnp.dot(causal3, _split3_rows(g_ref[0, rows, :]), preferred_element_type=_F32)
        mid = CHUNK // 2
        b_mid = b[mid:mid + 1, :]
        b_last = b[CHUNK - 1:CHUNK, :]
        q = q_ref[0, rows, :]
        k = k_ref[0, rows, :]
        qe = q * jnp.exp(b)
        qm = q * jnp.exp(b - b_mid)
        km = k * jnp.exp(b_mid - b)
        kl = k * jnp.exp(b_last - b)
        vs = [v_ref[0, rows, h * B_VAL_DIM:(h + 1) * B_VAL_DIM] for h in heads]
        a = [jnp.where(causal, _dot(head_half(qm, h), pair_lanes(km, h // 2), _NT), 0.0) for h in heads]
        av = [_dot(a[h], vs[h]) for h in heads]
        upd = [_dot(head_half(kl, h), vs[h], _TN) for h in heads]
        dcol = [jnp.sum(jnp.where(diag, jnp.exp(pair_lanes(b_last, p)), 0.0), axis=1, keepdims=True)
                for p in pairs]
        parts.append((qe, av, upd, dcol))

    st = [st_ref[p * V7X_LANES:(p + 1) * V7X_LANES, :] for p in pairs]
    for s in range(n_sub):
        rows = slice(s * CHUNK, (s + 1) * CHUNK)
        qe, av, upd, dcol = parts[s]
        o = [_dot(head_half(qe, h), st[h // 2]) + av[h] for h in heads]
        st = [dcol[p] * st[p] + upd[2 * p] + upd[2 * p + 1] for p in pairs]
        for h in heads:
            cols = slice(h * B_VAL_DIM, (h + 1) * B_VAL_DIM)
            o_ref[0, rows, cols] = (_rms(o[h], ng_ref[...]) * _silu(r_ref[0, rows, cols])).astype(_BF16)
    for p in pairs:
        st_ref[p * V7X_LANES:(p + 1) * V7X_LANES, :] = st[p]

    @pl.when(c == pl.num_programs(1) - 1)
    def _():
        s_out_ref[0] = st_ref[...]


def _gla(qb, kb, la, vb, rb, s0, norm_g):
    bsz, lq, _ = qb.shape
    n_sub = min(GLA_SUBCHUNKS, lq // CHUNK)
    tok = lambda n: pl.BlockSpec((1, n_sub * CHUNK, n), lambda b, c: (b, c, 0))
    state = pl.BlockSpec((1, B_KW, B_VAL_DIM), lambda b, c: (b, 0, 0))
    o, s_new = pl.pallas_call(
        functools.partial(_gla_kernel, n_sub=n_sub),
        grid=(bsz, lq // (n_sub * CHUNK)),
        in_specs=[tok(B_KW), tok(B_KW), tok(B_KW), tok(B_VW), tok(B_VW), state,
                  pl.BlockSpec((1, B_VAL_DIM), lambda b, c: (0, 0))],
        out_specs=[tok(B_VW), state],
        out_shape=[jax.ShapeDtypeStruct((bsz, lq, B_VW), _BF16),
                   jax.ShapeDtypeStruct((bsz, B_KW, B_VAL_DIM), _F32)],
        scratch_shapes=[pltpu.VMEM((B_KW, B_VAL_DIM), _F32)],
        compiler_params=_params(2),
        name="gla",
    )(qb, kb, la, vb, rb, s0.reshape(bsz, B_KW, B_VAL_DIM), norm_g.reshape(1, B_VAL_DIM))
    return o, s_new.reshape(s0.shape)


def _odd_in_kernel(x_ref, g_ref, w_ref, alog_ref, dt_ref, cw_ref, cprev_ref, act_ref, z_ref, gb_ref, tail_ref,
                   xbuf_ref, *, seg, tiles_per_batch):
    halo = V7X_SUBLANES
    h = _rms(x_ref[...], g_ref[...]).astype(_BF16)
    z_ref[...] = jnp.dot(h, w_ref[:, _OD_Z:_OD_Z + C_W], preferred_element_type=_F32)
    ab = jnp.dot(h, w_ref[:, _OD_AB:_OD_AB + V7X_LANES], preferred_element_type=_F32)
    xa = ab + dt_ref[...]
    g = -jnp.exp(alog_ref[...]) * (jnp.maximum(xa, 0.0) + jnp.log1p(jnp.exp(-jnp.abs(xa))))
    lane = lax.broadcasted_iota(jnp.int32, ab.shape, 1)
    gb_ref[...] = jnp.where(lane < C_HEADS, g, jax.nn.sigmoid(ab))

    for s in range(x_ref.shape[0] // seg):
        rows = slice(s * seg, (s + 1) * seg)
        if tiles_per_batch > 1:
            @pl.when(pl.program_id(0) % tiles_per_batch == 0)
            def _():
                xbuf_ref[0:halo, :] = cprev_ref[0]
        else:
            xbuf_ref[0:halo, :] = cprev_ref[s]
        xbuf_ref[halo:halo + seg, :] = jnp.dot(h[rows], w_ref[:, _OD_QKV:_OD_QKV + CONV_CH],
                                               preferred_element_type=_F32)
        conv = xbuf_ref[halo:halo + seg, :] * cw_ref[CONV_WIDTH - 1:CONV_WIDTH, :]
        for j in range(CONV_WIDTH - 1):
            sh = CONV_WIDTH - 1 - j
            conv = conv + xbuf_ref[halo - sh:halo - sh + seg, :] * cw_ref[j:j + 1, :]
        tail = xbuf_ref[seg:seg + halo, :]
        xbuf_ref[0:halo, :] = tail
        tail_ref[s] = tail
        act = _silu(conv)
        for hd in range(C_HEADS):
            for part, scale in ((0, C_HEAD_DIM ** -0.5), (1, 1.0)):
                cols = slice(part * C_W + hd * C_HEAD_DIM, part * C_W + (hd + 1) * C_HEAD_DIM)
                a = act[:, cols]
                act_ref[rows, cols] = a * (lax.rsqrt(jnp.sum(a * a, axis=-1, keepdims=True) + EPS) * scale)
        act_ref[rows, 2 * C_W:] = act[:, 2 * C_W:]


def _odd_in_proj(x, g, w, a_log, dt_bias, conv_w, conv_prev8, lq):
    t, d = x.shape
    tm = min(t, ODD_ROW_TILE)
    seg = min(tm, lq)
    batches_per_tile = tm // seg
    tiles_per_batch = lq // seg
    row = lambda n: pl.BlockSpec((tm, n), lambda i: (i, 0))
    per_batch = pl.BlockSpec((batches_per_tile, V7X_SUBLANES, CONV_CH), lambda i: (i // tiles_per_batch, 0, 0))
    outs = [CONV_CH, C_W, V7X_LANES]
    return pl.pallas_call(
        functools.partial(_odd_in_kernel, seg=seg, tiles_per_batch=tiles_per_batch),
        grid=(t // tm,),
        in_specs=[row(d), _full((1, d)), _full((d, _OD_COLS)), _full((1, V7X_LANES)), _full((1, V7X_LANES)),
                  _full((CONV_WIDTH, CONV_CH)), per_batch],
        out_specs=[row(n) for n in outs] + [per_batch],
        out_shape=([jax.ShapeDtypeStruct((t, n), _F32) for n in outs]
                   + [jax.ShapeDtypeStruct((t // lq, V7X_SUBLANES, CONV_CH), _F32)]),
        scratch_shapes=[pltpu.VMEM((seg + V7X_SUBLANES, CONV_CH), _F32)],
        compiler_params=_params(1),
        name="odd_in_proj",
    )(x, g.reshape(1, d), w, a_log, dt_bias, conv_w, conv_prev8)


def _delta_kernel(act_ref, z_ref, gb_ref, s0_ref, ng_ref, o_ref, s_out_ref, st_ref, *, n_sub):
    c = pl.program_id(1)

    @pl.when(c == 0)
    def _():
        st_ref[...] = s0_ref[0]

    t_i = lax.broadcasted_iota(jnp.int32, (CHUNK, V7X_LANES), 0)
    s_i = lax.broadcasted_iota(jnp.int32, (CHUNK, V7X_LANES), 1)
    lower_half = s_i < CHUNK
    s_mod = jnp.where(lower_half, s_i, s_i - CHUNK)
    incl = t_i >= s_mod
    strict = t_i > s_mod
    eye = (t_i == s_mod).astype(_F32)
    incl3 = jnp.concatenate([(lax.broadcasted_iota(jnp.int32, (CHUNK, CHUNK), 0)
                              >= lax.broadcasted_iota(jnp.int32, (CHUNK, CHUNK), 1)).astype(_BF16)] * 3, axis=1)

    def split3(x):
        hi16 = x.astype(_BF16)
        hi = hi16.astype(_F32)
        lo = x - hi
        lhs = jnp.concatenate([jnp.where(lower_half, hi, lo).astype(_BF16), hi16], axis=1)
        lo16 = lo.astype(_BF16)
        rhs = jnp.concatenate([hi16, hi16, lo16, jnp.zeros_like(lo16)], axis=0)
        return lhs, rhs

    def mm(lhs, rhs):
        return jnp.dot(lhs, rhs, preferred_element_type=_F32)

    heads = range(C_HEADS)
    units = [(s, h) for s in range(n_sub) for h in heads]
    g_lasts = []
    qs, ks, kbs, decays, rhss, qgs, kdecs = [], [], [], [], [], [], []
    for s in range(n_sub):
        rows = slice(s * CHUNK, (s + 1) * CHUNK)
        gb = gb_ref[0, rows, :]
        gcum_all = jnp.dot(incl3, _split3_rows(gb), preferred_element_type=_F32)
        gcum_rows = jnp.concatenate([gcum_all, gcum_all], axis=0).T
        g_last = gcum_all[CHUNK - 1:CHUNK, :]
        g_lasts.append(g_last)
        for h in heads:
            q = act_ref[0, rows, h * C_HEAD_DIM:(h + 1) * C_HEAD_DIM]
            k = act_ref[0, rows, C_W + h * C_HEAD_DIM:C_W + (h + 1) * C_HEAD_DIM]
            v = act_ref[0, rows, 2 * C_W + h * C_HEAD_DIM:2 * C_W + (h + 1) * C_HEAD_DIM]
            beta = gb[:, C_HEADS + h:C_HEADS + h + 1]
            gcum = gcum_all[:, h:h + 1]
            kb = k * beta
            qs.append(q)
            ks.append(k)
            kbs.append(kb)
            decays.append(jnp.exp(jnp.where(incl, gcum - gcum_rows[h:h + 1, :], _NEG)))
            rhss.append(jnp.concatenate([v * beta, kb * jnp.exp(gcum)], axis=-1))
            qgs.append(q * jnp.exp(gcum))
            kdecs.append(k * jnp.exp(g_last[:, h:h + 1] - gcum))

    n_units = range(len(units))
    lowers, attns = [], []
    for u in n_units:
        prod = _dot(jnp.concatenate([kbs[u], qs[u]], axis=0),
                    jnp.concatenate([ks[u], ks[u]], axis=0).T)
        lowers.append(jnp.where(strict, prod[:CHUNK] * decays[u], 0.0))
        attns.append(jnp.where(incl, prod[CHUNK:] * decays[u], 0.0))

    pws = [-lowers[u] for u in n_units]
    invs = [eye + pws[u] for u in n_units]
    pw_l, pw_r = zip(*[split3(pws[u]) for u in n_units])
    pws = [mm(pw_l[u], pw_r[u]) for u in n_units]
    for _ in range(4):
        pw_l, pw_r = zip(*[split3(pws[u]) for u in n_units])
        both = [mm(jnp.concatenate([split3(invs[u])[0], pw_l[u]], axis=0), pw_r[u]) for u in n_units]
        invs = [invs[u] + both[u][:CHUNK] for u in n_units]
        pws = [both[u][CHUNK:] for u in n_units]
    invs = [invs[u] + mm(split3(invs[u])[0], split3(pws[u])[1]) for u in n_units]
    sols = []
    for u in n_units:
        r_hi16 = rhss[u].astype(_BF16)
        r_lo16 = (rhss[u] - r_hi16.astype(_F32)).astype(_BF16)
        sols.append(mm(split3(invs[u])[0], jnp.concatenate([r_hi16, r_hi16, r_lo16, jnp.zeros_like(r_lo16)], axis=0)))

    st = [st_ref[h] for h in heads]
    for u, (s, h) in enumerate(units):
        rows = slice(s * CHUNK, (s + 1) * CHUNK)
        w_u = sols[u][:, :C_HEAD_DIM]
        w_w = sols[u][:, C_HEAD_DIM:]
        ws = _dot(jnp.concatenate([w_w, qgs[u]], axis=0), st[h])
        v_new = w_u - ws[:CHUNK]
        o = ws[CHUNK:] + _dot(attns[u][:, :CHUNK], v_new)
        st[h] = jnp.exp(g_lasts[s][:, h:h + 1]) * st[h] + _dot(kdecs[u], v_new, _TN)
        cols = slice(h * C_HEAD_DIM, (h + 1) * C_HEAD_DIM)
        o_ref[0, rows, cols] = (_rms(o, ng_ref[...]) * _silu(z_ref[0, rows, cols])).astype(_BF16)
    for h in heads:
        st_ref[h] = st[h]

    @pl.when(c == pl.num_programs(1) - 1)
    def _():
        s_out_ref[0] = st_ref[...]


def _delta(act, z, gb, s0, norm_g):
    bsz, lq, _ = act.shape
    n_sub = min(DELTA_SUBCHUNKS, lq // CHUNK)
    rows_all = n_sub * CHUNK
    tok = lambda n: pl.BlockSpec((1, rows_all, n), lambda b, c: (b, c, 0))
    state = pl.BlockSpec((1, C_HEADS, C_HEAD_DIM, C_HEAD_DIM), lambda b, c: (b, 0, 0, 0))
    return pl.pallas_call(
        functools.partial(_delta_kernel, n_sub=n_sub),
        grid=(bsz, lq // rows_all),
        in_specs=[tok(CONV_CH), tok(C_W), tok(V7X_LANES), state,
                  pl.BlockSpec((1, C_HEAD_DIM), lambda b, c: (0, 0))],
        out_specs=[tok(C_W), state],
        out_shape=[jax.ShapeDtypeStruct((bsz, lq, C_W), _BF16),
                   jax.ShapeDtypeStruct(s0.shape, _F32)],
        scratch_shapes=[pltpu.VMEM((C_HEADS, C_HEAD_DIM, C_HEAD_DIM), _F32)],
        compiler_params=_params(2),
        name="gated_delta",
    )(act, z, gb, s0, norm_g.reshape(1, C_HEAD_DIM))


def _prep_even_w_in(w):
    sizes = (A_W, A_W, A_W, IDX_HEADS * IDX_DIM, IDX_DIM, IDX_HEADS, B_KW, B_KW, B_VW, B_VW, B_GATE_RANK)
    offs = [0]
    for s in sizes:
        offs.append(offs[-1] + s)
    part = lambda i: w[:, offs[i]:offs[i + 1]]
    q, k, v, qi, ki, wi, qb, kb, vb, rb, ab = [part(i) for i in range(len(sizes))]
    pad = jnp.zeros((w.shape[0], _EV_IDX_W - (256 + IDX_DIM + IDX_HEADS + B_GATE_RANK)), w.dtype)
    return jnp.concatenate([q, k, v, qi, ki, wi, ab, pad, qb, kb, vb, rb], axis=1).astype(_BF16)


def _prep_odd_w_in(w):
    pad = jnp.zeros((w.shape[0], _OD_COLS - w.shape[1]), w.dtype)
    return jnp.concatenate([w, pad], axis=1).astype(_BF16)


def _lane_pad(v, n=V7X_LANES):
    return jnp.pad(v, (0, n - v.shape[0])).reshape(1, n)


def _trunk(x, past, prm):
    bsz, lq, d = x.shape
    t = bsz * lq
    xf = x.reshape(t, d)

    def ffn(xf, i, j, acts=(), proj_w=()):
        return _ffn_half(xf, prm['ffn_norm'][i, j], prm['ffn_w_gate'][i, j].astype(_BF16),
                         prm['ffn_w_up'][i, j].astype(_BF16), prm['ffn_w_down'][i, j].astype(_BF16),
                         acts, proj_w)

    xf = ffn(xf, 0, 0)
    head_mean = jnp.kron(jnp.eye(A_HEADS, dtype=_F32),
                         jnp.full((A_HEAD_DIM, A_HEAD_DIM), 1.0 / A_HEAD_DIM, _F32)).astype(_BF16)
    q_gain = (jnp.tile(prm['ev_q_norm'][0], A_HEADS) * (A_HEAD_DIM ** -0.5)).reshape(1, A_W)
    k_gain = jnp.tile(prm['ev_k_norm'][0], A_HEADS).reshape(1, A_W)
    w2 = jnp.zeros((V7X_LANES, B_KW), _F32).at[
        _EV_AB_OFF - 256:_EV_AB_OFF - 256 + B_GATE_RANK].set(prm['ev_gate_w2'][0]).astype(_BF16)
    k_new, v_new, k16, ki, ki2, qb, kb, vb, rb, la, qt, vt, qit, wit = _even_in_proj(
        xf, prm['mix_norm'][0], _prep_even_w_in(prm['ev_w_in'][0]), q_gain, k_gain, head_mean,
        w2, prm['ev_gate_b2'][0].reshape(1, B_KW), lq)
    r3 = lambda a: a.reshape(bsz, lq, a.shape[-1])
    ki_new = r3(ki)
    if past is None:
        past_len = 0
        k_all, vt_all, ki2_all = r3(k16), vt, r3(ki2)
        s0_gla = jnp.zeros((bsz, B_HEADS, B_KEY_DIM, B_VAL_DIM), _F32)
    else:
        past_len = past['k'].shape[2]
        k_all, vt_all, ki2_all = _cache_keys(past['k'][0].reshape(bsz, past_len, A_W),
                                             past['v'][0].reshape(bsz, past_len, A_W), past['ki'][0],
                                             r3(k16), vt, r3(ki2))
        s0_gla = past['gla'][0]
    o_a = _dsa_attention(qt, qit, wit, k_all, vt_all, ki2_all, prm['rel_bias'], past_len, past_len + lq)
    o_b, s_gla = _gla(r3(qb), r3(kb), r3(la), r3(vb), r3(rb), s0_gla, prm['ev_gla_norm'][0])
    w_out = prm['ev_w_out'][0].astype(_BF16)
    xf = ffn(xf, 0, 1, [o_a.reshape(t, A_W), o_b.reshape(t, B_VW)], [w_out[:A_W], w_out[A_W:]])

    xf = ffn(xf, 1, 0)
    a_log = _lane_pad(prm['od_a_log'][0])
    dt_bias = _lane_pad(prm['od_dt_bias'][0])
    if past is None:
        conv_prev = jnp.zeros((bsz, CONV_WIDTH - 1, CONV_CH), _F32)
        s0_delta = jnp.zeros((bsz, C_HEADS, C_HEAD_DIM, C_HEAD_DIM), _F32)
    else:
        conv_prev, s0_delta = past['conv'][0], past['delta'][0]
    conv_prev8 = jnp.pad(conv_prev, ((0, 0), (V7X_SUBLANES - (CONV_WIDTH - 1), 0), (0, 0)))
    act, z, gb, tail = _odd_in_proj(xf, prm['mix_norm'][1], _prep_odd_w_in(prm['od_w_in'][0]), a_log, dt_bias,
                                    prm['od_conv_w'][0], conv_prev8, lq)
    o_c, s_delta = _delta(r3(act), r3(z), r3(gb), s0_delta, prm['od_norm'][0])
    conv_new = tail[:, V7X_SUBLANES - (CONV_WIDTH - 1):]
    xf = ffn(xf, 1, 1, [o_c.reshape(t, C_W)], [prm['od_w_out'][0].astype(_BF16)])

    y = xf.reshape(bsz, lq, d)
    k_out = r3(k_new).reshape(1, bsz, lq, A_HEADS, A_HEAD_DIM)
    v_out = r3(v_new).reshape(1, bsz, lq, A_HEADS, A_HEAD_DIM)
    return y, k_out, v_out, ki_new[None], s_gla[None], s_delta[None], conv_new[None]


def kernel(x_prompt, x_sample, cache_attn_k, cache_attn_v, cache_idx_k, state_gla, state_delta, state_conv,
           ffn_norm, ffn_w_gate, ffn_w_up, ffn_w_down, mix_norm, ev_w_in, ev_q_norm, ev_k_norm, rel_bias,
           ev_gate_w2, ev_gate_b2, ev_gla_norm, ev_w_out, od_w_in, od_conv_w, od_a_log, od_dt_bias, od_norm,
           od_w_out):
    prm = {'ffn_norm': ffn_norm, 'ffn_w_gate': ffn_w_gate, 'ffn_w_up': ffn_w_up, 'ffn_w_down': ffn_w_down,
           'mix_norm': mix_norm, 'ev_w_in': ev_w_in, 'ev_q_norm': ev_q_norm, 'ev_k_norm': ev_k_norm,
           'rel_bias': rel_bias, 'ev_gate_w2': ev_gate_w2, 'ev_gate_b2': ev_gate_b2, 'ev_gla_norm': ev_gla_norm,
           'ev_w_out': ev_w_out, 'od_w_in': od_w_in, 'od_conv_w': od_conv_w, 'od_a_log': od_a_log,
           'od_dt_bias': od_dt_bias, 'od_norm': od_norm, 'od_w_out': od_w_out}
    past = {'k': cache_attn_k, 'v': cache_attn_v, 'ki': cache_idx_k, 'gla': state_gla,
            'delta': state_delta, 'conv': state_conv}
    y_prompt, p_k, p_v, p_ki, p_gla, p_delta, p_conv = _trunk(x_prompt, None, prm)
    y_sample, s_k, s_v, s_ki, s_gla, s_delta, s_conv = _trunk(x_sample, past, prm)
    return (y_prompt, y_sample, p_k, p_v, p_ki, p_gla, p_delta, p_conv,
            s_k, s_v, s_ki, s_gla, s_delta, s_conv)
```

```python
import functools
import math

import jax
import jax.numpy as jnp
from jax import lax
from jax.experimental import pallas as pl
from jax.experimental.pallas import tpu as pltpu

CHUNK = 64
EPS = 1e-6

A_HEADS = 8
A_HEAD_DIM = 64
A_W = A_HEADS * A_HEAD_DIM
IDX_HEADS = 4
IDX_DIM = 64
TOPK_MAX = 256
N_BUCKETS = 32
MAX_DISTANCE = 128

B_HEADS = 4
B_KEY_DIM = 64
B_VAL_DIM = 128
B_KW = B_HEADS * B_KEY_DIM
B_VW = B_HEADS * B_VAL_DIM
B_GATE_RANK = 16
B_GATE_TAU = 16.0

C_HEADS = 8
C_HEAD_DIM = 128
C_W = C_HEADS * C_HEAD_DIM
CONV_WIDTH = 4
CONV_CH = 3 * C_W

V7X_LANES = 128
V7X_SUBLANES = 8
V7X_VMEM_LIMIT_BYTES = 56 * 1024 * 1024
V7X_MXU_DIM = 256

ROW_TILE = 2 * V7X_MXU_DIM
ODD_ROW_TILE = V7X_MXU_DIM
DSA_KEY_CHUNK = V7X_MXU_DIM
GLA_SUBCHUNKS = 8
DELTA_SUBCHUNKS = 8

_F32 = jnp.float32
_BF16 = jnp.bfloat16
_NEG = -1e30
_LOG2E = math.log2(math.e)
_NN = (((1,), (0,)), ((), ()))
_NT = (((1,), (1,)), ((), ()))
_TN = (((0,), (0,)), ((), ()))

_EV_Q, _EV_K, _EV_V = 0, 512, 1024
_EV_IDX = 1536
_EV_IDX_W = 384
_EV_WI_OFF = 256 + IDX_DIM
_EV_AB_OFF = _EV_WI_OFF + IDX_HEADS
_EV_QB, _EV_KB, _EV_VB, _EV_RB = 1920, 2176, 2432, 2944
_EV_COLS = 3456
_OD_QKV, _OD_Z, _OD_AB = 0, 3072, 4096
_OD_COLS = 4224


def _params(n_axes):
    return pltpu.CompilerParams(dimension_semantics=("arbitrary",) * n_axes,
                                vmem_limit_bytes=V7X_VMEM_LIMIT_BYTES)


def _rms(x, g):
    return x * lax.rsqrt(jnp.mean(x * x, axis=-1, keepdims=True) + EPS) * g


def _silu(x):
    return x * jax.nn.sigmoid(x)


def _dot(a, b, dims=_NN):
    return lax.dot_general(a.astype(_BF16), b.astype(_BF16), dims, preferred_element_type=_F32)


def _full(shape):
    return pl.BlockSpec(shape, lambda *_: (0,) * len(shape))


_FFN_COLS = V7X_MXU_DIM


_FFN_GU_SLOTS = 4
_FFN_DOWN_CHUNKS = 4
_FFN_DOWN_SLOTS = 2


def _ffn_kernel(*refs, n_proj, layer, half):
    x_ref = refs[0]
    (g_ref, wg_hbm, wu_hbm, wd_hbm, o_ref, act_ref, wg_ref, wu_ref, wd_ref,
     stage_gu, stage_d, sem_gu, sem_d) = refs[1 + 2 * n_proj:]
    first = pl.program_id(0) == 0
    n_gu = wg_ref.shape[1] // _FFN_COLS
    d_rows = wd_ref.shape[0] // _FFN_DOWN_CHUNKS

    def gu_copy(c, which):
        src = (wg_hbm, wu_hbm)[which]
        return pltpu.make_async_copy(src.at[layer, half, :, pl.ds(c * _FFN_COLS, _FFN_COLS)],
                                     stage_gu.at[c % _FFN_GU_SLOTS, which], sem_gu.at[c % _FFN_GU_SLOTS, which])

    def d_copy(r):
        return pltpu.make_async_copy(wd_hbm.at[layer, half, pl.ds(r * d_rows, d_rows), :],
                                     stage_d.at[r % _FFN_DOWN_SLOTS], sem_d.at[r % _FFN_DOWN_SLOTS])

    @pl.when(first)
    def _():
        for c in range(_FFN_GU_SLOTS):
            gu_copy(c, 0).start()
            gu_copy(c, 1).start()
        for r in range(_FFN_DOWN_SLOTS):
            d_copy(r).start()

    x = x_ref[...]
    for i in range(n_proj):
        x = x + jnp.dot(refs[1 + i][...], refs[1 + n_proj + i][...], preferred_element_type=_F32)
    h = _rms(x, g_ref[...]).astype(_BF16)
    for c in range(n_gu):
        cols = slice(c * _FFN_COLS, (c + 1) * _FFN_COLS)

        @pl.when(first)
        def _(c=c, cols=cols):
            gu_copy(c, 0).wait()
            gu_copy(c, 1).wait()
            wg_ref[:, cols] = stage_gu[c % _FFN_GU_SLOTS, 0].astype(_BF16)
            wu_ref[:, cols] = stage_gu[c % _FFN_GU_SLOTS, 1].astype(_BF16)
            if c + _FFN_GU_SLOTS < n_gu:
                gu_copy(c + _FFN_GU_SLOTS, 0).start()
                gu_copy(c + _FFN_GU_SLOTS, 1).start()

        gate = jnp.dot(h, wg_ref[:, cols], preferred_element_type=_F32)
        up = jnp.dot(h, wu_ref[:, cols], preferred_element_type=_F32)
        act_ref[:, cols] = (_silu(gate) * up).astype(_BF16)

    @pl.when(first)
    def _():
        for r in range(_FFN_DOWN_CHUNKS):
            d_copy(r).wait()
            wd_ref[r * d_rows:(r + 1) * d_rows, :] = stage_d[r % _FFN_DOWN_SLOTS].astype(_BF16)
            if r + _FFN_DOWN_SLOTS < _FFN_DOWN_CHUNKS:
                d_copy(r + _FFN_DOWN_SLOTS).start()

    o_ref[...] = x + 0.5 * jnp.dot(act_ref[...], wd_ref[...], preferred_element_type=_F32)


def _ffn_half(x, g, wg, wu, wd, layer, half, acts=(), proj_w=()):
    t, d = x.shape
    ff = wg.shape[-1]
    tm = min(t, ROW_TILE)
    resident = lambda shape: pl.BlockSpec(shape, lambda i: (0, 0), pipeline_mode=pl.Buffered(1))
    hbm = pl.BlockSpec(memory_space=pl.ANY)
    return pl.pallas_call(
        functools.partial(_ffn_kernel, n_proj=len(acts), layer=layer, half=half),
        grid=(t // tm,),
        in_specs=([pl.BlockSpec((tm, d), lambda i: (i, 0))]
                  + [pl.BlockSpec((tm, a.shape[1]), lambda i: (i, 0)) for a in acts]
                  + [resident(w.shape) for w in proj_w]
                  + [resident((1, d)), hbm, hbm, hbm]),
        out_specs=pl.BlockSpec((tm, d), lambda i: (i, 0)),
        out_shape=jax.ShapeDtypeStruct((t, d), _F32),
        scratch_shapes=[pltpu.VMEM((tm, ff), _BF16),
                        pltpu.VMEM((d, ff), _BF16), pltpu.VMEM((d, ff), _BF16), pltpu.VMEM((ff, d), _BF16),
                        pltpu.VMEM((_FFN_GU_SLOTS, 2, d, _FFN_COLS), _F32),
                        pltpu.VMEM((_FFN_DOWN_SLOTS, ff // _FFN_DOWN_CHUNKS, d), _F32),
                        pltpu.SemaphoreType.DMA((_FFN_GU_SLOTS, 2)),
                        pltpu.SemaphoreType.DMA((_FFN_DOWN_SLOTS,))],
        compiler_params=_params(1),
        name="ffn_half",
    )(x, *acts, *proj_w, g.reshape(1, d), wg, wu, wd)


def _even_in_kernel(x_ref, g_ref, w_ref, qg_ref, kg_ref, hm_ref, w2_ref, b2_ref,
                    k_ref, v_ref, k16_ref, ki_ref, ki2_ref, qb_ref, kb_ref, vb_ref, rb_ref, la_ref,
                    qt_ref, vt_ref, qit_ref, wit_ref, *, seg):
    h = _rms(x_ref[...], g_ref[...]).astype(_BF16)

    def proj(lo, width):
        return jnp.dot(h, w_ref[:, lo:lo + width], preferred_element_type=_F32)

    def headnorm(t, gain):
        t2 = t * t
        hi = t2.astype(_BF16)
        lo = (t2 - hi.astype(_F32)).astype(_BF16)
        ms = (jnp.dot(hi, hm_ref[...], preferred_element_type=_F32)
              + jnp.dot(lo, hm_ref[...], preferred_element_type=_F32))
        return t * lax.rsqrt(ms + EPS) * gain

    qn = headnorm(proj(_EV_Q, A_W), qg_ref[...]).astype(_BF16)
    k = headnorm(proj(_EV_K, A_W), kg_ref[...])
    v = proj(_EV_V, A_W)
    k_ref[...] = k
    v_ref[...] = v
    k16_ref[...] = k.astype(_BF16)
    v16 = v.astype(_BF16)
    idx = proj(_EV_IDX, _EV_IDX_W)
    ki = idx[:, IDX_HEADS * IDX_DIM:IDX_HEADS * IDX_DIM + IDX_DIM]
    ki_ref[...] = ki
    ki2_ref[...] = jnp.concatenate([ki, ki], axis=1)
    for s in range(x_ref.shape[0] // seg):
        rows = slice(s * seg, (s + 1) * seg)
        qt_ref[s] = qn[rows].T
        vt_ref[s] = v16[rows].T
        qit_ref[s] = idx[rows, 0:IDX_HEADS * IDX_DIM].T
        wit_ref[s] = idx[rows, IDX_HEADS * IDX_DIM:_EV_IDX_W].T[IDX_DIM:IDX_DIM + V7X_SUBLANES]
    qb_ref[...] = proj(_EV_QB, B_KW) * (B_KEY_DIM ** -0.5)
    kb_ref[...] = proj(_EV_KB, B_KW)
    vb_ref[...] = proj(_EV_VB, B_VW)
    rb_ref[...] = proj(_EV_RB, B_VW)
    z = _dot(idx[:, 256:384], w2_ref[...]) + b2_ref[...]
    la_ref[...] = (jnp.minimum(z, 0.0) - jnp.log1p(jnp.exp(-jnp.abs(z)))) * (1.0 / B_GATE_TAU)


def _even_in_proj(x, g, w, q_gain, k_gain, head_mean, w2, b2, lq):
    t, d = x.shape
    tm = min(t, ROW_TILE)
    seg = min(tm, lq)
    tiles_per_batch = lq // seg
    row = lambda n: pl.BlockSpec((tm, n), lambda i: (i, 0))
    col = lambda n: pl.BlockSpec((tm // seg, n, seg), lambda i: (i // tiles_per_batch, 0, i % tiles_per_batch))
    outs = [(A_W, _F32), (A_W, _F32), (A_W, _BF16), (IDX_DIM, _F32), (2 * IDX_DIM, _F32), (B_KW, _F32), (B_KW, _F32),
            (B_VW, _F32), (B_VW, _F32), (B_KW, _F32)]
    outs_t = [(A_W, _BF16), (A_W, _BF16), (IDX_HEADS * IDX_DIM, _F32), (V7X_SUBLANES, _F32)]
    w_spec = pl.BlockSpec((d, _EV_COLS), lambda i: (0, 0), pipeline_mode=pl.Buffered(1))
    return pl.pallas_call(
        functools.partial(_even_in_kernel, seg=seg),
        grid=(t // tm,),
        in_specs=[row(d), _full((1, d)), w_spec, _full((1, A_W)), _full((1, A_W)),
                  _full((A_W, A_W)), _full((V7X_LANES, B_KW)), _full((1, B_KW))],
        out_specs=[row(n) for n, _ in outs] + [col(n) for n, _ in outs_t],
        out_shape=([jax.ShapeDtypeStruct((t, n), dt) for n, dt in outs]
                   + [jax.ShapeDtypeStruct((t // lq, n, lq), dt) for n, dt in outs_t]),
        compiler_params=_params(1),
        name="even_in_proj",
    )(x, g.reshape(1, d), w, q_gain, k_gain, head_mean, w2, b2)


def _dsa_kernel(rb_ref, qt_ref, qit_ref, wit_ref, k_ref, vt_ref, ki_ref, bkt_ref, o_ref,
                key_ref, mask_ref, nb_ref, ki3_ref, acc_ref, lt_ref, p_ref, *, past, tq, kc, topk, r0_off, chunk_counts):
    t = pl.program_id(1)

    @pl.when((pl.program_id(0) == 0) & (t == 0))
    def _():
        bk = bkt_ref[...]
        for h in range(A_HEADS):
            acc = jnp.zeros(bk.shape, _F32)
            for bb in range(N_BUCKETS):
                acc = jnp.where(bk == bb, rb_ref[bb, h], acc)
            nb_ref[h] = (acc - rb_ref[N_BUCKETS // 2 - 1, h]) * _LOG2E

    q0 = past + t * tq
    n_chunks = (q0 + tq + kc - 1) // kc
    q_limit = ((q0 + lax.broadcasted_iota(jnp.int32, (1, tq), 1)) // CHUNK + 1) * CHUNK
    k_iota = lax.broadcasted_iota(jnp.int32, (kc, tq), 0)

    def admissible(r0):
        return k_iota < (q_limit - r0)

    @pl.when(t == 0)
    def _():
        def split_body(j, carry):
            r0 = pl.multiple_of(j * kc, kc)
            x = ki_ref[0, pl.ds(r0, kc), :]
            hi = x.astype(_BF16)
            ki3_ref[pl.ds(r0, kc), 0:V7X_LANES] = hi
            ki3_ref[pl.ds(r0, kc), V7X_LANES:2 * V7X_LANES] = (x - hi.astype(_F32)).astype(_BF16)
            return carry
        lax.fori_loop(0, ki3_ref.shape[0] // kc, split_body, 0)

    qit = qit_ref[0]
    q_cols = []
    for h in range(IDX_HEADS):
        x = qit[h * IDX_DIM:(h + 1) * IDX_DIM, :]
        hi = x.astype(_BF16)
        lo = (x - hi.astype(_F32)).astype(_BF16)
        q_cols.append(jnp.concatenate([hi, lo, hi, jnp.zeros_like(hi)], axis=0))
    q3 = jnp.concatenate(q_cols, axis=1)
    wit = wit_ref[0]

    upper_rows = lax.broadcasted_iota(jnp.int32, (V7X_LANES, tq), 0) >= A_HEAD_DIM
    q_pairs = []
    for pr in range(A_HEADS // 2):
        blk = qt_ref[0, pr * V7X_LANES:(pr + 1) * V7X_LANES, :].astype(_F32)
        q_pairs.append(jnp.concatenate([jnp.where(upper_rows, 0.0, blk), jnp.where(upper_rows, blk, 0.0)],
                                       axis=1).astype(_BF16))

    def scores_and_logits(n):
        d_all = jnp.dot(ki3_ref[0:n * kc, :], q3, preferred_element_type=_F32)
        for j in range(n):
            d = d_all[j * kc:(j + 1) * kc]
            s = jnp.zeros((kc, tq), _F32)
            for h in range(IDX_HEADS):
                s = s + wit[h:h + 1, :] * jnp.maximum(d[:, h * tq:(h + 1) * tq], 0.0)
            s = jnp.where(s == 0.0, 0.0, s)
            s = jnp.where(admissible(j * kc), s, -jnp.inf)
            bits = pltpu.bitcast(s, jnp.int32)
            key_ref[j * kc:(j + 1) * kc, :] = bits ^ ((bits >> 31) & jnp.int32(0x7FFFFFFF))
        for pr in range(A_HEADS // 2):
            lt2 = jnp.dot(k_ref[0, 0:n * kc, pr * V7X_LANES:(pr + 1) * V7X_LANES], q_pairs[pr],
                          preferred_element_type=_F32)
            for hh in range(2):
                lt_ref[2 * pr + hh, 0:n * kc, :] = lt2[:, hh * tq:(hh + 1) * tq] * _LOG2E
        for j in range(max(n - 2, 0), n):
            off = pl.multiple_of(jnp.maximum(j * kc - q0 + r0_off, 0), CHUNK)
            for h in range(A_HEADS):
                lt_ref[h, j * kc:(j + 1) * kc, :] += nb_ref[h, pl.ds(off, kc), :]

    int_min = jnp.int32(-2 ** 31)

    def count_keys(pred, n):
        parts = []
        for j in range(n):
            m = jnp.where(pred(key_ref[j * kc:(j + 1) * kc, :]), 1, 0)
            parts.append(m.reshape(kc // V7X_SUBLANES, V7X_SUBLANES, tq).sum(axis=0))
        while len(parts) > 1:
            parts = [sum(parts[i:i + 2]) for i in range(0, len(parts), 2)]
        return parts[0].sum(axis=0, keepdims=True)

    def radix_select(n):
        def bit_body(i, carry):
            tu, n_ge = carry
            cu = tu | jnp.left_shift(jnp.int32(1), 31 - i)
            cs = cu ^ int_min
            cnt = count_keys(lambda blk: blk >= cs, n)
            keep = cnt >= topk
            return jnp.where(keep, cu, tu), jnp.where(keep, cnt, n_ge)

        start = (jnp.zeros((1, tq), jnp.int32), jnp.full((1, tq), n * kc, jnp.int32))
        if n * kc <= topk:
            return start
        return lax.fori_loop(0, 32, bit_body, start)

    def tile_select(n):
        scores_and_logits(n)
        return radix_select(n)

    tu, n_ge = lax.switch(n_chunks - chunk_counts[0], [functools.partial(tile_select, n) for n in chunk_counts])
    ts = tu ^ int_min

    tri = (lax.broadcasted_iota(jnp.int32, (kc, kc), 0)
           >= lax.broadcasted_iota(jnp.int32, (kc, kc), 1)).astype(_BF16)

    def tie_masks():
        def gt_body(j, acc):
            r0 = pl.multiple_of(j * kc, kc)
            m = jnp.where(key_ref[pl.ds(r0, kc), :] > ts, 1, 0)
            return acc + m.reshape(kc // V7X_SUBLANES, V7X_SUBLANES, tq).sum(axis=0)

        n_greater = lax.fori_loop(0, n_chunks, gt_body, jnp.zeros((V7X_SUBLANES, tq), jnp.int32))
        room = (topk - n_greater.sum(axis=0, keepdims=True)).astype(_F32)

        def body(j, seen):
            r0 = pl.multiple_of(j * kc, kc)
            blk = key_ref[pl.ds(r0, kc), :]
            eq = blk == ts
            rank = jnp.dot(tri, jnp.where(eq, 1.0, 0.0).astype(_BF16), preferred_element_type=_F32) + seen
            val = jnp.where(blk > ts, 0.0, jnp.where(eq, jnp.where(rank <= room, 0.0, _NEG), _NEG))
            mask_ref[pl.ds(r0, kc), :] = jnp.where(admissible(r0), val, _NEG)
            return rank[kc - 1:kc, :]

        return lax.fori_loop(0, n_chunks, body, jnp.zeros((1, tq), _F32))

    def plain_masks():
        def body(j, carry):
            r0 = pl.multiple_of(j * kc, kc)
            val = jnp.where(key_ref[pl.ds(r0, kc), :] >= ts, 0.0, _NEG)
            mask_ref[pl.ds(r0, kc), :] = jnp.where(admissible(r0), val, _NEG)
            return carry

        return lax.fori_loop(0, n_chunks, body, jnp.zeros((1, tq), _F32))

    lax.cond(jnp.max(n_ge) > topk, tie_masks, plain_masks)

    acc_ref[...] = jnp.zeros_like(acc_ref)

    def attn_body(j, carry):
        ms, ls = carry
        r0 = pl.multiple_of(j * kc, kc)
        msk = mask_ref[pl.ds(r0, kc), :]
        new_ms, new_ls, alphas = [], [], []
        for h in range(A_HEADS):
            lt = lt_ref[h, pl.ds(r0, kc), :] + msk
            m_new = jnp.maximum(ms[h], lt.max(axis=0, keepdims=True))
            p = jnp.exp2(lt - m_new)
            alpha = jnp.exp2(ms[h] - m_new)
            new_ms.append(m_new)
            new_ls.append(alpha * ls[h] + p.sum(axis=0, keepdims=True))
            alphas.append(alpha)
            p_ref[h] = p.astype(_BF16)
        for h in range(A_HEADS):
            rows = slice(h * A_HEAD_DIM, (h + 1) * A_HEAD_DIM)
            pv = jnp.dot(vt_ref[0, rows, pl.ds(r0, kc)], p_ref[h], preferred_element_type=_F32)
            acc_ref[rows, :] = alphas[h] * acc_ref[rows, :] + pv
        return tuple(new_ms), tuple(new_ls)

    init = (tuple(jnp.full((1, tq), _NEG, _F32) for _ in range(A_HEADS)),
            tuple(jnp.zeros((1, tq), _F32) for _ in range(A_HEADS)))

    def attend(n):
        carry = init
        for j in range(n):
            carry = attn_body(j, carry)
        return carry

    _, ls = lax.switch(n_chunks - chunk_counts[0], [functools.partial(attend, n) for n in chunk_counts])

    for pair in range(A_HEADS // 2):
        lanes = slice(pair * V7X_LANES, (pair + 1) * V7X_LANES)
        o_t = jnp.concatenate(
            [acc_ref[(2 * pair + hh) * A_HEAD_DIM:(2 * pair + hh + 1) * A_HEAD_DIM, :] / ls[2 * pair + hh]
             for hh in range(2)], axis=0)
        o_ref[0, :, lanes] = o_t.T.astype(_BF16)


def _t5_bucket_table(rel):
    half = N_BUCKETS // 2
    max_exact = half // 2
    n = jnp.abs(rel)
    nf = jnp.maximum(n, 1).astype(jnp.float32)
    large = max_exact + (jnp.log(nf / max_exact) / math.log(MAX_DISTANCE / max_exact)
                         * (half - max_exact)).astype(jnp.int32)
    large = jnp.minimum(large, half - 1)
    return jnp.where(rel > 0, half, 0) + jnp.where(n < max_exact, n, large)


def _dsa_attention(qt, qit, wit, k_all, vt_all, ki2_all, rel_bias, past):
    bsz, _, lq = qt.shape
    nk = k_all.shape[1]
    topk = min(TOPK_MAX, nk // 4)
    tq = min(V7X_LANES, lq)
    kc = DSA_KEY_CHUNK
    nk_pad = -(-nk // kc) * kc
    pad = ((0, 0), (0, nk_pad - nk), (0, 0))
    k16 = jnp.pad(k_all, pad)
    vt16 = jnp.pad(vt_all, ((0, 0), (0, 0), (0, nk_pad - nk)))
    ki2 = jnp.pad(ki2_all, pad)
    r0_off = kc + MAX_DISTANCE
    nbr = r0_off + kc + tq
    rel = (jnp.arange(nbr, dtype=jnp.int32)[:, None] - r0_off) - jnp.arange(tq, dtype=jnp.int32)[None, :]
    bkt = _t5_bucket_table(rel)
    chunk_counts = sorted({(past + (t + 1) * tq + kc - 1) // kc for t in range(lq // tq)})
    assert chunk_counts == list(range(chunk_counts[0], chunk_counts[-1] + 1))
    kern = functools.partial(_dsa_kernel, past=past, tq=tq, kc=kc, topk=topk, r0_off=r0_off,
                             chunk_counts=tuple(chunk_counts))
    return pl.pallas_call(
        kern,
        grid=(bsz, lq // tq),
        in_specs=[pl.BlockSpec(memory_space=pltpu.SMEM),
                  pl.BlockSpec((1, A_W, tq), lambda b, t: (b, 0, t)),
                  pl.BlockSpec((1, IDX_HEADS * IDX_DIM, tq), lambda b, t: (b, 0, t)),
                  pl.BlockSpec((1, V7X_SUBLANES, tq), lambda b, t: (b, 0, t)),
                  pl.BlockSpec((1, nk_pad, A_W), lambda b, t: (b, 0, 0)),
                  pl.BlockSpec((1, A_W, nk_pad), lambda b, t: (b, 0, 0)),
                  pl.BlockSpec((1, nk_pad, V7X_LANES), lambda b, t: (b, 0, 0)),
                  pl.BlockSpec((nbr, tq), lambda b, t: (0, 0))],
        out_specs=pl.BlockSpec((1, tq, A_W), lambda b, t: (b, t, 0)),
        out_shape=jax.ShapeDtypeStruct((bsz, lq, A_W), _BF16),
        scratch_shapes=[pltpu.VMEM((nk_pad, tq), jnp.int32), pltpu.VMEM((nk_pad, tq), _F32),
                        pltpu.VMEM((A_HEADS, nbr, tq), _F32),
                        pltpu.VMEM((nk_pad, 2 * V7X_LANES), _BF16), pltpu.VMEM((A_W, tq), _F32),
                        pltpu.VMEM((A_HEADS, nk_pad, tq), _F32), pltpu.VMEM((A_HEADS, kc, tq), _BF16)],
        compiler_params=_params(2),
        name="dsa_attention",
    )(rel_bias, qt, qit, wit, k16, vt16, ki2, bkt)


def _split3_rows(x):
    hi = x.astype(_BF16)
    r1 = x - hi.astype(_F32)
    mid = r1.astype(_BF16)
    lo = (r1 - mid.astype(_F32)).astype(_BF16)
    return jnp.concatenate([hi, mid, lo], axis=0)


def _gla_kernel(q_ref, k_ref, g_ref, v_ref, r_ref, s0_ref, ng_ref, o_ref, s_out_ref, st_ref, *, n_sub):
    c = pl.program_id(1)
    causal = (lax.broadcasted_iota(jnp.int32, (CHUNK, CHUNK), 0)
              >= lax.broadcasted_iota(jnp.int32, (CHUNK, CHUNK), 1))
    causal3 = jnp.concatenate([causal.astype(_BF16)] * 3, axis=1)
    upper_half = lax.broadcasted_iota(jnp.int32, (CHUNK, V7X_LANES), 1) >= B_KEY_DIM
    diag = (lax.broadcasted_iota(jnp.int32, (V7X_LANES, V7X_LANES), 0)
            == lax.broadcasted_iota(jnp.int32, (V7X_LANES, V7X_LANES), 1))
    pairs = range(B_HEADS // 2)
    heads = range(B_HEADS)

    @pl.when(c == 0)
    def _():
        st_ref[...] = s0_ref[0]

    def pair_lanes(x, p):
        return x[:, p * V7X_LANES:(p + 1) * V7X_LANES]

    def head_half(x, h):
        keep = upper_half if h % 2 == 1 else jnp.logical_not(upper_half)
        return jnp.where(keep, pair_lanes(x, h // 2), 0.0)

    parts = []
    for s in range(n_sub):
        rows = slice(s * CHUNK, (s + 1) * CHUNK)
        b = jnp.dot(causal3, _split3_rows(g_ref[0, rows, :]), preferred_element_type=_F32)
        mid = CHUNK // 2
        b_mid = b[mid:mid + 1, :]
        b_last = b[CHUNK - 1:CHUNK, :]
        q = q_ref[0, rows, :]
        k = k_ref[0, rows, :]
        qe = q * jnp.exp(b)
        qm = q * jnp.exp(b - b_mid)
        km = k * jnp.exp(b_mid - b)
        kl = k * jnp.exp(b_last - b)
        vs = [v_ref[0, rows, h * B_VAL_DIM:(h + 1) * B_VAL_DIM] for h in heads]
        a = [jnp.where(causal, _dot(head_half(qm, h), pair_lanes(km, h // 2), _NT), 0.0) for h in heads]
        av = [_dot(a[h], vs[h]) for h in heads]
        upd = [_dot(head_half(kl, h), vs[h], _TN) for h in heads]
        dcol = [jnp.sum(jnp.where(diag, jnp.exp(pair_lanes(b_last, p)), 0.0), axis=1, keepdims=True)
                for p in pairs]
        parts.append((qe, av, upd, dcol))

    st = [st_ref[p * V7X_LANES:(p + 1) * V7X_LANES, :] for p in pairs]
    for s in range(n_sub):
        rows = slice(s * CHUNK, (s + 1) * CHUNK)
        qe, av, upd, dcol = parts[s]
        o = [_dot(head_half(qe, h), st[h // 2]) + av[h] for h in heads]
        st = [dcol[p] * st[p] + upd[2 * p] + upd[2 * p + 1] for p in pairs]
        for h in heads:
            cols = slice(h * B_VAL_DIM, (h + 1) * B_VAL_DIM)
            o_ref[0, rows, cols] = (_rms(o[h], ng_ref[...]) * _silu(r_ref[0, rows, cols])).astype(_BF16)
    for p in pairs:
        st_ref[p * V7X_LANES:(p + 1) * V7X_LANES, :] = st[p]

    @pl.when(c == pl.num_programs(1) - 1)
    def _():
        s_out_ref[0] = st_ref[...]


def _gla(qb, kb, la, vb, rb, s0, norm_g):
    bsz, lq, _ = qb.shape
    n_sub = min(GLA_SUBCHUNKS, lq // CHUNK)
    tok = lambda n: pl.BlockSpec((1, n_sub * CHUNK, n), lambda b, c: (b, c, 0))
    state = pl.BlockSpec((1, B_KW, B_VAL_DIM), lambda b, c: (b, 0, 0))
    o, s_new = pl.pallas_call(
        functools.partial(_gla_kernel, n_sub=n_sub),
        grid=(bsz, lq // (n_sub * CHUNK)),
        in_specs=[tok(B_KW), tok(B_KW), tok(B_KW), tok(B_VW), tok(B_VW), state,
                  pl.BlockSpec((1, B_VAL_DIM), lambda b, c: (0, 0))],
        out_specs=[tok(B_VW), state],
        out_shape=[jax.ShapeDtypeStruct((bsz, lq, B_VW), _BF16),
                   jax.ShapeDtypeStruct((bsz, B_KW, B_VAL_DIM), _F32)],
        scratch_shapes=[pltpu.VMEM((B_KW, B_VAL_DIM), _F32)],
        compiler_params=_params(2),
        name="gla",
    )(qb, kb, la, vb, rb, s0.reshape(bsz, B_KW, B_VAL_DIM), norm_g.reshape(1, B_VAL_DIM))
    return o, s_new.reshape(s0.shape)


def _odd_in_kernel(x_ref, g_ref, w_ref, alog_ref, dt_ref, cw_ref, cprev_ref, act_ref, z_ref, gb_ref, tail_ref,
                   xbuf_ref, *, seg, tiles_per_batch):
    halo = V7X_SUBLANES
    h = _rms(x_ref[...], g_ref[...]).astype(_BF16)
    z_ref[...] = jnp.dot(h, w_ref[:, _OD_Z:_OD_Z + C_W], preferred_element_type=_F32)
    ab = jnp.dot(h, w_ref[:, _OD_AB:_OD_AB + V7X_LANES], preferred_element_type=_F32)
    xa = ab + dt_ref[...]
    g = -jnp.exp(alog_ref[...]) * (jnp.maximum(xa, 0.0) + jnp.log1p(jnp.exp(-jnp.abs(xa))))
    lane = lax.broadcasted_iota(jnp.int32, ab.shape, 1)
    gb_ref[...] = jnp.where(lane < C_HEADS, g, jax.nn.sigmoid(ab))

    for s in range(x_ref.shape[0] // seg):
        rows = slice(s * seg, (s + 1) * seg)
        if tiles_per_batch > 1:
            @pl.when(pl.program_id(0) % tiles_per_batch == 0)
            def _():
                xbuf_ref[0:halo, :] = cprev_ref[0]
        else:
            xbuf_ref[0:halo, :] = cprev_ref[s]
        xbuf_ref[halo:halo + seg, :] = jnp.dot(h[rows], w_ref[:, _OD_QKV:_OD_QKV + CONV_CH],
                                               preferred_element_type=_F32)
        conv = xbuf_ref[halo:halo + seg, :] * cw_ref[CONV_WIDTH - 1:CONV_WIDTH, :]
        for j in range(CONV_WIDTH - 1):
            sh = CONV_WIDTH - 1 - j
            conv = conv + xbuf_ref[halo - sh:halo - sh + seg, :] * cw_ref[j:j + 1, :]
        tail = xbuf_ref[seg:seg + halo, :]
        xbuf_ref[0:halo, :] = tail
        tail_ref[s] = tail
        act = _silu(conv)
        for hd in range(C_HEADS):
            for part, scale in ((0, C_HEAD_DIM ** -0.5), (1, 1.0)):
                cols = slice(part * C_W + hd * C_HEAD_DIM, part * C_W + (hd + 1) * C_HEAD_DIM)
                a = act[:, cols]
                act_ref[rows, cols] = a * (lax.rsqrt(jnp.sum(a * a, axis=-1, keepdims=True) + EPS) * scale)
        act_ref[rows, 2 * C_W:] = act[:, 2 * C_W:]


def _odd_in_proj(x, g, w, a_log, dt_bias, conv_w, conv_prev8, lq):
    t, d = x.shape
    tm = min(t, ODD_ROW_TILE)
    seg = min(tm, lq)
    batches_per_tile = tm // seg
    tiles_per_batch = lq // seg
    row = lambda n: pl.BlockSpec((tm, n), lambda i: (i, 0))
    per_batch = pl.BlockSpec((batches_per_tile, V7X_SUBLANES, CONV_CH), lambda i: (i // tiles_per_batch, 0, 0))
    outs = [CONV_CH, C_W, V7X_LANES]
    return pl.pallas_call(
        functools.partial(_odd_in_kernel, seg=seg, tiles_per_batch=tiles_per_batch),
        grid=(t // tm,),
        in_specs=[row(d), _full((1, d)), _full((d, _OD_COLS)), _full((1, V7X_LANES)), _full((1, V7X_LANES)),
                  _full((CONV_WIDTH, CONV_CH)), per_batch],
        out_specs=[row(n) for n in outs] + [per_batch],
        out_shape=([jax.ShapeDtypeStruct((t, n), _F32) for n in outs]
                   + [jax.ShapeDtypeStruct((t // lq, V7X_SUBLANES, CONV_CH), _F32)]),
        scratch_shapes=[pltpu.VMEM((seg + V7X_SUBLANES, CONV_CH), _F32)],
        compiler_params=_params(1),
        name="odd_in_proj",
    )(x, g.reshape(1, d), w, a_log, dt_bias, conv_w, conv_prev8)


def _delta_kernel(act_ref, z_ref, gb_ref, s0_ref, ng_ref, o_ref, s_out_ref, st_ref, *, n_sub):
    c = pl.program_id(1)

    @pl.when(c == 0)
    def _():
        st_ref[...] = s0_ref[0]

    t_i = lax.broadcasted_iota(jnp.int32, (CHUNK, V7X_LANES), 0)
    s_i = lax.broadcasted_iota(jnp.int32, (CHUNK, V7X_LANES), 1)
    lower_half = s_i < CHUNK
    s_mod = jnp.where(lower_half, s_i, s_i - CHUNK)
    incl = t_i >= s_mod
    strict = t_i > s_mod
    eye = (t_i == s_mod).astype(_F32)
    incl3 = jnp.concatenate([(lax.broadcasted_iota(jnp.int32, (CHUNK, CHUNK), 0)
                              >= lax.broadcasted_iota(jnp.int32, (CHUNK, CHUNK), 1)).astype(_BF16)] * 3, axis=1)

    def split3(x):
        hi16 = x.astype(_BF16)
        hi = hi16.astype(_F32)
        lo = x - hi
        lhs = jnp.concatenate([jnp.where(lower_half, hi, lo).astype(_BF16), hi16], axis=1)
        lo16 = lo.astype(_BF16)
        rhs = jnp.concatenate([hi16, hi16, lo16, jnp.zeros_like(lo16)], axis=0)
        return lhs, rhs

    def mm(lhs, rhs):
        return jnp.dot(lhs, rhs, preferred_element_type=_F32)

    heads = range(C_HEADS)
    units = [(s, h) for s in range(n_sub) for h in heads]
    g_lasts = []
    qs, ks, kbs, decays, rhss, qgs, kdecs = [], [], [], [], [], [], []
    for s in range(n_sub):
        rows = slice(s * CHUNK, (s + 1) * CHUNK)
        gb = gb_ref[0, rows, :]
        gcum_all = jnp.dot(incl3, _split3_rows(gb), preferred_element_type=_F32)
        gcum_rows = jnp.concatenate([gcum_all, gcum_all], axis=0).T
        g_last = gcum_all[CHUNK - 1:CHUNK, :]
        g_lasts.append(g_last)
        for h in heads:
            q = act_ref[0, rows, h * C_HEAD_DIM:(h + 1) * C_HEAD_DIM]
            k = act_ref[0, rows, C_W + h * C_HEAD_DIM:C_W + (h + 1) * C_HEAD_DIM]
            v = act_ref[0, rows, 2 * C_W + h * C_HEAD_DIM:2 * C_W + (h + 1) * C_HEAD_DIM]
            beta = gb[:, C_HEADS + h:C_HEADS + h + 1]
            gcum = gcum_all[:, h:h + 1]
            kb = k * beta
            qs.append(q)
            ks.append(k)
            kbs.append(kb)
            decays.append(jnp.exp(jnp.where(incl, gcum - gcum_rows[h:h + 1, :], _NEG)))
            rhss.append(jnp.concatenate([v * beta, kb * jnp.exp(gcum)], axis=-1))
            qgs.append(q * jnp.exp(gcum))
            kdecs.append(k * jnp.exp(g_last[:, h:h + 1] - gcum))

    n_units = range(len(units))
    lowers, attns = [], []
    for u in n_units:
        prod = _dot(jnp.concatenate([kbs[u], qs[u]], axis=0),
                    jnp.concatenate([ks[u], ks[u]], axis=0).T)
        lowers.append(jnp.where(strict, prod[:CHUNK] * decays[u], 0.0))
        attns.append(jnp.where(incl, prod[CHUNK:] * decays[u], 0.0))

    pws = [-lowers[u] for u in n_units]
    invs = [eye + pws[u] for u in n_units]
    pw_l, pw_r = zip(*[split3(pws[u]) for u in n_units])
    pws = [mm(pw_l[u], pw_r[u]) for u in n_units]
    for _ in range(4):
        pw_l, pw_r = zip(*[split3(pws[u]) for u in n_units])
        both = [mm(jnp.concatenate([split3(invs[u])[0], pw_l[u]], axis=0), pw_r[u]) for u in n_units]
        invs = [invs[u] + both[u][:CHUNK] for u in n_units]
        pws = [both[u][CHUNK:] for u in n_units]
    invs = [invs[u] + mm(split3(invs[u])[0], split3(pws[u])[1]) for u in n_units]
    sols = []
    for u in n_units:
        r_hi16 = rhss[u].astype(_BF16)
        r_lo16 = (rhss[u] - r_hi16.astype(_F32)).astype(_BF16)
        sols.append(mm(split3(invs[u])[0], jnp.concatenate([r_hi16, r_hi16, r_lo16, jnp.zeros_like(r_lo16)], axis=0)))

    st = [st_ref[h] for h in heads]
    for u, (s, h) in enumerate(units):
        rows = slice(s * CHUNK, (s + 1) * CHUNK)
        w_u = sols[u][:, :C_HEAD_DIM]
        w_w = sols[u][:, C_HEAD_DIM:]
        ws = _dot(jnp.concatenate([w_w, qgs[u]], axis=0), st[h])
        v_new = w_u - ws[:CHUNK]
        o = ws[CHUNK:] + _dot(attns[u][:, :CHUNK], v_new)
        st[h] = jnp.exp(g_lasts[s][:, h:h + 1]) * st[h] + _dot(kdecs[u], v_new, _TN)
        cols = slice(h * C_HEAD_DIM, (h + 1) * C_HEAD_DIM)
        o_ref[0, rows, cols] = (_rms(o, ng_ref[...]) * _silu(z_ref[0, rows, cols])).astype(_BF16)
    for h in heads:
        st_ref[h] = st[h]

    @pl.when(c == pl.num_programs(1) - 1)
    def _():
        s_out_ref[0] = st_ref[...]


def _delta(act, z, gb, s0, norm_g):
    bsz, lq, _ = act.shape
    n_sub = min(DELTA_SUBCHUNKS, lq // CHUNK)
    rows_all = n_sub * CHUNK
    tok = lambda n: pl.BlockSpec((1, rows_all, n), lambda b, c: (b, c, 0))
    state = pl.BlockSpec((1, C_HEADS, C_HEAD_DIM, C_HEAD_DIM), lambda b, c: (b, 0, 0, 0))
    return pl.pallas_call(
        functools.partial(_delta_kernel, n_sub=n_sub),
        grid=(bsz, lq // rows_all),
        in_specs=[tok(CONV_CH), tok(C_W), tok(V7X_LANES), state,
                  pl.BlockSpec((1, C_HEAD_DIM), lambda b, c: (0, 0))],
        out_specs=[tok(C_W), state],
        out_shape=[jax.ShapeDtypeStruct((bsz, lq, C_W), _BF16),
                   jax.ShapeDtypeStruct(s0.shape, _F32)],
        scratch_shapes=[pltpu.VMEM((C_HEADS, C_HEAD_DIM, C_HEAD_DIM), _F32)],
        compiler_params=_params(2),
        name="gated_delta",
    )(act, z, gb, s0, norm_g.reshape(1, C_HEAD_DIM))


def _prep_even_w_in(w):
    sizes = (A_W, A_W, A_W, IDX_HEADS * IDX_DIM, IDX_DIM, IDX_HEADS, B_KW, B_KW, B_VW, B_VW, B_GATE_RANK)
    offs = [0]
    for s in sizes:
        offs.append(offs[-1] + s)
    part = lambda i: w[:, offs[i]:offs[i + 1]]
    q, k, v, qi, ki, wi, qb, kb, vb, rb, ab = [part(i) for i in range(len(sizes))]
    pad = jnp.zeros((w.shape[0], _EV_IDX_W - (256 + IDX_DIM + IDX_HEADS + B_GATE_RANK)), w.dtype)
    return jnp.concatenate([q, k, v, qi, ki, wi, ab, pad, qb, kb, vb, rb], axis=1).astype(_BF16)


def _prep_odd_w_in(w):
    pad = jnp.zeros((w.shape[0], _OD_COLS - w.shape[1]), w.dtype)
    return jnp.concatenate([w, pad], axis=1).astype(_BF16)


def _lane_pad(v, n=V7X_LANES):
    return jnp.pad(v, (0, n - v.shape[0])).reshape(1, n)


def _trunk(x, past, prm):
    bsz, lq, d = x.shape
    t = bsz * lq
    xf = x.reshape(t, d)

    def ffn(xf, i, j, acts=(), proj_w=()):
        return _ffn_half(xf, prm['ffn_norm'][i, j], prm['ffn_w_gate'], prm['ffn_w_up'], prm['ffn_w_down'],
                         i, j, acts, proj_w)

    xf = ffn(xf, 0, 0)
    head_mean = jnp.kron(jnp.eye(A_HEADS, dtype=_F32),
                         jnp.full((A_HEAD_DIM, A_HEAD_DIM), 1.0 / A_HEAD_DIM, _F32)).astype(_BF16)
    q_gain = (jnp.tile(prm['ev_q_norm'][0], A_HEADS) * (A_HEAD_DIM ** -0.5)).reshape(1, A_W)
    k_gain = jnp.tile(prm['ev_k_norm'][0], A_HEADS).reshape(1, A_W)
    w2 = jnp.zeros((V7X_LANES, B_KW), _F32).at[
        _EV_AB_OFF - 256:_EV_AB_OFF - 256 + B_GATE_RANK].set(prm['ev_gate_w2'][0]).astype(_BF16)
    k_new, v_new, k16, ki, ki2, qb, kb, vb, rb, la, qt, vt, qit, wit = _even_in_proj(
        xf, prm['mix_norm'][0], _prep_even_w_in(prm['ev_w_in'][0]), q_gain, k_gain, head_mean,
        w2, prm['ev_gate_b2'][0].reshape(1, B_KW), lq)
    r3 = lambda a: a.reshape(bsz, lq, a.shape[-1])
    ki_new = r3(ki)
    if past is None:
        past_len = 0
        k_all, vt_all, ki2_all = r3(k16), vt, r3(ki2)
        s0_gla = jnp.zeros((bsz, B_HEADS, B_KEY_DIM, B_VAL_DIM), _F32)
    else:
        past_len = past['k'].shape[2]
        k_all = jnp.concatenate([past['k'][0].reshape(bsz, past_len, A_W).astype(_BF16), r3(k16)], axis=1)
        vt_all = jnp.concatenate([jnp.swapaxes(past['v'][0].reshape(bsz, past_len, A_W).astype(_BF16), 1, 2), vt],
                                 axis=2)
        ki2_all = jnp.concatenate([jnp.concatenate([past['ki'][0], past['ki'][0]], axis=-1), r3(ki2)], axis=1)
        s0_gla = past['gla'][0]
    o_a = _dsa_attention(qt, qit, wit, k_all, vt_all, ki2_all, prm['rel_bias'], past_len)
    o_b, s_gla = _gla(r3(qb), r3(kb), r3(la), r3(vb), r3(rb), s0_gla, prm['ev_gla_norm'][0])
    w_out = prm['ev_w_out'][0].astype(_BF16)
    xf = ffn(xf, 0, 1, [o_a.reshape(t, A_W), o_b.reshape(t, B_VW)], [w_out[:A_W], w_out[A_W:]])

    xf = ffn(xf, 1, 0)
    a_log = _lane_pad(prm['od_a_log'][0])
    dt_bias = _lane_pad(prm['od_dt_bias'][0])
    if past is None:
        conv_prev = jnp.zeros((bsz, CONV_WIDTH - 1, CONV_CH), _F32)
        s0_delta = jnp.zeros((bsz, C_HEADS, C_HEAD_DIM, C_HEAD_DIM), _F32)
    else:
        conv_prev, s0_delta = past['conv'][0], past['delta'][0]
    conv_prev8 = jnp.pad(conv_prev, ((0, 0), (V7X_SUBLANES - (CONV_WIDTH - 1), 0), (0, 0)))
    act, z, gb, tail = _odd_in_proj(xf, prm['mix_norm'][1], _prep_odd_w_in(prm['od_w_in'][0]), a_log, dt_bias,
                                    prm['od_conv_w'][0], conv_prev8, lq)
    o_c, s_delta = _delta(r3(act), r3(z), r3(gb), s0_delta, prm['od_norm'][0])
    conv_new = tail[:, V7X_SUBLANES - (CONV_WIDTH - 1):]
    xf = ffn(xf, 1, 1, [o_c.reshape(t, C_W)], [prm['od_w_out'][0].astype(_BF16)])

    y = xf.reshape(bsz, lq, d)
    k_out = r3(k_new).reshape(1, bsz, lq, A_HEADS, A_HEAD_DIM)
    v_out = r3(v_new).reshape(1, bsz, lq, A_HEADS, A_HEAD_DIM)
    return y, k_out, v_out, ki_new[None], s_gla[None], s_delta[None], conv_new[None]


def kernel(x_prompt, x_sample, cache_attn_k, cache_attn_v, cache_idx_k, state_gla, state_delta, state_conv,
           ffn_norm, ffn_w_gate, ffn_w_up, ffn_w_down, mix_norm, ev_w_in, ev_q_norm, ev_k_norm, rel_bias,
           ev_gate_w2, ev_gate_b2, ev_gla_norm, ev_w_out, od_w_in, od_conv_w, od_a_log, od_dt_bias, od_norm,
           od_w_out):
    prm = {'ffn_norm': ffn_norm, 'ffn_w_gate': ffn_w_gate, 'ffn_w_up': ffn_w_up, 'ffn_w_down': ffn_w_down,
           'mix_norm': mix_norm, 'ev_w_in': ev_w_in, 'ev_q_norm': ev_q_norm, 'ev_k_norm': ev_k_norm,
           'rel_bias': rel_bias, 'ev_gate_w2': ev_gate_w2, 'ev_gate_b2': ev_gate_b2, 'ev_gla_norm': ev_gla_norm,
           'ev_w_out': ev_w_out, 'od_w_in': od_w_in, 'od_conv_w': od_conv_w, 'od_a_log': od_a_log,
           'od_dt_bias': od_dt_bias, 'od_norm': od_norm, 'od_w_out': od_w_out}
    past = {'k': cache_attn_k, 'v': cache_attn_v, 'ki': cache_idx_k, 'gla': state_gla,
            'delta': state_delta, 'conv': state_conv}
    y_prompt, p_k, p_v, p_ki, p_gla, p_delta, p_conv = _trunk(x_prompt, None, prm)
    y_sample, s_k, s_v, s_ki, s_gla, s_delta, s_conv = _trunk(x_sample, past, prm)
    return (y_prompt, y_sample, p_k, p_v, p_ki, p_gla, p_delta, p_conv,
            s_k, s_v, s_ki, s_gla, s_delta, s_conv)
```

```python
import functools
import math

import jax
import jax.numpy as jnp
from jax import lax
from jax.experimental import pallas as pl
from jax.experimental.pallas import tpu as pltpu

CHUNK = 64
EPS = 1e-6

A_HEADS = 8
A_HEAD_DIM = 64
A_W = A_HEADS * A_HEAD_DIM
IDX_HEADS = 4
IDX_DIM = 64
TOPK_MAX = 256
N_BUCKETS = 32
MAX_DISTANCE = 128

B_HEADS = 4
B_KEY_DIM = 64
B_VAL_DIM = 128
B_KW = B_HEADS * B_KEY_DIM
B_VW = B_HEADS * B_VAL_DIM
B_GATE_RANK = 16
B_GATE_TAU = 16.0

C_HEADS = 8
C_HEAD_DIM = 128
C_W = C_HEADS * C_HEAD_DIM
CONV_WIDTH = 4
CONV_CH = 3 * C_W

V7X_LANES = 128
V7X_SUBLANES = 8
V7X_VMEM_LIMIT_BYTES = 56 * 1024 * 1024
V7X_MXU_DIM = 256

ROW_TILE = 2 * V7X_MXU_DIM
ODD_ROW_TILE = V7X_MXU_DIM
DSA_KEY_CHUNK = V7X_MXU_DIM
GLA_SUBCHUNKS = 8
DELTA_SUBCHUNKS = 8

_F32 = jnp.float32
_BF16 = jnp.bfloat16
_NEG = -1e30
_LOG2E = math.log2(math.e)
_NN = (((1,), (0,)), ((), ()))
_NT = (((1,), (1,)), ((), ()))
_TN = (((0,), (0,)), ((), ()))

_EV_Q, _EV_K, _EV_V = 0, 512, 1024
_EV_IDX = 1536
_EV_IDX_W = 384
_EV_WI_OFF = 256 + IDX_DIM
_EV_AB_OFF = _EV_WI_OFF + IDX_HEADS
_EV_QB, _EV_KB, _EV_VB, _EV_RB = 1920, 2176, 2432, 2944
_EV_COLS = 3456
_OD_QKV, _OD_Z, _OD_AB = 0, 3072, 4096
_OD_COLS = 4224


def _params(n_axes):
    return pltpu.CompilerParams(dimension_semantics=("arbitrary",) * n_axes,
                                vmem_limit_bytes=V7X_VMEM_LIMIT_BYTES)


def _rms(x, g):
    return x * lax.rsqrt(jnp.mean(x * x, axis=-1, keepdims=True) + EPS) * g


def _silu(x):
    return x * jax.nn.sigmoid(x)


def _dot(a, b, dims=_NN):
    return lax.dot_general(a.astype(_BF16), b.astype(_BF16), dims, preferred_element_type=_F32)


def _full(shape):
    return pl.BlockSpec(shape, lambda *_: (0,) * len(shape))


_FFN_COLS = V7X_MXU_DIM


_FFN_GU_SLOTS = 4
_FFN_DOWN_CHUNKS = 4
_FFN_DOWN_SLOTS = 2


def _ffn_kernel(*refs, n_proj, layer, half):
    x_ref = refs[0]
    (g_ref, wg_hbm, wu_hbm, wd_hbm, o_ref, act_ref, wg_ref, wu_ref, wd_ref,
     stage_gu, stage_d, sem_gu, sem_d) = refs[1 + 2 * n_proj:]
    first = pl.program_id(0) == 0
    n_gu = wg_ref.shape[1] // _FFN_COLS
    d_rows = wd_ref.shape[0] // _FFN_DOWN_CHUNKS

    def gu_copy(c, which):
        src = (wg_hbm, wu_hbm)[which]
        return pltpu.make_async_copy(src.at[layer, half, :, pl.ds(c * _FFN_COLS, _FFN_COLS)],
                                     stage_gu.at[c % _FFN_GU_SLOTS, which], sem_gu.at[c % _FFN_GU_SLOTS, which])

    def d_copy(r):
        return pltpu.make_async_copy(wd_hbm.at[layer, half, pl.ds(r * d_rows, d_rows), :],
                                     stage_d.at[r % _FFN_DOWN_SLOTS], sem_d.at[r % _FFN_DOWN_SLOTS])

    def tile(stream):
        if stream:
            for c in range(_FFN_GU_SLOTS):
                gu_copy(c, 0).start()
                gu_copy(c, 1).start()
            for r in range(_FFN_DOWN_SLOTS):
                d_copy(r).start()
        x = x_ref[...]
        for i in range(n_proj):
            x = x + jnp.dot(refs[1 + i][...], refs[1 + n_proj + i][...], preferred_element_type=_F32)
        h = _rms(x, g_ref[...]).astype(_BF16)
        for c in range(n_gu):
            cols = slice(c * _FFN_COLS, (c + 1) * _FFN_COLS)
            if stream:
                gu_copy(c, 0).wait()
                gu_copy(c, 1).wait()
                wg_ref[:, cols] = stage_gu[c % _FFN_GU_SLOTS, 0].astype(_BF16)
                wu_ref[:, cols] = stage_gu[c % _FFN_GU_SLOTS, 1].astype(_BF16)
                if c + _FFN_GU_SLOTS < n_gu:
                    gu_copy(c + _FFN_GU_SLOTS, 0).start()
                    gu_copy(c + _FFN_GU_SLOTS, 1).start()
            gate = jnp.dot(h, wg_ref[:, cols], preferred_element_type=_F32)
            up = jnp.dot(h, wu_ref[:, cols], preferred_element_type=_F32)
            act_ref[:, cols] = (_silu(gate) * up).astype(_BF16)
        if stream:
            for r in range(_FFN_DOWN_CHUNKS):
                d_copy(r).wait()
                wd_ref[r * d_rows:(r + 1) * d_rows, :] = stage_d[r % _FFN_DOWN_SLOTS].astype(_BF16)
                if r + _FFN_DOWN_SLOTS < _FFN_DOWN_CHUNKS:
                    d_copy(r + _FFN_DOWN_SLOTS).start()
        o_ref[...] = x + 0.5 * jnp.dot(act_ref[...], wd_ref[...], preferred_element_type=_F32)

    pl.when(first)(lambda: tile(True))
    pl.when(jnp.logical_not(first))(lambda: tile(False))


def _ffn_half(x, g, wg, wu, wd, layer, half, acts=(), proj_w=()):
    t, d = x.shape
    ff = wg.shape[-1]
    tm = min(t, ROW_TILE)
    resident = lambda shape: pl.BlockSpec(shape, lambda i: (0, 0), pipeline_mode=pl.Buffered(1))
    hbm = pl.BlockSpec(memory_space=pl.ANY)
    return pl.pallas_call(
        functools.partial(_ffn_kernel, n_proj=len(acts), layer=layer, half=half),
        grid=(t // tm,),
        in_specs=([pl.BlockSpec((tm, d), lambda i: (i, 0))]
                  + [pl.BlockSpec((tm, a.shape[1]), lambda i: (i, 0)) for a in acts]
                  + [resident(w.shape) for w in proj_w]
                  + [resident((1, d)), hbm, hbm, hbm]),
        out_specs=pl.BlockSpec((tm, d), lambda i: (i, 0)),
        out_shape=jax.ShapeDtypeStruct((t, d), _F32),
        scratch_shapes=[pltpu.VMEM((tm, ff), _BF16),
                        pltpu.VMEM((d, ff), _BF16), pltpu.VMEM((d, ff), _BF16), pltpu.VMEM((ff, d), _BF16),
                        pltpu.VMEM((_FFN_GU_SLOTS, 2, d, _FFN_COLS), _F32),
                        pltpu.VMEM((_FFN_DOWN_SLOTS, ff // _FFN_DOWN_CHUNKS, d), _F32),
                        pltpu.SemaphoreType.DMA((_FFN_GU_SLOTS, 2)),
                        pltpu.SemaphoreType.DMA((_FFN_DOWN_SLOTS,))],
        compiler_params=_params(1),
        name="ffn_half",
    )(x, *acts, *proj_w, g.reshape(1, d), wg, wu, wd)


def _even_in_kernel(x_ref, g_ref, w_ref, qg_ref, kg_ref, hm_ref, w2_ref, b2_ref,
                    k_ref, v_ref, k16_ref, ki_ref, ki2_ref, qb_ref, kb_ref, vb_ref, rb_ref, la_ref,
                    qt_ref, vt_ref, qit_ref, wit_ref, *, seg):
    h = _rms(x_ref[...], g_ref[...]).astype(_BF16)

    def proj(lo, width):
        return jnp.dot(h, w_ref[:, lo:lo + width], preferred_element_type=_F32)

    def headnorm(t, gain):
        t2 = t * t
        hi = t2.astype(_BF16)
        lo = (t2 - hi.astype(_F32)).astype(_BF16)
        ms = (jnp.dot(hi, hm_ref[...], preferred_element_type=_F32)
              + jnp.dot(lo, hm_ref[...], preferred_element_type=_F32))
        return t * lax.rsqrt(ms + EPS) * gain

    qn = headnorm(proj(_EV_Q, A_W), qg_ref[...]).astype(_BF16)
    k = headnorm(proj(_EV_K, A_W), kg_ref[...])
    v = proj(_EV_V, A_W)
    k_ref[...] = k
    v_ref[...] = v
    k16_ref[...] = k.astype(_BF16)
    v16 = v.astype(_BF16)
    idx = proj(_EV_IDX, _EV_IDX_W)
    ki = idx[:, IDX_HEADS * IDX_DIM:IDX_HEADS * IDX_DIM + IDX_DIM]
    ki_ref[...] = ki
    ki2_ref[...] = jnp.concatenate([ki, ki], axis=1)
    for s in range(x_ref.shape[0] // seg):
        rows = slice(s * seg, (s + 1) * seg)
        qt_ref[s] = qn[rows].T
        vt_ref[s] = v16[rows].T
        qit_ref[s] = idx[rows, 0:IDX_HEADS * IDX_DIM].T
        wit_ref[s] = idx[rows, IDX_HEADS * IDX_DIM:_EV_IDX_W].T[IDX_DIM:IDX_DIM + V7X_SUBLANES]
    qb_ref[...] = proj(_EV_QB, B_KW) * (B_KEY_DIM ** -0.5)
    kb_ref[...] = proj(_EV_KB, B_KW)
    vb_ref[...] = proj(_EV_VB, B_VW)
    rb_ref[...] = proj(_EV_RB, B_VW)
    z = _dot(idx[:, 256:384], w2_ref[...]) + b2_ref[...]
    la_ref[...] = (jnp.minimum(z, 0.0) - jnp.log1p(jnp.exp(-jnp.abs(z)))) * (1.0 / B_GATE_TAU)


def _even_in_proj(x, g, w, q_gain, k_gain, head_mean, w2, b2, lq):
    t, d = x.shape
    tm = min(t, ROW_TILE)
    seg = min(tm, lq)
    tiles_per_batch = lq // seg
    row = lambda n: pl.BlockSpec((tm, n), lambda i: (i, 0))
    col = lambda n: pl.BlockSpec((tm // seg, n, seg), lambda i: (i // tiles_per_batch, 0, i % tiles_per_batch))
    outs = [(A_W, _F32), (A_W, _F32), (A_W, _BF16), (IDX_DIM, _F32), (2 * IDX_DIM, _F32), (B_KW, _F32), (B_KW, _F32),
            (B_VW, _F32), (B_VW, _F32), (B_KW, _F32)]
    outs_t = [(A_W, _BF16), (A_W, _BF16), (IDX_HEADS * IDX_DIM, _F32), (V7X_SUBLANES, _F32)]
    w_spec = pl.BlockSpec((d, _EV_COLS), lambda i: (0, 0), pipeline_mode=pl.Buffered(1))
    return pl.pallas_call(
        functools.partial(_even_in_kernel, seg=seg),
        grid=(t // tm,),
        in_specs=[row(d), _full((1, d)), w_spec, _full((1, A_W)), _full((1, A_W)),
                  _full((A_W, A_W)), _full((V7X_LANES, B_KW)), _full((1, B_KW))],
        out_specs=[row(n) for n, _ in outs] + [col(n) for n, _ in outs_t],
        out_shape=([jax.ShapeDtypeStruct((t, n), dt) for n, dt in outs]
                   + [jax.ShapeDtypeStruct((t // lq, n, lq), dt) for n, dt in outs_t]),
        compiler_params=_params(1),
        name="even_in_proj",
    )(x, g.reshape(1, d), w, q_gain, k_gain, head_mean, w2, b2)


def _dsa_kernel(rb_ref, qt_ref, qit_ref, wit_ref, k_ref, vt_ref, ki_ref, bkt_ref, o_ref,
                key_ref, mask_ref, nb_ref, ki3_ref, acc_ref, lt_ref, p_ref, *, past, tq, kc, topk, r0_off, chunk_counts):
    t = pl.program_id(1)

    @pl.when((pl.program_id(0) == 0) & (t == 0))
    def _():
        bk = bkt_ref[...]
        for h in range(A_HEADS):
            acc = jnp.zeros(bk.shape, _F32)
            for bb in range(N_BUCKETS):
                acc = jnp.where(bk == bb, rb_ref[bb, h], acc)
            nb_ref[h] = (acc - rb_ref[N_BUCKETS // 2 - 1, h]) * _LOG2E

    q0 = past + t * tq
    n_chunks = (q0 + tq + kc - 1) // kc
    q_limit = ((q0 + lax.broadcasted_iota(jnp.int32, (1, tq), 1)) // CHUNK + 1) * CHUNK
    k_iota = lax.broadcasted_iota(jnp.int32, (kc, tq), 0)

    def admissible(r0):
        return k_iota < (q_limit - r0)

    @pl.when(t == 0)
    def _():
        def split_body(j, carry):
            r0 = pl.multiple_of(j * kc, kc)
            x = ki_ref[0, pl.ds(r0, kc), :]
            hi = x.astype(_BF16)
            ki3_ref[pl.ds(r0, kc), 0:V7X_LANES] = hi
            ki3_ref[pl.ds(r0, kc), V7X_LANES:2 * V7X_LANES] = (x - hi.astype(_F32)).astype(_BF16)
            return carry
        lax.fori_loop(0, ki3_ref.shape[0] // kc, split_body, 0)

    qit = qit_ref[0]
    q_cols = []
    for h in range(IDX_HEADS):
        x = qit[h * IDX_DIM:(h + 1) * IDX_DIM, :]
        hi = x.astype(_BF16)
        lo = (x - hi.astype(_F32)).astype(_BF16)
        q_cols.append(jnp.concatenate([hi, lo, hi, jnp.zeros_like(hi)], axis=0))
    q3 = jnp.concatenate(q_cols, axis=1)
    wit = wit_ref[0]

    upper_rows = lax.broadcasted_iota(jnp.int32, (V7X_LANES, tq), 0) >= A_HEAD_DIM
    q_pairs = []
    for pr in range(A_HEADS // 2):
        blk = qt_ref[0, pr * V7X_LANES:(pr + 1) * V7X_LANES, :].astype(_F32)
        q_pairs.append(jnp.concatenate([jnp.where(upper_rows, 0.0, blk), jnp.where(upper_rows, blk, 0.0)],
                                       axis=1).astype(_BF16))

    def scores_and_logits(n):
        d_all = jnp.dot(ki3_ref[0:n * kc, :], q3, preferred_element_type=_F32)
        for j in range(n):
            d = d_all[j * kc:(j + 1) * kc]
            s = jnp.zeros((kc, tq), _F32)
            for h in range(IDX_HEADS):
                s = s + wit[h:h + 1, :] * jnp.maximum(d[:, h * tq:(h + 1) * tq], 0.0)
            s = jnp.where(s == 0.0, 0.0, s)
            s = jnp.where(admissible(j * kc), s, -jnp.inf)
            bits = pltpu.bitcast(s, jnp.int32)
            key_ref[j * kc:(j + 1) * kc, :] = bits ^ ((bits >> 31) & jnp.int32(0x7FFFFFFF))
        for pr in range(A_HEADS // 2):
            lt2 = jnp.dot(k_ref[0, 0:n * kc, pr * V7X_LANES:(pr + 1) * V7X_LANES], q_pairs[pr],
                          preferred_element_type=_F32)
            for hh in range(2):
                lt_ref[2 * pr + hh, 0:n * kc, :] = lt2[:, hh * tq:(hh + 1) * tq] * _LOG2E
        for j in range(max(n - 2, 0), n):
            off = pl.multiple_of(jnp.maximum(j * kc - q0 + r0_off, 0), CHUNK)
            for h in range(A_HEADS):
                lt_ref[h, j * kc:(j + 1) * kc, :] += nb_ref[h, pl.ds(off, kc), :]

    int_min = jnp.int32(-2 ** 31)

    def count_keys(pred, n):
        parts = []
        for j in range(n):
            m = jnp.where(pred(key_ref[j * kc:(j + 1) * kc, :]), 1, 0)
            parts.append(m.reshape(kc // V7X_SUBLANES, V7X_SUBLANES, tq).sum(axis=0))
        while len(parts) > 1:
            parts = [sum(parts[i:i + 2]) for i in range(0, len(parts), 2)]
        return parts[0].sum(axis=0, keepdims=True)

    def radix_select(n):
        def bit_body(i, carry):
            tu, n_ge = carry
            cu = tu | jnp.left_shift(jnp.int32(1), 31 - i)
            cs = cu ^ int_min
            cnt = count_keys(lambda blk: blk >= cs, n)
            keep = cnt >= topk
            return jnp.where(keep, cu, tu), jnp.where(keep, cnt, n_ge)

        start = (jnp.zeros((1, tq), jnp.int32), jnp.full((1, tq), n * kc, jnp.int32))
        if n * kc <= topk:
            return start
        return lax.fori_loop(0, 32, bit_body, start)

    def tile_select(n):
        scores_and_logits(n)
        return radix_select(n)

    tu, n_ge = lax.switch(n_chunks - chunk_counts[0], [functools.partial(tile_select, n) for n in chunk_counts])
    ts = tu ^ int_min

    tri = (lax.broadcasted_iota(jnp.int32, (kc, kc), 0)
           >= lax.broadcasted_iota(jnp.int32, (kc, kc), 1)).astype(_BF16)

    def tie_masks():
        def gt_body(j, acc):
            r0 = pl.multiple_of(j * kc, kc)
            m = jnp.where(key_ref[pl.ds(r0, kc), :] > ts, 1, 0)
            return acc + m.reshape(kc // V7X_SUBLANES, V7X_SUBLANES, tq).sum(axis=0)

        n_greater = lax.fori_loop(0, n_chunks, gt_body, jnp.zeros((V7X_SUBLANES, tq), jnp.int32))
        room = (topk - n_greater.sum(axis=0, keepdims=True)).astype(_F32)

        def body(j, seen):
            r0 = pl.multiple_of(j * kc, kc)
            blk = key_ref[pl.ds(r0, kc), :]
            eq = blk == ts
            rank = jnp.dot(tri, jnp.where(eq, 1.0, 0.0).astype(_BF16), preferred_element_type=_F32) + seen
            val = jnp.where(blk > ts, 0.0, jnp.where(eq, jnp.where(rank <= room, 0.0, _NEG), _NEG))
            mask_ref[pl.ds(r0, kc), :] = jnp.where(admissible(r0), val, _NEG)
            return rank[kc - 1:kc, :]

        return lax.fori_loop(0, n_chunks, body, jnp.zeros((1, tq), _F32))

    def plain_masks():
        def body(j, carry):
            r0 = pl.multiple_of(j * kc, kc)
            val = jnp.where(key_ref[pl.ds(r0, kc), :] >= ts, 0.0, _NEG)
            mask_ref[pl.ds(r0, kc), :] = jnp.where(admissible(r0), val, _NEG)
            return carry

        return lax.fori_loop(0, n_chunks, body, jnp.zeros((1, tq), _F32))

    lax.cond(jnp.max(n_ge) > topk, tie_masks, plain_masks)

    acc_ref[...] = jnp.zeros_like(acc_ref)

    def attn_body(j, carry):
        ms, ls = carry
        r0 = pl.multiple_of(j * kc, kc)
        msk = mask_ref[pl.ds(r0, kc), :]
        new_ms, new_ls, alphas = [], [], []
        for h in range(A_HEADS):
            lt = lt_ref[h, pl.ds(r0, kc), :] + msk
            m_new = jnp.maximum(ms[h], lt.max(axis=0, keepdims=True))
            p = jnp.exp2(lt - m_new)
            alpha = jnp.exp2(ms[h] - m_new)
            new_ms.append(m_new)
            new_ls.append(alpha * ls[h] + p.sum(axis=0, keepdims=True))
            alphas.append(alpha)
            p_ref[h] = p.astype(_BF16)
        for h in range(A_HEADS):
            rows = slice(h * A_HEAD_DIM, (h + 1) * A_HEAD_DIM)
            pv = jnp.dot(vt_ref[0, rows, pl.ds(r0, kc)], p_ref[h], preferred_element_type=_F32)
            acc_ref[rows, :] = alphas[h] * acc_ref[rows, :] + pv
        return tuple(new_ms), tuple(new_ls)

    init = (tuple(jnp.full((1, tq), _NEG, _F32) for _ in range(A_HEADS)),
            tuple(jnp.zeros((1, tq), _F32) for _ in range(A_HEADS)))

    def attend(n):
        carry = init
        for j in range(n):
            carry = attn_body(j, carry)
        return carry

    _, ls = lax.switch(n_chunks - chunk_counts[0], [functools.partial(attend, n) for n in chunk_counts])

    for pair in range(A_HEADS // 2):
        lanes = slice(pair * V7X_LANES, (pair + 1) * V7X_LANES)
        o_t = jnp.concatenate(
            [acc_ref[(2 * pair + hh) * A_HEAD_DIM:(2 * pair + hh + 1) * A_HEAD_DIM, :] / ls[2 * pair + hh]
             for hh in range(2)], axis=0)
        o_ref[0, :, lanes] = o_t.T.astype(_BF16)


def _t5_bucket_table(rel):
    half = N_BUCKETS // 2
    max_exact = half // 2
    n = jnp.abs(rel)
    nf = jnp.maximum(n, 1).astype(jnp.float32)
    large = max_exact + (jnp.log(nf / max_exact) / math.log(MAX_DISTANCE / max_exact)
                         * (half - max_exact)).astype(jnp.int32)
    large = jnp.minimum(large, half - 1)
    return jnp.where(rel > 0, half, 0) + jnp.where(n < max_exact, n, large)


def _dsa_attention(qt, qit, wit, k_all, vt_all, ki2_all, rel_bias, past):
    bsz, _, lq = qt.shape
    nk = k_all.shape[1]
    topk = min(TOPK_MAX, nk // 4)
    tq = min(V7X_LANES, lq)
    kc = DSA_KEY_CHUNK
    nk_pad = -(-nk // kc) * kc
    pad = ((0, 0), (0, nk_pad - nk), (0, 0))
    k16 = jnp.pad(k_all, pad)
    vt16 = jnp.pad(vt_all, ((0, 0), (0, 0), (0, nk_pad - nk)))
    ki2 = jnp.pad(ki2_all, pad)
    r0_off = kc + MAX_DISTANCE
    nbr = r0_off + kc + tq
    rel = (jnp.arange(nbr, dtype=jnp.int32)[:, None] - r0_off) - jnp.arange(tq, dtype=jnp.int32)[None, :]
    bkt = _t5_bucket_table(rel)
    chunk_counts = sorted({(past + (t + 1) * tq + kc - 1) // kc for t in range(lq // tq)})
    assert chunk_counts == list(range(chunk_counts[0], chunk_counts[-1] + 1))
    kern = functools.partial(_dsa_kernel, past=past, tq=tq, kc=kc, topk=topk, r0_off=r0_off,
                             chunk_counts=tuple(chunk_counts))
    return pl.pallas_call(
        kern,
        grid=(bsz, lq // tq),
        in_specs=[pl.BlockSpec(memory_space=pltpu.SMEM),
                  pl.BlockSpec((1, A_W, tq), lambda b, t: (b, 0, t)),
                  pl.BlockSpec((1, IDX_HEADS * IDX_DIM, tq), lambda b, t: (b, 0, t)),
                  pl.BlockSpec((1, V7X_SUBLANES, tq), lambda b, t: (b, 0, t)),
                  pl.BlockSpec((1, nk_pad, A_W), lambda b, t: (b, 0, 0)),
                  pl.BlockSpec((1, A_W, nk_pad), lambda b, t: (b, 0, 0)),
                  pl.BlockSpec((1, nk_pad, V7X_LANES), lambda b, t: (b, 0, 0)),
                  pl.BlockSpec((nbr, tq), lambda b, t: (0, 0))],
        out_specs=pl.BlockSpec((1, tq, A_W), lambda b, t: (b, t, 0)),
        out_shape=jax.ShapeDtypeStruct((bsz, lq, A_W), _BF16),
        scratch_shapes=[pltpu.VMEM((nk_pad, tq), jnp.int32), pltpu.VMEM((nk_pad, tq), _F32),
                        pltpu.VMEM((A_HEADS, nbr, tq), _F32),
                        pltpu.VMEM((nk_pad, 2 * V7X_LANES), _BF16), pltpu.VMEM((A_W, tq), _F32),
                        pltpu.VMEM((A_HEADS, nk_pad, tq), _F32), pltpu.VMEM((A_HEADS, kc, tq), _BF16)],
        compiler_params=_params(2),
        name="dsa_attention",
    )(rel_bias, qt, qit, wit, k16, vt16, ki2, bkt)


def _split3_rows(x):
    hi = x.astype(_BF16)
    r1 = x - hi.astype(_F32)
    mid = r1.astype(_BF16)
    lo = (r1 - mid.astype(_F32)).astype(_BF16)
    return jnp.concatenate([hi, mid, lo], axis=0)


def _gla_kernel(q_ref, k_ref, g_ref, v_ref, r_ref, s0_ref, ng_ref, o_ref, s_out_ref, st_ref, *, n_sub):
    c = pl.program_id(1)
    causal = (lax.broadcasted_iota(jnp.int32, (CHUNK, CHUNK), 0)
              >= lax.broadcasted_iota(jnp.int32, (CHUNK, CHUNK), 1))
    causal3 = jnp.concatenate([causal.astype(_BF16)] * 3, axis=1)
    upper_half = lax.broadcasted_iota(jnp.int32, (CHUNK, V7X_LANES), 1) >= B_KEY_DIM
    diag = (lax.broadcasted_iota(jnp.int32, (V7X_LANES, V7X_LANES), 0)
            == lax.broadcasted_iota(jnp.int32, (V7X_LANES, V7X_LANES), 1))
    pairs = range(B_HEADS // 2)
    heads = range(B_HEADS)

    @pl.when(c == 0)
    def _():
        st_ref[...] = s0_ref[0]

    def pair_lanes(x, p):
        return x[:, p * V7X_LANES:(p + 1) * V7X_LANES]

    def head_half(x, h):
        keep = upper_half if h % 2 == 1 else jnp.logical_not(upper_half)
        return jnp.where(keep, pair_lanes(x, h // 2), 0.0)

    parts = []
    for s in range(n_sub):
        rows = slice(s * CHUNK, (s + 1) * CHUNK)
        b = jnp.dot(causal3, _split3_rows(g_ref[0, rows, :]), preferred_element_type=_F32)
        mid = CHUNK // 2
        b_mid = b[mid:mid + 1, :]
        b_last = b[CHUNK - 1:CHUNK, :]
        q = q_ref[0, rows, :]
        k = k_ref[0, rows, :]
        qe = q * jnp.exp(b)
        qm = q * jnp.exp(b - b_mid)
        km = k * jnp.exp(b_mid - b)
        kl = k * jnp.exp(b_last - b)
        vs = [v_ref[0, rows, h * B_VAL_DIM:(h + 1) * B_VAL_DIM] for h in heads]
        a = [jnp.where(causal, _dot(head_half(qm, h), pair_lanes(km, h // 2), _NT), 0.0) for h in heads]
        av = [_dot(a[h], vs[h]) for h in heads]
        upd = [_dot(head_half(kl, h), vs[h], _TN) for h in heads]
        dcol = [jnp.sum(jnp.where(diag, jnp.exp(pair_lanes(b_last, p)), 0.0), axis=1, keepdims=True)
                for p in pairs]
        parts.append((qe, av, upd, dcol))

    st = [st_ref[p * V7X_LANES:(p + 1) * V7X_LANES, :] for p in pairs]
    for s in range(n_sub):
        rows = slice(s * CHUNK, (s + 1) * CHUNK)
        qe, av, upd, dcol = parts[s]
        o = [_dot(head_half(qe, h), st[h // 2]) + av[h] for h in heads]
        st = [dcol[p] * st[p] + upd[2 * p] + upd[2 * p + 1] for p in pairs]
        for h in heads:
            cols = slice(h * B_VAL_DIM, (h + 1) * B_VAL_DIM)
            o_ref[0, rows, cols] = (_rms(o[h], ng_ref[...]) * _silu(r_ref[0, rows, cols])).astype(_BF16)
    for p in pairs:
        st_ref[p * V7X_LANES:(p + 1) * V7X_LANES, :] = st[p]

    @pl.when(c == pl.num_programs(1) - 1)
    def _():
        s_out_ref[0] = st_ref[...]


def _gla(qb, kb, la, vb, rb, s0, norm_g):
    bsz, lq, _ = qb.shape
    n_sub = min(GLA_SUBCHUNKS, lq // CHUNK)
    tok = lambda n: pl.BlockSpec((1, n_sub * CHUNK, n), lambda b, c: (b, c, 0))
    state = pl.BlockSpec((1, B_KW, B_VAL_DIM), lambda b, c: (b, 0, 0))
    o, s_new = pl.pallas_call(
        functools.partial(_gla_kernel, n_sub=n_sub),
        grid=(bsz, lq // (n_sub * CHUNK)),
        in_specs=[tok(B_KW), tok(B_KW), tok(B_KW), tok(B_VW), tok(B_VW), state,
                  pl.BlockSpec((1, B_VAL_DIM), lambda b, c: (0, 0))],
        out_specs=[tok(B_VW), state],
        out_shape=[jax.ShapeDtypeStruct((bsz, lq, B_VW), _BF16),
                   jax.ShapeDtypeStruct((bsz, B_KW, B_VAL_DIM), _F32)],
        scratch_shapes=[pltpu.VMEM((B_KW, B_VAL_DIM), _F32)],
        compiler_params=_params(2),
        name="gla",
    )(qb, kb, la, vb, rb, s0.reshape(bsz, B_KW, B_VAL_DIM), norm_g.reshape(1, B_VAL_DIM))
    return o, s_new.reshape(s0.shape)


def _odd_in_kernel(x_ref, g_ref, w_ref, alog_ref, dt_ref, cw_ref, cprev_ref, act_ref, z_ref, gb_ref, tail_ref,
                   xbuf_ref, *, seg, tiles_per_batch):
    halo = V7X_SUBLANES
    h = _rms(x_ref[...], g_ref[...]).astype(_BF16)
    z_ref[...] = jnp.dot(h, w_ref[:, _OD_Z:_OD_Z + C_W], preferred_element_type=_F32)
    ab = jnp.dot(h, w_ref[:, _OD_AB:_OD_AB + V7X_LANES], preferred_element_type=_F32)
    xa = ab + dt_ref[...]
    g = -jnp.exp(alog_ref[...]) * (jnp.maximum(xa, 0.0) + jnp.log1p(jnp.exp(-jnp.abs(xa))))
    lane = lax.broadcasted_iota(jnp.int32, ab.shape, 1)
    gb_ref[...] = jnp.where(lane < C_HEADS, g, jax.nn.sigmoid(ab))

    for s in range(x_ref.shape[0] // seg):
        rows = slice(s * seg, (s + 1) * seg)
        if tiles_per_batch > 1:
            @pl.when(pl.program_id(0) % tiles_per_batch == 0)
            def _():
                xbuf_ref[0:halo, :] = cprev_ref[0]
        else:
            xbuf_ref[0:halo, :] = cprev_ref[s]
        xbuf_ref[halo:halo + seg, :] = jnp.dot(h[rows], w_ref[:, _OD_QKV:_OD_QKV + CONV_CH],
                                               preferred_element_type=_F32)
        conv = xbuf_ref[halo:halo + seg, :] * cw_ref[CONV_WIDTH - 1:CONV_WIDTH, :]
        for j in range(CONV_WIDTH - 1):
            sh = CONV_WIDTH - 1 - j
            conv = conv + xbuf_ref[halo - sh:halo - sh + seg, :] * cw_ref[j:j + 1, :]
        tail = xbuf_ref[seg:seg + halo, :]
        xbuf_ref[0:halo, :] = tail
        tail_ref[s] = tail
        act = _silu(conv)
        for hd in range(C_HEADS):
            for part, scale in ((0, C_HEAD_DIM ** -0.5), (1, 1.0)):
                cols = slice(part * C_W + hd * C_HEAD_DIM, part * C_W + (hd + 1) * C_HEAD_DIM)
                a = act[:, cols]
                act_ref[rows, cols] = a * (lax.rsqrt(jnp.sum(a * a, axis=-1, keepdims=True) + EPS) * scale)
        act_ref[rows, 2 * C_W:] = act[:, 2 * C_W:]


def _odd_in_proj(x, g, w, a_log, dt_bias, conv_w, conv_prev8, lq):
    t, d = x.shape
    tm = min(t, ODD_ROW_TILE)
    seg = min(tm, lq)
    batches_per_tile = tm // seg
    tiles_per_batch = lq // seg
    row = lambda n: pl.BlockSpec((tm, n), lambda i: (i, 0))
    per_batch = pl.BlockSpec((batches_per_tile, V7X_SUBLANES, CONV_CH), lambda i: (i // tiles_per_batch, 0, 0))
    outs = [CONV_CH, C_W, V7X_LANES]
    return pl.pallas_call(
        functools.partial(_odd_in_kernel, seg=seg, tiles_per_batch=tiles_per_batch),
        grid=(t // tm,),
        in_specs=[row(d), _full((1, d)), _full((d, _OD_COLS)), _full((1, V7X_LANES)), _full((1, V7X_LANES)),
                  _full((CONV_WIDTH, CONV_CH)), per_batch],
        out_specs=[row(n) for n in outs] + [per_batch],
        out_shape=([jax.ShapeDtypeStruct((t, n), _F32) for n in outs]
                   + [jax.ShapeDtypeStruct((t // lq, V7X_SUBLANES, CONV_CH), _F32)]),
        scratch_shapes=[pltpu.VMEM((seg + V7X_SUBLANES, CONV_CH), _F32)],
        compiler_params=_params(1),
        name="odd_in_proj",
    )(x, g.reshape(1, d), w, a_log, dt_bias, conv_w, conv_prev8)


def _delta_kernel(act_ref, z_ref, gb_ref, s0_ref, ng_ref, o_ref, s_out_ref, st_ref, *, n_sub):
    c = pl.program_id(1)

    @pl.when(c == 0)
    def _():
        st_ref[...] = s0_ref[0]

    t_i = lax.broadcasted_iota(jnp.int32, (CHUNK, V7X_LANES), 0)
    s_i = lax.broadcasted_iota(jnp.int32, (CHUNK, V7X_LANES), 1)
    lower_half = s_i < CHUNK
    s_mod = jnp.where(lower_half, s_i, s_i - CHUNK)
    incl = t_i >= s_mod
    strict = t_i > s_mod
    eye = (t_i == s_mod).astype(_F32)
    incl3 = jnp.concatenate([(lax.broadcasted_iota(jnp.int32, (CHUNK, CHUNK), 0)
                              >= lax.broadcasted_iota(jnp.int32, (CHUNK, CHUNK), 1)).astype(_BF16)] * 3, axis=1)

    def split3(x):
        hi16 = x.astype(_BF16)
        hi = hi16.astype(_F32)
        lo = x - hi
        lhs = jnp.concatenate([jnp.where(lower_half, hi, lo).astype(_BF16), hi16], axis=1)
        lo16 = lo.astype(_BF16)
        rhs = jnp.concatenate([hi16, hi16, lo16, jnp.zeros_like(lo16)], axis=0)
        return lhs, rhs

    def mm(lhs, rhs):
        return jnp.dot(lhs, rhs, preferred_element_type=_F32)

    heads = range(C_HEADS)
    units = [(s, h) for s in range(n_sub) for h in heads]
    g_lasts = []
    qs, ks, kbs, decays, rhss, qgs, kdecs = [], [], [], [], [], [], []
    for s in range(n_sub):
        rows = slice(s * CHUNK, (s + 1) * CHUNK)
        gb = gb_ref[0, rows, :]
        gcum_all = jnp.dot(incl3, _split3_rows(gb), preferred_element_type=_F32)
        gcum_rows = jnp.concatenate([gcum_all, gcum_all], axis=0).T
        g_last = gcum_all[CHUNK - 1:CHUNK, :]
        g_lasts.append(g_last)
        for h in heads:
            q = act_ref[0, rows, h * C_HEAD_DIM:(h + 1) * C_HEAD_DIM]
            k = act_ref[0, rows, C_W + h * C_HEAD_DIM:C_W + (h + 1) * C_HEAD_DIM]
            v = act_ref[0, rows, 2 * C_W + h * C_HEAD_DIM:2 * C_W + (h + 1) * C_HEAD_DIM]
            beta = gb[:, C_HEADS + h:C_HEADS + h + 1]
            gcum = gcum_all[:, h:h + 1]
            kb = k * beta
            qs.append(q)
            ks.append(k)
            kbs.append(kb)
            decays.append(jnp.exp(jnp.where(incl, gcum - gcum_rows[h:h + 1, :], _NEG)))
            rhss.append(jnp.concatenate([v * beta, kb * jnp.exp(gcum)], axis=-1))
            qgs.append(q * jnp.exp(gcum))
            kdecs.append(k * jnp.exp(g_last[:, h:h + 1] - gcum))

    n_units = range(len(units))
    lowers, attns = [], []
    for u in n_units:
        prod = _dot(jnp.concatenate([kbs[u], qs[u]], axis=0),
                    jnp.concatenate([ks[u], ks[u]], axis=0).T)
        lowers.append(jnp.where(strict, prod[:CHUNK] * decays[u], 0.0))
        attns.append(jnp.where(incl, prod[CHUNK:] * decays[u], 0.0))

    pws = [-lowers[u] for u in n_units]
    invs = [eye + pws[u] for u in n_units]
    pw_l, pw_r = zip(*[split3(pws[u]) for u in n_units])
    pws = [mm(pw_l[u], pw_r[u]) for u in n_units]
    for _ in range(4):
        pw_l, pw_r = zip(*[split3(pws[u]) for u in n_units])
        both = [mm(jnp.concatenate([split3(invs[u])[0], pw_l[u]], axis=0), pw_r[u]) for u in n_units]
        invs = [invs[u] + both[u][:CHUNK] for u in n_units]
        pws = [both[u][CHUNK:] for u in n_units]
    invs = [invs[u] + mm(split3(invs[u])[0], split3(pws[u])[1]) for u in n_units]
    sols = []
    for u in n_units:
        r_hi16 = rhss[u].astype(_BF16)
        r_lo16 = (rhss[u] - r_hi16.astype(_F32)).astype(_BF16)
        sols.append(mm(split3(invs[u])[0], jnp.concatenate([r_hi16, r_hi16, r_lo16, jnp.zeros_like(r_lo16)], axis=0)))

    st = [st_ref[h] for h in heads]
    for u, (s, h) in enumerate(units):
        rows = slice(s * CHUNK, (s + 1) * CHUNK)
        w_u = sols[u][:, :C_HEAD_DIM]
        w_w = sols[u][:, C_HEAD_DIM:]
        ws = _dot(jnp.concatenate([w_w, qgs[u]], axis=0), st[h])
        v_new = w_u - ws[:CHUNK]
        o = ws[CHUNK:] + _dot(attns[u][:, :CHUNK], v_new)
        st[h] = jnp.exp(g_lasts[s][:, h:h + 1]) * st[h] + _dot(kdecs[u], v_new, _TN)
        cols = slice(h * C_HEAD_DIM, (h + 1) * C_HEAD_DIM)
        o_ref[0, rows, cols] = (_rms(o, ng_ref[...]) * _silu(z_ref[0, rows, cols])).astype(_BF16)
    for h in heads:
        st_ref[h] = st[h]

    @pl.when(c == pl.num_programs(1) - 1)
    def _():
        s_out_ref[0] = st_ref[...]


def _delta(act, z, gb, s0, norm_g):
    bsz, lq, _ = act.shape
    n_sub = min(DELTA_SUBCHUNKS, lq // CHUNK)
    rows_all = n_sub * CHUNK
    tok = lambda n: pl.BlockSpec((1, rows_all, n), lambda b, c: (b, c, 0))
    state = pl.BlockSpec((1, C_HEADS, C_HEAD_DIM, C_HEAD_DIM), lambda b, c: (b, 0, 0, 0))
    return pl.pallas_call(
        functools.partial(_delta_kernel, n_sub=n_sub),
        grid=(bsz, lq // rows_all),
        in_specs=[tok(CONV_CH), tok(C_W), tok(V7X_LANES), state,
                  pl.BlockSpec((1, C_HEAD_DIM), lambda b, c: (0, 0))],
        out_specs=[tok(C_W), state],
        out_shape=[jax.ShapeDtypeStruct((bsz, lq, C_W), _BF16),
                   jax.ShapeDtypeStruct(s0.shape, _F32)],
        scratch_shapes=[pltpu.VMEM((C_HEADS, C_HEAD_DIM, C_HEAD_DIM), _F32)],
        compiler_params=_params(2),
        name="gated_delta",
    )(act, z, gb, s0, norm_g.reshape(1, C_HEAD_DIM))


def _prep_even_w_in(w):
    sizes = (A_W, A_W, A_W, IDX_HEADS * IDX_DIM, IDX_DIM, IDX_HEADS, B_KW, B_KW, B_VW, B_VW, B_GATE_RANK)
    offs = [0]
    for s in sizes:
        offs.append(offs[-1] + s)
    part = lambda i: w[:, offs[i]:offs[i + 1]]
    q, k, v, qi, ki, wi, qb, kb, vb, rb, ab = [part(i) for i in range(len(sizes))]
    pad = jnp.zeros((w.shape[0], _EV_IDX_W - (256 + IDX_DIM + IDX_HEADS + B_GATE_RANK)), w.dtype)
    return jnp.concatenate([q, k, v, qi, ki, wi, ab, pad, qb, kb, vb, rb], axis=1).astype(_BF16)


def _prep_odd_w_in(w):
    pad = jnp.zeros((w.shape[0], _OD_COLS - w.shape[1]), w.dtype)
    return jnp.concatenate([w, pad], axis=1).astype(_BF16)


def _lane_pad(v, n=V7X_LANES):
    return jnp.pad(v, (0, n - v.shape[0])).reshape(1, n)


def _trunk(x, past, prm):
    bsz, lq, d = x.shape
    t = bsz * lq
    xf = x.reshape(t, d)

    def ffn(xf, i, j, acts=(), proj_w=()):
        return _ffn_half(xf, prm['ffn_norm'][i, j], prm['ffn_w_gate'], prm['ffn_w_up'], prm['ffn_w_down'],
                         i, j, acts, proj_w)

    xf = ffn(xf, 0, 0)
    head_mean = jnp.kron(jnp.eye(A_HEADS, dtype=_F32),
                         jnp.full((A_HEAD_DIM, A_HEAD_DIM), 1.0 / A_HEAD_DIM, _F32)).astype(_BF16)
    q_gain = (jnp.tile(prm['ev_q_norm'][0], A_HEADS) * (A_HEAD_DIM ** -0.5)).reshape(1, A_W)
    k_gain = jnp.tile(prm['ev_k_norm'][0], A_HEADS).reshape(1, A_W)
    w2 = jnp.zeros((V7X_LANES, B_KW), _F32).at[
        _EV_AB_OFF - 256:_EV_AB_OFF - 256 + B_GATE_RANK].set(prm['ev_gate_w2'][0]).astype(_BF16)
    k_new, v_new, k16, ki, ki2, qb, kb, vb, rb, la, qt, vt, qit, wit = _even_in_proj(
        xf, prm['mix_norm'][0], _prep_even_w_in(prm['ev_w_in'][0]), q_gain, k_gain, head_mean,
        w2, prm['ev_gate_b2'][0].reshape(1, B_KW), lq)
    r3 = lambda a: a.reshape(bsz, lq, a.shape[-1])
    ki_new = r3(ki)
    if past is None:
        past_len = 0
        k_all, vt_all, ki2_all = r3(k16), vt, r3(ki2)
        s0_gla = jnp.zeros((bsz, B_HEADS, B_KEY_DIM, B_VAL_DIM), _F32)
    else:
        past_len = past['k'].shape[2]
        k_all = jnp.concatenate([past['k'][0].reshape(bsz, past_len, A_W).astype(_BF16), r3(k16)], axis=1)
        vt_all = jnp.concatenate([jnp.swapaxes(past['v'][0].reshape(bsz, past_len, A_W).astype(_BF16), 1, 2), vt],
                                 axis=2)
        ki2_all = jnp.concatenate([jnp.concatenate([past['ki'][0], past['ki'][0]], axis=-1), r3(ki2)], axis=1)
        s0_gla = past['gla'][0]
    o_a = _dsa_attention(qt, qit, wit, k_all, vt_all, ki2_all, prm['rel_bias'], past_len)
    o_b, s_gla = _gla(r3(qb), r3(kb), r3(la), r3(vb), r3(rb), s0_gla, prm['ev_gla_norm'][0])
    w_out = prm['ev_w_out'][0].astype(_BF16)
    xf = ffn(xf, 0, 1, [o_a.reshape(t, A_W), o_b.reshape(t, B_VW)], [w_out[:A_W], w_out[A_W:]])

    xf = ffn(xf, 1, 0)
    a_log = _lane_pad(prm['od_a_log'][0])
    dt_bias = _lane_pad(prm['od_dt_bias'][0])
    if past is None:
        conv_prev = jnp.zeros((bsz, CONV_WIDTH - 1, CONV_CH), _F32)
        s0_delta = jnp.zeros((bsz, C_HEADS, C_HEAD_DIM, C_HEAD_DIM), _F32)
    else:
        conv_prev, s0_delta = past['conv'][0], past['delta'][0]
    conv_prev8 = jnp.pad(conv_prev, ((0, 0), (V7X_SUBLANES - (CONV_WIDTH - 1), 0), (0, 0)))
    act, z, gb, tail = _odd_in_proj(xf, prm['mix_norm'][1], _prep_odd_w_in(prm['od_w_in'][0]), a_log, dt_bias,
                                    prm['od_conv_w'][0], conv_prev8, lq)
    o_c, s_delta = _delta(r3(act), r3(z), r3(gb), s0_delta, prm['od_norm'][0])
    conv_new = tail[:, V7X_SUBLANES - (CONV_WIDTH - 1):]
    xf = ffn(xf, 1, 1, [o_c.reshape(t, C_W)], [prm['od_w_out'][0].astype(_BF16)])

    y = xf.reshape(bsz, lq, d)
    k_out = r3(k_new).reshape(1, bsz, lq, A_HEADS, A_HEAD_DIM)
    v_out = r3(v_new).reshape(1, bsz, lq, A_HEADS, A_HEAD_DIM)
    return y, k_out, v_out, ki_new[None], s_gla[None], s_delta[None], conv_new[None]


def kernel(x_prompt, x_sample, cache_attn_k, cache_attn_v, cache_idx_k, state_gla, state_delta, state_conv,
           ffn_norm, ffn_w_gate, ffn_w_up, ffn_w_down, mix_norm, ev_w_in, ev_q_norm, ev_k_norm, rel_bias,
           ev_gate_w2, ev_gate_b2, ev_gla_norm, ev_w_out, od_w_in, od_conv_w, od_a_log, od_dt_bias, od_norm,
           od_w_out):
    prm = {'ffn_norm': ffn_norm, 'ffn_w_gate': ffn_w_gate, 'ffn_w_up': ffn_w_up, 'ffn_w_down': ffn_w_down,
           'mix_norm': mix_norm, 'ev_w_in': ev_w_in, 'ev_q_norm': ev_q_norm, 'ev_k_norm': ev_k_norm,
           'rel_bias': rel_bias, 'ev_gate_w2': ev_gate_w2, 'ev_gate_b2': ev_gate_b2, 'ev_gla_norm': ev_gla_norm,
           'ev_w_out': ev_w_out, 'od_w_in': od_w_in, 'od_conv_w': od_conv_w, 'od_a_log': od_a_log,
           'od_dt_bias': od_dt_bias, 'od_norm': od_norm, 'od_w_out': od_w_out}
    past = {'k': cache_attn_k, 'v': cache_attn_v, 'ki': cache_idx_k, 'gla': state_gla,
            'delta': state_delta, 'conv': state_conv}
    y_prompt, p_k, p_v, p_ki, p_gla, p_delta, p_conv = _trunk(x_prompt, None, prm)
    y_sample, s_k, s_v, s_ki, s_gla, s_delta, s_conv = _trunk(x_sample, past, prm)
    return (y_prompt, y_sample, p_k, p_v, p_ki, p_gla, p_delta, p_conv,
            s_k, s_v, s_ki, s_gla, s_delta, s_conv)
```
